```python
import jax
import jax.numpy as jnp
from jax import lax
import numpy as np

D_MODEL = 2048
BATCH = 2
SEQ = 4096
DEPTH = 2
DEC_BATCH = 8
DEC_SEQ = 64
PAST_LEN = 4096

CHUNK = 64
N_MIXERS = 2
EPS = 1e-6
ROPE_THETA = 500000.0
ROPE_FRACTION = 4

A_HEADS = 16
A_KV_HEADS = 4
A_HEAD_DIM = D_MODEL // A_HEADS
IDX_HEADS = 8
IDX_DIM = 64
TOPK_MAX = 256
Q_BLOCK = 128
A_Q_DIM = A_HEADS * A_HEAD_DIM
A_KV_DIM = A_KV_HEADS * A_HEAD_DIM
A_SPLITS = (A_Q_DIM, A_Q_DIM + A_KV_DIM, A_Q_DIM + 2 * A_KV_DIM, A_Q_DIM + 2 * A_KV_DIM + IDX_HEADS * IDX_DIM, A_Q_DIM + 2 * A_KV_DIM + IDX_HEADS * IDX_DIM + IDX_DIM)
A_PROJ = A_SPLITS[-1] + IDX_HEADS

GDN_QK_HEADS = 16
GDN_V_HEADS = 32
GDN_DK = 128
GDN_DV = 128
CONV_W = 4
GDN_QK_DIM = GDN_QK_HEADS * GDN_DK
GDN_V_DIM = GDN_V_HEADS * GDN_DV
GDN_CONV_DIM = 2 * GDN_QK_DIM + GDN_V_DIM
GDN_SPLITS = (GDN_CONV_DIM, GDN_CONV_DIM + GDN_V_DIM, GDN_CONV_DIM + GDN_V_DIM + GDN_V_HEADS)
GDN_PROJ = GDN_SPLITS[-1] + GDN_V_HEADS

PEER_HEADS = 8
PEER_NKEYS = 128
PEER_EXPERTS = PEER_NKEYS * PEER_NKEYS
PEER_KEY_DIM = 256
PEER_HALF = PEER_KEY_DIM // 2
PEER_TOPK = 16
PEER_BLOCK = 64

kernel_name = 'hybrid_dsa_gdn_peer_stream_step'


def rmsnorm(x, gain):
    xf = x.astype(jnp.float32)
    y = xf * lax.rsqrt(jnp.mean(xf * xf, axis=-1, keepdims=True) + EPS)
    return (y * gain.astype(jnp.float32)).astype(x.dtype)


def adaln(c, w, b):
    mod = jax.nn.silu(c) @ w + b
    return [m[:, None, :] for m in jnp.split(mod, 6, axis=-1)]


def modulate(x, gain, shift, scale):
    return rmsnorm(x, gain) * (1.0 + scale) + shift


def rope_partial(x, pos):
    rot = x.shape[-1] // ROPE_FRACTION
    half = rot // 2
    inv_freq = ROPE_THETA ** (-jnp.arange(half, dtype=jnp.float32) / half)
    ang = pos.astype(jnp.float32)[:, None] * inv_freq[None, :]
    cos = jnp.cos(ang)[:, None, :]
    sin = jnp.sin(ang)[:, None, :]
    xf = x.astype(jnp.float32)
    x1, x2, rest = xf[..., :half], xf[..., half:rot], xf[..., rot:]
    out = jnp.concatenate([x1 * cos - x2 * sin, x2 * cos + x1 * sin, rest], axis=-1)
    return out.astype(x.dtype)


def dsa_core(q, qi, wi, k, v, kidx, q_pos, k_pos, topk):
    B, T, H, hd = q.shape
    qb = min(Q_BLOCK, T)
    nb = T // qb
    k_chunk = k_pos // CHUNK
    take = jax.vmap(lambda a, i: a[i])

    def to_blocks(a):
        return jnp.moveaxis(a.reshape(B, nb, qb, *a.shape[2:]), 1, 0)

    def block(args):
        q_b, qi_b, wi_b, p_b = args
        sc = jnp.einsum('bthd,bsd->bths', qi_b, kidx).astype(jnp.float32) * IDX_DIM ** -0.5
        score = jnp.einsum('bth,bths->bts', wi_b.astype(jnp.float32), jax.nn.relu(sc))
        admissible = k_chunk[None, :] <= (p_b // CHUNK)[:, None]
        score = jnp.where(admissible[None], score, -jnp.inf)
        top_val, top_idx = lax.top_k(score, topk)
        valid = top_val > -jnp.inf
        k_sel = take(k, top_idx)
        v_sel = take(v, top_idx)
        qg = q_b.reshape(B, qb, A_KV_HEADS, A_HEADS // A_KV_HEADS, hd)
        logit = jnp.einsum('btgrd,btkgd->btgrk', qg, k_sel).astype(jnp.float32) * hd ** -0.5
        logit = jnp.where(valid[:, :, None, None, :], logit, -jnp.inf)
        p = jax.nn.softmax(logit, axis=-1).astype(v.dtype)
        o = jnp.einsum('btgrk,btkgd->btgrd', p, v_sel)
        return o.reshape(B, qb, H * hd)

    out = lax.map(block, (to_blocks(q), to_blocks(qi), to_blocks(wi), q_pos.reshape(nb, qb)))
    return jnp.moveaxis(out, 0, 1).reshape(B, T, H * hd)


def mixer_dsa(h, pos0, past_k, past_v, past_kidx, w_in, w_out):
    B, T, _ = h.shape
    pos = pos0 + jnp.arange(T, dtype=jnp.int32)
    q, k, v, qi, ki, wi = jnp.split(h @ w_in, A_SPLITS, axis=-1)
    q = rope_partial(q.reshape(B, T, A_HEADS, A_HEAD_DIM), pos)
    k = rope_partial(k.reshape(B, T, A_KV_HEADS, A_HEAD_DIM), pos)
    v = v.reshape(B, T, A_KV_HEADS, A_HEAD_DIM)
    qi = rope_partial(qi.reshape(B, T, IDX_HEADS, IDX_DIM), pos)
    ki = rope_partial(ki[:, :, None, :], pos)[:, :, 0, :]
    wi = wi * IDX_HEADS ** -0.5
    if past_k is None:
        k_all, v_all, ki_all = k, v, ki
    else:
        k_all = jnp.concatenate([past_k, k], axis=1)
        v_all = jnp.concatenate([past_v, v], axis=1)
        ki_all = jnp.concatenate([past_kidx, ki], axis=1)
    n_keys = k_all.shape[1]
    k_pos = jnp.arange(n_keys, dtype=jnp.int32)
    topk = min(TOPK_MAX, n_keys // 4)
    o = dsa_core(q, qi, wi, k_all, v_all, ki_all, pos, k_pos, topk)
    return o @ w_out, k, v, ki


def gated_delta_chunked(q, k, v, g, beta, S0):
    B, T, H, dk = k.shape
    dv = v.shape[-1]
    C = min(CHUNK, T)
    N = T // C

    def heads_first(a):
        a = a.reshape(B, N, C, H, *a.shape[3:])
        return jnp.moveaxis(a, 3, 1)

    q, k, v, g, beta = [heads_first(a) for a in (q, k, v, g, beta)]
    G = jnp.cumsum(g, axis=-1)
    causal = jnp.tril(jnp.ones((C, C), bool))
    strict = jnp.tril(jnp.ones((C, C), bool), -1)
    decay = jnp.exp(jnp.where(causal, G[..., :, None] - G[..., None, :], -jnp.inf))
    kb = k * beta[..., None]
    A = jnp.where(strict, jnp.einsum('bhncd,bhnjd->bhncj', kb, k) * decay, 0.0)
    rhs = jnp.concatenate([v * beta[..., None], kb * jnp.exp(G)[..., None]], axis=-1)
    W = lax.linalg.triangular_solve(A + jnp.eye(C, dtype=jnp.float32), rhs, left_side=True, lower=True, unit_diagonal=True)
    W1, W2 = W[..., :dv], W[..., dv:]
    qk = jnp.einsum('bhncd,bhnjd->bhncj', q, k) * decay
    q_dec = q * jnp.exp(G)[..., None]
    k_end = k * jnp.exp(G[..., -1:] - G)[..., None]
    g_end = jnp.exp(G[..., -1])
    xs = [jnp.moveaxis(a, 2, 0) for a in (W1, W2, qk, q_dec, k_end, g_end)]

    def step(S, inp):
        w1, w2, qk_n, qd, ke, ge = inp
        U = w1 - jnp.einsum('bhcd,bhde->bhce', w2, S)
        o = jnp.einsum('bhcd,bhde->bhce', qd, S) + jnp.einsum('bhcj,bhje->bhce', qk_n, U)
        S = S * ge[..., None, None] + jnp.einsum('bhcd,bhce->bhde', ke, U)
        return S, o

    S, o = lax.scan(step, S0, xs)
    o = jnp.moveaxis(o, 0, 2).reshape(B, H, T, dv)
    return jnp.moveaxis(o, 1, 2), S


def mixer_gdn(h, conv_state, ssm_state, w_in, conv_w, a_log, dt_bias, norm_w, w_out):
    B, T, _ = h.shape
    qkv, z, beta_raw, a_raw = jnp.split(h @ w_in, GDN_SPLITS, axis=-1)
    if conv_state is None:
        conv_state = jnp.zeros((B, CONV_W - 1, GDN_CONV_DIM), h.dtype)
    if ssm_state is None:
        ssm_state = jnp.zeros((B, GDN_V_HEADS, GDN_DK, GDN_DV), jnp.float32)
    xe = jnp.concatenate([conv_state, qkv], axis=1)
    conv = xe[:, 0:T] * conv_w[0]
    for j in range(1, CONV_W):
        conv = conv + xe[:, j:j + T] * conv_w[j]
    conv = jax.nn.silu(conv).astype(jnp.float32)
    new_conv = xe[:, -(CONV_W - 1):]
    q, k, v = jnp.split(conv, (GDN_QK_DIM, 2 * GDN_QK_DIM), axis=-1)
    rep = GDN_V_HEADS // GDN_QK_HEADS
    q = q.reshape(B, T, GDN_QK_HEADS, GDN_DK)
    k = k.reshape(B, T, GDN_QK_HEADS, GDN_DK)
    q = q * lax.rsqrt(jnp.sum(q * q, axis=-1, keepdims=True) + EPS) * GDN_DK ** -0.5
    k = k * lax.rsqrt(jnp.sum(k * k, axis=-1, keepdims=True) + EPS)
    q = jnp.repeat(q, rep, axis=2)
    k = jnp.repeat(k, rep, axis=2)
    v = v.reshape(B, T, GDN_V_HEADS, GDN_DV)
    beta = jax.nn.sigmoid(beta_raw.astype(jnp.float32))
    g = -jnp.exp(a_log.astype(jnp.float32)) * jax.nn.softplus(a_raw.astype(jnp.float32) + dt_bias.astype(jnp.float32))
    o, S = gated_delta_chunked(q, k, v, g, beta, ssm_state.astype(jnp.float32))
    zf = z.reshape(B, T, GDN_V_HEADS, GDN_DV).astype(jnp.float32)
    o = rmsnorm(o, norm_w) * jax.nn.silu(zf)
    out = o.astype(h.dtype).reshape(B, T, GDN_V_DIM) @ w_out
    return out, new_conv, S.astype(h.dtype)


def peer(h, w_query, sub_keys, expert_u, expert_v):
    B, T, D = h.shape
    tb = min(PEER_BLOCK, T)
    xb = h.reshape(B * T // tb, tb, D)
    ncand = PEER_TOPK * PEER_TOPK

    def block(xt):
        qr = (xt @ w_query).reshape(tb, PEER_HEADS, 2, PEER_HALF)
        s = jnp.einsum('thpd,pkd->thpk', qr, sub_keys).astype(jnp.float32)
        s1, i1 = lax.top_k(s[:, :, 0], PEER_TOPK)
        s2, i2 = lax.top_k(s[:, :, 1], PEER_TOPK)
        cand = (s1[..., :, None] + s2[..., None, :]).reshape(tb, PEER_HEADS, ncand)
        cidx = (i1[..., :, None] * PEER_NKEYS + i2[..., None, :]).reshape(tb, PEER_HEADS, ncand)
        best, pick = lax.top_k(cand, PEER_TOPK)
        eidx = jnp.take_along_axis(cidx, pick, axis=-1)
        gate = jax.nn.softmax(best, axis=-1)
        u = expert_u[eidx]
        act = jax.nn.gelu(jnp.einsum('thkd,td->thk', u, xt).astype(jnp.float32), approximate=False)
        coef = (gate * act).astype(expert_v.dtype)
        return jnp.einsum('thk,thkd->td', coef, expert_v[eidx])

    return lax.map(block, xb).reshape(B, T, D)


def setup_inputs(seed: int = 0) -> dict:
    key = jax.random.key(seed)
    keys = jax.random.split(key, 48)
    counter = [0]

    def nxt():
        counter[0] += 1
        return keys[counter[0] - 1]

    def nrm(shape, scale):
        return jax.random.normal(nxt(), shape, jnp.float32) * scale

    def gain(n):
        return 1.0 + nrm((n,), 0.05)

    d = D_MODEL
    inp = {}
    inp['x_prompt'] = nrm((BATCH, SEQ, d), 1.0)
    inp['x_sample'] = nrm((DEC_BATCH, DEC_SEQ, d), 1.0)
    inp['c_prompt'] = nrm((BATCH, d), 1.0)
    inp['c_sample'] = nrm((DEC_BATCH, d), 1.0)
    inp['cache_k_l0'] = nrm((DEC_BATCH, PAST_LEN, A_KV_HEADS, A_HEAD_DIM), 1.0)
    inp['cache_v_l0'] = nrm((DEC_BATCH, PAST_LEN, A_KV_HEADS, A_HEAD_DIM), 1.0)
    inp['cache_kidx_l0'] = nrm((DEC_BATCH, PAST_LEN, IDX_DIM), 1.0)
    inp['state_conv_l1'] = nrm((DEC_BATCH, CONV_W - 1, GDN_CONV_DIM), 1.0)
    inp['state_ssm_l1'] = nrm((DEC_BATCH, GDN_V_HEADS, GDN_DK, GDN_DV), 0.05)
    inp['norm1_l0'] = gain(d)
    inp['norm2_l0'] = gain(d)
    inp['ada_w_l0'] = nrm((d, 6 * d), 0.5 * d ** -0.5)
    inp['ada_b_l0'] = nrm((6 * d,), 0.02)
    inp['attn_in_l0'] = nrm((d, A_PROJ), d ** -0.5)
    inp['attn_out_l0'] = nrm((A_Q_DIM, d), A_Q_DIM ** -0.5)
    inp['peer_query_l0'] = nrm((d, PEER_HEADS * PEER_KEY_DIM), d ** -0.5)
    inp['peer_keys_l0'] = nrm((2, PEER_NKEYS, PEER_HALF), PEER_HALF ** -0.5)
    inp['peer_u_l0'] = nrm((PEER_EXPERTS, d), d ** -0.5)
    inp['peer_v_l0'] = nrm((PEER_EXPERTS, d), 0.5)
    inp['norm1_l1'] = gain(d)
    inp['norm2_l1'] = gain(d)
    inp['ada_w_l1'] = nrm((d, 6 * d), 0.5 * d ** -0.5)
    inp['ada_b_l1'] = nrm((6 * d,), 0.02)
    inp['gdn_in_l1'] = nrm((d, GDN_PROJ), d ** -0.5)
    inp['gdn_conv_l1'] = nrm((CONV_W, GDN_CONV_DIM), CONV_W ** -0.5)
    inp['gdn_a_log_l1'] = jnp.log(jax.random.uniform(nxt(), (GDN_V_HEADS,), jnp.float32, 1.0, 16.0))
    dt = jnp.exp(jax.random.uniform(nxt(), (GDN_V_HEADS,), jnp.float32, np.log(0.001).astype(np.float32), np.log(0.1).astype(np.float32)))
    inp['gdn_dt_bias_l1'] = jnp.log(jnp.expm1(dt))
    inp['gdn_norm_l1'] = gain(GDN_DV)
    inp['gdn_out_l1'] = nrm((GDN_V_DIM, d), GDN_V_DIM ** -0.5)
    inp['peer_query_l1'] = nrm((d, PEER_HEADS * PEER_KEY_DIM), d ** -0.5)
    inp['peer_keys_l1'] = nrm((2, PEER_NKEYS, PEER_HALF), PEER_HALF ** -0.5)
    inp['peer_u_l1'] = nrm((PEER_EXPERTS, d), d ** -0.5)
    inp['peer_v_l1'] = nrm((PEER_EXPERTS, d), 0.5)
    inp['final_norm'] = gain(d)
    return inp


def reference(x_prompt, x_sample, c_prompt, c_sample, cache_k_l0, cache_v_l0, cache_kidx_l0, state_conv_l1, state_ssm_l1, norm1_l0, norm2_l0, ada_w_l0, ada_b_l0, attn_in_l0, attn_out_l0, peer_query_l0, peer_keys_l0, peer_u_l0, peer_v_l0, norm1_l1, norm2_l1, ada_w_l1, ada_b_l1, gdn_in_l1, gdn_conv_l1, gdn_a_log_l1, gdn_dt_bias_l1, gdn_norm_l1, gdn_out_l1, peer_query_l1, peer_keys_l1, peer_u_l1, peer_v_l1, final_norm):
    norm1 = (norm1_l0, norm1_l1)
    norm2 = (norm2_l0, norm2_l1)
    ada_w = (ada_w_l0, ada_w_l1)
    ada_b = (ada_b_l0, ada_b_l1)
    peer_query = (peer_query_l0, peer_query_l1)
    peer_keys = (peer_keys_l0, peer_keys_l1)
    peer_u = (peer_u_l0, peer_u_l1)
    peer_v = (peer_v_l0, peer_v_l1)
    xp, xs = x_prompt, x_sample
    for i in range(DEPTH):
        mp = adaln(c_prompt, ada_w[i], ada_b[i])
        ms = adaln(c_sample, ada_w[i], ada_b[i])
        hp = modulate(xp, norm1[i], mp[0], mp[1])
        hs = modulate(xs, norm1[i], ms[0], ms[1])
        if i % N_MIXERS == 0:
            op, new_k_prompt, new_v_prompt, new_kidx_prompt = mixer_dsa(hp, 0, None, None, None, attn_in_l0, attn_out_l0)
            os_, new_k_sample, new_v_sample, new_kidx_sample = mixer_dsa(hs, PAST_LEN, cache_k_l0, cache_v_l0, cache_kidx_l0, attn_in_l0, attn_out_l0)
        else:
            op, new_conv_prompt, new_ssm_prompt = mixer_gdn(hp, None, None, gdn_in_l1, gdn_conv_l1, gdn_a_log_l1, gdn_dt_bias_l1, gdn_norm_l1, gdn_out_l1)
            os_, new_conv_sample, new_ssm_sample = mixer_gdn(hs, state_conv_l1, state_ssm_l1, gdn_in_l1, gdn_conv_l1, gdn_a_log_l1, gdn_dt_bias_l1, gdn_norm_l1, gdn_out_l1)
        xp = xp + mp[2] * op
        xs = xs + ms[2] * os_
        hp = modulate(xp, norm2[i], mp[3], mp[4])
        hs = modulate(xs, norm2[i], ms[3], ms[4])
        xp = xp + mp[5] * peer(hp, peer_query[i], peer_keys[i], peer_u[i], peer_v[i])
        xs = xs + ms[5] * peer(hs, peer_query[i], peer_keys[i], peer_u[i], peer_v[i])
    y_prompt = rmsnorm(xp, final_norm)
    y_sample = rmsnorm(xs, final_norm)
    return (y_prompt, y_sample, new_k_prompt, new_v_prompt, new_kidx_prompt, new_k_sample, new_v_sample, new_kidx_sample, new_conv_prompt, new_ssm_prompt, new_conv_sample, new_ssm_sample)
```

```python
import functools

import jax
import jax.numpy as jnp
from jax import lax
from jax.experimental import pallas as pl
from jax.experimental.pallas import tpu as pltpu

CHUNK = 64
EPS = 1e-6
ROPE_THETA = 500000.0
ROPE_FRACTION = 4
A_HEADS = 16
A_KV_HEADS = 4
IDX_HEADS = 8
IDX_DIM = 64
TOPK_MAX = 256
Q_BLOCK = 128
GDN_QK_HEADS = 16
GDN_V_HEADS = 32
GDN_DK = 128
GDN_DV = 128
CONV_W = 4
PEER_HEADS = 8
PEER_NKEYS = 128
PEER_KEY_DIM = 256
PEER_HALF = PEER_KEY_DIM // 2
PEER_TOPK = 16
PEER_BLOCK = 64

V7X_VMEM_LIMIT_BYTES = 48 * 1024 * 1024
LANE = 128
SUBLANE = 8


def _round_up(n, m):
    return (n + m - 1) // m * m


def _matmul_kernel(x_ref, w_ref, o_ref, *, passes):
    x = x_ref[...]
    w = w_ref[...]
    xh = x.astype(jnp.bfloat16)
    wh = w.astype(jnp.bfloat16)
    acc = jnp.dot(xh, wh, preferred_element_type=jnp.float32)
    if passes == 3:
        xl = (x - xh.astype(jnp.float32)).astype(jnp.bfloat16)
        wl = (w - wh.astype(jnp.float32)).astype(jnp.bfloat16)
        acc = acc + jnp.dot(xh, wl, preferred_element_type=jnp.float32)
        acc = acc + jnp.dot(xl, wh, preferred_element_type=jnp.float32)
    o_ref[...] = acc


def pmatmul(x, w, *, passes=1, tm=512, tn=512):
    M, K = x.shape
    N = w.shape[1]
    tm = min(tm, _round_up(M, SUBLANE))
    tn = min(tn, _round_up(N, LANE))
    Mp, Np = _round_up(M, tm), _round_up(N, tn)
    if Mp != M:
        x = jnp.pad(x, ((0, Mp - M), (0, 0)))
    if Np != N:
        w = jnp.pad(w, ((0, 0), (0, Np - N)))
    out = pl.pallas_call(
        functools.partial(_matmul_kernel, passes=passes),
        grid=(Mp // tm, Np // tn),
        in_specs=[pl.BlockSpec((tm, K), lambda i, j: (i, 0)),
                  pl.BlockSpec((K, tn), lambda i, j: (0, j))],
        out_specs=pl.BlockSpec((tm, tn), lambda i, j: (i, j)),
        out_shape=jax.ShapeDtypeStruct((Mp, Np), jnp.float32),
        compiler_params=pltpu.CompilerParams(
            dimension_semantics=("parallel", "arbitrary"),
            vmem_limit_bytes=V7X_VMEM_LIMIT_BYTES),
    )(x, w)
    return out[:M, :N]


def _mm(x, w, **kw):
    lead = x.shape[:-1]
    return pmatmul(x.reshape(-1, x.shape[-1]), w, **kw).reshape(*lead, w.shape[1])


def _rmsnorm(x, gain):
    y = x * lax.rsqrt(jnp.mean(x * x, axis=-1, keepdims=True) + EPS)
    return y * gain


def _adaln(c, w, b):
    mod = _mm(jax.nn.silu(c), w, passes=3) + b
    return [m[:, None, :] for m in jnp.split(mod, 6, axis=-1)]


def _modulate(x, gain, shift, scale):
    return _rmsnorm(x, gain) * (1.0 + scale) + shift


def _rope_partial(x, pos):
    rot = x.shape[-1] // ROPE_FRACTION
    half = rot // 2
    inv_freq = ROPE_THETA ** (-jnp.arange(half, dtype=jnp.float32) / half)
    ang = pos.astype(jnp.float32)[:, None] * inv_freq[None, :]
    cos = jnp.cos(ang)[:, None, :]
    sin = jnp.sin(ang)[:, None, :]
    x1, x2, rest = x[..., :half], x[..., half:rot], x[..., rot:]
    return jnp.concatenate([x1 * cos - x2 * sin, x2 * cos + x1 * sin, rest], axis=-1)


def _dsa_core(q, qi, wi, k, v, kidx, q_pos, k_pos, topk):
    B, T, H, hd = q.shape
    qb = min(Q_BLOCK, T)
    nb = T // qb
    k_chunk = k_pos // CHUNK
    take = jax.vmap(lambda a, i: a[i])

    def to_blocks(a):
        return jnp.moveaxis(a.reshape(B, nb, qb, *a.shape[2:]), 1, 0)

    def block(args):
        q_b, qi_b, wi_b, p_b = args
        sc = jnp.einsum('bthd,bsd->bths', qi_b, kidx).astype(jnp.float32) * IDX_DIM ** -0.5
        score = jnp.einsum('bth,bths->bts', wi_b, jax.nn.relu(sc))
        admissible = k_chunk[None, :] <= (p_b // CHUNK)[:, None]
        score = jnp.where(admissible[None], score, -jnp.inf)
        top_val, top_idx = lax.top_k(score, topk)
        valid = top_val > -jnp.inf
        k_sel = take(k, top_idx)
        v_sel = take(v, top_idx)
        qg = q_b.reshape(B, qb, A_KV_HEADS, A_HEADS // A_KV_HEADS, hd)
        logit = jnp.einsum('btgrd,btkgd->btgrk', qg, k_sel) * hd ** -0.5
        logit = jnp.where(valid[:, :, None, None, :], logit, -jnp.inf)
        p = jax.nn.softmax(logit, axis=-1)
        o = jnp.einsum('btgrk,btkgd->btgrd', p, v_sel)
        return o.reshape(B, qb, H * hd)

    out = lax.map(block, (to_blocks(q), to_blocks(qi), to_blocks(wi), q_pos.reshape(nb, qb)))
    return jnp.moveaxis(out, 0, 1).reshape(B, T, H * hd)


def _mixer_dsa(h, pos0, past_k, past_v, past_kidx, w_in, w_out):
    B, T, D = h.shape
    hd = D // A_HEADS
    q_dim, kv_dim = A_HEADS * hd, A_KV_HEADS * hd
    splits = (q_dim, q_dim + kv_dim, q_dim + 2 * kv_dim, q_dim + 2 * kv_dim + IDX_HEADS * IDX_DIM,
              q_dim + 2 * kv_dim + IDX_HEADS * IDX_DIM + IDX_DIM)
    pos = pos0 + jnp.arange(T, dtype=jnp.int32)
    q, k, v, qi, ki, wi = jnp.split(_mm(h, w_in), splits, axis=-1)
    q = _rope_partial(q.reshape(B, T, A_HEADS, hd), pos)
    k = _rope_partial(k.reshape(B, T, A_KV_HEADS, hd), pos)
    v = v.reshape(B, T, A_KV_HEADS, hd)
    qi = _rope_partial(qi.reshape(B, T, IDX_HEADS, IDX_DIM), pos)
    ki = _rope_partial(ki[:, :, None, :], pos)[:, :, 0, :]
    wi = wi * IDX_HEADS ** -0.5
    if past_k is None:
        k_all, v_all, ki_all = k, v, ki
    else:
        k_all = jnp.concatenate([past_k, k], axis=1)
        v_all = jnp.concatenate([past_v, v], axis=1)
        ki_all = jnp.concatenate([past_kidx, ki], axis=1)
    n_keys = k_all.shape[1]
    k_pos = jnp.arange(n_keys, dtype=jnp.int32)
    topk = min(TOPK_MAX, n_keys // 4)
    o = _dsa_core(q, qi, wi, k_all, v_all, ki_all, pos, k_pos, topk)
    return _mm(o, w_out), k, v, ki


def _gated_delta_chunked(q, k, v, g, beta, S0):
    B, T, H, dk = k.shape
    dv = v.shape[-1]
    C = min(CHUNK, T)
    N = T // C

    def heads_first(a):
        a = a.reshape(B, N, C, H, *a.shape[3:])
        return jnp.moveaxis(a, 3, 1)

    q, k, v, g, beta = [heads_first(a) for a in (q, k, v, g, beta)]
    G = jnp.cumsum(g, axis=-1)
    causal = jnp.tril(jnp.ones((C, C), bool))
    strict = jnp.tril(jnp.ones((C, C), bool), -1)
    decay = jnp.exp(jnp.where(causal, G[..., :, None] - G[..., None, :], -jnp.inf))
    kb = k * beta[..., None]
    A = jnp.where(strict, jnp.einsum('bhncd,bhnjd->bhncj', kb, k) * decay, 0.0)
    rhs = jnp.concatenate([v * beta[..., None], kb * jnp.exp(G)[..., None]], axis=-1)
    W = lax.linalg.triangular_solve(A + jnp.eye(C, dtype=jnp.float32), rhs, left_side=True, lower=True,
                                    unit_diagonal=True)
    W1, W2 = W[..., :dv], W[..., dv:]
    qk = jnp.einsum('bhncd,bhnjd->bhncj', q, k) * decay
    q_dec = q * jnp.exp(G)[..., None]
    k_end = k * jnp.exp(G[..., -1:] - G)[..., None]
    g_end = jnp.exp(G[..., -1])
    xs = [jnp.moveaxis(a, 2, 0) for a in (W1, W2, qk, q_dec, k_end, g_end)]

    def step(S, inp):
        w1, w2, qk_n, qd, ke, ge = inp
        U = w1 - jnp.einsum('bhcd,bhde->bhce', w2, S)
        o = jnp.einsum('bhcd,bhde->bhce', qd, S) + jnp.einsum('bhcj,bhje->bhce', qk_n, U)
        S = S * ge[..., None, None] + jnp.einsum('bhcd,bhce->bhde', ke, U)
        return S, o

    S, o = lax.scan(step, S0, xs)
    o = jnp.moveaxis(o, 0, 2).reshape(B, H, T, dv)
    return jnp.moveaxis(o, 1, 2), S


def _mixer_gdn(h, conv_state, ssm_state, w_in, conv_w, a_log, dt_bias, norm_w, w_out):
    B, T, _ = h.shape
    qk_dim, v_dim = GDN_QK_HEADS * GDN_DK, GDN_V_HEADS * GDN_DV
    conv_dim = 2 * qk_dim + v_dim
    splits = (conv_dim, conv_dim + v_dim, conv_dim + v_dim + GDN_V_HEADS)
    qkv, z, beta_raw, a_raw = jnp.split(_mm(h, w_in), splits, axis=-1)
    if conv_state is None:
        conv_state = jnp.zeros((B, CONV_W - 1, conv_dim), h.dtype)
    if ssm_state is None:
        ssm_state = jnp.zeros((B, GDN_V_HEADS, GDN_DK, GDN_DV), jnp.float32)
    xe = jnp.concatenate([conv_state, qkv], axis=1)
    conv = xe[:, 0:T] * conv_w[0]
    for j in range(1, CONV_W):
        conv = conv + xe[:, j:j + T] * conv_w[j]
    conv = jax.nn.silu(conv)
    new_conv = xe[:, -(CONV_W - 1):]
    q, k, v = jnp.split(conv, (qk_dim, 2 * qk_dim), axis=-1)
    rep = GDN_V_HEADS // GDN_QK_HEADS
    q = q.reshape(B, T, GDN_QK_HEADS, GDN_DK)
    k = k.reshape(B, T, GDN_QK_HEADS, GDN_DK)
    q = q * lax.rsqrt(jnp.sum(q * q, axis=-1, keepdims=True) + EPS) * GDN_DK ** -0.5
    k = k * lax.rsqrt(jnp.sum(k * k, axis=-1, keepdims=True) + EPS)
    q = jnp.repeat(q, rep, axis=2)
    k = jnp.repeat(k, rep, axis=2)
    v = v.reshape(B, T, GDN_V_HEADS, GDN_DV)
    beta = jax.nn.sigmoid(beta_raw)
    g = -jnp.exp(a_log) * jax.nn.softplus(a_raw + dt_bias)
    o, S = _gated_delta_chunked(q, k, v, g, beta, ssm_state)
    zf = z.reshape(B, T, GDN_V_HEADS, GDN_DV)
    o = _rmsnorm(o, norm_w) * jax.nn.silu(zf)
    out = _mm(o.reshape(B, T, v_dim), w_out)
    return out, new_conv, S


def _peer(h, w_query, sub_keys, expert_u, expert_v):
    B, T, D = h.shape
    tb = min(PEER_BLOCK, T)
    qall = _mm(h, w_query, passes=3)
    xb = h.reshape(B * T // tb, tb, D)
    qb = qall.reshape(B * T // tb, tb, -1)
    ncand = PEER_TOPK * PEER_TOPK

    def block(args):
        xt, qr = args
        qr = qr.reshape(tb, PEER_HEADS, 2, PEER_HALF)
        s = jnp.einsum('thpd,pkd->thpk', qr, sub_keys, precision=lax.Precision.HIGHEST)
        s1, i1 = lax.top_k(s[:, :, 0], PEER_TOPK)
        s2, i2 = lax.top_k(s[:, :, 1], PEER_TOPK)
        cand = (s1[..., :, None] + s2[..., None, :]).reshape(tb, PEER_HEADS, ncand)
        cidx = (i1[..., :, None] * PEER_NKEYS + i2[..., None, :]).reshape(tb, PEER_HEADS, ncand)
        best, pick = lax.top_k(cand, PEER_TOPK)
        eidx = jnp.take_along_axis(cidx, pick, axis=-1)
        gate = jax.nn.softmax(best, axis=-1)
        u = expert_u[eidx]
        act = jax.nn.gelu(jnp.einsum('thkd,td->thk', u, xt), approximate=False)
        coef = gate * act
        return jnp.einsum('thk,thkd->td', coef, expert_v[eidx])

    return lax.map(block, (xb, qb)).reshape(B, T, D)


def kernel(x_prompt, x_sample, c_prompt, c_sample, cache_k_l0, cache_v_l0, cache_kidx_l0, state_conv_l1, state_ssm_l1, norm1_l0, norm2_l0, ada_w_l0, ada_b_l0, attn_in_l0, attn_out_l0, peer_query_l0, peer_keys_l0, peer_u_l0, peer_v_l0, norm1_l1, norm2_l1, ada_w_l1, ada_b_l1, gdn_in_l1, gdn_conv_l1, gdn_a_log_l1, gdn_dt_bias_l1, gdn_norm_l1, gdn_out_l1, peer_query_l1, peer_keys_l1, peer_u_l1, peer_v_l1, final_norm):
    past_len = cache_k_l0.shape[1]
    norm1 = (norm1_l0, norm1_l1)
    norm2 = (norm2_l0, norm2_l1)
    ada_w = (ada_w_l0, ada_w_l1)
    ada_b = (ada_b_l0, ada_b_l1)
    peer_query = (peer_query_l0, peer_query_l1)
    peer_keys = (peer_keys_l0, peer_keys_l1)
    peer_u = (peer_u_l0, peer_u_l1)
    peer_v = (peer_v_l0, peer_v_l1)
    xp, xs = x_prompt, x_sample
    for i in range(2):
        mp = _adaln(c_prompt, ada_w[i], ada_b[i])
        ms = _adaln(c_sample, ada_w[i], ada_b[i])
        hp = _modulate(xp, norm1[i], mp[0], mp[1])
        hs = _modulate(xs, norm1[i], ms[0], ms[1])
        if i == 0:
            op, nkp, nvp, nkip = _mixer_dsa(hp, 0, None, None, None, attn_in_l0, attn_out_l0)
            os_, nks, nvs, nkis = _mixer_dsa(hs, past_len, cache_k_l0, cache_v_l0, cache_kidx_l0, attn_in_l0, attn_out_l0)
        else:
            op, ncp, nsp = _mixer_gdn(hp, None, None, gdn_in_l1, gdn_conv_l1, gdn_a_log_l1, gdn_dt_bias_l1, gdn_norm_l1, gdn_out_l1)
            os_, ncs, nss = _mixer_gdn(hs, state_conv_l1, state_ssm_l1, gdn_in_l1, gdn_conv_l1, gdn_a_log_l1, gdn_dt_bias_l1, gdn_norm_l1, gdn_out_l1)
        xp = xp + mp[2] * op
        xs = xs + ms[2] * os_
        hp = _modulate(xp, norm2[i], mp[3], mp[4])
        hs = _modulate(xs, norm2[i], ms[3], ms[4])
        xp = xp + mp[5] * _peer(hp, peer_query[i], peer_keys[i], peer_u[i], peer_v[i])
        xs = xs + ms[5] * _peer(hs, peer_query[i], peer_keys[i], peer_u[i], peer_v[i])
    y_prompt = _rmsnorm(xp, final_norm)
    y_sample = _rmsnorm(xs, final_norm)
    return (y_prompt, y_sample, nkp, nvp, nkip, nks, nvs, nkis, ncp, nsp, ncs, nss)
```

```python
import functools

import jax
import jax.numpy as jnp
from jax import lax
from jax.experimental import pallas as pl
from jax.experimental.pallas import tpu as pltpu

CHUNK = 64
EPS = 1e-6
ROPE_THETA = 500000.0
ROPE_FRACTION = 4
A_HEADS = 16
A_KV_HEADS = 4
IDX_HEADS = 8
IDX_DIM = 64
TOPK_MAX = 256
Q_BLOCK = 128
GDN_QK_HEADS = 16
GDN_V_HEADS = 32
GDN_DK = 128
GDN_DV = 128
CONV_W = 4
PEER_HEADS = 8
PEER_NKEYS = 128
PEER_KEY_DIM = 256
PEER_HALF = PEER_KEY_DIM // 2
PEER_TOPK = 16
PEER_BLOCK = 64

V7X_VMEM_LIMIT_BYTES = 48 * 1024 * 1024
LANE = 128
SUBLANE = 8


def _round_up(n, m):
    return (n + m - 1) // m * m


def _matmul_kernel(x_ref, w_ref, o_ref, *, passes):
    x = x_ref[...]
    w = w_ref[...]
    xh = x.astype(jnp.bfloat16)
    wh = w.astype(jnp.bfloat16)
    acc = jnp.dot(xh, wh, preferred_element_type=jnp.float32)
    if passes == 3:
        xl = (x - xh.astype(jnp.float32)).astype(jnp.bfloat16)
        wl = (w - wh.astype(jnp.float32)).astype(jnp.bfloat16)
        acc = acc + jnp.dot(xh, wl, preferred_element_type=jnp.float32)
        acc = acc + jnp.dot(xl, wh, preferred_element_type=jnp.float32)
    o_ref[...] = acc


def pmatmul(x, w, *, passes=1, tm=512, tn=512):
    M, K = x.shape
    N = w.shape[1]
    tm = min(tm, _round_up(M, SUBLANE))
    tn = min(tn, _round_up(N, LANE))
    Mp, Np = _round_up(M, tm), _round_up(N, tn)
    if Mp != M:
        x = jnp.pad(x, ((0, Mp - M), (0, 0)))
    if Np != N:
        w = jnp.pad(w, ((0, 0), (0, Np - N)))
    out = pl.pallas_call(
        functools.partial(_matmul_kernel, passes=passes),
        grid=(Mp // tm, Np // tn),
        in_specs=[pl.BlockSpec((tm, K), lambda i, j: (i, 0)),
                  pl.BlockSpec((K, tn), lambda i, j: (0, j))],
        out_specs=pl.BlockSpec((tm, tn), lambda i, j: (i, j)),
        out_shape=jax.ShapeDtypeStruct((Mp, Np), jnp.float32),
        compiler_params=pltpu.CompilerParams(
            dimension_semantics=("parallel", "arbitrary"),
            vmem_limit_bytes=V7X_VMEM_LIMIT_BYTES),
    )(x, w)
    return out[:M, :N]


def _mm(x, w, **kw):
    lead = x.shape[:-1]
    return pmatmul(x.reshape(-1, x.shape[-1]), w, **kw).reshape(*lead, w.shape[1])


def _rmsnorm(x, gain):
    y = x * lax.rsqrt(jnp.mean(x * x, axis=-1, keepdims=True) + EPS)
    return y * gain


def _adaln(c, w, b):
    mod = _mm(jax.nn.silu(c), w, passes=3) + b
    return [m[:, None, :] for m in jnp.split(mod, 6, axis=-1)]


def _modulate(x, gain, shift, scale):
    return _rmsnorm(x, gain) * (1.0 + scale) + shift


def _rope_partial(x, pos):
    rot = x.shape[-1] // ROPE_FRACTION
    half = rot // 2
    inv_freq = ROPE_THETA ** (-jnp.arange(half, dtype=jnp.float32) / half)
    ang = pos.astype(jnp.float32)[:, None] * inv_freq[None, :]
    cos = jnp.cos(ang)[:, None, :]
    sin = jnp.sin(ang)[:, None, :]
    x1, x2, rest = x[..., :half], x[..., half:rot], x[..., rot:]
    return jnp.concatenate([x1 * cos - x2 * sin, x2 * cos + x1 * sin, rest], axis=-1)


_NT_DIMS = (((1,), (1,)), ((), ()))
INT32_MIN = -2 ** 31
_NEG_INF_KEY = -2139095041
IDX_PACK = 4 * IDX_DIM


def _ordered_key(x):
    bits = pltpu.bitcast(x, jnp.int32)
    return bits ^ ((bits >> 31) & 0x7FFFFFFF)


def _lane_tile_sum(x, width=LANE):
    out = x[:, :width]
    for c in range(1, x.shape[1] // width):
        out = out + x[:, c * width:(c + 1) * width]
    return out


def _dsa_kernel(qi_ref, wi_ref, kidx_ref, q_ref, k_ref, v_ref, o_ref,
                key_ref, bias_ref, m_ref, l_ref, acc_ref, *, pos0, topk, tk, scale):
    i = pl.program_id(1)
    g = pl.program_id(2)
    tq = q_ref.shape[1]
    hd = k_ref.shape[2]
    rep = q_ref.shape[2] // hd
    first = pos0 + i * tq
    n_valid = (((first + tq - 1) >> 6) + 1) * CHUNK
    n_tiles = (n_valid + tk - 1) // tk

    @pl.when(g == 0)
    def _():
        row = lax.broadcasted_iota(jnp.int32, (tq, 1), 0)
        lim = (((first + row) >> 6) + 1) * CHUNK
        w = wi_ref[0]

        def score_tile(j, c):
            off = pl.multiple_of(j * tk, tk)
            kt = kidx_ref[0, pl.ds(off, tk), :]
            s = jnp.zeros((tq, tk), jnp.float32)
            for h in range(IDX_HEADS):
                sc = lax.dot_general(qi_ref[0, :, h * IDX_PACK:(h + 1) * IDX_PACK], kt, _NT_DIMS,
                                     preferred_element_type=jnp.float32)
                s = s + w[:, h:h + 1] * jnp.maximum(sc, 0.0)
            col = off + lax.broadcasted_iota(jnp.int32, (tq, tk), 1)
            s = jnp.where(col < lim, s + 0.0, NEG_INF)
            key_ref[:, pl.ds(off, tk)] = _ordered_key(s)
            return c

        lax.fori_loop(0, n_tiles, score_tile, 0)

        def bit_step(b, thr):
            cand = thr + lax.shift_left(jnp.int32(1), 31 - b)

            def count_tile(j, c):
                off = pl.multiple_of(j * tk, tk)
                ge = jnp.where(key_ref[:, pl.ds(off, tk)] >= cand, 1.0, 0.0)
                return c + _lane_tile_sum(ge)

            c = lax.fori_loop(0, n_tiles, count_tile, jnp.zeros((tq, LANE), jnp.float32))
            cnt = jnp.sum(c, axis=1, keepdims=True)
            return jnp.where(cnt >= topk, cand, thr)

        thr = lax.fori_loop(0, 32, bit_step, jnp.full((tq, 1), INT32_MIN, jnp.int32))
        thr = jnp.maximum(thr, _NEG_INF_KEY + 1)

        def bias_tile(j, c):
            off = pl.multiple_of(j * tk, tk)
            bias_ref[:, pl.ds(off, tk)] = jnp.where(key_ref[:, pl.ds(off, tk)] >= thr, 0.0, NEG_INF)
            return c

        lax.fori_loop(0, n_tiles, bias_tile, 0)

    qg = jnp.concatenate([q_ref[0, :, r * hd:(r + 1) * hd] for r in range(rep)], axis=0)
    m_ref[...] = jnp.full(m_ref.shape, NEG_INF, jnp.float32)
    l_ref[...] = jnp.zeros(l_ref.shape, jnp.float32)
    acc_ref[...] = jnp.zeros(acc_ref.shape, jnp.float32)

    def att_tile(j, c):
        off = pl.multiple_of(j * tk, tk)
        kt = k_ref[0, pl.ds(off, tk), :]
        vt = v_ref[0, pl.ds(off, tk), :]
        lg = lax.dot_general(qg, kt, _NT_DIMS, preferred_element_type=jnp.float32) * scale
        b = bias_ref[:, pl.ds(off, tk)]
        lg = lg + jnp.concatenate([b] * rep, axis=0)
        m_old = m_ref[...]
        m_new = jnp.maximum(m_old, jnp.max(lg, axis=1, keepdims=True))
        m_safe = jnp.where(m_new == NEG_INF, 0.0, m_new)
        p = jnp.exp(lg - m_safe)
        alpha = jnp.exp(m_old - m_safe)
        l_ref[...] = alpha * l_ref[...] + jnp.sum(p, axis=1, keepdims=True)
        acc_ref[...] = alpha * acc_ref[...] + jnp.dot(p.astype(jnp.bfloat16), vt,
                                                      preferred_element_type=jnp.float32)
        m_ref[...] = m_new
        return c

    lax.fori_loop(0, n_tiles, att_tile, 0)
    out = acc_ref[...] / l_ref[...]
    for r in range(rep):
        o_ref[0, :, r * hd:(r + 1) * hd] = out[r * tq:(r + 1) * tq].astype(o_ref.dtype)


def _dsa_core(q, qi, wi, k, v, kidx, pos0, topk, *, tq, tk=512):
    B, T, qd = q.shape
    S = k.shape[1]
    hd = qd // A_HEADS
    gw = qd // A_KV_HEADS
    Sp = _round_up(S, tk)
    qh, ql = _split_bf16(qi)
    qi3 = jnp.concatenate([qh, ql, qh, jnp.zeros_like(qh)], axis=-1).reshape(B, T, IDX_HEADS * IDX_PACK)
    kh, kl = _split_bf16(kidx)
    kidx3 = jnp.concatenate([kh, kh, kl, jnp.zeros_like(kh)], axis=-1)
    pad = ((0, 0), (0, Sp - S), (0, 0))
    kidx3 = jnp.pad(kidx3, pad)
    kb = jnp.pad(k.astype(jnp.bfloat16), pad)
    vb = jnp.pad(v.astype(jnp.bfloat16), pad)
    rows = (gw // hd) * tq
    return pl.pallas_call(
        functools.partial(_dsa_kernel, pos0=pos0, topk=topk, tk=tk, scale=hd ** -0.5),
        grid=(B, T // tq, A_KV_HEADS),
        in_specs=[pl.BlockSpec((1, tq, IDX_HEADS * IDX_PACK), lambda b, i, g: (b, i, 0)),
                  pl.BlockSpec((1, tq, IDX_HEADS), lambda b, i, g: (b, i, 0)),
                  pl.BlockSpec((1, Sp, IDX_PACK), lambda b, i, g: (b, 0, 0)),
                  pl.BlockSpec((1, tq, gw), lambda b, i, g: (b, i, g)),
                  pl.BlockSpec((1, Sp, hd), lambda b, i, g: (b, 0, g)),
                  pl.BlockSpec((1, Sp, hd), lambda b, i, g: (b, 0, g))],
        out_specs=pl.BlockSpec((1, tq, gw), lambda b, i, g: (b, i, g)),
        out_shape=jax.ShapeDtypeStruct((B, T, qd), jnp.bfloat16),
        scratch_shapes=[pltpu.VMEM((tq, Sp), jnp.int32),
                        pltpu.VMEM((tq, Sp), jnp.float32),
                        pltpu.VMEM((rows, 1), jnp.float32),
                        pltpu.VMEM((rows, 1), jnp.float32),
                        pltpu.VMEM((rows, hd), jnp.float32)],
        compiler_params=pltpu.CompilerParams(
            dimension_semantics=("parallel", "arbitrary", "arbitrary"),
            vmem_limit_bytes=V7X_VMEM_LIMIT_BYTES),
        name="dsa_core",
    )(qi3, wi, kidx3, q.astype(jnp.bfloat16), kb, vb)


def _mixer_dsa(h, pos0, past_k, past_v, past_kidx, w_in, w_out):
    B, T, D = h.shape
    hd = D // A_HEADS
    q_dim, kv_dim = A_HEADS * hd, A_KV_HEADS * hd
    splits = (q_dim, q_dim + kv_dim, q_dim + 2 * kv_dim, q_dim + 2 * kv_dim + IDX_HEADS * IDX_DIM,
              q_dim + 2 * kv_dim + IDX_HEADS * IDX_DIM + IDX_DIM)
    pos = pos0 + jnp.arange(T, dtype=jnp.int32)
    q, k, v, qi, ki, wi = jnp.split(_mm(h, w_in), splits, axis=-1)
    q = _rope_partial(q.reshape(B, T, A_HEADS, hd), pos)
    k = _rope_partial(k.reshape(B, T, A_KV_HEADS, hd), pos)
    v = v.reshape(B, T, A_KV_HEADS, hd)
    qi = _rope_partial(qi.reshape(B, T, IDX_HEADS, IDX_DIM), pos)
    ki = _rope_partial(ki[:, :, None, :], pos)[:, :, 0, :]
    wi = wi * (IDX_HEADS ** -0.5 * IDX_DIM ** -0.5)
    if past_k is None:
        k_all, v_all, ki_all = k, v, ki
    else:
        k_all = jnp.concatenate([past_k, k], axis=1)
        v_all = jnp.concatenate([past_v, v], axis=1)
        ki_all = jnp.concatenate([past_kidx, ki], axis=1)
    n_keys = k_all.shape[1]
    topk = min(TOPK_MAX, n_keys // 4)
    o = _dsa_core(q.reshape(B, T, q_dim), qi, wi, k_all.reshape(B, n_keys, kv_dim),
                  v_all.reshape(B, n_keys, kv_dim), ki_all, pos0, topk, tq=min(Q_BLOCK, T))
    return _mm(o.astype(jnp.float32), w_out), k, v, ki


def _gated_delta_chunked(q, k, v, g, beta, S0):
    B, T, H, dk = k.shape
    dv = v.shape[-1]
    C = min(CHUNK, T)
    N = T // C

    def heads_first(a):
        a = a.reshape(B, N, C, H, *a.shape[3:])
        return jnp.moveaxis(a, 3, 1)

    q, k, v, g, beta = [heads_first(a) for a in (q, k, v, g, beta)]
    G = jnp.cumsum(g, axis=-1)
    causal = jnp.tril(jnp.ones((C, C), bool))
    strict = jnp.tril(jnp.ones((C, C), bool), -1)
    decay = jnp.exp(jnp.where(causal, G[..., :, None] - G[..., None, :], -jnp.inf))
    kb = k * beta[..., None]
    A = jnp.where(strict, jnp.einsum('bhncd,bhnjd->bhncj', kb, k) * decay, 0.0)
    rhs = jnp.concatenate([v * beta[..., None], kb * jnp.exp(G)[..., None]], axis=-1)
    W = lax.linalg.triangular_solve(A + jnp.eye(C, dtype=jnp.float32), rhs, left_side=True, lower=True,
                                    unit_diagonal=True)
    W1, W2 = W[..., :dv], W[..., dv:]
    qk = jnp.einsum('bhncd,bhnjd->bhncj', q, k) * decay
    q_dec = q * jnp.exp(G)[..., None]
    k_end = k * jnp.exp(G[..., -1:] - G)[..., None]
    g_end = jnp.exp(G[..., -1])
    xs = [jnp.moveaxis(a, 2, 0) for a in (W1, W2, qk, q_dec, k_end, g_end)]

    def step(S, inp):
        w1, w2, qk_n, qd, ke, ge = inp
        U = w1 - jnp.einsum('bhcd,bhde->bhce', w2, S)
        o = jnp.einsum('bhcd,bhde->bhce', qd, S) + jnp.einsum('bhcj,bhje->bhce', qk_n, U)
        S = S * ge[..., None, None] + jnp.einsum('bhcd,bhce->bhde', ke, U)
        return S, o

    S, o = lax.scan(step, S0, xs)
    o = jnp.moveaxis(o, 0, 2).reshape(B, H, T, dv)
    return jnp.moveaxis(o, 1, 2), S


def _mixer_gdn(h, conv_state, ssm_state, w_in, conv_w, a_log, dt_bias, norm_w, w_out):
    B, T, _ = h.shape
    qk_dim, v_dim = GDN_QK_HEADS * GDN_DK, GDN_V_HEADS * GDN_DV
    conv_dim = 2 * qk_dim + v_dim
    splits = (conv_dim, conv_dim + v_dim, conv_dim + v_dim + GDN_V_HEADS)
    qkv, z, beta_raw, a_raw = jnp.split(_mm(h, w_in), splits, axis=-1)
    if conv_state is None:
        conv_state = jnp.zeros((B, CONV_W - 1, conv_dim), h.dtype)
    if ssm_state is None:
        ssm_state = jnp.zeros((B, GDN_V_HEADS, GDN_DK, GDN_DV), jnp.float32)
    xe = jnp.concatenate([conv_state, qkv], axis=1)
    conv = xe[:, 0:T] * conv_w[0]
    for j in range(1, CONV_W):
        conv = conv + xe[:, j:j + T] * conv_w[j]
    conv = jax.nn.silu(conv)
    new_conv = xe[:, -(CONV_W - 1):]
    q, k, v = jnp.split(conv, (qk_dim, 2 * qk_dim), axis=-1)
    rep = GDN_V_HEADS // GDN_QK_HEADS
    q = q.reshape(B, T, GDN_QK_HEADS, GDN_DK)
    k = k.reshape(B, T, GDN_QK_HEADS, GDN_DK)
    q = q * lax.rsqrt(jnp.sum(q * q, axis=-1, keepdims=True) + EPS) * GDN_DK ** -0.5
    k = k * lax.rsqrt(jnp.sum(k * k, axis=-1, keepdims=True) + EPS)
    q = jnp.repeat(q, rep, axis=2)
    k = jnp.repeat(k, rep, axis=2)
    v = v.reshape(B, T, GDN_V_HEADS, GDN_DV)
    beta = jax.nn.sigmoid(beta_raw)
    g = -jnp.exp(a_log) * jax.nn.softplus(a_raw + dt_bias)
    o, S = _gated_delta_chunked(q, k, v, g, beta, ssm_state)
    zf = z.reshape(B, T, GDN_V_HEADS, GDN_DV)
    o = _rmsnorm(o, norm_w) * jax.nn.silu(zf)
    out = _mm(o.reshape(B, T, v_dim), w_out)
    return out, new_conv, S


NEG_INF = float("-inf")
POS_INF = float("inf")
_PEER_CAND_ROWS = tuple((i, PEER_TOPK // (i + 1)) for i in range(PEER_TOPK // 2))


def _split_bf16(x):
    hi = x.astype(jnp.bfloat16)
    lo = (x - hi.astype(jnp.float32)).astype(jnp.bfloat16)
    return hi, lo


def _dot3_nt(a, b):
    dn = (((1,), (1,)), ((), ()))
    ah, al = _split_bf16(a)
    bh, bl = _split_bf16(b)
    out = lax.dot_general(ah, bh, dn, preferred_element_type=jnp.float32)
    out = out + lax.dot_general(ah, bl, dn, preferred_element_type=jnp.float32)
    return out + lax.dot_general(al, bh, dn, preferred_element_type=jnp.float32)


def _top_rows_desc(s, n):
    rows = []
    cur = s
    for _ in range(n):
        m = jnp.max(cur, axis=0, keepdims=True)
        rows.append(m)
        cur = jnp.where(cur == m, NEG_INF, cur)
    return rows


def _stack_rows(rows, lanes):
    n = len(rows)
    rid = lax.broadcasted_iota(jnp.int32, (n, lanes), 0)
    out = jnp.zeros((n, lanes), jnp.float32)
    for i, r in enumerate(rows):
        out = jnp.where(rid == i, r, out)
    return out


def _peer_select_kernel(q_ref, keys_ref, s2_ref, e2_ref, ca_ref, e1_ref):
    tm = q_ref.shape[0]
    k = PEER_TOPK
    s1 = _dot3_nt(keys_ref[0], q_ref[:, :PEER_HALF])
    s2 = _dot3_nt(keys_ref[1], q_ref[:, PEER_HALF:])
    r1 = _top_rows_desc(s1, k)
    r2 = _top_rows_desc(s2, k)
    v1 = _stack_rows(r1, tm)
    v2 = _stack_rows(r2, tm)
    v2h = v2[:k // 2]
    rid = lax.broadcasted_iota(jnp.int32, (k // 2, tm), 0)
    pieces = [r1[0] + v2]
    for i, n in _PEER_CAND_ROWS[1:]:
        pieces.append(jnp.where(rid < n, r1[i] + v2h, NEG_INF))
    pieces.append(v1[k // 2:] + r2[0])
    cand = jnp.concatenate(pieces, axis=0)
    tau = _top_rows_desc(cand, k)[-1]
    top = r1[0] + r2[0]
    z = jnp.sum(jnp.where(cand >= tau, jnp.exp(cand - top), 0.0), axis=0, keepdims=True)
    cut = jnp.full(s1.shape, POS_INF, jnp.float32)
    for j in range(k):
        cut = jnp.where(s1 + r2[j] >= tau, r2[j], cut)
    s2_ref[0] = s2
    e2_ref[0] = jnp.exp(s2 - r2[0]) / z
    ca_ref[0] = cut
    e1_ref[0] = jnp.exp(s1 - r1[0])


def _peer_select(q, keys, *, tm):
    M = q.shape[0]
    nk = keys.shape[1]
    shp = jax.ShapeDtypeStruct((PEER_HEADS, nk, M), jnp.float32)
    ospec = pl.BlockSpec((1, nk, tm), lambda i, h: (h, 0, i))
    return pl.pallas_call(
        _peer_select_kernel,
        grid=(M // tm, PEER_HEADS),
        in_specs=[pl.BlockSpec((tm, PEER_KEY_DIM), lambda i, h: (i, h)),
                  pl.BlockSpec(keys.shape, lambda i, h: (0, 0, 0))],
        out_specs=[ospec] * 4,
        out_shape=[shp] * 4,
        compiler_params=pltpu.CompilerParams(
            dimension_semantics=("parallel", "arbitrary"),
            vmem_limit_bytes=V7X_VMEM_LIMIT_BYTES),
        name="peer_select",
    )(q, keys)


_PEER_ROWS_PER_STEP = 16


def _peer_main_kernel(hT_ref, u_ref, vT_ref, s2_ref, e2_ref, ca_ref, e1_ref, o_ref,
                      act_ref, coef_ref, acc_ref):
    e = pl.program_id(1)
    te = u_ref.shape[0]
    nk = s2_ref.shape[1]
    rows = _PEER_ROWS_PER_STEP
    act_ref[...] = jnp.dot(u_ref[...], hT_ref[...], preferred_element_type=jnp.float32)

    for al in range(te // nk):
        a = e * (te // nk) + al

        def row_block(r, carry, a=a, al=al):
            r0 = pl.multiple_of(r * rows, rows)
            w = None
            for h in range(PEER_HEADS):
                s2 = s2_ref[h, pl.ds(r0, rows), :]
                e2 = e2_ref[h, pl.ds(r0, rows), :]
                cut = ca_ref[h, pl.ds(a, 1), :]
                e1 = e1_ref[h, pl.ds(a, 1), :]
                t = jnp.where(s2 >= cut, e2, 0.0) * e1
                w = t if w is None else w + t
            x = act_ref[pl.ds(al * nk + r0, rows), :]
            g = 0.5 * x * (1.0 + lax.erf(x * (2.0 ** -0.5)))
            coef_ref[pl.ds(al * nk + r0, rows), :] = (w * g).astype(jnp.bfloat16)
            return carry

        lax.fori_loop(0, nk // rows, row_block, 0)

    @pl.when(e == 0)
    def _():
        acc_ref[...] = jnp.zeros_like(acc_ref)

    acc_ref[...] += jnp.dot(vT_ref[...], coef_ref[...], preferred_element_type=jnp.float32)

    @pl.when(e == pl.num_programs(1) - 1)
    def _():
        o_ref[...] = acc_ref[...].T


def _peer_main(hT, u, vT, s2, e2, ca, e1, *, tm, te):
    D, M = hT.shape
    E = u.shape[0]
    nk = s2.shape[1]
    sel_spec = pl.BlockSpec((PEER_HEADS, nk, tm), lambda i, e: (0, 0, i))
    return pl.pallas_call(
        _peer_main_kernel,
        grid=(M // tm, E // te),
        in_specs=[pl.BlockSpec((D, tm), lambda i, e: (0, i)),
                  pl.BlockSpec((te, D), lambda i, e: (e, 0)),
                  pl.BlockSpec((D, te), lambda i, e: (0, e)),
                  sel_spec, sel_spec, sel_spec, sel_spec],
        out_specs=pl.BlockSpec((tm, D), lambda i, e: (i, 0)),
        out_shape=jax.ShapeDtypeStruct((M, D), jnp.float32),
        scratch_shapes=[pltpu.VMEM((te, tm), jnp.float32),
                        pltpu.VMEM((te, tm), jnp.bfloat16),
                        pltpu.VMEM((D, tm), jnp.float32)],
        compiler_params=pltpu.CompilerParams(
            dimension_semantics=("parallel", "arbitrary"),
            vmem_limit_bytes=V7X_VMEM_LIMIT_BYTES),
        name="peer_main",
    )(hT, u, vT, s2, e2, ca, e1)


def _peer(h, w_query, sub_keys, expert_u, expert_v, *, tm_sel=256, tm=512, te=512):
    q = pmatmul(h, w_query, passes=3)
    s2, e2, ca, e1 = _peer_select(q, sub_keys, tm=min(tm_sel, h.shape[0]))
    hT = h.T.astype(jnp.bfloat16)
    u = expert_u.astype(jnp.bfloat16)
    vT = expert_v.T.astype(jnp.bfloat16)
    return _peer_main(hT, u, vT, s2, e2, ca, e1, tm=min(tm, h.shape[0]), te=te)


def kernel(x_prompt, x_sample, c_prompt, c_sample, cache_k_l0, cache_v_l0, cache_kidx_l0, state_conv_l1, state_ssm_l1, norm1_l0, norm2_l0, ada_w_l0, ada_b_l0, attn_in_l0, attn_out_l0, peer_query_l0, peer_keys_l0, peer_u_l0, peer_v_l0, norm1_l1, norm2_l1, ada_w_l1, ada_b_l1, gdn_in_l1, gdn_conv_l1, gdn_a_log_l1, gdn_dt_bias_l1, gdn_norm_l1, gdn_out_l1, peer_query_l1, peer_keys_l1, peer_u_l1, peer_v_l1, final_norm):
    past_len = cache_k_l0.shape[1]
    norm1 = (norm1_l0, norm1_l1)
    norm2 = (norm2_l0, norm2_l1)
    ada_w = (ada_w_l0, ada_w_l1)
    ada_b = (ada_b_l0, ada_b_l1)
    peer_query = (peer_query_l0, peer_query_l1)
    peer_keys = (peer_keys_l0, peer_keys_l1)
    peer_u = (peer_u_l0, peer_u_l1)
    peer_v = (peer_v_l0, peer_v_l1)
    xp, xs = x_prompt, x_sample
    for i in range(2):
        mp = _adaln(c_prompt, ada_w[i], ada_b[i])
        ms = _adaln(c_sample, ada_w[i], ada_b[i])
        hp = _modulate(xp, norm1[i], mp[0], mp[1])
        hs = _modulate(xs, norm1[i], ms[0], ms[1])
        if i == 0:
            op, nkp, nvp, nkip = _mixer_dsa(hp, 0, None, None, None, attn_in_l0, attn_out_l0)
            os_, nks, nvs, nkis = _mixer_dsa(hs, past_len, cache_k_l0, cache_v_l0, cache_kidx_l0, attn_in_l0, attn_out_l0)
        else:
            op, ncp, nsp = _mixer_gdn(hp, None, None, gdn_in_l1, gdn_conv_l1, gdn_a_log_l1, gdn_dt_bias_l1, gdn_norm_l1, gdn_out_l1)
            os_, ncs, nss = _mixer_gdn(hs, state_conv_l1, state_ssm_l1, gdn_in_l1, gdn_conv_l1, gdn_a_log_l1, gdn_dt_bias_l1, gdn_norm_l1, gdn_out_l1)
        xp = xp + mp[2] * op
        xs = xs + ms[2] * os_
        hp = _modulate(xp, norm2[i], mp[3], mp[4])
        hs = _modulate(xs, norm2[i], ms[3], ms[4])
        D = xp.shape[-1]
        n_p = xp.shape[0] * xp.shape[1]
        h_all = jnp.concatenate([hp.reshape(-1, D), hs.reshape(-1, D)], axis=0)
        po = _peer(h_all, peer_query[i], peer_keys[i], peer_u[i], peer_v[i])
        xp = xp + mp[5] * po[:n_p].reshape(xp.shape)
        xs = xs + ms[5] * po[n_p:].reshape(xs.shape)
    y_prompt = _rmsnorm(xp, final_norm)
    y_sample = _rmsnorm(xs, final_norm)
    return (y_prompt, y_sample, nkp, nvp, nkip, nks, nvs, nkis, ncp, nsp, ncs, nss)
```

```python
import functools

import jax
import jax.numpy as jnp
from jax import lax
from jax.experimental import pallas as pl
from jax.experimental.pallas import tpu as pltpu

CHUNK = 64
EPS = 1e-6
ROPE_THETA = 500000.0
ROPE_FRACTION = 4
A_HEADS = 16
A_KV_HEADS = 4
IDX_HEADS = 8
IDX_DIM = 64
TOPK_MAX = 256
Q_BLOCK = 128
GDN_QK_HEADS = 16
GDN_V_HEADS = 32
GDN_DK = 128
GDN_DV = 128
CONV_W = 4
PEER_HEADS = 8
PEER_NKEYS = 128
PEER_KEY_DIM = 256
PEER_HALF = PEER_KEY_DIM // 2
PEER_TOPK = 16
PEER_BLOCK = 64

V7X_VMEM_LIMIT_BYTES = 48 * 1024 * 1024
LANE = 128
SUBLANE = 8


def _round_up(n, m):
    return (n + m - 1) // m * m


def _matmul_kernel(x_ref, w_ref, o_ref, *, passes):
    x = x_ref[...]
    w = w_ref[...]
    xh = x.astype(jnp.bfloat16)
    wh = w.astype(jnp.bfloat16)
    acc = jnp.dot(xh, wh, preferred_element_type=jnp.float32)
    if passes == 3:
        xl = (x - xh.astype(jnp.float32)).astype(jnp.bfloat16)
        wl = (w - wh.astype(jnp.float32)).astype(jnp.bfloat16)
        acc = acc + jnp.dot(xh, wl, preferred_element_type=jnp.float32)
        acc = acc + jnp.dot(xl, wh, preferred_element_type=jnp.float32)
    o_ref[...] = acc


def pmatmul(x, w, *, passes=1, tm=512, tn=512):
    M, K = x.shape
    N = w.shape[1]
    tm = min(tm, _round_up(M, SUBLANE))
    tn = min(tn, _round_up(N, LANE))
    Mp, Np = _round_up(M, tm), _round_up(N, tn)
    if Mp != M:
        x = jnp.pad(x, ((0, Mp - M), (0, 0)))
    if Np != N:
        w = jnp.pad(w, ((0, 0), (0, Np - N)))
    out = pl.pallas_call(
        functools.partial(_matmul_kernel, passes=passes),
        grid=(Mp // tm, Np // tn),
        in_specs=[pl.BlockSpec((tm, K), lambda i, j: (i, 0)),
                  pl.BlockSpec((K, tn), lambda i, j: (0, j))],
        out_specs=pl.BlockSpec((tm, tn), lambda i, j: (i, j)),
        out_shape=jax.ShapeDtypeStruct((Mp, Np), jnp.float32),
        compiler_params=pltpu.CompilerParams(
            dimension_semantics=("parallel", "arbitrary"),
            vmem_limit_bytes=V7X_VMEM_LIMIT_BYTES),
    )(x, w)
    return out[:M, :N]


def _mm(x, w, **kw):
    lead = x.shape[:-1]
    return pmatmul(x.reshape(-1, x.shape[-1]), w, **kw).reshape(*lead, w.shape[1])


def _rmsnorm(x, gain):
    y = x * lax.rsqrt(jnp.mean(x * x, axis=-1, keepdims=True) + EPS)
    return y * gain


def _adaln(c, w, b):
    mod = _mm(jax.nn.silu(c), w, passes=3) + b
    return [m[:, None, :] for m in jnp.split(mod, 6, axis=-1)]


def _modulate(x, gain, shift, scale):
    return _rmsnorm(x, gain) * (1.0 + scale) + shift


def _rope_partial(x, pos):
    rot = x.shape[-1] // ROPE_FRACTION
    half = rot // 2
    inv_freq = ROPE_THETA ** (-jnp.arange(half, dtype=jnp.float32) / half)
    ang = pos.astype(jnp.float32)[:, None] * inv_freq[None, :]
    cos = jnp.cos(ang)[:, None, :]
    sin = jnp.sin(ang)[:, None, :]
    x1, x2, rest = x[..., :half], x[..., half:rot], x[..., rot:]
    return jnp.concatenate([x1 * cos - x2 * sin, x2 * cos + x1 * sin, rest], axis=-1)


_NT_DIMS = (((1,), (1,)), ((), ()))
INT32_MIN = -2 ** 31
_NEG_INF_KEY = -2139095041
IDX_PACK = 4 * IDX_DIM


def _ordered_key(x):
    bits = pltpu.bitcast(x, jnp.int32)
    return bits ^ ((bits >> 31) & 0x7FFFFFFF)


def _lane_tile_sum(x, width=LANE):
    out = x[:, :width]
    for c in range(1, x.shape[1] // width):
        out = out + x[:, c * width:(c + 1) * width]
    return out


def _dsa_kernel(qi_ref, wi_ref, kidx_ref, q_ref, k_ref, v_ref, o_ref,
                key_ref, bias_ref, m_ref, l_ref, acc_ref, *, pos0, topk, tk, scale):
    i = pl.program_id(1)
    g = pl.program_id(2)
    tq = q_ref.shape[1]
    hd = k_ref.shape[2]
    rep = q_ref.shape[2] // hd
    first = pos0 + i * tq
    n_valid = (((first + tq - 1) >> 6) + 1) * CHUNK
    n_tiles = (n_valid + tk - 1) // tk

    @pl.when(g == 0)
    def _():
        row = lax.broadcasted_iota(jnp.int32, (tq, 1), 0)
        lim = (((first + row) >> 6) + 1) * CHUNK
        w = wi_ref[0]

        def score_tile(j, c):
            off = pl.multiple_of(j * tk, tk)
            kt = kidx_ref[0, pl.ds(off, tk), :]
            s = jnp.zeros((tq, tk), jnp.float32)
            for h in range(IDX_HEADS):
                sc = lax.dot_general(qi_ref[0, :, h * IDX_PACK:(h + 1) * IDX_PACK], kt, _NT_DIMS,
                                     preferred_element_type=jnp.float32)
                s = s + w[:, h:h + 1] * jnp.maximum(sc, 0.0)
            col = off + lax.broadcasted_iota(jnp.int32, (tq, tk), 1)
            s = jnp.where(col < lim, s + 0.0, NEG_INF)
            key_ref[:, pl.ds(off, tk)] = _ordered_key(s)
            return c

        lax.fori_loop(0, n_tiles, score_tile, 0)

        def bit_step(b, thr):
            cand = thr + lax.shift_left(jnp.int32(1), 31 - b)

            def count_tile(j, c):
                off = pl.multiple_of(j * tk, tk)
                ge = jnp.where(key_ref[:, pl.ds(off, tk)] >= cand, 1.0, 0.0)
                return c + _lane_tile_sum(ge)

            c = lax.fori_loop(0, n_tiles, count_tile, jnp.zeros((tq, LANE), jnp.float32))
            cnt = jnp.sum(c, axis=1, keepdims=True)
            return jnp.where(cnt >= topk, cand, thr)

        thr = lax.fori_loop(0, 32, bit_step, jnp.full((tq, 1), INT32_MIN, jnp.int32))
        thr = jnp.maximum(thr, _NEG_INF_KEY + 1)

        def bias_tile(j, c):
            off = pl.multiple_of(j * tk, tk)
            bias_ref[:, pl.ds(off, tk)] = jnp.where(key_ref[:, pl.ds(off, tk)] >= thr, 0.0, NEG_INF)
            return c

        lax.fori_loop(0, n_tiles, bias_tile, 0)

    qg = jnp.concatenate([q_ref[0, :, r * hd:(r + 1) * hd] for r in range(rep)], axis=0)
    m_ref[...] = jnp.full(m_ref.shape, NEG_INF, jnp.float32)
    l_ref[...] = jnp.zeros(l_ref.shape, jnp.float32)
    acc_ref[...] = jnp.zeros(acc_ref.shape, jnp.float32)

    def att_tile(j, c):
        off = pl.multiple_of(j * tk, tk)
        kt = k_ref[0, pl.ds(off, tk), :]
        vt = v_ref[0, pl.ds(off, tk), :]
        lg = lax.dot_general(qg, kt, _NT_DIMS, preferred_element_type=jnp.float32) * scale
        b = bias_ref[:, pl.ds(off, tk)]
        lg = lg + jnp.concatenate([b] * rep, axis=0)
        m_old = m_ref[...]
        m_new = jnp.maximum(m_old, jnp.max(lg, axis=1, keepdims=True))
        m_safe = jnp.where(m_new == NEG_INF, 0.0, m_new)
        p = jnp.exp(lg - m_safe)
        alpha = jnp.exp(m_old - m_safe)
        l_ref[...] = alpha * l_ref[...] + jnp.sum(p, axis=1, keepdims=True)
        acc_ref[...] = alpha * acc_ref[...] + jnp.dot(p.astype(jnp.bfloat16), vt,
                                                      preferred_element_type=jnp.float32)
        m_ref[...] = m_new
        return c

    lax.fori_loop(0, n_tiles, att_tile, 0)
    out = acc_ref[...] / l_ref[...]
    for r in range(rep):
        o_ref[0, :, r * hd:(r + 1) * hd] = out[r * tq:(r + 1) * tq].astype(o_ref.dtype)


def _dsa_core(q, qi, wi, k, v, kidx, pos0, topk, *, tq, tk=512):
    B, T, qd = q.shape
    S = k.shape[1]
    hd = qd // A_HEADS
    gw = qd // A_KV_HEADS
    Sp = _round_up(S, tk)
    qh, ql = _split_bf16(qi)
    qi3 = jnp.concatenate([qh, ql, qh, jnp.zeros_like(qh)], axis=-1).reshape(B, T, IDX_HEADS * IDX_PACK)
    kh, kl = _split_bf16(kidx)
    kidx3 = jnp.concatenate([kh, kh, kl, jnp.zeros_like(kh)], axis=-1)
    pad = ((0, 0), (0, Sp - S), (0, 0))
    kidx3 = jnp.pad(kidx3, pad)
    kb = jnp.pad(k.astype(jnp.bfloat16), pad)
    vb = jnp.pad(v.astype(jnp.bfloat16), pad)
    rows = (gw // hd) * tq
    return pl.pallas_call(
        functools.partial(_dsa_kernel, pos0=pos0, topk=topk, tk=tk, scale=hd ** -0.5),
        grid=(B, T // tq, A_KV_HEADS),
        in_specs=[pl.BlockSpec((1, tq, IDX_HEADS * IDX_PACK), lambda b, i, g: (b, i, 0)),
                  pl.BlockSpec((1, tq, IDX_HEADS), lambda b, i, g: (b, i, 0)),
                  pl.BlockSpec((1, Sp, IDX_PACK), lambda b, i, g: (b, 0, 0)),
                  pl.BlockSpec((1, tq, gw), lambda b, i, g: (b, i, g)),
                  pl.BlockSpec((1, Sp, hd), lambda b, i, g: (b, 0, g)),
                  pl.BlockSpec((1, Sp, hd), lambda b, i, g: (b, 0, g))],
        out_specs=pl.BlockSpec((1, tq, gw), lambda b, i, g: (b, i, g)),
        out_shape=jax.ShapeDtypeStruct((B, T, qd), jnp.bfloat16),
        scratch_shapes=[pltpu.VMEM((tq, Sp), jnp.int32),
                        pltpu.VMEM((tq, Sp), jnp.float32),
                        pltpu.VMEM((rows, 1), jnp.float32),
                        pltpu.VMEM((rows, 1), jnp.float32),
                        pltpu.VMEM((rows, hd), jnp.float32)],
        compiler_params=pltpu.CompilerParams(
            dimension_semantics=("parallel", "arbitrary", "arbitrary"),
            vmem_limit_bytes=V7X_VMEM_LIMIT_BYTES),
        name="dsa_core",
    )(qi3, wi, kidx3, q.astype(jnp.bfloat16), kb, vb)


def _mixer_dsa(h, pos0, past_k, past_v, past_kidx, w_in, w_out):
    B, T, D = h.shape
    hd = D // A_HEADS
    q_dim, kv_dim = A_HEADS * hd, A_KV_HEADS * hd
    splits = (q_dim, q_dim + kv_dim, q_dim + 2 * kv_dim, q_dim + 2 * kv_dim + IDX_HEADS * IDX_DIM,
              q_dim + 2 * kv_dim + IDX_HEADS * IDX_DIM + IDX_DIM)
    pos = pos0 + jnp.arange(T, dtype=jnp.int32)
    q, k, v, qi, ki, wi = jnp.split(_mm(h, w_in), splits, axis=-1)
    q = _rope_partial(q.reshape(B, T, A_HEADS, hd), pos)
    k = _rope_partial(k.reshape(B, T, A_KV_HEADS, hd), pos)
    v = v.reshape(B, T, A_KV_HEADS, hd)
    qi = _rope_partial(qi.reshape(B, T, IDX_HEADS, IDX_DIM), pos)
    ki = _rope_partial(ki[:, :, None, :], pos)[:, :, 0, :]
    wi = wi * (IDX_HEADS ** -0.5 * IDX_DIM ** -0.5)
    if past_k is None:
        k_all, v_all, ki_all = k, v, ki
    else:
        k_all = jnp.concatenate([past_k, k], axis=1)
        v_all = jnp.concatenate([past_v, v], axis=1)
        ki_all = jnp.concatenate([past_kidx, ki], axis=1)
    n_keys = k_all.shape[1]
    topk = min(TOPK_MAX, n_keys // 4)
    o = _dsa_core(q.reshape(B, T, q_dim), qi, wi, k_all.reshape(B, n_keys, kv_dim),
                  v_all.reshape(B, n_keys, kv_dim), ki_all, pos0, topk, tq=min(Q_BLOCK, T))
    return _mm(o.astype(jnp.float32), w_out), k, v, ki


GDN_HEAD_GROUP = 8


def _bf16_dot(a, b):
    return jnp.dot(a.astype(jnp.bfloat16), b.astype(jnp.bfloat16), preferred_element_type=jnp.float32)


def _dot3(a, b):
    ah, al = _split_bf16(a)
    bh, bl = _split_bf16(b)
    out = jnp.dot(ah, bh, preferred_element_type=jnp.float32)
    out = out + jnp.dot(ah, bl, preferred_element_type=jnp.float32)
    return out + jnp.dot(al, bh, preferred_element_type=jnp.float32)


def _gdn_kernel(q_ref, k_ref, v_ref, z_ref, g_ref, gt_ref, beta_ref, nw_ref, s0_ref,
                o_ref, s_out_ref, s_ref):
    n = pl.program_id(2)
    C = q_ref.shape[1]
    hg = g_ref.shape[3]
    dk = s_ref.shape[1]
    dv = s_ref.shape[2]
    rep = hg // (k_ref.shape[2] // dk)

    @pl.when(n == 0)
    def _():
        s_ref[...] = s0_ref[0]

    ri = lax.broadcasted_iota(jnp.int32, (C, C), 0)
    ci = lax.broadcasted_iota(jnp.int32, (C, C), 1)
    causal = ri >= ci
    strict = ri > ci
    eye = jnp.where(ri == ci, 1.0, 0.0)
    g = g_ref[0, 0]
    gc_all = _dot3(jnp.where(causal, 1.0, 0.0), g)
    gr_all = _dot3(gt_ref[0, 0, 0], jnp.where(ri <= ci, 1.0, 0.0))
    beta = beta_ref[0, 0]
    nw = nw_ref[...]

    heads = range(hg)
    qs = [q_ref[0, :, (h // rep) * dk:(h // rep + 1) * dk] for h in heads]
    ks = [k_ref[0, :, (h // rep) * dk:(h // rep + 1) * dk] for h in heads]
    gcs = [gc_all[:, h:h + 1] for h in heads]
    bcols = [beta[:, h:h + 1] for h in heads]
    decays = [jnp.where(causal, jnp.exp(jnp.where(causal, gcs[h] - gr_all[h:h + 1, :], 0.0)), 0.0)
              for h in heads]
    kbs = [ks[h] * bcols[h] for h in heads]
    kks = [lax.dot_general(kbs[h].astype(jnp.bfloat16), ks[h].astype(jnp.bfloat16), _NT_DIMS,
                           preferred_element_type=jnp.float32) for h in heads]
    bms = [jnp.where(strict, -(kks[h] * decays[h]), 0.0) for h in heads]
    egs = [jnp.exp(gcs[h]) for h in heads]
    rhss = [jnp.concatenate([v_ref[0, :, h * dv:(h + 1) * dv] * bcols[h], kbs[h] * egs[h]], axis=1)
            for h in heads]
    ps = [eye + bms[h] for h in heads]
    step = 2
    while step < C:
        bms = [_dot3(bms[h], bms[h]) for h in heads]
        ps = [ps[h] + _dot3(ps[h], bms[h]) for h in heads]
        step *= 2
    ws = [_dot3(ps[h], rhss[h]) for h in heads]
    qks = [lax.dot_general(qs[h].astype(jnp.bfloat16), ks[h].astype(jnp.bfloat16), _NT_DIMS,
                           preferred_element_type=jnp.float32) * decays[h] for h in heads]
    g_lasts = [gcs[h][C - 1:C, :] for h in heads]
    ss = [s_ref[h] for h in heads]
    us = [ws[h][:, :dv] - _bf16_dot(ws[h][:, dv:], ss[h]) for h in heads]
    os_ = [_bf16_dot(qs[h] * egs[h], ss[h]) + _bf16_dot(qks[h], us[h]) for h in heads]
    for h in heads:
        ke = ks[h] * jnp.exp(g_lasts[h] - gcs[h])
        s_ref[h] = ss[h] * jnp.exp(g_lasts[h]) + _bf16_dot(ke.T, us[h])
    for h in heads:
        o = os_[h]
        o = o * lax.rsqrt(jnp.mean(o * o, axis=1, keepdims=True) + EPS) * nw
        z = z_ref[0, :, h * dv:(h + 1) * dv]
        o_ref[0, :, h * dv:(h + 1) * dv] = (o * (z * jax.nn.sigmoid(z))).astype(o_ref.dtype)

    @pl.when(n == pl.num_programs(2) - 1)
    def _():
        s_out_ref[0] = s_ref[...]


def _gdn_core(q, k, v, z, g, beta, norm_w, s0):
    B, T, vd = v.shape
    VH = g.shape[2]
    dv = vd // VH
    dk = s0.shape[2]
    qd = q.shape[2]
    C = min(CHUNK, T)
    N = T // C
    hg = GDN_HEAD_GROUP
    ng = VH // hg
    qw = qd // ng

    def grouped(a):
        return a.reshape(B, T, ng, hg).transpose(0, 2, 1, 3)

    gg, bg = grouped(g), grouped(beta)
    gt = gg.reshape(B, ng, N, C, hg).transpose(0, 1, 2, 4, 3)
    small = pl.BlockSpec((1, 1, C, hg), lambda b, j, n: (b, j, n, 0))
    o, s = pl.pallas_call(
        _gdn_kernel,
        grid=(B, ng, N),
        in_specs=[pl.BlockSpec((1, C, qw), lambda b, j, n: (b, n, j)),
                  pl.BlockSpec((1, C, qw), lambda b, j, n: (b, n, j)),
                  pl.BlockSpec((1, C, hg * dv), lambda b, j, n: (b, n, j)),
                  pl.BlockSpec((1, C, hg * dv), lambda b, j, n: (b, n, j)),
                  small,
                  pl.BlockSpec((1, 1, 1, hg, C), lambda b, j, n: (b, j, n, 0, 0)),
                  small,
                  pl.BlockSpec((1, dv), lambda b, j, n: (0, 0)),
                  pl.BlockSpec((1, hg, dk, dv), lambda b, j, n: (b, j, 0, 0))],
        out_specs=[pl.BlockSpec((1, C, hg * dv), lambda b, j, n: (b, n, j)),
                   pl.BlockSpec((1, hg, dk, dv), lambda b, j, n: (b, j, 0, 0))],
        out_shape=[jax.ShapeDtypeStruct((B, T, vd), jnp.bfloat16),
                   jax.ShapeDtypeStruct(s0.shape, jnp.float32)],
        scratch_shapes=[pltpu.VMEM((hg, dk, dv), jnp.float32)],
        compiler_params=pltpu.CompilerParams(
            dimension_semantics=("parallel", "parallel", "arbitrary"),
            vmem_limit_bytes=V7X_VMEM_LIMIT_BYTES),
        name="gdn_core",
    )(q, k, v, z, gg, gt, bg, norm_w.reshape(1, dv), s0)
    return o, s


def _mixer_gdn(h, conv_state, ssm_state, w_in, conv_w, a_log, dt_bias, norm_w, w_out):
    B, T, _ = h.shape
    qk_dim, v_dim = GDN_QK_HEADS * GDN_DK, GDN_V_HEADS * GDN_DV
    conv_dim = 2 * qk_dim + v_dim
    splits = (conv_dim, conv_dim + v_dim, conv_dim + v_dim + GDN_V_HEADS)
    qkv, z, beta_raw, a_raw = jnp.split(_mm(h, w_in), splits, axis=-1)
    if conv_state is None:
        conv_state = jnp.zeros((B, CONV_W - 1, conv_dim), h.dtype)
    if ssm_state is None:
        ssm_state = jnp.zeros((B, GDN_V_HEADS, GDN_DK, GDN_DV), jnp.float32)
    xe = jnp.concatenate([conv_state, qkv], axis=1)
    conv = xe[:, 0:T] * conv_w[0]
    for j in range(1, CONV_W):
        conv = conv + xe[:, j:j + T] * conv_w[j]
    conv = jax.nn.silu(conv)
    new_conv = xe[:, -(CONV_W - 1):]
    q, k, v = jnp.split(conv, (qk_dim, 2 * qk_dim), axis=-1)
    rep = GDN_V_HEADS // GDN_QK_HEADS
    q = q.reshape(B, T, GDN_QK_HEADS, GDN_DK)
    k = k.reshape(B, T, GDN_QK_HEADS, GDN_DK)
    q = q * lax.rsqrt(jnp.sum(q * q, axis=-1, keepdims=True) + EPS) * GDN_DK ** -0.5
    k = k * lax.rsqrt(jnp.sum(k * k, axis=-1, keepdims=True) + EPS)
    beta = jax.nn.sigmoid(beta_raw)
    g = -jnp.exp(a_log) * jax.nn.softplus(a_raw + dt_bias)
    o, S = _gdn_core(q.reshape(B, T, qk_dim), k.reshape(B, T, qk_dim), v, z, g, beta, norm_w, ssm_state)
    out = _mm(o.astype(jnp.float32), w_out)
    return out, new_conv, S


NEG_INF = float("-inf")
POS_INF = float("inf")
_PEER_CAND_ROWS = tuple((i, PEER_TOPK // (i + 1)) for i in range(PEER_TOPK // 2))


def _split_bf16(x):
    hi = x.astype(jnp.bfloat16)
    lo = (x - hi.astype(jnp.float32)).astype(jnp.bfloat16)
    return hi, lo


def _dot3_nt(a, b):
    dn = (((1,), (1,)), ((), ()))
    ah, al = _split_bf16(a)
    bh, bl = _split_bf16(b)
    out = lax.dot_general(ah, bh, dn, preferred_element_type=jnp.float32)
    out = out + lax.dot_general(ah, bl, dn, preferred_element_type=jnp.float32)
    return out + lax.dot_general(al, bh, dn, preferred_element_type=jnp.float32)


def _top_rows_desc(s, n, with_rank=False):
    rows = []
    cur = s
    rank = jnp.full(s.shape, float(n), jnp.float32)
    for i in range(n):
        m = jnp.max(cur, axis=0, keepdims=True)
        rows.append(m)
        hit = cur == m
        if with_rank:
            rank = jnp.where(hit, float(i), rank)
        cur = jnp.where(hit, NEG_INF, cur)
    return (rows, rank) if with_rank else rows


def _stack_rows(rows, lanes):
    n = len(rows)
    rid = lax.broadcasted_iota(jnp.int32, (n, lanes), 0)
    out = jnp.zeros((n, lanes), jnp.float32)
    for i, r in enumerate(rows):
        out = jnp.where(rid == i, r, out)
    return out


def _peer_select_kernel(q_ref, keys_ref, rk2_ref, e2_ref, cnt_ref, e1_ref):
    tm = q_ref.shape[0]
    k = PEER_TOPK
    s1 = _dot3_nt(keys_ref[0], q_ref[:, :PEER_HALF])
    s2 = _dot3_nt(keys_ref[1], q_ref[:, PEER_HALF:])
    r1 = _top_rows_desc(s1, k)
    r2, rank2 = _top_rows_desc(s2, k, with_rank=True)
    v1 = _stack_rows(r1, tm)
    v2 = _stack_rows(r2, tm)
    v2h = v2[:k // 2]
    rid = lax.broadcasted_iota(jnp.int32, (k // 2, tm), 0)
    pieces = [r1[0] + v2]
    for i, n in _PEER_CAND_ROWS[1:]:
        pieces.append(jnp.where(rid < n, r1[i] + v2h, NEG_INF))
    pieces.append(v1[k // 2:] + r2[0])
    cand = jnp.concatenate(pieces, axis=0)
    tau = _top_rows_desc(cand, k)[-1]
    top = r1[0] + r2[0]
    z = jnp.sum(jnp.where(cand >= tau, jnp.exp(cand - top), 0.0), axis=0, keepdims=True)
    cnt = jnp.zeros(s1.shape, jnp.float32)
    for j in range(k):
        cnt = cnt + jnp.where(s1 + r2[j] >= tau, 1.0, 0.0)
    rk2_ref[0] = rank2.astype(rk2_ref.dtype)
    e2_ref[0] = (jnp.exp(s2 - r2[0]) / z).astype(e2_ref.dtype)
    cnt_ref[0] = cnt
    e1_ref[0] = jnp.exp(s1 - r1[0])


def _peer_select(q, keys, *, tm):
    M = q.shape[0]
    nk = keys.shape[1]
    ospec = pl.BlockSpec((1, nk, tm), lambda i, h: (h, 0, i))
    return pl.pallas_call(
        _peer_select_kernel,
        grid=(M // tm, PEER_HEADS),
        in_specs=[pl.BlockSpec((tm, PEER_KEY_DIM), lambda i, h: (i, h)),
                  pl.BlockSpec(keys.shape, lambda i, h: (0, 0, 0))],
        out_specs=[ospec] * 4,
        out_shape=[jax.ShapeDtypeStruct((PEER_HEADS, nk, M), dt)
                   for dt in (jnp.bfloat16, jnp.bfloat16, jnp.float32, jnp.float32)],
        compiler_params=pltpu.CompilerParams(
            dimension_semantics=("parallel", "arbitrary"),
            vmem_limit_bytes=V7X_VMEM_LIMIT_BYTES),
        name="peer_select",
    )(q, keys)


BF16_SUBLANES = 16
PEER_SUB_EXPERTS = 256


def _peer_main_kernel(hT_ref, u_ref, vT_ref, rk2_ref, e2_ref, cnt_ref, e1_ref, o_ref,
                      coef_ref, acc_ref):
    e = pl.program_id(1)
    te, tm = coef_ref.shape
    nk = rk2_ref.shape[1]
    rows = BF16_SUBLANES
    sub = PEER_SUB_EXPERTS

    @pl.when(e == 0)
    def _():
        acc_ref[...] = jnp.zeros_like(acc_ref)

    for sb in range(te // sub):
        act = jnp.dot(u_ref[sb * sub:(sb + 1) * sub, :], hT_ref[...],
                      preferred_element_type=jnp.float32)
        for al in range(sub // nk):
            a = sb * (sub // nk) + al
            for lg in range(tm // LANE):
                lanes = slice(lg * LANE, (lg + 1) * LANE)
                cnts = [jnp.broadcast_to(cnt_ref[h, a:a + 1, lanes].astype(jnp.bfloat16), (rows, LANE))
                        for h in range(PEER_HEADS)]
                e1s = [jnp.broadcast_to(e1_ref[h, a:a + 1, lanes].astype(jnp.bfloat16), (rows, LANE))
                       for h in range(PEER_HEADS)]
                for r in range(nk // rows):
                    rs = slice(r * rows, (r + 1) * rows)
                    w = None
                    for h in range(PEER_HEADS):
                        e2 = e2_ref[h, rs, lanes]
                        t = jnp.where(rk2_ref[h, rs, lanes] < cnts[h], e2, jnp.zeros_like(e2)) * e1s[h]
                        w = t if w is None else w + t
                    x = act[al * nk + r * rows:al * nk + (r + 1) * rows, lanes]
                    g = 0.5 * x * (1.0 + lax.erf(x * (2.0 ** -0.5)))
                    coef_ref[sb * sub + al * nk + r * rows:sb * sub + al * nk + (r + 1) * rows, lanes] = (
                        w * g.astype(jnp.bfloat16))
        acc_ref[...] += jnp.dot(vT_ref[:, sb * sub:(sb + 1) * sub], coef_ref[sb * sub:(sb + 1) * sub, :],
                                preferred_element_type=jnp.float32)

    @pl.when(e == pl.num_programs(1) - 1)
    def _():
        o_ref[...] = acc_ref[...].T


def _peer_main(hT, u, vT, s2, e2, ca, e1, *, tm, te):
    D, M = hT.shape
    E = u.shape[0]
    nk = s2.shape[1]
    col_spec = pl.BlockSpec((PEER_HEADS, nk, tm), lambda i, e: (0, 0, i))
    row_spec = pl.BlockSpec((PEER_HEADS, te // nk, tm), lambda i, e: (0, e, i))
    return pl.pallas_call(
        _peer_main_kernel,
        grid=(M // tm, E // te),
        in_specs=[pl.BlockSpec((D, tm), lambda i, e: (0, i)),
                  pl.BlockSpec((te, D), lambda i, e: (e, 0)),
                  pl.BlockSpec((D, te), lambda i, e: (0, e)),
                  col_spec, col_spec, row_spec, row_spec],
        out_specs=pl.BlockSpec((tm, D), lambda i, e: (i, 0)),
        out_shape=jax.ShapeDtypeStruct((M, D), jnp.float32),
        scratch_shapes=[pltpu.VMEM((te, tm), jnp.bfloat16),
                        pltpu.VMEM((D, tm), jnp.float32)],
        compiler_params=pltpu.CompilerParams(
            dimension_semantics=("parallel", "arbitrary"),
            vmem_limit_bytes=V7X_VMEM_LIMIT_BYTES),
        name="peer_main",
    )(hT, u, vT, s2, e2, ca, e1)


def _peer(h, w_query, sub_keys, expert_u, expert_v, *, tm_sel=256, tm=512, te=1024):
    q = pmatmul(h, w_query, passes=3)
    s2, e2, ca, e1 = _peer_select(q, sub_keys, tm=min(tm_sel, h.shape[0]))
    hT = h.T.astype(jnp.bfloat16)
    u = expert_u.astype(jnp.bfloat16)
    vT = expert_v.T.astype(jnp.bfloat16)
    return _peer_main(hT, u, vT, s2, e2, ca, e1, tm=min(tm, h.shape[0]), te=te)


def kernel(x_prompt, x_sample, c_prompt, c_sample, cache_k_l0, cache_v_l0, cache_kidx_l0, state_conv_l1, state_ssm_l1, norm1_l0, norm2_l0, ada_w_l0, ada_b_l0, attn_in_l0, attn_out_l0, peer_query_l0, peer_keys_l0, peer_u_l0, peer_v_l0, norm1_l1, norm2_l1, ada_w_l1, ada_b_l1, gdn_in_l1, gdn_conv_l1, gdn_a_log_l1, gdn_dt_bias_l1, gdn_norm_l1, gdn_out_l1, peer_query_l1, peer_keys_l1, peer_u_l1, peer_v_l1, final_norm):
    past_len = cache_k_l0.shape[1]
    norm1 = (norm1_l0, norm1_l1)
    norm2 = (norm2_l0, norm2_l1)
    ada_w = (ada_w_l0, ada_w_l1)
    ada_b = (ada_b_l0, ada_b_l1)
    peer_query = (peer_query_l0, peer_query_l1)
    peer_keys = (peer_keys_l0, peer_keys_l1)
    peer_u = (peer_u_l0, peer_u_l1)
    peer_v = (peer_v_l0, peer_v_l1)
    xp, xs = x_prompt, x_sample
    for i in range(2):
        mp = _adaln(c_prompt, ada_w[i], ada_b[i])
        ms = _adaln(c_sample, ada_w[i], ada_b[i])
        hp = _modulate(xp, norm1[i], mp[0], mp[1])
        hs = _modulate(xs, norm1[i], ms[0], ms[1])
        if i == 0:
            op, nkp, nvp, nkip = _mixer_dsa(hp, 0, None, None, None, attn_in_l0, attn_out_l0)
            os_, nks, nvs, nkis = _mixer_dsa(hs, past_len, cache_k_l0, cache_v_l0, cache_kidx_l0, attn_in_l0, attn_out_l0)
        else:
            op, ncp, nsp = _mixer_gdn(hp, None, None, gdn_in_l1, gdn_conv_l1, gdn_a_log_l1, gdn_dt_bias_l1, gdn_norm_l1, gdn_out_l1)
            os_, ncs, nss = _mixer_gdn(hs, state_conv_l1, state_ssm_l1, gdn_in_l1, gdn_conv_l1, gdn_a_log_l1, gdn_dt_bias_l1, gdn_norm_l1, gdn_out_l1)
        xp = xp + mp[2] * op
        xs = xs + ms[2] * os_
        hp = _modulate(xp, norm2[i], mp[3], mp[4])
        hs = _modulate(xs, norm2[i], ms[3], ms[4])
        D = xp.shape[-1]
        n_p = xp.shape[0] * xp.shape[1]
        h_all = jnp.concatenate([hp.reshape(-1, D), hs.reshape(-1, D)], axis=0)
        po = _peer(h_all, peer_query[i], peer_keys[i], peer_u[i], peer_v[i])
        xp = xp + mp[5] * po[:n_p].reshape(xp.shape)
        xs = xs + ms[5] * po[n_p:].reshape(xs.shape)
    y_prompt = _rmsnorm(xp, final_norm)
    y_sample = _rmsnorm(xs, final_norm)
    return (y_prompt, y_sample, nkp, nvp, nkip, nks, nvs, nkis, ncp, nsp, ncs, nss)
```

```python
import functools

import jax
import jax.numpy as jnp
from jax import lax
from jax.experimental import pallas as pl
from jax.experimental.pallas import tpu as pltpu

CHUNK = 64
EPS = 1e-6
ROPE_THETA = 500000.0
ROPE_FRACTION = 4
A_HEADS = 16
A_KV_HEADS = 4
IDX_HEADS = 8
IDX_DIM = 64
TOPK_MAX = 256
Q_BLOCK = 128
GDN_QK_HEADS = 16
GDN_V_HEADS = 32
GDN_DK = 128
GDN_DV = 128
CONV_W = 4
PEER_HEADS = 8
PEER_NKEYS = 128
PEER_KEY_DIM = 256
PEER_HALF = PEER_KEY_DIM // 2
PEER_TOPK = 16
PEER_BLOCK = 64

V7X_VMEM_LIMIT_BYTES = 48 * 1024 * 1024
LANE = 128
SUBLANE = 8


def _round_up(n, m):
    return (n + m - 1) // m * m


def _matmul_kernel(x_ref, *refs, passes):
    nparts = 1 + passes // 2
    w_refs, o_ref, x_parts = refs[:nparts], refs[nparts], refs[nparts + 1:]

    @pl.when(pl.program_id(1) == 0)
    def _():
        x = x_ref[...].astype(jnp.float32)
        hi = x.astype(jnp.bfloat16)
        x_parts[0][...] = hi
        if passes == 3:
            x_parts[1][...] = (x - hi.astype(jnp.float32)).astype(jnp.bfloat16)

    acc = jnp.dot(x_parts[0][...], w_refs[0][...], preferred_element_type=jnp.float32)
    if passes == 3:
        acc = acc + jnp.dot(x_parts[0][...], w_refs[1][...], preferred_element_type=jnp.float32)
        acc = acc + jnp.dot(x_parts[1][...], w_refs[0][...], preferred_element_type=jnp.float32)
    o_ref[...] = acc.astype(o_ref.dtype)


def _weight_parts(w, passes):
    hi = w.astype(jnp.bfloat16)
    if passes == 1:
        return (hi,)
    return (hi, (w - hi.astype(jnp.float32)).astype(jnp.bfloat16))


def pmatmul(x, w_parts, *, col0=0, ncols=None, tm=1024, tn=512, out_dtype=jnp.float32):
    passes = 1 if len(w_parts) == 1 else 3
    M, K = x.shape
    n_total = w_parts[0].shape[1]
    ncols = n_total - col0 if ncols is None else ncols
    tm = min(tm, _round_up(M, 2 * SUBLANE))
    tn = min(tn, ncols)
    assert ncols % tn == 0 and col0 % tn == 0 and (tn % LANE == 0 or tn == n_total)
    Mp = _round_up(M, tm)
    if Mp != M:
        x = jnp.pad(x, ((0, Mp - M), (0, 0)))
    c0 = col0 // tn
    out = pl.pallas_call(
        functools.partial(_matmul_kernel, passes=passes),
        grid=(Mp // tm, ncols // tn),
        in_specs=[pl.BlockSpec((tm, K), lambda i, j: (i, 0))]
        + [pl.BlockSpec((K, tn), lambda i, j: (0, c0 + j))] * len(w_parts),
        out_specs=pl.BlockSpec((tm, tn), lambda i, j: (i, j)),
        out_shape=jax.ShapeDtypeStruct((Mp, ncols), out_dtype),
        scratch_shapes=[pltpu.VMEM((tm, K), jnp.bfloat16)] * (1 + passes // 2),
        compiler_params=pltpu.CompilerParams(
            dimension_semantics=("parallel", "arbitrary"),
            vmem_limit_bytes=V7X_VMEM_LIMIT_BYTES),
        name="matmul",
    )(x, *w_parts)
    return out[:M]


def _mm(x, w, *, passes=1, **kw):
    lead = x.shape[:-1]
    N = w.shape[1]
    Np = _round_up(N, LANE)
    if Np != N:
        w = jnp.pad(w, ((0, 0), (0, Np - N)))
    tn = kw.pop("tn", 512)
    while Np % tn:
        tn -= LANE
    out = pmatmul(x.reshape(-1, x.shape[-1]), _weight_parts(w, passes), tn=tn, **kw)
    return out[:, :N].reshape(*lead, N)


def _rmsnorm(x, gain):
    y = x * lax.rsqrt(jnp.mean(x * x, axis=-1, keepdims=True) + EPS)
    return y * gain


def _adaln_kernel(c_ref, w_ref, b_ref, o_ref):
    c = c_ref[...]
    o_ref[...] = _dot3(c * jax.nn.sigmoid(c), w_ref[...]) + b_ref[...]


def _adaln(c, w, b, *, tn=512):
    n, D = c.shape
    N = w.shape[1]
    rows = _round_up(n, SUBLANE)
    mod = pl.pallas_call(
        _adaln_kernel,
        grid=(N // tn,),
        in_specs=[pl.BlockSpec((rows, D), lambda j: (0, 0)),
                  pl.BlockSpec((D, tn), lambda j: (0, j)),
                  pl.BlockSpec((1, tn), lambda j: (0, j))],
        out_specs=pl.BlockSpec((rows, tn), lambda j: (0, j)),
        out_shape=jax.ShapeDtypeStruct((rows, N), jnp.float32),
        compiler_params=pltpu.CompilerParams(
            dimension_semantics=("parallel",), vmem_limit_bytes=V7X_VMEM_LIMIT_BYTES),
        name="adaln",
    )(jnp.pad(c, ((0, rows - n), (0, 0))), w, b.reshape(1, N))
    return [m[:, None, :] for m in jnp.split(mod[:n], 6, axis=-1)]


def _modulate(x, gain, shift, scale):
    return _rmsnorm(x, gain) * (1.0 + scale) + shift


def _rope_partial(x, pos):
    rot = x.shape[-1] // ROPE_FRACTION
    half = rot // 2
    inv_freq = ROPE_THETA ** (-jnp.arange(half, dtype=jnp.float32) / half)
    ang = pos.astype(jnp.float32)[:, None] * inv_freq[None, :]
    cos = jnp.cos(ang)[:, None, :]
    sin = jnp.sin(ang)[:, None, :]
    x1, x2, rest = x[..., :half], x[..., half:rot], x[..., rot:]
    return jnp.concatenate([x1 * cos - x2 * sin, x2 * cos + x1 * sin, rest], axis=-1)


_NT_DIMS = (((1,), (1,)), ((), ()))
INT32_MIN = -2 ** 31
LOG2_E = 1.4426950408889634
_NEG_INF_KEY = -2139095041
IDX_PACK = 4 * IDX_DIM


def _ordered_key(x):
    bits = pltpu.bitcast(x, jnp.int32)
    return bits ^ ((bits >> 31) & 0x7FFFFFFF)


def _lane_tile_sum(x, width=LANE):
    out = x[:, :width]
    for c in range(1, x.shape[1] // width):
        out = out + x[:, c * width:(c + 1) * width]
    return out


def _dsa_select_bias(qi_ref, wi_ref, kidx_ref, key_ref, bias_ref, *, first, n_tiles, topk, tk):
    tq = wi_ref.shape[1]
    row = lax.broadcasted_iota(jnp.int32, (tq, 1), 0)
    lim = (((first + row) >> 6) + 1) * CHUNK
    w = wi_ref[0]

    def score_tile(j, c):
        off = pl.multiple_of(j * tk, tk)
        kt = kidx_ref[0, pl.ds(off, tk), :]
        sc = lax.dot_general(qi_ref[0, 0], kt, _NT_DIMS,
                             preferred_element_type=jnp.float32)
        s = jnp.zeros((tq, tk), jnp.float32)
        for h in range(IDX_HEADS):
            s = s + w[:, h:h + 1] * jnp.maximum(sc[h * tq:(h + 1) * tq], 0.0)
        col = off + lax.broadcasted_iota(jnp.int32, (tq, tk), 1)
        s = jnp.where(col < lim, s + 0.0, NEG_INF)
        key_ref[:, pl.ds(off, tk)] = _ordered_key(s)
        return c

    lax.fori_loop(0, n_tiles, score_tile, 0)

    def bit_step(b, thr):
        cand = thr + lax.shift_left(jnp.int32(1), 31 - b)

        def count_tile(j, c):
            off = pl.multiple_of(j * tk, tk)
            ge = jnp.where(key_ref[:, pl.ds(off, tk)] >= cand, 1.0, 0.0)
            return c + _lane_tile_sum(ge)

        c = lax.fori_loop(0, n_tiles, count_tile, jnp.zeros((tq, LANE), jnp.float32))
        cnt = jnp.sum(c, axis=1, keepdims=True)
        return jnp.where(cnt >= topk, cand, thr)

    thr = lax.fori_loop(0, 32, bit_step, jnp.full((tq, 1), INT32_MIN, jnp.int32))
    thr = jnp.maximum(thr, _NEG_INF_KEY + 1)

    def bias_tile(j, c):
        off = pl.multiple_of(j * tk, tk)
        bias_ref[:, pl.ds(off, tk)] = jnp.where(key_ref[:, pl.ds(off, tk)] >= thr, 0.0, NEG_INF)
        return c

    lax.fori_loop(0, n_tiles, bias_tile, 0)


def _dsa_kernel(qi_ref, wi_ref, kidx_ref, q_ref, k_ref, v_ref, o_ref,
                key_ref, bias_ref, qg_ref, m_ref, l_ref, acc_ref, *, pos0, topk, tk):
    i = pl.program_id(1)
    tq = q_ref.shape[1]
    hd = acc_ref.shape[2]
    groups = acc_ref.shape[0]
    rep = q_ref.shape[2] // (groups * hd)
    first = pos0 + i * tq
    n_valid = (((first + tq - 1) >> 6) + 1) * CHUNK
    n_tiles = (n_valid + tk - 1) // tk
    _dsa_select_bias(qi_ref, wi_ref, kidx_ref, key_ref, bias_ref, first=first, n_tiles=n_tiles, topk=topk, tk=tk)

    for g in range(groups):
        for r in range(rep):
            c0 = (g * rep + r) * hd
            qg_ref[g, r * tq:(r + 1) * tq, :] = q_ref[0, :, c0:c0 + hd]
    m_ref[...] = jnp.full(m_ref.shape, NEG_INF, jnp.float32)
    l_ref[...] = jnp.zeros(l_ref.shape, jnp.float32)
    acc_ref[...] = jnp.zeros(acc_ref.shape, jnp.float32)
    lane_reps = tk // LANE

    def att_tile(j, c):
        off = pl.multiple_of(j * tk, tk)
        b = bias_ref[:, pl.ds(off, tk)]
        bias = jnp.concatenate([b] * rep, axis=0)
        for g in range(groups):
            kt = k_ref[0, pl.ds(off, tk), g * hd:(g + 1) * hd]
            vt = v_ref[0, pl.ds(off, tk), g * hd:(g + 1) * hd]
            lg = lax.dot_general(qg_ref[g], kt, _NT_DIMS, preferred_element_type=jnp.float32) + bias
            m_old = m_ref[g]
            m_new = jnp.maximum(m_old, jnp.max(lg, axis=1, keepdims=True))
            m_safe = jnp.where(m_new == NEG_INF, 0.0, m_new)
            p = jnp.exp2(lg - jnp.tile(m_safe, (1, lane_reps)))
            alpha = jnp.exp2(m_old - m_safe)
            l_ref[g] = alpha * l_ref[g] + jnp.sum(p, axis=1, keepdims=True)
            acc_ref[g] = alpha * acc_ref[g] + jnp.dot(p.astype(jnp.bfloat16), vt,
                                                      preferred_element_type=jnp.float32)
            m_ref[g] = m_new
        return c

    lax.fori_loop(0, n_tiles, att_tile, 0)
    for g in range(groups):
        out = acc_ref[g] / l_ref[g]
        for r in range(rep):
            c0 = (g * rep + r) * hd
            o_ref[0, :, c0:c0 + hd] = out[r * tq:(r + 1) * tq].astype(o_ref.dtype)


def _dsa_core(q, qi, wi, k, v, kidx, pos0, topk, *, tq, tk=512):
    B, T, qd = q.shape
    S = k.shape[1]
    hd = qd // A_HEADS
    gw = qd // A_KV_HEADS
    Sp = _round_up(S, tk)
    qh, ql = _split_bf16(qi)
    nb = T // tq
    qi3 = jnp.concatenate([qh, ql, qh, jnp.zeros_like(qh)], axis=-1)
    qi3 = qi3.reshape(B, nb, tq, IDX_HEADS, IDX_PACK).transpose(0, 1, 3, 2, 4)
    qi3 = qi3.reshape(B, nb, IDX_HEADS * tq, IDX_PACK)
    kh, kl = _split_bf16(kidx)
    kidx3 = jnp.concatenate([kh, kh, kl, jnp.zeros_like(kh)], axis=-1)
    pad = ((0, 0), (0, Sp - S), (0, 0))
    kidx3 = jnp.pad(kidx3, pad)
    kb = jnp.pad(k.astype(jnp.bfloat16), pad)
    vb = jnp.pad(v.astype(jnp.bfloat16), pad)
    rows = (gw // hd) * tq
    kvw = A_KV_HEADS * hd
    return pl.pallas_call(
        functools.partial(_dsa_kernel, pos0=pos0, topk=topk, tk=tk),
        grid=(B, nb),
        in_specs=[pl.BlockSpec((1, 1, IDX_HEADS * tq, IDX_PACK), lambda b, i: (b, i, 0, 0)),
                  pl.BlockSpec((1, tq, IDX_HEADS), lambda b, i: (b, i, 0)),
                  pl.BlockSpec((1, Sp, IDX_PACK), lambda b, i: (b, 0, 0)),
                  pl.BlockSpec((1, tq, qd), lambda b, i: (b, i, 0)),
                  pl.BlockSpec((1, Sp, kvw), lambda b, i: (b, 0, 0)),
                  pl.BlockSpec((1, Sp, kvw), lambda b, i: (b, 0, 0))],
        out_specs=pl.BlockSpec((1, tq, qd), lambda b, i: (b, i, 0)),
        out_shape=jax.ShapeDtypeStruct((B, T, qd), jnp.bfloat16),
        scratch_shapes=[pltpu.VMEM((tq, Sp), jnp.int32),
                        pltpu.VMEM((tq, Sp), jnp.float32),
                        pltpu.VMEM((A_KV_HEADS, rows, hd), jnp.bfloat16),
                        pltpu.VMEM((A_KV_HEADS, rows, LANE), jnp.float32),
                        pltpu.VMEM((A_KV_HEADS, rows, LANE), jnp.float32),
                        pltpu.VMEM((A_KV_HEADS, rows, hd), jnp.float32)],
        compiler_params=pltpu.CompilerParams(
            dimension_semantics=("parallel", "arbitrary"),
            vmem_limit_bytes=V7X_VMEM_LIMIT_BYTES),
        name="dsa_core",
    )(qi3, wi, kidx3, (q * (hd ** -0.5 * LOG2_E)).astype(jnp.bfloat16), kb, vb)


def _mixer_dsa(h, pos0, past_k, past_v, past_kidx, w_in, w_out):
    B, T, D = h.shape
    hd = D // A_HEADS
    q_dim, kv_dim = A_HEADS * hd, A_KV_HEADS * hd
    pos = pos0 + jnp.arange(T, dtype=jnp.int32)
    hm = h.reshape(B * T, D)
    w_main = _weight_parts(w_in[:, :q_dim + 2 * kv_dim], 1)
    q = pmatmul(hm, w_main, col0=0, ncols=q_dim).reshape(B, T, q_dim)
    kv = pmatmul(hm, w_main, col0=q_dim, ncols=2 * kv_dim).reshape(B, T, 2 * kv_dim)
    idx = _mm(h, w_in[:, q_dim + 2 * kv_dim:], passes=3)
    k, v = kv[..., :kv_dim], kv[..., kv_dim:]
    qi = idx[..., :IDX_HEADS * IDX_DIM]
    ki = idx[..., IDX_HEADS * IDX_DIM:IDX_HEADS * IDX_DIM + IDX_DIM]
    wi = idx[..., IDX_HEADS * IDX_DIM + IDX_DIM:]
    q = _rope_partial(q.reshape(B, T, A_HEADS, hd), pos)
    k = _rope_partial(k.reshape(B, T, A_KV_HEADS, hd), pos)
    v = v.reshape(B, T, A_KV_HEADS, hd)
    qi = _rope_partial(qi.reshape(B, T, IDX_HEADS, IDX_DIM), pos)
    ki = _rope_partial(ki[:, :, None, :], pos)[:, :, 0, :]
    wi = wi * (IDX_HEADS ** -0.5 * IDX_DIM ** -0.5)
    if past_k is None:
        k_all, v_all, ki_all = k, v, ki
    else:
        k_all = jnp.concatenate([past_k, k], axis=1)
        v_all = jnp.concatenate([past_v, v], axis=1)
        ki_all = jnp.concatenate([past_kidx, ki], axis=1)
    n_keys = k_all.shape[1]
    topk = min(TOPK_MAX, n_keys // 4)
    o = _dsa_core(q.reshape(B, T, q_dim), qi, wi, k_all.reshape(B, n_keys, kv_dim),
                  v_all.reshape(B, n_keys, kv_dim), ki_all, pos0, topk, tq=min(Q_BLOCK, T))
    return _mm(o, w_out), k, v, ki


GDN_HEAD_GROUP = 8


def _bf16_dot(a, b):
    return jnp.dot(a.astype(jnp.bfloat16), b.astype(jnp.bfloat16), preferred_element_type=jnp.float32)


def _dot3(a, b):
    ah, al = _split_bf16(a)
    bh, bl = _split_bf16(b)
    out = jnp.dot(ah, bh, preferred_element_type=jnp.float32)
    out = out + jnp.dot(ah, bl, preferred_element_type=jnp.float32)
    return out + jnp.dot(al, bh, preferred_element_type=jnp.float32)


def _conv_silu(x_ref, w_ref, xe_ref):
    C = x_ref.shape[1]
    taps = w_ref.shape[0]
    xe_ref[SUBLANE:, :] = x_ref[0]
    first = SUBLANE - (taps - 1)
    acc = xe_ref[first:first + C, :] * w_ref[0:1, :]
    for j in range(1, taps):
        acc = acc + xe_ref[first + j:first + j + C, :] * w_ref[j:j + 1, :]
    xe_ref[:SUBLANE, :] = xe_ref[C:, :]
    return acc * jax.nn.sigmoid(acc)


def _gdn_kernel(xq_ref, xk_ref, xv_ref, wq_ref, wk_ref, wv_ref, cq_ref, ck_ref, cv_ref,
                z_ref, g_ref, gt_ref, beta_ref, nw_ref, s0_ref,
                o_ref, s_out_ref, s_ref, eq_ref, ek_ref, ev_ref):
    n = pl.program_id(2)
    C = xq_ref.shape[1]
    hg = g_ref.shape[3]
    dk = s_ref.shape[1]
    dv = s_ref.shape[2]
    rep = hg // (xk_ref.shape[2] // dk)

    @pl.when(n == 0)
    def _():
        s_ref[...] = s0_ref[0]
        for e_ref, c_ref in ((eq_ref, cq_ref), (ek_ref, ck_ref), (ev_ref, cv_ref)):
            e_ref[:SUBLANE, :] = jnp.zeros((SUBLANE, e_ref.shape[1]), jnp.float32)
            e_ref[SUBLANE - c_ref.shape[1]:SUBLANE, :] = c_ref[0]

    qc = _conv_silu(xq_ref, wq_ref, eq_ref)
    kc = _conv_silu(xk_ref, wk_ref, ek_ref)
    vc = _conv_silu(xv_ref, wv_ref, ev_ref)
    q_heads, k_heads = [], []
    for i in range(hg // rep):
        qh = qc[:, i * dk:(i + 1) * dk]
        kh = kc[:, i * dk:(i + 1) * dk]
        q_heads.append(qh * (lax.rsqrt(jnp.sum(qh * qh, axis=1, keepdims=True) + EPS) * dk ** -0.5))
        k_heads.append(kh * lax.rsqrt(jnp.sum(kh * kh, axis=1, keepdims=True) + EPS))

    ri = lax.broadcasted_iota(jnp.int32, (C, C), 0)
    ci = lax.broadcasted_iota(jnp.int32, (C, C), 1)
    causal = ri >= ci
    strict = ri > ci
    eye = jnp.where(ri == ci, 1.0, 0.0)
    g = g_ref[0, 0]
    gc_all = _dot3(jnp.where(causal, 1.0, 0.0), g)
    gr_all = _dot3(gt_ref[0, 0, 0], jnp.where(ri <= ci, 1.0, 0.0))
    beta = beta_ref[0, 0]
    nw = nw_ref[...]

    heads = range(hg)
    qs = [q_heads[h // rep] for h in heads]
    ks = [k_heads[h // rep] for h in heads]
    gcs = [gc_all[:, h:h + 1] for h in heads]
    bcols = [beta[:, h:h + 1] for h in heads]
    decays = [jnp.where(causal, jnp.exp(jnp.where(causal, gcs[h] - gr_all[h:h + 1, :], 0.0)), 0.0)
              for h in heads]
    kbs = [ks[h] * bcols[h] for h in heads]
    kks = [lax.dot_general(kbs[h].astype(jnp.bfloat16), ks[h].astype(jnp.bfloat16), _NT_DIMS,
                           preferred_element_type=jnp.float32) for h in heads]
    bms = [jnp.where(strict, -(kks[h] * decays[h]), 0.0) for h in heads]
    egs = [jnp.exp(gcs[h]) for h in heads]
    rhss = [jnp.concatenate([vc[:, h * dv:(h + 1) * dv] * bcols[h], kbs[h] * egs[h]], axis=1)
            for h in heads]
    ps = [eye + bms[h] for h in heads]
    step = 2
    while step < C:
        bms = [_dot3(bms[h], bms[h]) for h in heads]
        ps = [ps[h] + _dot3(ps[h], bms[h]) for h in heads]
        step *= 2
    ws = [_dot3(ps[h], rhss[h]) for h in heads]
    qks = [lax.dot_general(qs[h].astype(jnp.bfloat16), ks[h].astype(jnp.bfloat16), _NT_DIMS,
                           preferred_element_type=jnp.float32) * decays[h] for h in heads]
    g_lasts = [gcs[h][C - 1:C, :] for h in heads]
    ss = [s_ref[h] for h in heads]
    us = [ws[h][:, :dv] - _bf16_dot(ws[h][:, dv:], ss[h]) for h in heads]
    os_ = [_bf16_dot(qs[h] * egs[h], ss[h]) + _bf16_dot(qks[h], us[h]) for h in heads]
    for h in heads:
        ke = ks[h] * jnp.exp(g_lasts[h] - gcs[h])
        s_ref[h] = ss[h] * jnp.exp(g_lasts[h]) + _bf16_dot(ke.T, us[h])
    for h in heads:
        o = os_[h]
        o = o * lax.rsqrt(jnp.mean(o * o, axis=1, keepdims=True) + EPS) * nw
        z = z_ref[0, :, h * dv:(h + 1) * dv]
        o_ref[0, :, h * dv:(h + 1) * dv] = (o * (z * jax.nn.sigmoid(z))).astype(o_ref.dtype)

    @pl.when(n == pl.num_programs(2) - 1)
    def _():
        s_out_ref[0] = s_ref[...]


def _gdn_core(qkv, conv_w, conv_state, z, g, beta, norm_w, s0):
    B, T, vd = z.shape
    VH = g.shape[2]
    dv = vd // VH
    dk = s0.shape[2]
    qd = (qkv.shape[2] - vd) // 2
    C = min(CHUNK, T)
    N = T // C
    hg = GDN_HEAD_GROUP
    ng = VH // hg
    qw = qd // ng
    vw = hg * dv
    assert qd % qw == 0 and (2 * qd) % vw == 0
    k0, v0 = qd // qw, (2 * qd) // vw
    taps = conv_w.shape[0]

    def grouped(a):
        return a.reshape(B, T, ng, hg).transpose(0, 2, 1, 3)

    gg, bg = grouped(g), grouped(beta)
    gt = gg.reshape(B, ng, N, C, hg).transpose(0, 1, 2, 4, 3)
    small = pl.BlockSpec((1, 1, C, hg), lambda b, j, n: (b, j, n, 0))
    o, s = pl.pallas_call(
        _gdn_kernel,
        grid=(B, ng, N),
        in_specs=[pl.BlockSpec((1, C, qw), lambda b, j, n: (b, n, j)),
                  pl.BlockSpec((1, C, qw), lambda b, j, n: (b, n, k0 + j)),
                  pl.BlockSpec((1, C, vw), lambda b, j, n: (b, n, v0 + j)),
                  pl.BlockSpec((taps, qw), lambda b, j, n: (0, j)),
                  pl.BlockSpec((taps, qw), lambda b, j, n: (0, k0 + j)),
                  pl.BlockSpec((taps, vw), lambda b, j, n: (0, v0 + j)),
                  pl.BlockSpec((1, taps - 1, qw), lambda b, j, n: (b, 0, j)),
                  pl.BlockSpec((1, taps - 1, qw), lambda b, j, n: (b, 0, k0 + j)),
                  pl.BlockSpec((1, taps - 1, vw), lambda b, j, n: (b, 0, v0 + j)),
                  pl.BlockSpec((1, C, vw), lambda b, j, n: (b, n, j)),
                  small,
                  pl.BlockSpec((1, 1, 1, hg, C), lambda b, j, n: (b, j, n, 0, 0)),
                  small,
                  pl.BlockSpec((1, dv), lambda b, j, n: (0, 0)),
                  pl.BlockSpec((1, hg, dk, dv), lambda b, j, n: (b, j, 0, 0))],
        out_specs=[pl.BlockSpec((1, C, hg * dv), lambda b, j, n: (b, n, j)),
                   pl.BlockSpec((1, hg, dk, dv), lambda b, j, n: (b, j, 0, 0))],
        out_shape=[jax.ShapeDtypeStruct((B, T, vd), jnp.bfloat16),
                   jax.ShapeDtypeStruct(s0.shape, jnp.float32)],
        scratch_shapes=[pltpu.VMEM((hg, dk, dv), jnp.float32),
                        pltpu.VMEM((SUBLANE + C, qw), jnp.float32),
                        pltpu.VMEM((SUBLANE + C, qw), jnp.float32),
                        pltpu.VMEM((SUBLANE + C, vw), jnp.float32)],
        compiler_params=pltpu.CompilerParams(
            dimension_semantics=("parallel", "parallel", "arbitrary"),
            vmem_limit_bytes=V7X_VMEM_LIMIT_BYTES),
        name="gdn_core",
    )(qkv, qkv, qkv, conv_w, conv_w, conv_w, conv_state, conv_state, conv_state,
      z, gg, gt, bg, norm_w.reshape(1, dv), s0)
    return o, s


def _mixer_gdn(h, conv_state, ssm_state, w_in, conv_w, a_log, dt_bias, norm_w, w_out):
    B, T, _ = h.shape
    qk_dim, v_dim = GDN_QK_HEADS * GDN_DK, GDN_V_HEADS * GDN_DV
    conv_dim = 2 * qk_dim + v_dim
    hm = h.reshape(B * T, -1)
    w_main = _weight_parts(w_in[:, :conv_dim + v_dim], 1)
    qkv = pmatmul(hm, w_main, col0=0, ncols=conv_dim).reshape(B, T, conv_dim)
    z = pmatmul(hm, w_main, col0=conv_dim, ncols=v_dim).reshape(B, T, v_dim)
    gates = _mm(h, w_in[:, conv_dim + v_dim:])
    beta_raw, a_raw = gates[..., :GDN_V_HEADS], gates[..., GDN_V_HEADS:]
    if conv_state is None:
        conv_state = jnp.zeros((B, CONV_W - 1, conv_dim), h.dtype)
    if ssm_state is None:
        ssm_state = jnp.zeros((B, GDN_V_HEADS, GDN_DK, GDN_DV), jnp.float32)
    new_conv = jnp.concatenate([conv_state, qkv[:, T - (CONV_W - 1):]], axis=1)[:, -(CONV_W - 1):]
    beta = jax.nn.sigmoid(beta_raw)
    g = -jnp.exp(a_log) * jax.nn.softplus(a_raw + dt_bias)
    o, S = _gdn_core(qkv, conv_w, conv_state, z, g, beta, norm_w, ssm_state)
    out = _mm(o, w_out)
    return out, new_conv, S


NEG_INF = float("-inf")
POS_INF = float("inf")
_PEER_CAND_ROWS = tuple((i, PEER_TOPK // (i + 1)) for i in range(PEER_TOPK // 2))


def _split_bf16(x):
    hi = x.astype(jnp.bfloat16)
    lo = (x - hi.astype(jnp.float32)).astype(jnp.bfloat16)
    return hi, lo


def _dot3_nt(a, b):
    dn = (((1,), (1,)), ((), ()))
    ah, al = _split_bf16(a)
    bh, bl = _split_bf16(b)
    out = lax.dot_general(ah, bh, dn, preferred_element_type=jnp.float32)
    out = out + lax.dot_general(ah, bl, dn, preferred_element_type=jnp.float32)
    return out + lax.dot_general(al, bh, dn, preferred_element_type=jnp.float32)


def _top_rows_desc(s, n, with_rank=False):
    rows = []
    cur = s
    rank = jnp.full(s.shape, float(n), jnp.float32)
    for i in range(n):
        m = jnp.max(cur, axis=0, keepdims=True)
        rows.append(m)
        hit = cur == m
        if with_rank:
            rank = jnp.where(hit, float(i), rank)
        cur = jnp.where(hit, NEG_INF, cur)
    return (rows, rank) if with_rank else rows


def _stack_rows(rows, lanes):
    n = len(rows)
    rid = lax.broadcasted_iota(jnp.int32, (n, lanes), 0)
    out = jnp.zeros((n, lanes), jnp.float32)
    for i, r in enumerate(rows):
        out = jnp.where(rid == i, r, out)
    return out


def _peer_select_kernel(q_ref, keys_ref, rk2_ref, e2_ref, cnt_ref, e1_ref):
    tm = q_ref.shape[0]
    k = PEER_TOPK
    s1 = _dot3_nt(keys_ref[0], q_ref[:, :PEER_HALF])
    s2 = _dot3_nt(keys_ref[1], q_ref[:, PEER_HALF:])
    r1 = _top_rows_desc(s1, k)
    r2, rank2 = _top_rows_desc(s2, k, with_rank=True)
    v1 = _stack_rows(r1, tm)
    v2 = _stack_rows(r2, tm)
    v2h = v2[:k // 2]
    rid = lax.broadcasted_iota(jnp.int32, (k // 2, tm), 0)
    pieces = [r1[0] + v2]
    for i, n in _PEER_CAND_ROWS[1:]:
        pieces.append(jnp.where(rid < n, r1[i] + v2h, NEG_INF))
    pieces.append(v1[k // 2:] + r2[0])
    cand = jnp.concatenate(pieces, axis=0)
    tau = _top_rows_desc(cand, k)[-1]
    top = r1[0] + r2[0]
    z = jnp.sum(jnp.where(cand >= tau, jnp.exp(cand - top), 0.0), axis=0, keepdims=True)
    cnt = jnp.zeros(s1.shape, jnp.float32)
    for j in range(k):
        cnt = cnt + jnp.where(s1 + r2[j] >= tau, 1.0, 0.0)
    rk2_ref[0] = rank2.astype(rk2_ref.dtype)
    e2_ref[0] = (jnp.exp(s2 - r2[0]) / z).astype(e2_ref.dtype)
    cnt_ref[0] = cnt
    e1_ref[0] = jnp.exp(s1 - r1[0])


def _peer_select(q, keys, *, tm):
    M = q.shape[0]
    nk = keys.shape[1]
    ospec = pl.BlockSpec((1, nk, tm), lambda i, h: (h, 0, i))
    return pl.pallas_call(
        _peer_select_kernel,
        grid=(M // tm, PEER_HEADS),
        in_specs=[pl.BlockSpec((tm, PEER_KEY_DIM), lambda i, h: (i, h)),
                  pl.BlockSpec(keys.shape, lambda i, h: (0, 0, 0))],
        out_specs=[ospec] * 4,
        out_shape=[jax.ShapeDtypeStruct((PEER_HEADS, nk, M), dt)
                   for dt in (jnp.bfloat16, jnp.bfloat16, jnp.float32, jnp.float32)],
        compiler_params=pltpu.CompilerParams(
            dimension_semantics=("parallel", "arbitrary"),
            vmem_limit_bytes=V7X_VMEM_LIMIT_BYTES),
        name="peer_select",
    )(q, keys)


BF16_SUBLANES = 16
PEER_SUB_EXPERTS = 256


def _peer_main_kernel(hT_ref, u_ref, vT_ref, rk2_ref, e2_ref, cnt_ref, e1_ref, o_ref,
                      coef_ref, acc_ref):
    e = pl.program_id(1)
    te, tm = coef_ref.shape
    nk = rk2_ref.shape[1]
    rows = BF16_SUBLANES
    sub = PEER_SUB_EXPERTS

    @pl.when(e == 0)
    def _():
        acc_ref[...] = jnp.zeros_like(acc_ref)

    for sb in range(te // sub):
        act = jnp.dot(u_ref[sb * sub:(sb + 1) * sub, :], hT_ref[...],
                      preferred_element_type=jnp.float32)
        for al in range(sub // nk):
            a = sb * (sub // nk) + al
            for lg in range(tm // LANE):
                lanes = slice(lg * LANE, (lg + 1) * LANE)
                cnts = [jnp.broadcast_to(cnt_ref[h, a:a + 1, lanes].astype(jnp.bfloat16), (rows, LANE))
                        for h in range(PEER_HEADS)]
                e1s = [jnp.broadcast_to(e1_ref[h, a:a + 1, lanes].astype(jnp.bfloat16), (rows, LANE))
                       for h in range(PEER_HEADS)]
                for r in range(nk // rows):
                    rs = slice(r * rows, (r + 1) * rows)
                    w = None
                    for h in range(PEER_HEADS):
                        e2 = e2_ref[h, rs, lanes]
                        t = jnp.where(rk2_ref[h, rs, lanes] < cnts[h], e2, jnp.zeros_like(e2)) * e1s[h]
                        w = t if w is None else w + t
                    x = act[al * nk + r * rows:al * nk + (r + 1) * rows, lanes]
                    g = 0.5 * x * (1.0 + lax.erf(x * (2.0 ** -0.5)))
                    coef_ref[sb * sub + al * nk + r * rows:sb * sub + al * nk + (r + 1) * rows, lanes] = (
                        w * g.astype(jnp.bfloat16))
        acc_ref[...] += jnp.dot(vT_ref[:, sb * sub:(sb + 1) * sub], coef_ref[sb * sub:(sb + 1) * sub, :],
                                preferred_element_type=jnp.float32)

    @pl.when(e == pl.num_programs(1) - 1)
    def _():
        o_ref[...] = acc_ref[...].T


def _peer_main(hT, u, vT, s2, e2, ca, e1, *, tm, te):
    D, M = hT.shape
    E = u.shape[0]
    nk = s2.shape[1]
    col_spec = pl.BlockSpec((PEER_HEADS, nk, tm), lambda i, e: (0, 0, i))
    row_spec = pl.BlockSpec((PEER_HEADS, te // nk, tm), lambda i, e: (0, e, i))
    return pl.pallas_call(
        _peer_main_kernel,
        grid=(M // tm, E // te),
        in_specs=[pl.BlockSpec((D, tm), lambda i, e: (0, i)),
                  pl.BlockSpec((te, D), lambda i, e: (e, 0)),
                  pl.BlockSpec((D, te), lambda i, e: (0, e)),
                  col_spec, col_spec, row_spec, row_spec],
        out_specs=pl.BlockSpec((tm, D), lambda i, e: (i, 0)),
        out_shape=jax.ShapeDtypeStruct((M, D), jnp.float32),
        scratch_shapes=[pltpu.VMEM((te, tm), jnp.bfloat16),
                        pltpu.VMEM((D, tm), jnp.float32)],
        compiler_params=pltpu.CompilerParams(
            dimension_semantics=("parallel", "arbitrary"),
            vmem_limit_bytes=V7X_VMEM_LIMIT_BYTES),
        name="peer_main",
    )(hT, u, vT, s2, e2, ca, e1)


def _peer(h, w_query, sub_keys, expert_u, expert_v, *, tm_sel=256, tm=512, te=1024):
    q = _mm(h, w_query, passes=3)
    s2, e2, ca, e1 = _peer_select(q, sub_keys, tm=min(tm_sel, h.shape[0]))
    hT = h.T.astype(jnp.bfloat16)
    u = expert_u.astype(jnp.bfloat16)
    vT = expert_v.T.astype(jnp.bfloat16)
    return _peer_main(hT, u, vT, s2, e2, ca, e1, tm=min(tm, h.shape[0]), te=te)


def kernel(x_prompt, x_sample, c_prompt, c_sample, cache_k_l0, cache_v_l0, cache_kidx_l0, state_conv_l1, state_ssm_l1, norm1_l0, norm2_l0, ada_w_l0, ada_b_l0, attn_in_l0, attn_out_l0, peer_query_l0, peer_keys_l0, peer_u_l0, peer_v_l0, norm1_l1, norm2_l1, ada_w_l1, ada_b_l1, gdn_in_l1, gdn_conv_l1, gdn_a_log_l1, gdn_dt_bias_l1, gdn_norm_l1, gdn_out_l1, peer_query_l1, peer_keys_l1, peer_u_l1, peer_v_l1, final_norm):
    past_len = cache_k_l0.shape[1]
    norm1 = (norm1_l0, norm1_l1)
    norm2 = (norm2_l0, norm2_l1)
    ada_w = (ada_w_l0, ada_w_l1)
    ada_b = (ada_b_l0, ada_b_l1)
    peer_query = (peer_query_l0, peer_query_l1)
    peer_keys = (peer_keys_l0, peer_keys_l1)
    peer_u = (peer_u_l0, peer_u_l1)
    peer_v = (peer_v_l0, peer_v_l1)
    xp, xs = x_prompt, x_sample
    for i in range(2):
        n_seq_p = c_prompt.shape[0]
        mods = _adaln(jnp.concatenate([c_prompt, c_sample], axis=0), ada_w[i], ada_b[i])
        mp = [m[:n_seq_p] for m in mods]
        ms = [m[n_seq_p:] for m in mods]
        hp = _modulate(xp, norm1[i], mp[0], mp[1])
        hs = _modulate(xs, norm1[i], ms[0], ms[1])
        if i == 0:
            op, nkp, nvp, nkip = _mixer_dsa(hp, 0, None, None, None, attn_in_l0, attn_out_l0)
            os_, nks, nvs, nkis = _mixer_dsa(hs, past_len, cache_k_l0, cache_v_l0, cache_kidx_l0, attn_in_l0, attn_out_l0)
        else:
            op, ncp, nsp = _mixer_gdn(hp, None, None, gdn_in_l1, gdn_conv_l1, gdn_a_log_l1, gdn_dt_bias_l1, gdn_norm_l1, gdn_out_l1)
            os_, ncs, nss = _mixer_gdn(hs, state_conv_l1, state_ssm_l1, gdn_in_l1, gdn_conv_l1, gdn_a_log_l1, gdn_dt_bias_l1, gdn_norm_l1, gdn_out_l1)
        xp = xp + mp[2] * op
        xs = xs + ms[2] * os_
        hp = _modulate(xp, norm2[i], mp[3], mp[4])
        hs = _modulate(xs, norm2[i], ms[3], ms[4])
        D = xp.shape[-1]
        n_p = xp.shape[0] * xp.shape[1]
        h_all = jnp.concatenate([hp.reshape(-1, D), hs.reshape(-1, D)], axis=0)
        po = _peer(h_all, peer_query[i], peer_keys[i], peer_u[i], peer_v[i])
        xp = xp + mp[5] * po[:n_p].reshape(xp.shape)
        xs = xs + ms[5] * po[n_p:].reshape(xs.shape)
    y_prompt = _rmsnorm(xp, final_norm)
    y_sample = _rmsnorm(xs, final_norm)
    return (y_prompt, y_sample, nkp, nvp, nkip, nks, nvs, nkis, ncp, nsp, ncs, nss)
```

```python
import functools

import jax
import jax.numpy as jnp
from jax import lax
from jax.experimental import pallas as pl
from jax.experimental.pallas import tpu as pltpu

CHUNK = 64
EPS = 1e-6
ROPE_THETA = 500000.0
ROPE_FRACTION = 4
A_HEADS = 16
A_KV_HEADS = 4
IDX_HEADS = 8
IDX_DIM = 64
TOPK_MAX = 256
Q_BLOCK = 128
GDN_QK_HEADS = 16
GDN_V_HEADS = 32
GDN_DK = 128
GDN_DV = 128
CONV_W = 4
PEER_HEADS = 8
PEER_NKEYS = 128
PEER_KEY_DIM = 256
PEER_HALF = PEER_KEY_DIM // 2
PEER_TOPK = 16
PEER_BLOCK = 64

V7X_VMEM_LIMIT_BYTES = 48 * 1024 * 1024
V7X_VMEM_LIMIT_PEER_BYTES = 56 * 1024 * 1024
LANE = 128
SUBLANE = 8


def _round_up(n, m):
    return (n + m - 1) // m * m


SEQ_BLOCK = CHUNK


def _modulated_rows(x, gain, shift_ref, scale_ref, sid):
    y = x * lax.rsqrt(jnp.mean(x * x, axis=-1, keepdims=True) + EPS)
    return y * gain * (1.0 + scale_ref[pl.ds(sid, 1), :]) + shift_ref[pl.ds(sid, 1), :]


def _matmul_kernel(sid_ref, x_ref, *refs, passes, has_mod, has_res):
    refs = list(refs)
    mod_refs = [refs.pop(0) for _ in range(3)] if has_mod else None
    nparts = 1 + passes // 2
    w_refs = [refs.pop(0) for _ in range(nparts)]
    res_refs = [refs.pop(0) for _ in range(2)] if has_res else None
    o_ref, x_parts = refs[0], refs[1:]
    tm = x_ref.shape[0]
    blocks = range(tm // SEQ_BLOCK) if (has_mod or has_res) else ()
    blk0 = pl.program_id(0) * (tm // SEQ_BLOCK)

    @pl.when(pl.program_id(1) == 0)
    def _():
        def put(rows, x):
            hi = x.astype(jnp.bfloat16)
            x_parts[0][rows, :] = hi
            if passes == 3:
                x_parts[1][rows, :] = (x - hi.astype(jnp.float32)).astype(jnp.bfloat16)

        if has_mod:
            gain_ref, shift_ref, scale_ref = mod_refs
            for r in blocks:
                rows = slice(r * SEQ_BLOCK, (r + 1) * SEQ_BLOCK)
                put(rows, _modulated_rows(x_ref[rows, :], gain_ref[...], shift_ref, scale_ref, sid_ref[blk0 + r]))
        else:
            put(slice(None), x_ref[...].astype(jnp.float32))

    acc = jnp.dot(x_parts[0][...], w_refs[0][...], preferred_element_type=jnp.float32)
    if passes == 3:
        acc = acc + jnp.dot(x_parts[0][...], w_refs[1][...], preferred_element_type=jnp.float32)
        acc = acc + jnp.dot(x_parts[1][...], w_refs[0][...], preferred_element_type=jnp.float32)
    if has_res:
        res_ref, gate_ref = res_refs
        for r in blocks:
            rows = slice(r * SEQ_BLOCK, (r + 1) * SEQ_BLOCK)
            gate = gate_ref[pl.ds(sid_ref[blk0 + r], 1), :]
            o_ref[rows, :] = res_ref[rows, :] + gate * acc[rows, :]
    else:
        o_ref[...] = acc.astype(o_ref.dtype)


def _weight_parts(w, passes):
    hi = w.astype(jnp.bfloat16)
    if passes == 1:
        return (hi,)
    return (hi, (w - hi.astype(jnp.float32)).astype(jnp.bfloat16))


def pmatmul(x, w_parts, *, col0=0, ncols=None, tm=512, tn=512, out_dtype=jnp.float32,
            mod=None, res=None, sid=None):
    passes = 1 if len(w_parts) == 1 else 3
    M, K = x.shape
    n_total = w_parts[0].shape[1]
    ncols = n_total - col0 if ncols is None else ncols
    tm = min(tm, _round_up(M, 2 * SUBLANE))
    tn = min(tn, ncols)
    assert ncols % tn == 0 and col0 % tn == 0 and (tn % LANE == 0 or tn == n_total)
    Mp = _round_up(M, tm)
    if mod is not None or res is not None:
        assert Mp == M and tm % SEQ_BLOCK == 0 and sid is not None
    else:
        sid = jnp.zeros((1,), jnp.int32)
    if Mp != M:
        x = jnp.pad(x, ((0, Mp - M), (0, 0)))
    c0 = col0 // tn
    in_specs = [pl.BlockSpec((tm, K), lambda i, j, s: (i, 0))]
    args = [x]
    if mod is not None:
        in_specs += [pl.BlockSpec(a.shape, lambda i, j, s: (0, 0)) for a in mod]
        args += list(mod)
    in_specs += [pl.BlockSpec((K, tn), lambda i, j, s: (0, c0 + j))] * len(w_parts)
    args += list(w_parts)
    if res is not None:
        in_specs += [pl.BlockSpec((tm, tn), lambda i, j, s: (i, j)),
                     pl.BlockSpec((res[1].shape[0], tn), lambda i, j, s: (0, j))]
        args += list(res)
        out_dtype = jnp.float32
    out = pl.pallas_call(
        functools.partial(_matmul_kernel, passes=passes, has_mod=mod is not None, has_res=res is not None),
        grid_spec=pltpu.PrefetchScalarGridSpec(
            num_scalar_prefetch=1,
            grid=(Mp // tm, ncols // tn),
            in_specs=in_specs,
            out_specs=pl.BlockSpec((tm, tn), lambda i, j, s: (i, j)),
            scratch_shapes=[pltpu.VMEM((tm, K), jnp.bfloat16)] * (1 + passes // 2)),
        out_shape=jax.ShapeDtypeStruct((Mp, ncols), out_dtype),
        compiler_params=pltpu.CompilerParams(
            dimension_semantics=("parallel", "arbitrary"),
            vmem_limit_bytes=V7X_VMEM_LIMIT_BYTES),
        name="matmul",
    )(sid, *args)
    return out[:M]


def _mm(x, w, *, passes=1, **kw):
    lead = x.shape[:-1]
    N = w.shape[1]
    Np = _round_up(N, LANE)
    if Np != N:
        w = jnp.pad(w, ((0, 0), (0, Np - N)))
    tn = kw.pop("tn", 512)
    while Np % tn:
        tn -= LANE
    out = pmatmul(x.reshape(-1, x.shape[-1]), _weight_parts(w, passes), tn=tn, **kw)
    return out[:, :N].reshape(*lead, N)


def _rmsnorm(x, gain):
    y = x * lax.rsqrt(jnp.mean(x * x, axis=-1, keepdims=True) + EPS)
    return y * gain


def _adaln_kernel(c_ref, w_ref, b_ref, o_ref):
    c = c_ref[...]
    o_ref[...] = _dot3(c * jax.nn.sigmoid(c), w_ref[...]) + b_ref[...]


def _adaln(c, w, b, *, tn=512):
    n, D = c.shape
    N = w.shape[1]
    rows = _round_up(n, SUBLANE)
    mod = pl.pallas_call(
        _adaln_kernel,
        grid=(N // tn,),
        in_specs=[pl.BlockSpec((rows, D), lambda j: (0, 0)),
                  pl.BlockSpec((D, tn), lambda j: (0, j)),
                  pl.BlockSpec((1, tn), lambda j: (0, j))],
        out_specs=pl.BlockSpec((rows, tn), lambda j: (0, j)),
        out_shape=jax.ShapeDtypeStruct((rows, N), jnp.float32),
        compiler_params=pltpu.CompilerParams(
            dimension_semantics=("parallel",), vmem_limit_bytes=V7X_VMEM_LIMIT_BYTES),
        name="adaln",
    )(jnp.pad(c, ((0, rows - n), (0, 0))), w, b.reshape(1, N))
    return jnp.split(mod, 6, axis=-1)


def _modulate(x, gain, shift, scale):
    return _rmsnorm(x, gain) * (1.0 + scale) + shift


def _rope_partial(x, pos):
    rot = x.shape[-1] // ROPE_FRACTION
    half = rot // 2
    inv_freq = ROPE_THETA ** (-jnp.arange(half, dtype=jnp.float32) / half)
    ang = pos.astype(jnp.float32)[:, None] * inv_freq[None, :]
    cos = jnp.cos(ang)[:, None, :]
    sin = jnp.sin(ang)[:, None, :]
    x1, x2, rest = x[..., :half], x[..., half:rot], x[..., rot:]
    return jnp.concatenate([x1 * cos - x2 * sin, x2 * cos + x1 * sin, rest], axis=-1)


_NT_DIMS = (((1,), (1,)), ((), ()))
INT32_MIN = -2 ** 31
LOG2_E = 1.4426950408889634
_NEG_INF_KEY = -2139095041
IDX_PACK = 4 * IDX_DIM


def _ordered_key(x):
    bits = pltpu.bitcast(x, jnp.int32)
    return bits ^ ((bits >> 31) & 0x7FFFFFFF)


def _lane_tile_sum(x, width=LANE):
    out = x[:, :width]
    for c in range(1, x.shape[1] // width):
        out = out + x[:, c * width:(c + 1) * width]
    return out


def _dsa_select_bias(qi_ref, wi_ref, kidx_ref, key_ref, bias_ref, *, first, n_tiles, topk, tk):
    tq = wi_ref.shape[1]
    row = lax.broadcasted_iota(jnp.int32, (tq, 1), 0)
    lim = (((first + row) >> 6) + 1) * CHUNK
    w = wi_ref[0]

    def score_tile(j, c):
        off = pl.multiple_of(j * tk, tk)
        kt = kidx_ref[0, pl.ds(off, tk), :]
        sc = lax.dot_general(qi_ref[0, 0], kt, _NT_DIMS,
                             preferred_element_type=jnp.float32)
        s = jnp.zeros((tq, tk), jnp.float32)
        for h in range(IDX_HEADS):
            s = s + w[:, h:h + 1] * jnp.maximum(sc[h * tq:(h + 1) * tq], 0.0)
        col = off + lax.broadcasted_iota(jnp.int32, (tq, tk), 1)
        s = jnp.where(col < lim, s + 0.0, NEG_INF)
        key_ref[:, pl.ds(off, tk)] = _ordered_key(s)
        return c

    lax.fori_loop(0, n_tiles, score_tile, 0)

    def bit_step(b, thr):
        cand = thr + lax.shift_left(jnp.int32(1), 31 - b)

        def count_tile(j, c):
            off = pl.multiple_of(j * tk, tk)
            ge = jnp.where(key_ref[:, pl.ds(off, tk)] >= cand, 1.0, 0.0)
            return c + _lane_tile_sum(ge)

        c = lax.fori_loop(0, n_tiles, count_tile, jnp.zeros((tq, LANE), jnp.float32))
        cnt = jnp.sum(c, axis=1, keepdims=True)
        return jnp.where(cnt >= topk, cand, thr)

    thr = lax.fori_loop(0, 32, bit_step, jnp.full((tq, 1), INT32_MIN, jnp.int32))
    thr = jnp.maximum(thr, _NEG_INF_KEY + 1)

    def bias_tile(j, c):
        off = pl.multiple_of(j * tk, tk)
        bias_ref[:, pl.ds(off, tk)] = jnp.where(key_ref[:, pl.ds(off, tk)] >= thr, 0.0, NEG_INF)
        return c

    lax.fori_loop(0, n_tiles, bias_tile, 0)


def _dsa_kernel(qi_ref, wi_ref, kidx_ref, q_ref, k_ref, v_ref, o_ref,
                key_ref, bias_ref, qg_ref, m_ref, l_ref, acc_ref, *, pos0, topk, tk):
    i = pl.program_id(1)
    tq = q_ref.shape[1]
    hd = acc_ref.shape[2]
    groups = acc_ref.shape[0]
    rep = q_ref.shape[2] // (groups * hd)
    first = pos0 + i * tq
    n_valid = (((first + tq - 1) >> 6) + 1) * CHUNK
    n_tiles = (n_valid + tk - 1) // tk
    _dsa_select_bias(qi_ref, wi_ref, kidx_ref, key_ref, bias_ref, first=first, n_tiles=n_tiles, topk=topk, tk=tk)

    for g in range(groups):
        for r in range(rep):
            c0 = (g * rep + r) * hd
            qg_ref[g, r * tq:(r + 1) * tq, :] = q_ref[0, :, c0:c0 + hd]
    m_ref[...] = jnp.full(m_ref.shape, NEG_INF, jnp.float32)
    l_ref[...] = jnp.zeros(l_ref.shape, jnp.float32)
    acc_ref[...] = jnp.zeros(acc_ref.shape, jnp.float32)
    lane_reps = tk // LANE

    def att_tile(j, c):
        off = pl.multiple_of(j * tk, tk)
        b = bias_ref[:, pl.ds(off, tk)]
        bias = jnp.concatenate([b] * rep, axis=0)
        for g in range(groups):
            kt = k_ref[0, pl.ds(off, tk), g * hd:(g + 1) * hd]
            vt = v_ref[0, pl.ds(off, tk), g * hd:(g + 1) * hd]
            lg = lax.dot_general(qg_ref[g], kt, _NT_DIMS, preferred_element_type=jnp.float32) + bias
            m_old = m_ref[g]
            m_new = jnp.maximum(m_old, jnp.max(lg, axis=1, keepdims=True))
            m_safe = jnp.where(m_new == NEG_INF, 0.0, m_new)
            p = jnp.exp2(lg - jnp.tile(m_safe, (1, lane_reps)))
            alpha = jnp.exp2(m_old - m_safe)
            l_ref[g] = alpha * l_ref[g] + jnp.sum(p, axis=1, keepdims=True)
            acc_ref[g] = alpha * acc_ref[g] + jnp.dot(p.astype(jnp.bfloat16), vt,
                                                      preferred_element_type=jnp.float32)
            m_ref[g] = m_new
        return c

    lax.fori_loop(0, n_tiles, att_tile, 0)
    for g in range(groups):
        out = acc_ref[g] / l_ref[g]
        for r in range(rep):
            c0 = (g * rep + r) * hd
            o_ref[0, :, c0:c0 + hd] = out[r * tq:(r + 1) * tq].astype(o_ref.dtype)


def _dsa_core(q, qi, wi, k, v, kidx, pos0, topk, *, tq, tk=512):
    B, T, qd = q.shape
    S = k.shape[1]
    hd = qd // A_HEADS
    gw = qd // A_KV_HEADS
    Sp = _round_up(S, tk)
    qh, ql = _split_bf16(qi)
    nb = T // tq
    qi3 = jnp.concatenate([qh, ql, qh, jnp.zeros_like(qh)], axis=-1)
    qi3 = qi3.reshape(B, nb, tq, IDX_HEADS, IDX_PACK).transpose(0, 1, 3, 2, 4)
    qi3 = qi3.reshape(B, nb, IDX_HEADS * tq, IDX_PACK)
    kh, kl = _split_bf16(kidx)
    kidx3 = jnp.concatenate([kh, kh, kl, jnp.zeros_like(kh)], axis=-1)
    pad = ((0, 0), (0, Sp - S), (0, 0))
    kidx3 = jnp.pad(kidx3, pad)
    kb = jnp.pad(k.astype(jnp.bfloat16), pad)
    vb = jnp.pad(v.astype(jnp.bfloat16), pad)
    rows = (gw // hd) * tq
    kvw = A_KV_HEADS * hd
    return pl.pallas_call(
        functools.partial(_dsa_kernel, pos0=pos0, topk=topk, tk=tk),
        grid=(B, nb),
        in_specs=[pl.BlockSpec((1, 1, IDX_HEADS * tq, IDX_PACK), lambda b, i: (b, i, 0, 0)),
                  pl.BlockSpec((1, tq, IDX_HEADS), lambda b, i: (b, i, 0)),
                  pl.BlockSpec((1, Sp, IDX_PACK), lambda b, i: (b, 0, 0)),
                  pl.BlockSpec((1, tq, qd), lambda b, i: (b, i, 0)),
                  pl.BlockSpec((1, Sp, kvw), lambda b, i: (b, 0, 0)),
                  pl.BlockSpec((1, Sp, kvw), lambda b, i: (b, 0, 0))],
        out_specs=pl.BlockSpec((1, tq, qd), lambda b, i: (b, i, 0)),
        out_shape=jax.ShapeDtypeStruct((B, T, qd), jnp.bfloat16),
        scratch_shapes=[pltpu.VMEM((tq, Sp), jnp.int32),
                        pltpu.VMEM((tq, Sp), jnp.float32),
                        pltpu.VMEM((A_KV_HEADS, rows, hd), jnp.bfloat16),
                        pltpu.VMEM((A_KV_HEADS, rows, LANE), jnp.float32),
                        pltpu.VMEM((A_KV_HEADS, rows, LANE), jnp.float32),
                        pltpu.VMEM((A_KV_HEADS, rows, hd), jnp.float32)],
        compiler_params=pltpu.CompilerParams(
            dimension_semantics=("parallel", "arbitrary"),
            vmem_limit_bytes=V7X_VMEM_LIMIT_BYTES),
        name="dsa_core",
    )(qi3, wi, kidx3, (q * (hd ** -0.5 * LOG2_E)).astype(jnp.bfloat16), kb, vb)


def _dsa_layer(x, mod, gate, sid, streams, w_in, w_out):
    D = x.shape[1]
    hd = D // A_HEADS
    q_dim, kv_dim = A_HEADS * hd, A_KV_HEADS * hd
    n_idx = IDX_HEADS * IDX_DIM + IDX_DIM + IDX_HEADS
    w_main = _weight_parts(w_in[:, :q_dim + 2 * kv_dim], 1)
    w_idx = jnp.pad(w_in[:, q_dim + 2 * kv_dim:], ((0, 0), (0, _round_up(n_idx, LANE) - n_idx)))
    q_all = pmatmul(x, w_main, col0=0, ncols=q_dim, mod=mod, sid=sid)
    kv_all = pmatmul(x, w_main, col0=q_dim, ncols=2 * kv_dim, mod=mod, sid=sid)
    idx_all = pmatmul(x, _weight_parts(w_idx, 3), tn=w_idx.shape[1], mod=mod, sid=sid)
    outs, caches = [], []
    for row0, B, T, pos0, past_k, past_v, past_kidx in streams:
        rows = slice(row0, row0 + B * T)
        o, k, v, ki = _dsa_stream(q_all[rows].reshape(B, T, q_dim), kv_all[rows].reshape(B, T, 2 * kv_dim),
                                  idx_all[rows].reshape(B, T, -1), pos0, past_k, past_v, past_kidx)
        outs.append(o.reshape(B * T, q_dim))
        caches.append((k, v, ki))
    x = pmatmul(jnp.concatenate(outs, axis=0), _weight_parts(w_out, 1), res=(x, gate), sid=sid)
    return x, caches


def _dsa_stream(q, kv, idx, pos0, past_k, past_v, past_kidx):
    B, T, q_dim = q.shape
    kv_dim = kv.shape[2] // 2
    hd = q_dim // A_HEADS
    pos = pos0 + jnp.arange(T, dtype=jnp.int32)
    k, v = kv[..., :kv_dim], kv[..., kv_dim:]
    qi = idx[..., :IDX_HEADS * IDX_DIM]
    ki = idx[..., IDX_HEADS * IDX_DIM:IDX_HEADS * IDX_DIM + IDX_DIM]
    wi = idx[..., IDX_HEADS * IDX_DIM + IDX_DIM:IDX_HEADS * IDX_DIM + IDX_DIM + IDX_HEADS]
    q = _rope_partial(q.reshape(B, T, A_HEADS, hd), pos)
    k = _rope_partial(k.reshape(B, T, A_KV_HEADS, hd), pos)
    v = v.reshape(B, T, A_KV_HEADS, hd)
    qi = _rope_partial(qi.reshape(B, T, IDX_HEADS, IDX_DIM), pos)
    ki = _rope_partial(ki[:, :, None, :], pos)[:, :, 0, :]
    wi = wi * (IDX_HEADS ** -0.5 * IDX_DIM ** -0.5)
    if past_k is None:
        k_all, v_all, ki_all = k, v, ki
    else:
        k_all = jnp.concatenate([past_k, k], axis=1)
        v_all = jnp.concatenate([past_v, v], axis=1)
        ki_all = jnp.concatenate([past_kidx, ki], axis=1)
    n_keys = k_all.shape[1]
    topk = min(TOPK_MAX, n_keys // 4)
    o = _dsa_core(q.reshape(B, T, q_dim), qi, wi, k_all.reshape(B, n_keys, kv_dim),
                  v_all.reshape(B, n_keys, kv_dim), ki_all, pos0, topk, tq=min(Q_BLOCK, T))
    return o, k, v, ki


GDN_HEAD_GROUP = 8


def _bf16_dot(a, b):
    return jnp.dot(a.astype(jnp.bfloat16), b.astype(jnp.bfloat16), preferred_element_type=jnp.float32)


def _dot3(a, b):
    ah, al = _split_bf16(a)
    bh, bl = _split_bf16(b)
    out = jnp.dot(ah, bh, preferred_element_type=jnp.float32)
    out = out + jnp.dot(ah, bl, preferred_element_type=jnp.float32)
    return out + jnp.dot(al, bh, preferred_element_type=jnp.float32)


def _conv_silu(x_ref, w_ref, xe_ref):
    C = x_ref.shape[1]
    taps = w_ref.shape[0]
    xe_ref[SUBLANE:, :] = x_ref[0]
    first = SUBLANE - (taps - 1)
    acc = xe_ref[first:first + C, :] * w_ref[0:1, :]
    for j in range(1, taps):
        acc = acc + xe_ref[first + j:first + j + C, :] * w_ref[j:j + 1, :]
    xe_ref[:SUBLANE, :] = xe_ref[C:, :]
    return acc * jax.nn.sigmoid(acc)


def _gdn_kernel(xq_ref, xk_ref, xv_ref, wq_ref, wk_ref, wv_ref, cq_ref, ck_ref, cv_ref,
                z_ref, g_ref, gt_ref, beta_ref, nw_ref, s0_ref,
                o_ref, s_out_ref, s_ref, eq_ref, ek_ref, ev_ref):
    n = pl.program_id(2)
    C = xq_ref.shape[1]
    hg = g_ref.shape[3]
    dk = s_ref.shape[1]
    dv = s_ref.shape[2]
    rep = hg // (xk_ref.shape[2] // dk)

    @pl.when(n == 0)
    def _():
        s_ref[...] = s0_ref[0]
        for e_ref, c_ref in ((eq_ref, cq_ref), (ek_ref, ck_ref), (ev_ref, cv_ref)):
            e_ref[:SUBLANE, :] = jnp.zeros((SUBLANE, e_ref.shape[1]), jnp.float32)
            e_ref[SUBLANE - c_ref.shape[1]:SUBLANE, :] = c_ref[0]

    qc = _conv_silu(xq_ref, wq_ref, eq_ref)
    kc = _conv_silu(xk_ref, wk_ref, ek_ref)
    vc = _conv_silu(xv_ref, wv_ref, ev_ref)
    q_heads, k_heads = [], []
    for i in range(hg // rep):
        qh = qc[:, i * dk:(i + 1) * dk]
        kh = kc[:, i * dk:(i + 1) * dk]
        q_heads.append(qh * (lax.rsqrt(jnp.sum(qh * qh, axis=1, keepdims=True) + EPS) * dk ** -0.5))
        k_heads.append(kh * lax.rsqrt(jnp.sum(kh * kh, axis=1, keepdims=True) + EPS))

    ri = lax.broadcasted_iota(jnp.int32, (C, C), 0)
    ci = lax.broadcasted_iota(jnp.int32, (C, C), 1)
    causal = ri >= ci
    strict = ri > ci
    eye = jnp.where(ri == ci, 1.0, 0.0)
    g = g_ref[0, 0]
    gc_all = _dot3(jnp.where(causal, 1.0, 0.0), g)
    gr_all = _dot3(gt_ref[0, 0, 0], jnp.where(ri <= ci, 1.0, 0.0))
    beta = beta_ref[0, 0]
    nw = nw_ref[...]

    heads = range(hg)
    qs = [q_heads[h // rep] for h in heads]
    ks = [k_heads[h // rep] for h in heads]
    gcs = [gc_all[:, h:h + 1] for h in heads]
    bcols = [beta[:, h:h + 1] for h in heads]
    decays = [jnp.where(causal, jnp.exp(jnp.where(causal, gcs[h] - gr_all[h:h + 1, :], 0.0)), 0.0)
              for h in heads]
    kbs = [ks[h] * bcols[h] for h in heads]
    kks = [lax.dot_general(kbs[h].astype(jnp.bfloat16), ks[h].astype(jnp.bfloat16), _NT_DIMS,
                           preferred_element_type=jnp.float32) for h in heads]
    bms = [jnp.where(strict, -(kks[h] * decays[h]), 0.0) for h in heads]
    egs = [jnp.exp(gcs[h]) for h in heads]
    rhss = [jnp.concatenate([vc[:, h * dv:(h + 1) * dv] * bcols[h], kbs[h] * egs[h]], axis=1)
            for h in heads]
    ps = [eye + bms[h] for h in heads]
    step = 2
    while step < C:
        bms = [_dot3(bms[h], bms[h]) for h in heads]
        ps = [ps[h] + _dot3(ps[h], bms[h]) for h in heads]
        step *= 2
    ws = [_dot3(ps[h], rhss[h]) for h in heads]
    qks = [lax.dot_general(qs[h].astype(jnp.bfloat16), ks[h].astype(jnp.bfloat16), _NT_DIMS,
                           preferred_element_type=jnp.float32) * decays[h] for h in heads]
    g_lasts = [gcs[h][C - 1:C, :] for h in heads]
    ss = [s_ref[h] for h in heads]
    us = [ws[h][:, :dv] - _bf16_dot(ws[h][:, dv:], ss[h]) for h in heads]
    os_ = [_bf16_dot(qs[h] * egs[h], ss[h]) + _bf16_dot(qks[h], us[h]) for h in heads]
    for h in heads:
        ke = ks[h] * jnp.exp(g_lasts[h] - gcs[h])
        s_ref[h] = ss[h] * jnp.exp(g_lasts[h]) + _bf16_dot(ke.T, us[h])
    for h in heads:
        o = os_[h]
        o = o * lax.rsqrt(jnp.mean(o * o, axis=1, keepdims=True) + EPS) * nw
        z = z_ref[0, :, h * dv:(h + 1) * dv]
        o_ref[0, :, h * dv:(h + 1) * dv] = (o * (z * jax.nn.sigmoid(z))).astype(o_ref.dtype)

    @pl.when(n == pl.num_programs(2) - 1)
    def _():
        s_out_ref[0] = s_ref[...]


def _gdn_core(qkv, conv_w, conv_state, z, g, beta, norm_w, s0):
    B, T, vd = z.shape
    VH = g.shape[2]
    dv = vd // VH
    dk = s0.shape[2]
    qd = (qkv.shape[2] - vd) // 2
    C = min(CHUNK, T)
    N = T // C
    hg = GDN_HEAD_GROUP
    ng = VH // hg
    qw = qd // ng
    vw = hg * dv
    assert qd % qw == 0 and (2 * qd) % vw == 0
    k0, v0 = qd // qw, (2 * qd) // vw
    taps = conv_w.shape[0]

    def grouped(a):
        return a.reshape(B, T, ng, hg).transpose(0, 2, 1, 3)

    gg, bg = grouped(g), grouped(beta)
    gt = gg.reshape(B, ng, N, C, hg).transpose(0, 1, 2, 4, 3)
    small = pl.BlockSpec((1, 1, C, hg), lambda b, j, n: (b, j, n, 0))
    o, s = pl.pallas_call(
        _gdn_kernel,
        grid=(B, ng, N),
        in_specs=[pl.BlockSpec((1, C, qw), lambda b, j, n: (b, n, j)),
                  pl.BlockSpec((1, C, qw), lambda b, j, n: (b, n, k0 + j)),
                  pl.BlockSpec((1, C, vw), lambda b, j, n: (b, n, v0 + j)),
                  pl.BlockSpec((taps, qw), lambda b, j, n: (0, j)),
                  pl.BlockSpec((taps, qw), lambda b, j, n: (0, k0 + j)),
                  pl.BlockSpec((taps, vw), lambda b, j, n: (0, v0 + j)),
                  pl.BlockSpec((1, taps - 1, qw), lambda b, j, n: (b, 0, j)),
                  pl.BlockSpec((1, taps - 1, qw), lambda b, j, n: (b, 0, k0 + j)),
                  pl.BlockSpec((1, taps - 1, vw), lambda b, j, n: (b, 0, v0 + j)),
                  pl.BlockSpec((1, C, vw), lambda b, j, n: (b, n, j)),
                  small,
                  pl.BlockSpec((1, 1, 1, hg, C), lambda b, j, n: (b, j, n, 0, 0)),
                  small,
                  pl.BlockSpec((1, dv), lambda b, j, n: (0, 0)),
                  pl.BlockSpec((1, hg, dk, dv), lambda b, j, n: (b, j, 0, 0))],
        out_specs=[pl.BlockSpec((1, C, hg * dv), lambda b, j, n: (b, n, j)),
                   pl.BlockSpec((1, hg, dk, dv), lambda b, j, n: (b, j, 0, 0))],
        out_shape=[jax.ShapeDtypeStruct((B, T, vd), jnp.bfloat16),
                   jax.ShapeDtypeStruct(s0.shape, jnp.float32)],
        scratch_shapes=[pltpu.VMEM((hg, dk, dv), jnp.float32),
                        pltpu.VMEM((SUBLANE + C, qw), jnp.float32),
                        pltpu.VMEM((SUBLANE + C, qw), jnp.float32),
                        pltpu.VMEM((SUBLANE + C, vw), jnp.float32)],
        compiler_params=pltpu.CompilerParams(
            dimension_semantics=("parallel", "parallel", "arbitrary"),
            vmem_limit_bytes=V7X_VMEM_LIMIT_BYTES),
        name="gdn_core",
    )(qkv, qkv, qkv, conv_w, conv_w, conv_w, conv_state, conv_state, conv_state,
      z, gg, gt, bg, norm_w.reshape(1, dv), s0)
    return o, s


def _gdn_layer(x, mod, gate, sid, streams, w_in, conv_w, a_log, dt_bias, norm_w, w_out):
    qk_dim, v_dim = GDN_QK_HEADS * GDN_DK, GDN_V_HEADS * GDN_DV
    conv_dim = 2 * qk_dim + v_dim
    n_gate = 2 * GDN_V_HEADS
    w_main = _weight_parts(w_in[:, :conv_dim + v_dim], 1)
    w_gate = jnp.pad(w_in[:, conv_dim + v_dim:], ((0, 0), (0, _round_up(n_gate, LANE) - n_gate)))
    qkv_all = pmatmul(x, w_main, col0=0, ncols=conv_dim, mod=mod, sid=sid)
    z_all = pmatmul(x, w_main, col0=conv_dim, ncols=v_dim, mod=mod, sid=sid)
    gates_all = pmatmul(x, _weight_parts(w_gate, 1), tn=w_gate.shape[1], mod=mod, sid=sid)
    outs, states = [], []
    for row0, B, T, conv_state, ssm_state in streams:
        rows = slice(row0, row0 + B * T)
        qkv = qkv_all[rows].reshape(B, T, conv_dim)
        gates = gates_all[rows].reshape(B, T, -1)
        beta_raw, a_raw = gates[..., :GDN_V_HEADS], gates[..., GDN_V_HEADS:n_gate]
        if conv_state is None:
            conv_state = jnp.zeros((B, CONV_W - 1, conv_dim), x.dtype)
        if ssm_state is None:
            ssm_state = jnp.zeros((B, GDN_V_HEADS, GDN_DK, GDN_DV), jnp.float32)
        new_conv = jnp.concatenate([conv_state, qkv[:, T - (CONV_W - 1):]], axis=1)[:, -(CONV_W - 1):]
        beta = jax.nn.sigmoid(beta_raw)
        g = -jnp.exp(a_log) * jax.nn.softplus(a_raw + dt_bias)
        o, S = _gdn_core(qkv, conv_w, conv_state, z_all[rows].reshape(B, T, v_dim), g, beta, norm_w, ssm_state)
        outs.append(o.reshape(B * T, v_dim))
        states.append((new_conv, S))
    x = pmatmul(jnp.concatenate(outs, axis=0), _weight_parts(w_out, 1), res=(x, gate), sid=sid)
    return x, states


NEG_INF = float("-inf")
POS_INF = float("inf")
_PEER_CAND_ROWS = tuple((i, PEER_TOPK // (i + 1)) for i in range(PEER_TOPK // 2))


def _split_bf16(x):
    hi = x.astype(jnp.bfloat16)
    lo = (x - hi.astype(jnp.float32)).astype(jnp.bfloat16)
    return hi, lo


def _dot3_nt(a, b):
    dn = (((1,), (1,)), ((), ()))
    ah, al = _split_bf16(a)
    bh, bl = _split_bf16(b)
    out = lax.dot_general(ah, bh, dn, preferred_element_type=jnp.float32)
    out = out + lax.dot_general(ah, bl, dn, preferred_element_type=jnp.float32)
    return out + lax.dot_general(al, bh, dn, preferred_element_type=jnp.float32)


def _top_rows_desc(s, n, with_rank=False):
    rows = []
    cur = s
    rank = jnp.full(s.shape, float(n), jnp.float32)
    for i in range(n):
        m = jnp.max(cur, axis=0, keepdims=True)
        rows.append(m)
        hit = cur == m
        if with_rank:
            rank = jnp.where(hit, float(i), rank)
        cur = jnp.where(hit, NEG_INF, cur)
    return (rows, rank) if with_rank else rows


def _stack_rows(rows, lanes):
    n = len(rows)
    rid = lax.broadcasted_iota(jnp.int32, (n, lanes), 0)
    out = jnp.zeros((n, lanes), jnp.float32)
    for i, r in enumerate(rows):
        out = jnp.where(rid == i, r, out)
    return out


def _peer_select_kernel(q_ref, keys_ref, rk2_ref, e2_ref, cnt_ref, e1_ref):
    tm = q_ref.shape[0]
    k = PEER_TOPK
    s1 = _dot3_nt(keys_ref[0], q_ref[:, :PEER_HALF])
    s2 = _dot3_nt(keys_ref[1], q_ref[:, PEER_HALF:])
    r1 = _top_rows_desc(s1, k)
    r2, rank2 = _top_rows_desc(s2, k, with_rank=True)
    v1 = _stack_rows(r1, tm)
    v2 = _stack_rows(r2, tm)
    v2h = v2[:k // 2]
    rid = lax.broadcasted_iota(jnp.int32, (k // 2, tm), 0)
    pieces = [r1[0] + v2]
    for i, n in _PEER_CAND_ROWS[1:]:
        pieces.append(jnp.where(rid < n, r1[i] + v2h, NEG_INF))
    pieces.append(v1[k // 2:] + r2[0])
    cand = jnp.concatenate(pieces, axis=0)
    tau = _top_rows_desc(cand, k)[-1]
    top = r1[0] + r2[0]
    z = jnp.sum(jnp.where(cand >= tau, jnp.exp(cand - top), 0.0), axis=0, keepdims=True)
    cnt = jnp.zeros(s1.shape, jnp.float32)
    for j in range(k):
        cnt = cnt + jnp.where(s1 + r2[j] >= tau, 1.0, 0.0)
    rk2_ref[0] = rank2.astype(rk2_ref.dtype)
    e2_ref[0] = (jnp.exp(s2 - r2[0]) / z).astype(e2_ref.dtype)
    cnt_ref[0] = cnt
    e1_ref[0] = jnp.exp(s1 - r1[0])


def _peer_select(q, keys, *, tm):
    M = q.shape[0]
    nk = keys.shape[1]
    ospec = pl.BlockSpec((1, nk, tm), lambda i, h: (h, 0, i))
    return pl.pallas_call(
        _peer_select_kernel,
        grid=(M // tm, PEER_HEADS),
        in_specs=[pl.BlockSpec((tm, PEER_KEY_DIM), lambda i, h: (i, h)),
                  pl.BlockSpec(keys.shape, lambda i, h: (0, 0, 0))],
        out_specs=[ospec] * 4,
        out_shape=[jax.ShapeDtypeStruct((PEER_HEADS, nk, M), dt)
                   for dt in (jnp.bfloat16, jnp.bfloat16, jnp.float32, jnp.float32)],
        compiler_params=pltpu.CompilerParams(
            dimension_semantics=("parallel", "arbitrary"),
            vmem_limit_bytes=V7X_VMEM_LIMIT_BYTES),
        name="peer_select",
    )(q, keys)


BF16_SUBLANES = 16
PEER_SUB_EXPERTS = 512


def _peer_main_kernel(sid_ref, x_ref, gain_ref, shift_ref, scale_ref, gate_ref, fnorm_ref,
                      u_ref, vT_ref, rk2_ref, e2_ref, cnt_ref, e1_ref, *refs, final_norm):
    out_refs, (hT_ref, coef_ref, acc_ref) = refs[:-3], refs[-3:]
    e = pl.program_id(1)
    te, tm = coef_ref.shape
    nk = rk2_ref.shape[1]
    rows = BF16_SUBLANES
    sub = PEER_SUB_EXPERTS
    blk0 = pl.program_id(0) * (tm // SEQ_BLOCK)

    @pl.when(e == 0)
    def _():
        acc_ref[...] = jnp.zeros_like(acc_ref)
        per_lane_tile = LANE // SEQ_BLOCK
        for t in range(tm // LANE):
            hs = []
            for r in range(t * per_lane_tile, (t + 1) * per_lane_tile):
                rs = slice(r * SEQ_BLOCK, (r + 1) * SEQ_BLOCK)
                hs.append(_modulated_rows(x_ref[rs, :], gain_ref[...], shift_ref, scale_ref, sid_ref[blk0 + r]))
            hT_ref[:, t * LANE:(t + 1) * LANE] = jnp.concatenate(hs, axis=0).T.astype(jnp.bfloat16)

    def expert_acts(sb):
        return jnp.dot(u_ref[sb * sub:(sb + 1) * sub, :], hT_ref[...], preferred_element_type=jnp.float32)

    def weigh(sb, act):
        for al in range(sub // nk):
            a = sb * (sub // nk) + al
            for lg in range(tm // LANE):
                lanes = slice(lg * LANE, (lg + 1) * LANE)
                cnts = [jnp.broadcast_to(cnt_ref[h, a:a + 1, lanes].astype(jnp.bfloat16), (rows, LANE))
                        for h in range(PEER_HEADS)]
                e1s = [jnp.broadcast_to(e1_ref[h, a:a + 1, lanes].astype(jnp.bfloat16), (rows, LANE))
                       for h in range(PEER_HEADS)]
                for r in range(nk // rows):
                    rs = slice(r * rows, (r + 1) * rows)
                    w = None
                    for h in range(PEER_HEADS):
                        e2 = e2_ref[h, rs, lanes]
                        t = jnp.where(rk2_ref[h, rs, lanes] < cnts[h], e2, jnp.zeros_like(e2)) * e1s[h]
                        w = t if w is None else w + t
                    x = act[al * nk + r * rows:al * nk + (r + 1) * rows, lanes]
                    g = 0.5 * x * (1.0 + lax.erf(x * (2.0 ** -0.5)))
                    coef_ref[sb * sub + al * nk + r * rows:sb * sub + al * nk + (r + 1) * rows, lanes] = (
                        w * g.astype(jnp.bfloat16))

    def accumulate(sb):
        acc_ref[...] += jnp.dot(vT_ref[:, sb * sub:(sb + 1) * sub], coef_ref[sb * sub:(sb + 1) * sub, :],
                                preferred_element_type=jnp.float32)

    n_sub = te // sub
    act = expert_acts(0)
    for sb in range(n_sub):
        nxt = expert_acts(sb + 1) if sb + 1 < n_sub else None
        weigh(sb, act)
        accumulate(sb)
        act = nxt

    @pl.when(e == pl.num_programs(1) - 1)
    def _():
        out = acc_ref[...].T
        for r in range(tm // SEQ_BLOCK):
            rs = slice(r * SEQ_BLOCK, (r + 1) * SEQ_BLOCK)
            y = x_ref[rs, :] + gate_ref[pl.ds(sid_ref[blk0 + r], 1), :] * out[rs, :]
            out_refs[0][rs, :] = y
            if final_norm:
                out_refs[1][rs, :] = y * lax.rsqrt(jnp.mean(y * y, axis=-1, keepdims=True) + EPS) * fnorm_ref[...]


def _peer_main(x, mod, gate, fnorm, sid, u, vT, rk2, e2, cnt, e1, *, tm, te, final_norm):
    M, D = x.shape
    E = u.shape[0]
    nk = rk2.shape[1]
    const = lambda a: pl.BlockSpec(a.shape, lambda i, e, s: (0,) * a.ndim)
    col_spec = pl.BlockSpec((PEER_HEADS, nk, tm), lambda i, e, s: (0, 0, i))
    row_spec = pl.BlockSpec((PEER_HEADS, te // nk, tm), lambda i, e, s: (0, e, i))
    tok_spec = pl.BlockSpec((tm, D), lambda i, e, s: (i, 0))
    n_out = 2 if final_norm else 1
    return pl.pallas_call(
        functools.partial(_peer_main_kernel, final_norm=final_norm),
        grid_spec=pltpu.PrefetchScalarGridSpec(
            num_scalar_prefetch=1,
            grid=(M // tm, E // te),
            in_specs=[tok_spec, const(mod[0]), const(mod[1]), const(mod[2]), const(gate), const(fnorm),
                      pl.BlockSpec((te, D), lambda i, e, s: (e, 0)),
                      pl.BlockSpec((D, te), lambda i, e, s: (0, e)),
                      col_spec, col_spec, row_spec, row_spec],
            out_specs=[tok_spec] * n_out,
            scratch_shapes=[pltpu.VMEM((D, tm), jnp.bfloat16),
                            pltpu.VMEM((te, tm), jnp.bfloat16),
                            pltpu.VMEM((D, tm), jnp.float32)]),
        out_shape=[jax.ShapeDtypeStruct((M, D), jnp.float32)] * n_out,
        compiler_params=pltpu.CompilerParams(
            dimension_semantics=("parallel", "arbitrary"),
            vmem_limit_bytes=V7X_VMEM_LIMIT_PEER_BYTES),
        name="peer_main",
    )(sid, x, *mod, gate, fnorm, u, vT, rk2, e2, cnt, e1)


def _peer(x, mod, gate, fnorm, sid, w_query, sub_keys, expert_u, expert_v, *, final_norm,
          tm_sel=256, tm=512, te=1024):
    M = x.shape[0]
    q = pmatmul(x, _weight_parts(w_query, 3), mod=mod, sid=sid)
    rk2, e2, cnt, e1 = _peer_select(q, sub_keys, tm=min(tm_sel, M))
    u = expert_u.astype(jnp.bfloat16)
    vT = expert_v.T.astype(jnp.bfloat16)
    return _peer_main(x, mod, gate, fnorm, sid, u, vT, rk2, e2, cnt, e1, tm=min(tm, M), te=te,
                      final_norm=final_norm)


def kernel(x_prompt, x_sample, c_prompt, c_sample, cache_k_l0, cache_v_l0, cache_kidx_l0, state_conv_l1, state_ssm_l1, norm1_l0, norm2_l0, ada_w_l0, ada_b_l0, attn_in_l0, attn_out_l0, peer_query_l0, peer_keys_l0, peer_u_l0, peer_v_l0, norm1_l1, norm2_l1, ada_w_l1, ada_b_l1, gdn_in_l1, gdn_conv_l1, gdn_a_log_l1, gdn_dt_bias_l1, gdn_norm_l1, gdn_out_l1, peer_query_l1, peer_keys_l1, peer_u_l1, peer_v_l1, final_norm):
    past_len = cache_k_l0.shape[1]
    norm1 = (norm1_l0, norm1_l1)
    norm2 = (norm2_l0, norm2_l1)
    ada_w = (ada_w_l0, ada_w_l1)
    ada_b = (ada_b_l0, ada_b_l1)
    peer_query = (peer_query_l0, peer_query_l1)
    peer_keys = (peer_keys_l0, peer_keys_l1)
    peer_u = (peer_u_l0, peer_u_l1)
    peer_v = (peer_v_l0, peer_v_l1)
    Bp, Tp, D = x_prompt.shape
    Bs, Ts, _ = x_sample.shape
    n_p, n_s = Bp * Tp, Bs * Ts
    x = jnp.concatenate([x_prompt.reshape(n_p, D), x_sample.reshape(n_s, D)], axis=0)
    sid = jnp.concatenate([jnp.repeat(jnp.arange(Bp, dtype=jnp.int32), Tp // SEQ_BLOCK),
                           Bp + jnp.repeat(jnp.arange(Bs, dtype=jnp.int32), Ts // SEQ_BLOCK)])
    c_all = jnp.concatenate([c_prompt, c_sample], axis=0)
    fnorm = final_norm.reshape(1, D)
    for i in range(2):
        mods = _adaln(c_all, ada_w[i], ada_b[i])
        mod1 = (norm1[i].reshape(1, D), mods[0], mods[1])
        if i == 0:
            x, ((nkp, nvp, nkip), (nks, nvs, nkis)) = _dsa_layer(
                x, mod1, mods[2], sid,
                [(0, Bp, Tp, 0, None, None, None), (n_p, Bs, Ts, past_len, cache_k_l0, cache_v_l0, cache_kidx_l0)],
                attn_in_l0, attn_out_l0)
        else:
            x, ((ncp, nsp), (ncs, nss)) = _gdn_layer(
                x, mod1, mods[2], sid, [(0, Bp, Tp, None, None), (n_p, Bs, Ts, state_conv_l1, state_ssm_l1)],
                gdn_in_l1, gdn_conv_l1, gdn_a_log_l1, gdn_dt_bias_l1, gdn_norm_l1, gdn_out_l1)
        mod2 = (norm2[i].reshape(1, D), mods[3], mods[4])
        outs = _peer(x, mod2, mods[5], fnorm, sid, peer_query[i], peer_keys[i], peer_u[i], peer_v[i],
                     final_norm=(i == 1))
        x = outs[0]
    y = outs[1]
    y_prompt = y[:n_p].reshape(Bp, Tp, D)
    y_sample = y[n_p:].reshape(Bs, Ts, D)
    return (y_prompt, y_sample, nkp, nvp, nkip, nks, nvs, nkis, ncp, nsp, ncs, nss)
```

```python
import functools

import jax
import jax.numpy as jnp
from jax import lax
from jax.experimental import pallas as pl
from jax.experimental.pallas import tpu as pltpu

CHUNK = 64
EPS = 1e-6
ROPE_THETA = 500000.0
ROPE_FRACTION = 4
A_HEADS = 16
A_KV_HEADS = 4
IDX_HEADS = 8
IDX_DIM = 64
TOPK_MAX = 256
Q_BLOCK = 128
GDN_QK_HEADS = 16
GDN_V_HEADS = 32
GDN_DK = 128
GDN_DV = 128
CONV_W = 4
PEER_HEADS = 8
PEER_NKEYS = 128
PEER_KEY_DIM = 256
PEER_HALF = PEER_KEY_DIM // 2
PEER_TOPK = 16
PEER_BLOCK = 64

V7X_VMEM_LIMIT_BYTES = 48 * 1024 * 1024
V7X_VMEM_LIMIT_PEER_BYTES = 56 * 1024 * 1024
LANE = 128
SUBLANE = 8


def _round_up(n, m):
    return (n + m - 1) // m * m


SEQ_BLOCK = CHUNK


def _modulated_rows(x, gain, shift_ref, scale_ref, sid):
    y = x * lax.rsqrt(jnp.mean(x * x, axis=-1, keepdims=True) + EPS)
    return y * gain * (1.0 + scale_ref[pl.ds(sid, 1), :]) + shift_ref[pl.ds(sid, 1), :]


def _matmul_kernel(sid_ref, x_ref, *refs, passes, has_mod, has_res):
    refs = list(refs)
    mod_refs = [refs.pop(0) for _ in range(3)] if has_mod else None
    nparts = 1 + passes // 2
    w_refs = [refs.pop(0) for _ in range(nparts)]
    res_refs = [refs.pop(0) for _ in range(2)] if has_res else None
    o_ref, x_parts = refs[0], refs[1:]
    tm = x_ref.shape[0]
    blocks = range(tm // SEQ_BLOCK) if (has_mod or has_res) else ()
    blk0 = pl.program_id(0) * (tm // SEQ_BLOCK)

    @pl.when(pl.program_id(1) == 0)
    def _():
        def put(rows, x):
            hi = x.astype(jnp.bfloat16)
            x_parts[0][rows, :] = hi
            if passes == 3:
                x_parts[1][rows, :] = (x - hi.astype(jnp.float32)).astype(jnp.bfloat16)

        if has_mod:
            gain_ref, shift_ref, scale_ref = mod_refs
            for r in blocks:
                rows = slice(r * SEQ_BLOCK, (r + 1) * SEQ_BLOCK)
                put(rows, _modulated_rows(x_ref[rows, :], gain_ref[...], shift_ref, scale_ref, sid_ref[blk0 + r]))
        else:
            put(slice(None), x_ref[...].astype(jnp.float32))

    acc = jnp.dot(x_parts[0][...], w_refs[0][...], preferred_element_type=jnp.float32)
    if passes == 3:
        acc = acc + jnp.dot(x_parts[0][...], w_refs[1][...], preferred_element_type=jnp.float32)
        acc = acc + jnp.dot(x_parts[1][...], w_refs[0][...], preferred_element_type=jnp.float32)
    if has_res:
        res_ref, gate_ref = res_refs
        for r in blocks:
            rows = slice(r * SEQ_BLOCK, (r + 1) * SEQ_BLOCK)
            gate = gate_ref[pl.ds(sid_ref[blk0 + r], 1), :]
            o_ref[rows, :] = res_ref[rows, :] + gate * acc[rows, :]
    else:
        o_ref[...] = acc.astype(o_ref.dtype)


def _weight_parts(w, passes):
    hi = w.astype(jnp.bfloat16)
    if passes == 1:
        return (hi,)
    return (hi, (w - hi.astype(jnp.float32)).astype(jnp.bfloat16))


def pmatmul(x, w_parts, *, col0=0, ncols=None, tm=512, tn=1024, out_dtype=jnp.float32,
            mod=None, res=None, sid=None):
    passes = 1 if len(w_parts) == 1 else 3
    M, K = x.shape
    n_total = w_parts[0].shape[1]
    ncols = n_total - col0 if ncols is None else ncols
    tm = min(tm, _round_up(M, 2 * SUBLANE))
    tn = min(tn, ncols)
    assert ncols % tn == 0 and col0 % tn == 0 and (tn % LANE == 0 or tn == n_total)
    Mp = _round_up(M, tm)
    if mod is not None or res is not None:
        assert Mp == M and tm % SEQ_BLOCK == 0 and sid is not None
    else:
        sid = jnp.zeros((1,), jnp.int32)
    if Mp != M:
        x = jnp.pad(x, ((0, Mp - M), (0, 0)))
    c0 = col0 // tn
    in_specs = [pl.BlockSpec((tm, K), lambda i, j, s: (i, 0))]
    args = [x]
    if mod is not None:
        in_specs += [pl.BlockSpec(a.shape, lambda i, j, s: (0, 0)) for a in mod]
        args += list(mod)
    in_specs += [pl.BlockSpec((K, tn), lambda i, j, s: (0, c0 + j))] * len(w_parts)
    args += list(w_parts)
    if res is not None:
        in_specs += [pl.BlockSpec((tm, tn), lambda i, j, s: (i, j)),
                     pl.BlockSpec((res[1].shape[0], tn), lambda i, j, s: (0, j))]
        args += list(res)
        out_dtype = jnp.float32
    out = pl.pallas_call(
        functools.partial(_matmul_kernel, passes=passes, has_mod=mod is not None, has_res=res is not None),
        grid_spec=pltpu.PrefetchScalarGridSpec(
            num_scalar_prefetch=1,
            grid=(Mp // tm, ncols // tn),
            in_specs=in_specs,
            out_specs=pl.BlockSpec((tm, tn), lambda i, j, s: (i, j)),
            scratch_shapes=[pltpu.VMEM((tm, K), jnp.bfloat16)] * (1 + passes // 2)),
        out_shape=jax.ShapeDtypeStruct((Mp, ncols), out_dtype),
        compiler_params=pltpu.CompilerParams(
            dimension_semantics=("parallel", "arbitrary"),
            vmem_limit_bytes=V7X_VMEM_LIMIT_BYTES),
        name="matmul",
    )(sid, *args)
    return out[:M]


def _adaln_kernel(c_ref, w_ref, b_ref, o_ref):
    c = c_ref[...]
    o_ref[...] = _dot3(c * jax.nn.sigmoid(c), w_ref[...]) + b_ref[...]


def _adaln(c, w, b, *, tn=512):
    n, D = c.shape
    N = w.shape[1]
    rows = _round_up(n, SUBLANE)
    mod = pl.pallas_call(
        _adaln_kernel,
        grid=(N // tn,),
        in_specs=[pl.BlockSpec((rows, D), lambda j: (0, 0)),
                  pl.BlockSpec((D, tn), lambda j: (0, j)),
                  pl.BlockSpec((1, tn), lambda j: (0, j))],
        out_specs=pl.BlockSpec((rows, tn), lambda j: (0, j)),
        out_shape=jax.ShapeDtypeStruct((rows, N), jnp.float32),
        compiler_params=pltpu.CompilerParams(
            dimension_semantics=("parallel",), vmem_limit_bytes=V7X_VMEM_LIMIT_BYTES),
        name="adaln",
    )(jnp.pad(c, ((0, rows - n), (0, 0))), w, b.reshape(1, N))
    return jnp.split(mod, 6, axis=-1)


_NT_DIMS = (((1,), (1,)), ((), ()))
INT32_MIN = -2 ** 31
LOG2_E = 1.4426950408889634
_NEG_INF_KEY = -2139095041
IDX_PACK = 4 * IDX_DIM


def _ordered_key(x):
    bits = pltpu.bitcast(x, jnp.int32)
    return bits ^ ((bits >> 31) & 0x7FFFFFFF)


def _lane_tile_sum(x, width=LANE):
    out = x[:, :width]
    for c in range(1, x.shape[1] // width):
        out = out + x[:, c * width:(c + 1) * width]
    return out


def _dsa_select_bias(qi_ref, wi_ref, kidx_ref, key_ref, bias_ref, *, first, n_tiles, topk, tk):
    tq = wi_ref.shape[1]
    row = lax.broadcasted_iota(jnp.int32, (tq, 1), 0)
    lim = (((first + row) >> 6) + 1) * CHUNK
    w = wi_ref[0]

    def score_tile(j, c):
        off = pl.multiple_of(j * tk, tk)
        kt = kidx_ref[0, pl.ds(off, tk), :]
        sc = lax.dot_general(qi_ref[0, 0], kt, _NT_DIMS,
                             preferred_element_type=jnp.float32)
        s = jnp.zeros((tq, tk), jnp.float32)
        for h in range(IDX_HEADS):
            s = s + w[:, h:h + 1] * jnp.maximum(sc[h * tq:(h + 1) * tq], 0.0)
        col = off + lax.broadcasted_iota(jnp.int32, (tq, tk), 1)
        s = jnp.where(col < lim, s + 0.0, NEG_INF)
        key_ref[:, pl.ds(off, tk)] = _ordered_key(s)
        return c

    lax.fori_loop(0, n_tiles, score_tile, 0)

    def bit_step(b, thr):
        cand = thr + lax.shift_left(jnp.int32(1), 31 - b)

        def count_tile(j, c):
            off = pl.multiple_of(j * tk, tk)
            ge = jnp.where(key_ref[:, pl.ds(off, tk)] >= cand, 1.0, 0.0)
            return c + _lane_tile_sum(ge)

        c = lax.fori_loop(0, n_tiles, count_tile, jnp.zeros((tq, LANE), jnp.float32))
        cnt = jnp.sum(c, axis=1, keepdims=True)
        return jnp.where(cnt >= topk, cand, thr)

    thr = lax.fori_loop(0, 32, bit_step, jnp.full((tq, 1), INT32_MIN, jnp.int32))
    thr = jnp.maximum(thr, _NEG_INF_KEY + 1)

    def bias_tile(j, c):
        off = pl.multiple_of(j * tk, tk)
        bias_ref[:, pl.ds(off, tk)] = jnp.where(key_ref[:, pl.ds(off, tk)] >= thr, 0.0, NEG_INF)
        return c

    lax.fori_loop(0, n_tiles, bias_tile, 0)


def _dsa_kernel(qi_ref, wi_ref, kidx_ref, q_ref, k_ref, v_ref, o_ref,
                key_ref, bias_ref, qg_ref, m_ref, l_ref, acc_ref, *, pos0, topk, tk):
    i = pl.program_id(1)
    tq = q_ref.shape[1]
    hd = acc_ref.shape[2]
    groups = acc_ref.shape[0]
    rep = q_ref.shape[2] // (groups * hd)
    first = pos0 + i * tq
    n_valid = (((first + tq - 1) >> 6) + 1) * CHUNK
    n_tiles = (n_valid + tk - 1) // tk
    _dsa_select_bias(qi_ref, wi_ref, kidx_ref, key_ref, bias_ref, first=first, n_tiles=n_tiles, topk=topk, tk=tk)

    for g in range(groups):
        for r in range(rep):
            c0 = (g * rep + r) * hd
            qg_ref[g, r * tq:(r + 1) * tq, :] = q_ref[0, :, c0:c0 + hd]
    m_ref[...] = jnp.full(m_ref.shape, NEG_INF, jnp.float32)
    l_ref[...] = jnp.zeros(l_ref.shape, jnp.float32)
    acc_ref[...] = jnp.zeros(acc_ref.shape, jnp.float32)
    lane_reps = tk // LANE

    def att_tile(j, c):
        off = pl.multiple_of(j * tk, tk)
        b = bias_ref[:, pl.ds(off, tk)]
        bias = jnp.concatenate([b] * rep, axis=0)
        for g in range(groups):
            kt = k_ref[0, pl.ds(off, tk), g * hd:(g + 1) * hd]
            vt = v_ref[0, pl.ds(off, tk), g * hd:(g + 1) * hd]
            lg = lax.dot_general(qg_ref[g], kt, _NT_DIMS, preferred_element_type=jnp.float32) + bias
            m_old = m_ref[g]
            m_new = jnp.maximum(m_old, jnp.max(lg, axis=1, keepdims=True))
            m_safe = jnp.where(m_new == NEG_INF, 0.0, m_new)
            p = jnp.exp2(lg - jnp.tile(m_safe, (1, lane_reps)))
            alpha = jnp.exp2(m_old - m_safe)
            l_ref[g] = alpha * l_ref[g] + jnp.sum(p, axis=1, keepdims=True)
            acc_ref[g] = alpha * acc_ref[g] + jnp.dot(p.astype(jnp.bfloat16), vt,
                                                      preferred_element_type=jnp.float32)
            m_ref[g] = m_new
        return c

    lax.fori_loop(0, n_tiles, att_tile, 0)
    for g in range(groups):
        out = acc_ref[g] / l_ref[g]
        for r in range(rep):
            c0 = (g * rep + r) * hd
            o_ref[0, :, c0:c0 + hd] = out[r * tq:(r + 1) * tq].astype(o_ref.dtype)


def _dsa_core(q, qi3, wi, kb, vb, kidx3, pos0, topk, *, tq, tk):
    B, T, qd = q.shape
    Sp = kb.shape[1]
    hd = qd // A_HEADS
    gw = qd // A_KV_HEADS
    nb = T // tq
    assert Sp % tk == 0 and hd == LANE
    rows = (gw // hd) * tq
    kvw = A_KV_HEADS * hd
    return pl.pallas_call(
        functools.partial(_dsa_kernel, pos0=pos0, topk=topk, tk=tk),
        grid=(B, nb),
        in_specs=[pl.BlockSpec((1, 1, IDX_HEADS * tq, IDX_PACK), lambda b, i: (b, i, 0, 0)),
                  pl.BlockSpec((1, tq, IDX_HEADS), lambda b, i: (b, i, 0)),
                  pl.BlockSpec((1, Sp, IDX_PACK), lambda b, i: (b, 0, 0)),
                  pl.BlockSpec((1, tq, qd), lambda b, i: (b, i, 0)),
                  pl.BlockSpec((1, Sp, kvw), lambda b, i: (b, 0, 0)),
                  pl.BlockSpec((1, Sp, kvw), lambda b, i: (b, 0, 0))],
        out_specs=pl.BlockSpec((1, tq, qd), lambda b, i: (b, i, 0)),
        out_shape=jax.ShapeDtypeStruct((B, T, qd), jnp.bfloat16),
        scratch_shapes=[pltpu.VMEM((tq, Sp), jnp.int32),
                        pltpu.VMEM((tq, Sp), jnp.float32),
                        pltpu.VMEM((A_KV_HEADS, rows, hd), jnp.bfloat16),
                        pltpu.VMEM((A_KV_HEADS, rows, LANE), jnp.float32),
                        pltpu.VMEM((A_KV_HEADS, rows, LANE), jnp.float32),
                        pltpu.VMEM((A_KV_HEADS, rows, hd), jnp.float32)],
        compiler_params=pltpu.CompilerParams(
            dimension_semantics=("parallel", "arbitrary"),
            vmem_limit_bytes=V7X_VMEM_LIMIT_BYTES),
        name="dsa_core",
    )(qi3, wi, kidx3, q, kb, vb)


def _rope_tables(pos, width, period):
    half = period // ROPE_FRACTION // 2
    inv_freq = ROPE_THETA ** (-jnp.arange(half, dtype=jnp.float32) / half)
    ang = pos.astype(jnp.float32)[:, None] * inv_freq[None, :]
    cos, sin = jnp.cos(ang), jnp.sin(ang)
    T = pos.shape[0]
    rest = period - 2 * half
    c = jnp.concatenate([cos, cos, jnp.ones((T, rest), jnp.float32)], axis=1)
    s_next = jnp.concatenate([-sin, jnp.zeros((T, period - half), jnp.float32)], axis=1)
    s_prev = jnp.concatenate([jnp.zeros((T, half), jnp.float32), sin, jnp.zeros((T, rest), jnp.float32)], axis=1)
    return jnp.stack([jnp.tile(t, (1, width // period)) for t in (c, s_next, s_prev)])


def _rope_lanes(x, tab_ref, half):
    return (x * tab_ref[0] + pltpu.roll(x, LANE - half, 1) * tab_ref[1] + pltpu.roll(x, half, 1) * tab_ref[2])


def _dsa_prep_kernel(q_ref, kv_ref, idx_ref, tq_ref, ti_ref,
                     qa_ref, kn_ref, vn_ref, kin_ref, kb_ref, vb_ref, qi3_ref, kidx3_ref, wi_ref, *, q_scale):
    tq = q_ref.shape[0]
    hd = LANE
    kvw = kv_ref.shape[1] // 2
    half_q = hd // ROPE_FRACTION // 2
    half_i = IDX_DIM // ROPE_FRACTION // 2
    left = lax.broadcasted_iota(jnp.int32, (tq, LANE), 1) < IDX_DIM
    zero = jnp.zeros((tq, LANE), jnp.float32)

    for h in range(q_ref.shape[1] // hd):
        cols = slice(h * hd, (h + 1) * hd)
        qa_ref[0, :, cols] = (_rope_lanes(q_ref[:, cols], tq_ref, half_q) * q_scale).astype(qa_ref.dtype)
    for h in range(kvw // hd):
        cols = slice(h * hd, (h + 1) * hd)
        k = _rope_lanes(kv_ref[:, cols], tq_ref, half_q)
        kn_ref[0, :, cols] = k
        kb_ref[0, :, cols] = k.astype(kb_ref.dtype)
    v = kv_ref[:, kvw:]
    vn_ref[0] = v
    vb_ref[0] = v.astype(vb_ref.dtype)

    def hi_lo(x):
        hi = x.astype(jnp.bfloat16).astype(jnp.float32)
        return hi, x - hi

    for t in range(IDX_HEADS * IDX_DIM // LANE):
        hi, lo = hi_lo(_rope_lanes(idx_ref[:, t * LANE:(t + 1) * LANE], ti_ref, half_i))
        hi_sw, lo_sw = pltpu.roll(hi, IDX_DIM, 1), pltpu.roll(lo, IDX_DIM, 1)
        even = jnp.concatenate([jnp.where(left, hi, lo_sw), jnp.where(left, hi, zero)], axis=1)
        odd = jnp.concatenate([jnp.where(left, hi_sw, lo), jnp.where(left, hi_sw, zero)], axis=1)
        qi3_ref[0, 0, (2 * t) * tq:(2 * t + 1) * tq, :] = even.astype(qi3_ref.dtype)
        qi3_ref[0, 0, (2 * t + 1) * tq:(2 * t + 2) * tq, :] = odd.astype(qi3_ref.dtype)
    c0 = IDX_HEADS * IDX_DIM
    x = idx_ref[:, c0:c0 + LANE]
    r = jnp.where(left, _rope_lanes(x, ti_ref, half_i), x)
    kin_ref[0] = r[:, :IDX_DIM]
    hi, lo = hi_lo(r)
    kidx3 = jnp.concatenate([jnp.where(left, hi, pltpu.roll(hi, IDX_DIM, 1)), jnp.where(left, lo, zero)], axis=1)
    kidx3_ref[0] = kidx3.astype(kidx3_ref.dtype)
    wi_ref[0] = x[:, IDX_DIM:IDX_DIM + IDX_HEADS] * (IDX_HEADS ** -0.5 * IDX_DIM ** -0.5)


def _dsa_prep(q_all, kv_all, idx_all, row0, B, T, pos0, *, tq):
    qd, kv2 = q_all.shape[1], kv_all.shape[1]
    kvw = kv2 // 2
    nb = T // tq
    blk0 = row0 // tq
    assert row0 % tq == 0 and qd // A_HEADS == LANE
    pos = pos0 + jnp.arange(T, dtype=jnp.int32)
    tab_q = _rope_tables(pos, LANE, LANE)
    tab_i = _rope_tables(pos, LANE, IDX_DIM)
    row = lambda w: pl.BlockSpec((tq, w), lambda b, i: (blk0 + b * nb + i, 0))
    tab = pl.BlockSpec((3, tq, LANE), lambda b, i: (0, i, 0))
    out = lambda w: pl.BlockSpec((1, tq, w), lambda b, i: (b, i, 0))
    f32, bf16 = jnp.float32, jnp.bfloat16
    shapes = [((B, T, qd), bf16), ((B, T, kvw), f32), ((B, T, kvw), f32), ((B, T, IDX_DIM), f32),
              ((B, T, kvw), bf16), ((B, T, kvw), bf16), ((B, nb, IDX_HEADS * tq, IDX_PACK), bf16),
              ((B, T, IDX_PACK), bf16), ((B, T, IDX_HEADS), f32)]
    return pl.pallas_call(
        functools.partial(_dsa_prep_kernel, q_scale=LANE ** -0.5 * LOG2_E),
        grid=(B, nb),
        in_specs=[row(qd), row(kv2), row(idx_all.shape[1]), tab, tab],
        out_specs=[out(qd), out(kvw), out(kvw), out(IDX_DIM), out(kvw), out(kvw),
                   pl.BlockSpec((1, 1, IDX_HEADS * tq, IDX_PACK), lambda b, i: (b, i, 0, 0)),
                   out(IDX_PACK), out(IDX_HEADS)],
        out_shape=[jax.ShapeDtypeStruct(s, d) for s, d in shapes],
        compiler_params=pltpu.CompilerParams(
            dimension_semantics=("parallel", "parallel"), vmem_limit_bytes=V7X_VMEM_LIMIT_BYTES),
        name="dsa_prep",
    )(q_all, kv_all, idx_all, tab_q, tab_i)


def _dsa_layer(x, mod, gate, sid, streams, w_in, w_out):
    D = x.shape[1]
    hd = D // A_HEADS
    q_dim, kv_dim = A_HEADS * hd, A_KV_HEADS * hd
    n_idx = IDX_HEADS * IDX_DIM + IDX_DIM + IDX_HEADS
    w_main = _weight_parts(w_in[:, :q_dim + 2 * kv_dim], 1)
    w_idx = jnp.pad(w_in[:, q_dim + 2 * kv_dim:], ((0, 0), (0, _round_up(n_idx, LANE) - n_idx)))
    q_all = pmatmul(x, w_main, col0=0, ncols=q_dim, mod=mod, sid=sid)
    kv_all = pmatmul(x, w_main, col0=q_dim, ncols=2 * kv_dim, mod=mod, sid=sid)
    idx_all = pmatmul(x, _weight_parts(w_idx, 3), tn=w_idx.shape[1], mod=mod, sid=sid)
    outs, caches = [], []
    for row0, B, T, pos0, past_k, past_v, past_kidx in streams:
        o, k, v, ki = _dsa_stream(q_all, kv_all, idx_all, row0, B, T, pos0, past_k, past_v, past_kidx)
        outs.append(o.reshape(B * T, q_dim))
        caches.append((k, v, ki))
    x = pmatmul(jnp.concatenate(outs, axis=0), _weight_parts(w_out, 1), res=(x, gate), sid=sid)
    return x, caches


DSA_KEY_TILE = 512


def _dsa_stream(q_all, kv_all, idx_all, row0, B, T, pos0, past_k, past_v, past_kidx):
    tq = min(Q_BLOCK, T)
    q, k, v, ki, kb, vb, qi3, kidx3, wi = _dsa_prep(q_all, kv_all, idx_all, row0, B, T, pos0, tq=tq)
    n_keys = T
    if past_k is not None:
        P = past_k.shape[1]
        n_keys = P + T
        kb = jnp.concatenate([past_k.reshape(B, P, -1).astype(jnp.bfloat16), kb], axis=1)
        vb = jnp.concatenate([past_v.reshape(B, P, -1).astype(jnp.bfloat16), vb], axis=1)
        ph, pl_ = _split_bf16(past_kidx)
        kidx3 = jnp.concatenate([jnp.concatenate([ph, ph, pl_, jnp.zeros_like(ph)], axis=-1), kidx3], axis=1)
    pad = ((0, 0), (0, _round_up(n_keys, DSA_KEY_TILE) - n_keys), (0, 0))
    kb, vb, kidx3 = (jnp.pad(a, pad) for a in (kb, vb, kidx3))
    topk = min(TOPK_MAX, n_keys // 4)
    o = _dsa_core(q, qi3, wi, kb, vb, kidx3, pos0, topk, tq=tq, tk=DSA_KEY_TILE)
    hd = q.shape[2] // A_HEADS
    return o, k.reshape(B, T, A_KV_HEADS, hd), v.reshape(B, T, A_KV_HEADS, hd), ki


GDN_HEAD_GROUP = 8


def _bf16_dot(a, b):
    return jnp.dot(a.astype(jnp.bfloat16), b.astype(jnp.bfloat16), preferred_element_type=jnp.float32)


def _dot3(a, b):
    ah, al = _split_bf16(a)
    bh, bl = _split_bf16(b)
    out = jnp.dot(ah, bh, preferred_element_type=jnp.float32)
    out = out + jnp.dot(ah, bl, preferred_element_type=jnp.float32)
    return out + jnp.dot(al, bh, preferred_element_type=jnp.float32)


def _conv_silu(x_ref, w_ref, xe_ref):
    C = x_ref.shape[0]
    taps = w_ref.shape[0]
    xe_ref[SUBLANE:, :] = x_ref[...]
    first = SUBLANE - (taps - 1)
    acc = xe_ref[first:first + C, :] * w_ref[0:1, :]
    for j in range(1, taps):
        acc = acc + xe_ref[first + j:first + j + C, :] * w_ref[j:j + 1, :]
    xe_ref[:SUBLANE, :] = xe_ref[C:, :]
    return acc * jax.nn.sigmoid(acc)


def _gdn_kernel(xq_ref, xk_ref, xv_ref, wq_ref, wk_ref, wv_ref, cq_ref, ck_ref, cv_ref,
                z_ref, g_ref, gt_ref, beta_ref, nw_ref, s0_ref,
                o_ref, s_out_ref, s_ref, eq_ref, ek_ref, ev_ref):
    n = pl.program_id(2)
    C = xq_ref.shape[0]
    hg = g_ref.shape[3]
    dk = s_ref.shape[1]
    dv = s_ref.shape[2]
    rep = hg // (xk_ref.shape[1] // dk)

    @pl.when(n == 0)
    def _():
        s_ref[...] = s0_ref[0]
        for e_ref, c_ref in ((eq_ref, cq_ref), (ek_ref, ck_ref), (ev_ref, cv_ref)):
            e_ref[:SUBLANE, :] = jnp.zeros((SUBLANE, e_ref.shape[1]), jnp.float32)
            e_ref[SUBLANE - c_ref.shape[1]:SUBLANE, :] = c_ref[0]

    qc = _conv_silu(xq_ref, wq_ref, eq_ref)
    kc = _conv_silu(xk_ref, wk_ref, ek_ref)
    vc = _conv_silu(xv_ref, wv_ref, ev_ref)
    q_heads, k_heads = [], []
    for i in range(hg // rep):
        qh = qc[:, i * dk:(i + 1) * dk]
        kh = kc[:, i * dk:(i + 1) * dk]
        q_heads.append(qh * (lax.rsqrt(jnp.sum(qh * qh, axis=1, keepdims=True) + EPS) * dk ** -0.5))
        k_heads.append(kh * lax.rsqrt(jnp.sum(kh * kh, axis=1, keepdims=True) + EPS))

    ri = lax.broadcasted_iota(jnp.int32, (C, C), 0)
    ci = lax.broadcasted_iota(jnp.int32, (C, C), 1)
    causal = ri >= ci
    strict = ri > ci
    eye = jnp.where(ri == ci, 1.0, 0.0)
    g = g_ref[0, 0]
    gc_all = _dot3(jnp.where(causal, 1.0, 0.0), g)
    gr_all = _dot3(gt_ref[0, 0, 0], jnp.where(ri <= ci, 1.0, 0.0))
    beta = beta_ref[0, 0]
    nw = nw_ref[...]

    heads = range(hg)
    qs = [q_heads[h // rep] for h in heads]
    ks = [k_heads[h // rep] for h in heads]
    gcs = [gc_all[:, h:h + 1] for h in heads]
    bcols = [beta[:, h:h + 1] for h in heads]
    decays = [jnp.where(causal, jnp.exp(jnp.where(causal, gcs[h] - gr_all[h:h + 1, :], 0.0)), 0.0)
              for h in heads]
    kbs = [ks[h] * bcols[h] for h in heads]
    kks = [lax.dot_general(kbs[h].astype(jnp.bfloat16), ks[h].astype(jnp.bfloat16), _NT_DIMS,
                           preferred_element_type=jnp.float32) for h in heads]
    bms = [jnp.where(strict, -(kks[h] * decays[h]), 0.0) for h in heads]
    egs = [jnp.exp(gcs[h]) for h in heads]
    rhss = [jnp.concatenate([vc[:, h * dv:(h + 1) * dv] * bcols[h], kbs[h] * egs[h]], axis=1)
            for h in heads]
    ps = [eye + bms[h] for h in heads]
    step = 2
    while step < C:
        bms = [_dot3(bms[h], bms[h]) for h in heads]
        ps = [ps[h] + _dot3(ps[h], bms[h]) for h in heads]
        step *= 2
    ws = [_dot3(ps[h], rhss[h]) for h in heads]
    qks = [lax.dot_general(qs[h].astype(jnp.bfloat16), ks[h].astype(jnp.bfloat16), _NT_DIMS,
                           preferred_element_type=jnp.float32) * decays[h] for h in heads]
    g_lasts = [gcs[h][C - 1:C, :] for h in heads]
    ss = [s_ref[h] for h in heads]
    us = [ws[h][:, :dv] - _bf16_dot(ws[h][:, dv:], ss[h]) for h in heads]
    os_ = [_bf16_dot(qs[h] * egs[h], ss[h]) + _bf16_dot(qks[h], us[h]) for h in heads]
    for h in heads:
        ke = ks[h] * jnp.exp(g_lasts[h] - gcs[h])
        s_ref[h] = ss[h] * jnp.exp(g_lasts[h]) + _bf16_dot(ke.T, us[h])
    for h in heads:
        o = os_[h]
        o = o * lax.rsqrt(jnp.mean(o * o, axis=1, keepdims=True) + EPS) * nw
        z = z_ref[:, h * dv:(h + 1) * dv]
        o_ref[0, :, h * dv:(h + 1) * dv] = (o * (z * jax.nn.sigmoid(z))).astype(o_ref.dtype)

    @pl.when(n == pl.num_programs(2) - 1)
    def _():
        s_out_ref[0] = s_ref[...]


def _gdn_core(qkv, z, row0, conv_w, conv_state, g, beta, norm_w, s0):
    B, T, VH = g.shape
    vd = z.shape[1]
    dv = vd // VH
    dk = s0.shape[2]
    qd = (qkv.shape[1] - vd) // 2
    C = min(CHUNK, T)
    N = T // C
    assert row0 % C == 0
    blk0 = row0 // C
    hg = GDN_HEAD_GROUP
    ng = VH // hg
    qw = qd // ng
    vw = hg * dv
    assert qd % qw == 0 and (2 * qd) % vw == 0
    k0, v0 = qd // qw, (2 * qd) // vw
    taps = conv_w.shape[0]

    def grouped(a):
        return a.reshape(B, T, ng, hg).transpose(0, 2, 1, 3)

    gg, bg = grouped(g), grouped(beta)
    gt = gg.reshape(B, ng, N, C, hg).transpose(0, 1, 2, 4, 3)
    small = pl.BlockSpec((1, 1, C, hg), lambda b, j, n: (b, j, n, 0))
    o, s = pl.pallas_call(
        _gdn_kernel,
        grid=(B, ng, N),
        in_specs=[pl.BlockSpec((C, qw), lambda b, j, n: (blk0 + b * N + n, j)),
                  pl.BlockSpec((C, qw), lambda b, j, n: (blk0 + b * N + n, k0 + j)),
                  pl.BlockSpec((C, vw), lambda b, j, n: (blk0 + b * N + n, v0 + j)),
                  pl.BlockSpec((taps, qw), lambda b, j, n: (0, j)),
                  pl.BlockSpec((taps, qw), lambda b, j, n: (0, k0 + j)),
                  pl.BlockSpec((taps, vw), lambda b, j, n: (0, v0 + j)),
                  pl.BlockSpec((1, taps - 1, qw), lambda b, j, n: (b, 0, j)),
                  pl.BlockSpec((1, taps - 1, qw), lambda b, j, n: (b, 0, k0 + j)),
                  pl.BlockSpec((1, taps - 1, vw), lambda b, j, n: (b, 0, v0 + j)),
                  pl.BlockSpec((C, vw), lambda b, j, n: (blk0 + b * N + n, j)),
                  small,
                  pl.BlockSpec((1, 1, 1, hg, C), lambda b, j, n: (b, j, n, 0, 0)),
                  small,
                  pl.BlockSpec((1, dv), lambda b, j, n: (0, 0)),
                  pl.BlockSpec((1, hg, dk, dv), lambda b, j, n: (b, j, 0, 0))],
        out_specs=[pl.BlockSpec((1, C, hg * dv), lambda b, j, n: (b, n, j)),
                   pl.BlockSpec((1, hg, dk, dv), lambda b, j, n: (b, j, 0, 0))],
        out_shape=[jax.ShapeDtypeStruct((B, T, vd), jnp.bfloat16),
                   jax.ShapeDtypeStruct(s0.shape, jnp.float32)],
        scratch_shapes=[pltpu.VMEM((hg, dk, dv), jnp.float32),
                        pltpu.VMEM((SUBLANE + C, qw), jnp.float32),
                        pltpu.VMEM((SUBLANE + C, qw), jnp.float32),
                        pltpu.VMEM((SUBLANE + C, vw), jnp.float32)],
        compiler_params=pltpu.CompilerParams(
            dimension_semantics=("parallel", "parallel", "arbitrary"),
            vmem_limit_bytes=V7X_VMEM_LIMIT_BYTES),
        name="gdn_core",
    )(qkv, qkv, qkv, conv_w, conv_w, conv_w, conv_state, conv_state, conv_state,
      z, gg, gt, bg, norm_w.reshape(1, dv), s0)
    return o, s


def _gdn_layer(x, mod, gate, sid, streams, w_in, conv_w, a_log, dt_bias, norm_w, w_out):
    qk_dim, v_dim = GDN_QK_HEADS * GDN_DK, GDN_V_HEADS * GDN_DV
    conv_dim = 2 * qk_dim + v_dim
    n_gate = 2 * GDN_V_HEADS
    w_main = _weight_parts(w_in[:, :conv_dim + v_dim], 1)
    w_gate = jnp.pad(w_in[:, conv_dim + v_dim:], ((0, 0), (0, _round_up(n_gate, LANE) - n_gate)))
    qkv_all = pmatmul(x, w_main, col0=0, ncols=conv_dim, mod=mod, sid=sid)
    z_all = pmatmul(x, w_main, col0=conv_dim, ncols=v_dim, mod=mod, sid=sid)
    gates_all = pmatmul(x, _weight_parts(w_gate, 1), tn=w_gate.shape[1], mod=mod, sid=sid)
    outs, states = [], []
    for row0, B, T, conv_state, ssm_state in streams:
        rows = slice(row0, row0 + B * T)
        gates = gates_all[rows].reshape(B, T, -1)
        beta_raw, a_raw = gates[..., :GDN_V_HEADS], gates[..., GDN_V_HEADS:n_gate]
        if conv_state is None:
            conv_state = jnp.zeros((B, CONV_W - 1, conv_dim), x.dtype)
        if ssm_state is None:
            ssm_state = jnp.zeros((B, GDN_V_HEADS, GDN_DK, GDN_DV), jnp.float32)
        tail = jnp.stack([qkv_all[row0 + (b + 1) * T - (CONV_W - 1):row0 + (b + 1) * T] for b in range(B)])
        new_conv = jnp.concatenate([conv_state, tail], axis=1)[:, -(CONV_W - 1):]
        beta = jax.nn.sigmoid(beta_raw)
        g = -jnp.exp(a_log) * jax.nn.softplus(a_raw + dt_bias)
        o, S = _gdn_core(qkv_all, z_all, row0, conv_w, conv_state, g, beta, norm_w, ssm_state)
        outs.append(o.reshape(B * T, v_dim))
        states.append((new_conv, S))
    x = pmatmul(jnp.concatenate(outs, axis=0), _weight_parts(w_out, 1), res=(x, gate), sid=sid)
    return x, states


NEG_INF = float("-inf")
POS_INF = float("inf")
_PEER_CAND_ROWS = tuple((i, PEER_TOPK // (i + 1)) for i in range(PEER_TOPK // 2))


def _split_bf16(x):
    hi = x.astype(jnp.bfloat16)
    lo = (x - hi.astype(jnp.float32)).astype(jnp.bfloat16)
    return hi, lo


def _dot3_nt(a, b):
    dn = (((1,), (1,)), ((), ()))
    ah, al = _split_bf16(a)
    bh, bl = _split_bf16(b)
    out = lax.dot_general(ah, bh, dn, preferred_element_type=jnp.float32)
    out = out + lax.dot_general(ah, bl, dn, preferred_element_type=jnp.float32)
    return out + lax.dot_general(al, bh, dn, preferred_element_type=jnp.float32)


def _top_rows_desc(s, n, with_rank=False):
    rows = []
    cur = s
    rank = jnp.full(s.shape, float(n), jnp.float32)
    for i in range(n):
        m = jnp.max(cur, axis=0, keepdims=True)
        rows.append(m)
        hit = cur == m
        if with_rank:
            rank = jnp.where(hit, float(i), rank)
        cur = jnp.where(hit, NEG_INF, cur)
    return (rows, rank) if with_rank else rows


def _stack_rows(rows, lanes):
    n = len(rows)
    rid = lax.broadcasted_iota(jnp.int32, (n, lanes), 0)
    out = jnp.zeros((n, lanes), jnp.float32)
    for i, r in enumerate(rows):
        out = jnp.where(rid == i, r, out)
    return out


def _peer_select_kernel(q_ref, keys_ref, rk2_ref, e2_ref, cnt_ref, e1_ref):
    tm = q_ref.shape[0]
    k = PEER_TOPK
    s1 = _dot3_nt(keys_ref[0], q_ref[:, :PEER_HALF])
    s2 = _dot3_nt(keys_ref[1], q_ref[:, PEER_HALF:])
    r1 = _top_rows_desc(s1, k)
    r2, rank2 = _top_rows_desc(s2, k, with_rank=True)
    v1 = _stack_rows(r1, tm)
    v2 = _stack_rows(r2, tm)
    v2h = v2[:k // 2]
    rid = lax.broadcasted_iota(jnp.int32, (k // 2, tm), 0)
    pieces = [r1[0] + v2]
    for i, n in _PEER_CAND_ROWS[1:]:
        pieces.append(jnp.where(rid < n, r1[i] + v2h, NEG_INF))
    pieces.append(v1[k // 2:] + r2[0])
    cand = jnp.concatenate(pieces, axis=0)
    tau = _top_rows_desc(cand, k)[-1]
    top = r1[0] + r2[0]
    z = jnp.sum(jnp.where(cand >= tau, jnp.exp(cand - top), 0.0), axis=0, keepdims=True)
    cnt = jnp.zeros(s1.shape, jnp.float32)
    for j in range(k):
        cnt = cnt + jnp.where(s1 + r2[j] >= tau, 1.0, 0.0)
    rk2_ref[0] = rank2.astype(rk2_ref.dtype)
    e2_ref[0] = (jnp.exp(s2 - r2[0]) / z).astype(e2_ref.dtype)
    cnt_ref[0] = cnt
    e1_ref[0] = jnp.exp(s1 - r1[0])


def _peer_select(q, keys, *, tm):
    M = q.shape[0]
    nk = keys.shape[1]
    ospec = pl.BlockSpec((1, nk, tm), lambda i, h: (h, 0, i))
    return pl.pallas_call(
        _peer_select_kernel,
        grid=(M // tm, PEER_HEADS),
        in_specs=[pl.BlockSpec((tm, PEER_KEY_DIM), lambda i, h: (i, h)),
                  pl.BlockSpec(keys.shape, lambda i, h: (0, 0, 0))],
        out_specs=[ospec] * 4,
        out_shape=[jax.ShapeDtypeStruct((PEER_HEADS, nk, M), dt)
                   for dt in (jnp.bfloat16, jnp.bfloat16, jnp.float32, jnp.float32)],
        compiler_params=pltpu.CompilerParams(
            dimension_semantics=("parallel", "arbitrary"),
            vmem_limit_bytes=V7X_VMEM_LIMIT_BYTES),
        name="peer_select",
    )(q, keys)


BF16_SUBLANES = 16
PEER_SUB_EXPERTS = 512


def _peer_main_kernel(sid_ref, x_ref, gain_ref, shift_ref, scale_ref, gate_ref, fnorm_ref,
                      u_ref, vT_ref, rk2_ref, e2_ref, cnt_ref, e1_ref, *refs, final_norm):
    out_refs, (hT_ref, coef_ref, acc_ref) = refs[:-3], refs[-3:]
    e = pl.program_id(1)
    te, tm = coef_ref.shape
    nk = rk2_ref.shape[1]
    rows = BF16_SUBLANES
    sub = PEER_SUB_EXPERTS
    blk0 = pl.program_id(0) * (tm // SEQ_BLOCK)

    @pl.when(e == 0)
    def _():
        acc_ref[...] = jnp.zeros_like(acc_ref)
        per_lane_tile = LANE // SEQ_BLOCK
        for t in range(tm // LANE):
            hs = []
            for r in range(t * per_lane_tile, (t + 1) * per_lane_tile):
                rs = slice(r * SEQ_BLOCK, (r + 1) * SEQ_BLOCK)
                hs.append(_modulated_rows(x_ref[rs, :], gain_ref[...], shift_ref, scale_ref, sid_ref[blk0 + r]))
            hT_ref[:, t * LANE:(t + 1) * LANE] = jnp.concatenate(hs, axis=0).T.astype(jnp.bfloat16)

    def expert_acts(sb):
        return jnp.dot(u_ref[sb * sub:(sb + 1) * sub, :], hT_ref[...], preferred_element_type=jnp.float32)

    def weigh(sb, act):
        for al in range(sub // nk):
            a = sb * (sub // nk) + al
            for lg in range(tm // LANE):
                lanes = slice(lg * LANE, (lg + 1) * LANE)
                cnts = [jnp.broadcast_to(cnt_ref[h, a:a + 1, lanes].astype(jnp.bfloat16), (rows, LANE))
                        for h in range(PEER_HEADS)]
                e1s = [jnp.broadcast_to(e1_ref[h, a:a + 1, lanes].astype(jnp.bfloat16), (rows, LANE))
                       for h in range(PEER_HEADS)]
                for r in range(nk // rows):
                    rs = slice(r * rows, (r + 1) * rows)
                    w = None
                    for h in range(PEER_HEADS):
                        e2 = e2_ref[h, rs, lanes]
                        t = jnp.where(rk2_ref[h, rs, lanes] < cnts[h], e2, jnp.zeros_like(e2)) * e1s[h]
                        w = t if w is None else w + t
                    x = act[al * nk + r * rows:al * nk + (r + 1) * rows, lanes]
                    g = 0.5 * x * (1.0 + lax.erf(x * (2.0 ** -0.5)))
                    coef_ref[sb * sub + al * nk + r * rows:sb * sub + al * nk + (r + 1) * rows, lanes] = (
                        w * g.astype(jnp.bfloat16))

    def accumulate(sb):
        acc_ref[...] += jnp.dot(vT_ref[:, sb * sub:(sb + 1) * sub], coef_ref[sb * sub:(sb + 1) * sub, :],
                                preferred_element_type=jnp.float32)

    n_sub = te // sub
    act = expert_acts(0)
    for sb in range(n_sub):
        nxt = expert_acts(sb + 1) if sb + 1 < n_sub else None
        weigh(sb, act)
        accumulate(sb)
        act = nxt

    @pl.when(e == pl.num_programs(1) - 1)
    def _():
        out = acc_ref[...].T
        for r in range(tm // SEQ_BLOCK):
            rs = slice(r * SEQ_BLOCK, (r + 1) * SEQ_BLOCK)
            y = x_ref[rs, :] + gate_ref[pl.ds(sid_ref[blk0 + r], 1), :] * out[rs, :]
            out_refs[0][rs, :] = y
            if final_norm:
                out_refs[1][rs, :] = y * lax.rsqrt(jnp.mean(y * y, axis=-1, keepdims=True) + EPS) * fnorm_ref[...]


def _peer_main(x, mod, gate, fnorm, sid, u, vT, rk2, e2, cnt, e1, *, tm, te, final_norm):
    M, D = x.shape
    E = u.shape[0]
    nk = rk2.shape[1]
    const = lambda a: pl.BlockSpec(a.shape, lambda i, e, s: (0,) * a.ndim)
    col_spec = pl.BlockSpec((PEER_HEADS, nk, tm), lambda i, e, s: (0, 0, i))
    row_spec = pl.BlockSpec((PEER_HEADS, te // nk, tm), lambda i, e, s: (0, e, i))
    tok_spec = pl.BlockSpec((tm, D), lambda i, e, s: (i, 0))
    n_out = 2 if final_norm else 1
    return pl.pallas_call(
        functools.partial(_peer_main_kernel, final_norm=final_norm),
        grid_spec=pltpu.PrefetchScalarGridSpec(
            num_scalar_prefetch=1,
            grid=(M // tm, E // te),
            in_specs=[tok_spec, const(mod[0]), const(mod[1]), const(mod[2]), const(gate), const(fnorm),
                      pl.BlockSpec((te, D), lambda i, e, s: (e, 0)),
                      pl.BlockSpec((D, te), lambda i, e, s: (0, e)),
                      col_spec, col_spec, row_spec, row_spec],
            out_specs=[tok_spec] * n_out,
            scratch_shapes=[pltpu.VMEM((D, tm), jnp.bfloat16),
                            pltpu.VMEM((te, tm), jnp.bfloat16),
                            pltpu.VMEM((D, tm), jnp.float32)]),
        out_shape=[jax.ShapeDtypeStruct((M, D), jnp.float32)] * n_out,
        compiler_params=pltpu.CompilerParams(
            dimension_semantics=("parallel", "arbitrary"),
            vmem_limit_bytes=V7X_VMEM_LIMIT_PEER_BYTES),
        name="peer_main",
    )(sid, x, *mod, gate, fnorm, u, vT, rk2, e2, cnt, e1)


def _peer(x, mod, gate, fnorm, sid, w_query, sub_keys, expert_u, expert_v, *, final_norm,
          tm_sel=256, tm=512, te=1024):
    M = x.shape[0]
    q = pmatmul(x, _weight_parts(w_query, 3), mod=mod, sid=sid)
    rk2, e2, cnt, e1 = _peer_select(q, sub_keys, tm=min(tm_sel, M))
    u = expert_u.astype(jnp.bfloat16)
    vT = expert_v.T.astype(jnp.bfloat16)
    return _peer_main(x, mod, gate, fnorm, sid, u, vT, rk2, e2, cnt, e1, tm=min(tm, M), te=te,
                      final_norm=final_norm)


def kernel(x_prompt, x_sample, c_prompt, c_sample, cache_k_l0, cache_v_l0, cache_kidx_l0, state_conv_l1, state_ssm_l1, norm1_l0, norm2_l0, ada_w_l0, ada_b_l0, attn_in_l0, attn_out_l0, peer_query_l0, peer_keys_l0, peer_u_l0, peer_v_l0, norm1_l1, norm2_l1, ada_w_l1, ada_b_l1, gdn_in_l1, gdn_conv_l1, gdn_a_log_l1, gdn_dt_bias_l1, gdn_norm_l1, gdn_out_l1, peer_query_l1, peer_keys_l1, peer_u_l1, peer_v_l1, final_norm):
    past_len = cache_k_l0.shape[1]
    norm1 = (norm1_l0, norm1_l1)
    norm2 = (norm2_l0, norm2_l1)
    ada_w = (ada_w_l0, ada_w_l1)
    ada_b = (ada_b_l0, ada_b_l1)
    peer_query = (peer_query_l0, peer_query_l1)
    peer_keys = (peer_keys_l0, peer_keys_l1)
    peer_u = (peer_u_l0, peer_u_l1)
    peer_v = (peer_v_l0, peer_v_l1)
    Bp, Tp, D = x_prompt.shape
    Bs, Ts, _ = x_sample.shape
    n_p, n_s = Bp * Tp, Bs * Ts
    x = jnp.concatenate([x_prompt.reshape(n_p, D), x_sample.reshape(n_s, D)], axis=0)
    sid = jnp.concatenate([jnp.repeat(jnp.arange(Bp, dtype=jnp.int32), Tp // SEQ_BLOCK),
                           Bp + jnp.repeat(jnp.arange(Bs, dtype=jnp.int32), Ts // SEQ_BLOCK)])
    c_all = jnp.concatenate([c_prompt, c_sample], axis=0)
    fnorm = final_norm.reshape(1, D)
    for i in range(2):
        mods = _adaln(c_all, ada_w[i], ada_b[i])
        mod1 = (norm1[i].reshape(1, D), mods[0], mods[1])
        if i == 0:
            x, ((nkp, nvp, nkip), (nks, nvs, nkis)) = _dsa_layer(
                x, mod1, mods[2], sid,
                [(0, Bp, Tp, 0, None, None, None), (n_p, Bs, Ts, past_len, cache_k_l0, cache_v_l0, cache_kidx_l0)],
                attn_in_l0, attn_out_l0)
        else:
            x, ((ncp, nsp), (ncs, nss)) = _gdn_layer(
                x, mod1, mods[2], sid, [(0, Bp, Tp, None, None), (n_p, Bs, Ts, state_conv_l1, state_ssm_l1)],
                gdn_in_l1, gdn_conv_l1, gdn_a_log_l1, gdn_dt_bias_l1, gdn_norm_l1, gdn_out_l1)
        mod2 = (norm2[i].reshape(1, D), mods[3], mods[4])
        outs = _peer(x, mod2, mods[5], fnorm, sid, peer_query[i], peer_keys[i], peer_u[i], peer_v[i],
                     final_norm=(i == 1))
        x = outs[0]
    y = outs[1]
    y_prompt = y[:n_p].reshape(Bp, Tp, D)
    y_sample = y[n_p:].reshape(Bs, Ts, D)
    return (y_prompt, y_sample, nkp, nvp, nkip, nks, nvs, nkis, ncp, nsp, ncs, nss)
```

```python
import functools

import jax
import jax.numpy as jnp
from jax import lax
from jax.experimental import pallas as pl
from jax.experimental.pallas import tpu as pltpu

CHUNK = 64
EPS = 1e-6
ROPE_THETA = 500000.0
ROPE_FRACTION = 4
A_HEADS = 16
A_KV_HEADS = 4
IDX_HEADS = 8
IDX_DIM = 64
TOPK_MAX = 256
Q_BLOCK = 128
GDN_QK_HEADS = 16
GDN_V_HEADS = 32
GDN_DK = 128
GDN_DV = 128
CONV_W = 4
PEER_HEADS = 8
PEER_NKEYS = 128
PEER_KEY_DIM = 256
PEER_HALF = PEER_KEY_DIM // 2
PEER_TOPK = 16
PEER_BLOCK = 64

V7X_VMEM_LIMIT_BYTES = 48 * 1024 * 1024
V7X_VMEM_LIMIT_PEER_BYTES = 56 * 1024 * 1024
LANE = 128
SUBLANE = 8


def _round_up(n, m):
    return (n + m - 1) // m * m


SEQ_BLOCK = CHUNK


def _modulated_rows(x, gain, shift_ref, scale_ref, sid):
    y = x * lax.rsqrt(jnp.mean(x * x, axis=-1, keepdims=True) + EPS)
    return y * gain * (1.0 + scale_ref[pl.ds(sid, 1), :]) + shift_ref[pl.ds(sid, 1), :]


def _matmul_kernel(sid_ref, x_ref, *refs, passes, has_mod, has_res):
    refs = list(refs)
    mod_refs = [refs.pop(0) for _ in range(3)] if has_mod else None
    nparts = 1 + passes // 2
    w_refs = [refs.pop(0) for _ in range(nparts)]
    res_refs = [refs.pop(0) for _ in range(2)] if has_res else None
    o_ref, x_parts = refs[0], refs[1:]
    tm = x_ref.shape[0]
    blocks = range(tm // SEQ_BLOCK) if (has_mod or has_res) else ()
    blk0 = pl.program_id(0) * (tm // SEQ_BLOCK)

    @pl.when(pl.program_id(1) == 0)
    def _():
        def put(rows, x):
            hi = x.astype(jnp.bfloat16)
            x_parts[0][rows, :] = hi
            if passes == 3:
                x_parts[1][rows, :] = (x - hi.astype(jnp.float32)).astype(jnp.bfloat16)

        if has_mod:
            gain_ref, shift_ref, scale_ref = mod_refs
            for r in blocks:
                rows = slice(r * SEQ_BLOCK, (r + 1) * SEQ_BLOCK)
                put(rows, _modulated_rows(x_ref[rows, :], gain_ref[...], shift_ref, scale_ref, sid_ref[blk0 + r]))
        else:
            put(slice(None), x_ref[...].astype(jnp.float32))

    acc = jnp.dot(x_parts[0][...], w_refs[0][...], preferred_element_type=jnp.float32)
    if passes == 3:
        acc = acc + jnp.dot(x_parts[0][...], w_refs[1][...], preferred_element_type=jnp.float32)
        acc = acc + jnp.dot(x_parts[1][...], w_refs[0][...], preferred_element_type=jnp.float32)
    if has_res:
        res_ref, gate_ref = res_refs
        for r in blocks:
            rows = slice(r * SEQ_BLOCK, (r + 1) * SEQ_BLOCK)
            gate = gate_ref[pl.ds(sid_ref[blk0 + r], 1), :]
            o_ref[rows, :] = res_ref[rows, :] + gate * acc[rows, :]
    else:
        o_ref[...] = acc.astype(o_ref.dtype)


def _weight_parts(w, passes):
    hi = w.astype(jnp.bfloat16)
    if passes == 1:
        return (hi,)
    return (hi, (w - hi.astype(jnp.float32)).astype(jnp.bfloat16))


def pmatmul(x, w_parts, *, col0=0, ncols=None, tm=512, tn=1024, out_dtype=jnp.float32,
            mod=None, res=None, sid=None):
    passes = 1 if len(w_parts) == 1 else 3
    M, K = x.shape
    n_total = w_parts[0].shape[1]
    ncols = n_total - col0 if ncols is None else ncols
    tm = min(tm, _round_up(M, 2 * SUBLANE))
    tn = min(tn, ncols)
    assert ncols % tn == 0 and col0 % tn == 0 and (tn % LANE == 0 or tn == n_total)
    Mp = _round_up(M, tm)
    if mod is not None or res is not None:
        assert Mp == M and tm % SEQ_BLOCK == 0 and sid is not None
    else:
        sid = jnp.zeros((1,), jnp.int32)
    if Mp != M:
        x = jnp.pad(x, ((0, Mp - M), (0, 0)))
    c0 = col0 // tn
    in_specs = [pl.BlockSpec((tm, K), lambda i, j, s: (i, 0))]
    args = [x]
    if mod is not None:
        in_specs += [pl.BlockSpec(a.shape, lambda i, j, s: (0, 0)) for a in mod]
        args += list(mod)
    in_specs += [pl.BlockSpec((K, tn), lambda i, j, s: (0, c0 + j))] * len(w_parts)
    args += list(w_parts)
    if res is not None:
        in_specs += [pl.BlockSpec((tm, tn), lambda i, j, s: (i, j)),
                     pl.BlockSpec((res[1].shape[0], tn), lambda i, j, s: (0, j))]
        args += list(res)
        out_dtype = jnp.float32
    out = pl.pallas_call(
        functools.partial(_matmul_kernel, passes=passes, has_mod=mod is not None, has_res=res is not None),
        grid_spec=pltpu.PrefetchScalarGridSpec(
            num_scalar_prefetch=1,
            grid=(Mp // tm, ncols // tn),
            in_specs=in_specs,
            out_specs=pl.BlockSpec((tm, tn), lambda i, j, s: (i, j)),
            scratch_shapes=[pltpu.VMEM((tm, K), jnp.bfloat16)] * (1 + passes // 2)),
        out_shape=jax.ShapeDtypeStruct((Mp, ncols), out_dtype),
        compiler_params=pltpu.CompilerParams(
            dimension_semantics=("parallel", "arbitrary"),
            vmem_limit_bytes=V7X_VMEM_LIMIT_BYTES),
        name="matmul",
    )(sid, *args)
    return out[:M]


def _adaln_kernel(c_ref, w_ref, b_ref, o_ref):
    c = c_ref[...]
    o_ref[...] = _dot3(c * jax.nn.sigmoid(c), w_ref[...]) + b_ref[...]


def _adaln(c, w, b, *, tn=512):
    n, D = c.shape
    N = w.shape[1]
    rows = _round_up(n, SUBLANE)
    mod = pl.pallas_call(
        _adaln_kernel,
        grid=(N // tn,),
        in_specs=[pl.BlockSpec((rows, D), lambda j: (0, 0)),
                  pl.BlockSpec((D, tn), lambda j: (0, j)),
                  pl.BlockSpec((1, tn), lambda j: (0, j))],
        out_specs=pl.BlockSpec((rows, tn), lambda j: (0, j)),
        out_shape=jax.ShapeDtypeStruct((rows, N), jnp.float32),
        compiler_params=pltpu.CompilerParams(
            dimension_semantics=("parallel",), vmem_limit_bytes=V7X_VMEM_LIMIT_BYTES),
        name="adaln",
    )(jnp.pad(c, ((0, rows - n), (0, 0))), w, b.reshape(1, N))
    return jnp.split(mod, 6, axis=-1)


_NT_DIMS = (((1,), (1,)), ((), ()))
INT32_MIN = -2 ** 31
LOG2_E = 1.4426950408889634
_NEG_INF_KEY = -2139095041
IDX_PACK = 4 * IDX_DIM


def _ordered_key(x):
    bits = pltpu.bitcast(x, jnp.int32)
    return bits ^ ((bits >> 31) & 0x7FFFFFFF)


def _lane_tile_sum(x, width=LANE):
    out = x[:, :width]
    for c in range(1, x.shape[1] // width):
        out = out + x[:, c * width:(c + 1) * width]
    return out


def _dsa_select_bias(qi_ref, wi_ref, kidx_ref, key_ref, bias_ref, *, first, n_tiles, topk, tk):
    tq = wi_ref.shape[1]
    row = lax.broadcasted_iota(jnp.int32, (tq, 1), 0)
    lim = (((first + row) >> 6) + 1) * CHUNK
    w = wi_ref[0]

    def score_tile(j, c):
        off = pl.multiple_of(j * tk, tk)
        kt = kidx_ref[0, pl.ds(off, tk), :]
        sc = lax.dot_general(qi_ref[0, 0], kt, _NT_DIMS,
                             preferred_element_type=jnp.float32)
        s = jnp.zeros((tq, tk), jnp.float32)
        for h in range(IDX_HEADS):
            s = s + w[:, h:h + 1] * jnp.maximum(sc[h * tq:(h + 1) * tq], 0.0)
        col = off + lax.broadcasted_iota(jnp.int32, (tq, tk), 1)
        s = jnp.where(col < lim, s + 0.0, NEG_INF)
        key_ref[:, pl.ds(off, tk)] = _ordered_key(s)
        return c

    lax.fori_loop(0, n_tiles, score_tile, 0)

    def bit_step(b, thr):
        cand = thr + lax.shift_left(jnp.int32(1), 31 - b)

        def count_tile(j, c):
            off = pl.multiple_of(j * tk, tk)
            ge = jnp.where(key_ref[:, pl.ds(off, tk)] >= cand, 1.0, 0.0)
            return c + _lane_tile_sum(ge)

        c = lax.fori_loop(0, n_tiles, count_tile, jnp.zeros((tq, LANE), jnp.float32))
        cnt = jnp.sum(c, axis=1, keepdims=True)
        return jnp.where(cnt >= topk, cand, thr)

    thr = lax.fori_loop(0, 32, bit_step, jnp.full((tq, 1), INT32_MIN, jnp.int32))
    thr = jnp.maximum(thr, _NEG_INF_KEY + 1)

    def bias_tile(j, c):
        off = pl.multiple_of(j * tk, tk)
        bias_ref[:, pl.ds(off, tk)] = jnp.where(key_ref[:, pl.ds(off, tk)] >= thr, 0.0, NEG_INF)
        return c

    lax.fori_loop(0, n_tiles, bias_tile, 0)


def _dsa_kernel(qi_ref, wi_ref, kidx_ref, q_ref, k_ref, v_ref, o_ref,
                key_ref, bias_ref, qg_ref, m_ref, l_ref, acc_ref, *, pos0, topk, tk):
    i = pl.program_id(1)
    tq = q_ref.shape[1]
    hd = acc_ref.shape[2]
    groups = acc_ref.shape[0]
    rep = q_ref.shape[2] // (groups * hd)
    first = pos0 + i * tq
    n_valid = (((first + tq - 1) >> 6) + 1) * CHUNK
    n_tiles = (n_valid + tk - 1) // tk
    _dsa_select_bias(qi_ref, wi_ref, kidx_ref, key_ref, bias_ref, first=first, n_tiles=n_tiles, topk=topk, tk=tk)

    for g in range(groups):
        for r in range(rep):
            c0 = (g * rep + r) * hd
            qg_ref[g, r * tq:(r + 1) * tq, :] = q_ref[0, :, c0:c0 + hd]
    m_ref[...] = jnp.full(m_ref.shape, NEG_INF, jnp.float32)
    l_ref[...] = jnp.zeros(l_ref.shape, jnp.float32)
    acc_ref[...] = jnp.zeros(acc_ref.shape, jnp.float32)
    lane_reps = tk // LANE

    def att_tile(j, c):
        off = pl.multiple_of(j * tk, tk)
        b = bias_ref[:, pl.ds(off, tk)]
        bias = jnp.concatenate([b] * rep, axis=0)
        for g in range(groups):
            kt = k_ref[0, pl.ds(off, tk), g * hd:(g + 1) * hd]
            vt = v_ref[0, pl.ds(off, tk), g * hd:(g + 1) * hd]
            lg = lax.dot_general(qg_ref[g], kt, _NT_DIMS, preferred_element_type=jnp.float32) + bias
            m_old = m_ref[g]
            m_new = jnp.maximum(m_old, jnp.max(lg, axis=1, keepdims=True))
            m_safe = jnp.where(m_new == NEG_INF, 0.0, m_new)
            p = jnp.exp2(lg - jnp.tile(m_safe, (1, lane_reps)))
            alpha = jnp.exp2(m_old - m_safe)
            l_ref[g] = alpha * l_ref[g] + jnp.sum(p, axis=1, keepdims=True)
            acc_ref[g] = alpha * acc_ref[g] + jnp.dot(p.astype(jnp.bfloat16), vt,
                                                      preferred_element_type=jnp.float32)
            m_ref[g] = m_new
        return c

    lax.fori_loop(0, n_tiles, att_tile, 0)
    for g in range(groups):
        out = acc_ref[g] / l_ref[g]
        for r in range(rep):
            c0 = (g * rep + r) * hd
            o_ref[0, :, c0:c0 + hd] = out[r * tq:(r + 1) * tq].astype(o_ref.dtype)


def _dsa_core(q, qi3, wi, kb, vb, kidx3, pos0, topk, *, tq, tk):
    B, T, qd = q.shape
    Sp = kb.shape[1]
    hd = qd // A_HEADS
    gw = qd // A_KV_HEADS
    nb = T // tq
    assert Sp % tk == 0 and hd == LANE
    rows = (gw // hd) * tq
    kvw = A_KV_HEADS * hd
    return pl.pallas_call(
        functools.partial(_dsa_kernel, pos0=pos0, topk=topk, tk=tk),
        grid=(B, nb),
        in_specs=[pl.BlockSpec((1, 1, IDX_HEADS * tq, IDX_PACK), lambda b, i: (b, i, 0, 0)),
                  pl.BlockSpec((1, tq, IDX_HEADS), lambda b, i: (b, i, 0)),
                  pl.BlockSpec((1, Sp, IDX_PACK), lambda b, i: (b, 0, 0)),
                  pl.BlockSpec((1, tq, qd), lambda b, i: (b, i, 0)),
                  pl.BlockSpec((1, Sp, kvw), lambda b, i: (b, 0, 0)),
                  pl.BlockSpec((1, Sp, kvw), lambda b, i: (b, 0, 0))],
        out_specs=pl.BlockSpec((1, tq, qd), lambda b, i: (b, i, 0)),
        out_shape=jax.ShapeDtypeStruct((B, T, qd), jnp.bfloat16),
        scratch_shapes=[pltpu.VMEM((tq, Sp), jnp.int32),
                        pltpu.VMEM((tq, Sp), jnp.float32),
                        pltpu.VMEM((A_KV_HEADS, rows, hd), jnp.bfloat16),
                        pltpu.VMEM((A_KV_HEADS, rows, LANE), jnp.float32),
                        pltpu.VMEM((A_KV_HEADS, rows, LANE), jnp.float32),
                        pltpu.VMEM((A_KV_HEADS, rows, hd), jnp.float32)],
        compiler_params=pltpu.CompilerParams(
            dimension_semantics=("parallel", "arbitrary"),
            vmem_limit_bytes=V7X_VMEM_LIMIT_BYTES),
        name="dsa_core",
    )(qi3, wi, kidx3, q, kb, vb)


def _rope_tables(pos, width, period):
    half = period // ROPE_FRACTION // 2
    inv_freq = ROPE_THETA ** (-jnp.arange(half, dtype=jnp.float32) / half)
    ang = pos.astype(jnp.float32)[:, None] * inv_freq[None, :]
    cos, sin = jnp.cos(ang), jnp.sin(ang)
    T = pos.shape[0]
    rest = period - 2 * half
    c = jnp.concatenate([cos, cos, jnp.ones((T, rest), jnp.float32)], axis=1)
    s_next = jnp.concatenate([-sin, jnp.zeros((T, period - half), jnp.float32)], axis=1)
    s_prev = jnp.concatenate([jnp.zeros((T, half), jnp.float32), sin, jnp.zeros((T, rest), jnp.float32)], axis=1)
    return jnp.stack([jnp.tile(t, (1, width // period)) for t in (c, s_next, s_prev)])


def _rope_lanes(x, tab_ref, half):
    return (x * tab_ref[0] + pltpu.roll(x, LANE - half, 1) * tab_ref[1] + pltpu.roll(x, half, 1) * tab_ref[2])


def _dsa_prep_kernel(q_ref, kv_ref, idx_ref, tq_ref, ti_ref,
                     qa_ref, kn_ref, vn_ref, kin_ref, kb_ref, vb_ref, qi3_ref, kidx3_ref, wi_ref, *, q_scale):
    tq = q_ref.shape[0]
    hd = LANE
    kvw = kv_ref.shape[1] // 2
    half_q = hd // ROPE_FRACTION // 2
    half_i = IDX_DIM // ROPE_FRACTION // 2
    left = lax.broadcasted_iota(jnp.int32, (tq, LANE), 1) < IDX_DIM
    zero = jnp.zeros((tq, LANE), jnp.float32)

    for h in range(q_ref.shape[1] // hd):
        cols = slice(h * hd, (h + 1) * hd)
        qa_ref[0, :, cols] = (_rope_lanes(q_ref[:, cols], tq_ref, half_q) * q_scale).astype(qa_ref.dtype)
    for h in range(kvw // hd):
        cols = slice(h * hd, (h + 1) * hd)
        k = _rope_lanes(kv_ref[:, cols], tq_ref, half_q)
        kn_ref[0, :, cols] = k
        kb_ref[0, :, cols] = k.astype(kb_ref.dtype)
    v = kv_ref[:, kvw:]
    vn_ref[0] = v
    vb_ref[0] = v.astype(vb_ref.dtype)

    def hi_lo(x):
        hi = x.astype(jnp.bfloat16).astype(jnp.float32)
        return hi, x - hi

    for t in range(IDX_HEADS * IDX_DIM // LANE):
        hi, lo = hi_lo(_rope_lanes(idx_ref[:, t * LANE:(t + 1) * LANE], ti_ref, half_i))
        hi_sw, lo_sw = pltpu.roll(hi, IDX_DIM, 1), pltpu.roll(lo, IDX_DIM, 1)
        even = jnp.concatenate([jnp.where(left, hi, lo_sw), jnp.where(left, hi, zero)], axis=1)
        odd = jnp.concatenate([jnp.where(left, hi_sw, lo), jnp.where(left, hi_sw, zero)], axis=1)
        qi3_ref[0, 0, (2 * t) * tq:(2 * t + 1) * tq, :] = even.astype(qi3_ref.dtype)
        qi3_ref[0, 0, (2 * t + 1) * tq:(2 * t + 2) * tq, :] = odd.astype(qi3_ref.dtype)
    c0 = IDX_HEADS * IDX_DIM
    x = idx_ref[:, c0:c0 + LANE]
    r = jnp.where(left, _rope_lanes(x, ti_ref, half_i), x)
    kin_ref[0] = r[:, :IDX_DIM]
    hi, lo = hi_lo(r)
    kidx3 = jnp.concatenate([jnp.where(left, hi, pltpu.roll(hi, IDX_DIM, 1)), jnp.where(left, lo, zero)], axis=1)
    kidx3_ref[0] = kidx3.astype(kidx3_ref.dtype)
    wi_ref[0] = x[:, IDX_DIM:IDX_DIM + IDX_HEADS] * (IDX_HEADS ** -0.5 * IDX_DIM ** -0.5)


def _dsa_prep(q_all, kv_all, idx_all, row0, B, T, pos0, *, tq):
    qd, kv2 = q_all.shape[1], kv_all.shape[1]
    kvw = kv2 // 2
    nb = T // tq
    blk0 = row0 // tq
    assert row0 % tq == 0 and qd // A_HEADS == LANE
    pos = pos0 + jnp.arange(T, dtype=jnp.int32)
    tab_q = _rope_tables(pos, LANE, LANE)
    tab_i = _rope_tables(pos, LANE, IDX_DIM)
    row = lambda w: pl.BlockSpec((tq, w), lambda b, i: (blk0 + b * nb + i, 0))
    tab = pl.BlockSpec((3, tq, LANE), lambda b, i: (0, i, 0))
    out = lambda w: pl.BlockSpec((1, tq, w), lambda b, i: (b, i, 0))
    f32, bf16 = jnp.float32, jnp.bfloat16
    shapes = [((B, T, qd), bf16), ((B, T, kvw), f32), ((B, T, kvw), f32), ((B, T, IDX_DIM), f32),
              ((B, T, kvw), bf16), ((B, T, kvw), bf16), ((B, nb, IDX_HEADS * tq, IDX_PACK), bf16),
              ((B, T, IDX_PACK), bf16), ((B, T, IDX_HEADS), f32)]
    return pl.pallas_call(
        functools.partial(_dsa_prep_kernel, q_scale=LANE ** -0.5 * LOG2_E),
        grid=(B, nb),
        in_specs=[row(qd), row(kv2), row(idx_all.shape[1]), tab, tab],
        out_specs=[out(qd), out(kvw), out(kvw), out(IDX_DIM), out(kvw), out(kvw),
                   pl.BlockSpec((1, 1, IDX_HEADS * tq, IDX_PACK), lambda b, i: (b, i, 0, 0)),
                   out(IDX_PACK), out(IDX_HEADS)],
        out_shape=[jax.ShapeDtypeStruct(s, d) for s, d in shapes],
        compiler_params=pltpu.CompilerParams(
            dimension_semantics=("parallel", "parallel"), vmem_limit_bytes=V7X_VMEM_LIMIT_BYTES),
        name="dsa_prep",
    )(q_all, kv_all, idx_all, tab_q, tab_i)


def _dsa_layer(x, mod, gate, sid, streams, w_in, w_out):
    D = x.shape[1]
    hd = D // A_HEADS
    q_dim, kv_dim = A_HEADS * hd, A_KV_HEADS * hd
    n_idx = IDX_HEADS * IDX_DIM + IDX_DIM + IDX_HEADS
    w_main = _weight_parts(w_in[:, :q_dim + 2 * kv_dim], 1)
    w_idx = jnp.pad(w_in[:, q_dim + 2 * kv_dim:], ((0, 0), (0, _round_up(n_idx, LANE) - n_idx)))
    q_all = pmatmul(x, w_main, col0=0, ncols=q_dim, mod=mod, sid=sid)
    kv_all = pmatmul(x, w_main, col0=q_dim, ncols=2 * kv_dim, mod=mod, sid=sid)
    idx_all = pmatmul(x, _weight_parts(w_idx, 3), tn=w_idx.shape[1], mod=mod, sid=sid)
    outs, caches = [], []
    for row0, B, T, pos0, past_k, past_v, past_kidx in streams:
        o, k, v, ki = _dsa_stream(q_all, kv_all, idx_all, row0, B, T, pos0, past_k, past_v, past_kidx)
        outs.append(o.reshape(B * T, q_dim))
        caches.append((k, v, ki))
    x = pmatmul(jnp.concatenate(outs, axis=0), _weight_parts(w_out, 1), res=(x, gate), sid=sid)
    return x, caches


DSA_KEY_TILE = 512


def _dsa_stream(q_all, kv_all, idx_all, row0, B, T, pos0, past_k, past_v, past_kidx):
    tq = min(Q_BLOCK, T)
    q, k, v, ki, kb, vb, qi3, kidx3, wi = _dsa_prep(q_all, kv_all, idx_all, row0, B, T, pos0, tq=tq)
    n_keys = T
    if past_k is not None:
        P = past_k.shape[1]
        n_keys = P + T
        kb = jnp.concatenate([past_k.reshape(B, P, -1).astype(jnp.bfloat16), kb], axis=1)
        vb = jnp.concatenate([past_v.reshape(B, P, -1).astype(jnp.bfloat16), vb], axis=1)
        ph, pl_ = _split_bf16(past_kidx)
        kidx3 = jnp.concatenate([jnp.concatenate([ph, ph, pl_, jnp.zeros_like(ph)], axis=-1), kidx3], axis=1)
    pad = ((0, 0), (0, _round_up(n_keys, DSA_KEY_TILE) - n_keys), (0, 0))
    kb, vb, kidx3 = (jnp.pad(a, pad) for a in (kb, vb, kidx3))
    topk = min(TOPK_MAX, n_keys // 4)
    o = _dsa_core(q, qi3, wi, kb, vb, kidx3, pos0, topk, tq=tq, tk=DSA_KEY_TILE)
    hd = q.shape[2] // A_HEADS
    return o, k.reshape(B, T, A_KV_HEADS, hd), v.reshape(B, T, A_KV_HEADS, hd), ki


GDN_HEAD_GROUP = 32


def _bf16_dot(a, b):
    return jnp.dot(a.astype(jnp.bfloat16), b.astype(jnp.bfloat16), preferred_element_type=jnp.float32)


def _dot3(a, b):
    ah, al = _split_bf16(a)
    bh, bl = _split_bf16(b)
    out = jnp.dot(ah, bh, preferred_element_type=jnp.float32)
    out = out + jnp.dot(ah, bl, preferred_element_type=jnp.float32)
    return out + jnp.dot(al, bh, preferred_element_type=jnp.float32)


def _conv_silu(x_ref, w_ref, xe_ref):
    C = x_ref.shape[0]
    taps = w_ref.shape[0]
    xe_ref[SUBLANE:, :] = x_ref[...]
    first = SUBLANE - (taps - 1)
    acc = xe_ref[first:first + C, :] * w_ref[0:1, :]
    for j in range(1, taps):
        acc = acc + xe_ref[first + j:first + j + C, :] * w_ref[j:j + 1, :]
    xe_ref[:SUBLANE, :] = xe_ref[C:, :]
    return acc * jax.nn.sigmoid(acc)


def _gdn_kernel(xq_ref, xk_ref, xv_ref, wq_ref, wk_ref, wv_ref, cq_ref, ck_ref, cv_ref,
                z_ref, g_ref, gt_ref, beta_ref, nw_ref, s0_ref,
                o_ref, s_out_ref, s_ref, eq_ref, ek_ref, ev_ref):
    n = pl.program_id(2)
    C = xq_ref.shape[0]
    hg = g_ref.shape[3]
    dk = s_ref.shape[1]
    dv = s_ref.shape[2]
    rep = hg // (xk_ref.shape[1] // dk)

    @pl.when(n == 0)
    def _():
        s_ref[...] = s0_ref[0]
        for e_ref, c_ref in ((eq_ref, cq_ref), (ek_ref, ck_ref), (ev_ref, cv_ref)):
            e_ref[:SUBLANE, :] = jnp.zeros((SUBLANE, e_ref.shape[1]), jnp.float32)
            e_ref[SUBLANE - c_ref.shape[1]:SUBLANE, :] = c_ref[0]

    qc = _conv_silu(xq_ref, wq_ref, eq_ref)
    kc = _conv_silu(xk_ref, wk_ref, ek_ref)
    vc = _conv_silu(xv_ref, wv_ref, ev_ref)
    q_heads, k_heads = [], []
    for i in range(hg // rep):
        qh = qc[:, i * dk:(i + 1) * dk]
        kh = kc[:, i * dk:(i + 1) * dk]
        q_heads.append(qh * (lax.rsqrt(jnp.sum(qh * qh, axis=1, keepdims=True) + EPS) * dk ** -0.5))
        k_heads.append(kh * lax.rsqrt(jnp.sum(kh * kh, axis=1, keepdims=True) + EPS))

    ri = lax.broadcasted_iota(jnp.int32, (C, C), 0)
    ci = lax.broadcasted_iota(jnp.int32, (C, C), 1)
    causal = ri >= ci
    strict = ri > ci
    eye = jnp.where(ri == ci, 1.0, 0.0)
    g = g_ref[0, 0]
    gc_all = _dot3(jnp.where(causal, 1.0, 0.0), g)
    gr_all = _dot3(gt_ref[0, 0, 0], jnp.where(ri <= ci, 1.0, 0.0))
    beta = beta_ref[0, 0]
    nw = nw_ref[...]

    heads = range(hg)
    qs = [q_heads[h // rep] for h in heads]
    ks = [k_heads[h // rep] for h in heads]
    gcs = [gc_all[:, h:h + 1] for h in heads]
    bcols = [beta[:, h:h + 1] for h in heads]
    decays = [jnp.where(causal, jnp.exp(jnp.where(causal, gcs[h] - gr_all[h:h + 1, :], 0.0)), 0.0)
              for h in heads]
    kbs = [ks[h] * bcols[h] for h in heads]
    kks = [lax.dot_general(kbs[h].astype(jnp.bfloat16), ks[h].astype(jnp.bfloat16), _NT_DIMS,
                           preferred_element_type=jnp.float32) for h in heads]
    bms = [jnp.where(strict, -(kks[h] * decays[h]), 0.0) for h in heads]
    egs = [jnp.exp(gcs[h]) for h in heads]
    rhss = [jnp.concatenate([vc[:, h * dv:(h + 1) * dv] * bcols[h], kbs[h] * egs[h]], axis=1)
            for h in heads]
    pairs = range(hg // 2)
    left = lax.broadcasted_iota(jnp.int32, (C, 2 * C), 1) < C
    zero16 = jnp.zeros((C, 2 * C), jnp.bfloat16)

    def blockdiag(part):
        return jnp.concatenate([jnp.where(left, part, zero16), jnp.where(left, zero16, part)], axis=0)

    def dot3_pairs(x_parts, y_parts):
        xh, xl = x_parts
        yh, yl = blockdiag(y_parts[0]), blockdiag(y_parts[1])
        out = jnp.dot(xh, yh, preferred_element_type=jnp.float32)
        out = out + jnp.dot(xh, yl, preferred_element_type=jnp.float32)
        return out + jnp.dot(xl, yh, preferred_element_type=jnp.float32)

    b2 = [jnp.concatenate([bms[2 * i], bms[2 * i + 1]], axis=1) for i in pairs]
    eye2 = jnp.concatenate([eye, eye], axis=1)
    p2 = [eye2 + b2[i] for i in pairs]
    b2_parts = [_split_bf16(b2[i]) for i in pairs]
    step = 2
    while step < C:
        b2 = [dot3_pairs(b2_parts[i], b2_parts[i]) for i in pairs]
        b2_parts = [_split_bf16(b2[i]) for i in pairs]
        p2 = [p2[i] + dot3_pairs(_split_bf16(p2[i]), b2_parts[i]) for i in pairs]
        step *= 2
    ps = [p2[h // 2][:, (h % 2) * C:(h % 2 + 1) * C] for h in heads]
    ws = [_dot3(ps[h], rhss[h]) for h in heads]
    qks = [lax.dot_general(qs[h].astype(jnp.bfloat16), ks[h].astype(jnp.bfloat16), _NT_DIMS,
                           preferred_element_type=jnp.float32) * decays[h] for h in heads]
    g_lasts = [gcs[h][C - 1:C, :] for h in heads]
    ss = [s_ref[h] for h in heads]
    us = [ws[h][:, :dv] - _bf16_dot(ws[h][:, dv:], ss[h]) for h in heads]
    os_ = [_bf16_dot(qs[h] * egs[h], ss[h]) + _bf16_dot(qks[h], us[h]) for h in heads]
    for h in heads:
        ke = ks[h] * jnp.exp(g_lasts[h] - gcs[h])
        s_ref[h] = ss[h] * jnp.exp(g_lasts[h]) + _bf16_dot(ke.T, us[h])
    for h in heads:
        o = os_[h]
        o = o * lax.rsqrt(jnp.mean(o * o, axis=1, keepdims=True) + EPS) * nw
        z = z_ref[:, h * dv:(h + 1) * dv]
        o_ref[0, :, h * dv:(h + 1) * dv] = (o * (z * jax.nn.sigmoid(z))).astype(o_ref.dtype)

    @pl.when(n == pl.num_programs(2) - 1)
    def _():
        s_out_ref[0] = s_ref[...]


def _gdn_core(qkv, z, row0, conv_w, conv_state, g, beta, norm_w, s0):
    B, T, VH = g.shape
    vd = z.shape[1]
    dv = vd // VH
    dk = s0.shape[2]
    qd = (qkv.shape[1] - vd) // 2
    C = min(CHUNK, T)
    N = T // C
    assert row0 % C == 0
    blk0 = row0 // C
    hg = min(GDN_HEAD_GROUP, VH)
    ng = VH // hg
    qw = qd // ng
    vw = hg * dv
    assert qd % qw == 0 and (2 * qd) % vw == 0
    k0, v0 = qd // qw, (2 * qd) // vw
    taps = conv_w.shape[0]

    def grouped(a):
        return a.reshape(B, T, ng, hg).transpose(0, 2, 1, 3)

    gg, bg = grouped(g), grouped(beta)
    gt = gg.reshape(B, ng, N, C, hg).transpose(0, 1, 2, 4, 3)
    small = pl.BlockSpec((1, 1, C, hg), lambda b, j, n: (b, j, n, 0))
    o, s = pl.pallas_call(
        _gdn_kernel,
        grid=(B, ng, N),
        in_specs=[pl.BlockSpec((C, qw), lambda b, j, n: (blk0 + b * N + n, j)),
                  pl.BlockSpec((C, qw), lambda b, j, n: (blk0 + b * N + n, k0 + j)),
                  pl.BlockSpec((C, vw), lambda b, j, n: (blk0 + b * N + n, v0 + j)),
                  pl.BlockSpec((taps, qw), lambda b, j, n: (0, j)),
                  pl.BlockSpec((taps, qw), lambda b, j, n: (0, k0 + j)),
                  pl.BlockSpec((taps, vw), lambda b, j, n: (0, v0 + j)),
                  pl.BlockSpec((1, taps - 1, qw), lambda b, j, n: (b, 0, j)),
                  pl.BlockSpec((1, taps - 1, qw), lambda b, j, n: (b, 0, k0 + j)),
                  pl.BlockSpec((1, taps - 1, vw), lambda b, j, n: (b, 0, v0 + j)),
                  pl.BlockSpec((C, vw), lambda b, j, n: (blk0 + b * N + n, j)),
                  small,
                  pl.BlockSpec((1, 1, 1, hg, C), lambda b, j, n: (b, j, n, 0, 0)),
                  small,
                  pl.BlockSpec((1, dv), lambda b, j, n: (0, 0)),
                  pl.BlockSpec((1, hg, dk, dv), lambda b, j, n: (b, j, 0, 0))],
        out_specs=[pl.BlockSpec((1, C, hg * dv), lambda b, j, n: (b, n, j)),
                   pl.BlockSpec((1, hg, dk, dv), lambda b, j, n: (b, j, 0, 0))],
        out_shape=[jax.ShapeDtypeStruct((B, T, vd), jnp.bfloat16),
                   jax.ShapeDtypeStruct(s0.shape, jnp.float32)],
        scratch_shapes=[pltpu.VMEM((hg, dk, dv), jnp.float32),
                        pltpu.VMEM((SUBLANE + C, qw), jnp.float32),
                        pltpu.VMEM((SUBLANE + C, qw), jnp.float32),
                        pltpu.VMEM((SUBLANE + C, vw), jnp.float32)],
        compiler_params=pltpu.CompilerParams(
            dimension_semantics=("parallel", "parallel", "arbitrary"),
            vmem_limit_bytes=V7X_VMEM_LIMIT_BYTES),
        name="gdn_core",
    )(qkv, qkv, qkv, conv_w, conv_w, conv_w, conv_state, conv_state, conv_state,
      z, gg, gt, bg, norm_w.reshape(1, dv), s0)
    return o, s


def _gdn_layer(x, mod, gate, sid, streams, w_in, conv_w, a_log, dt_bias, norm_w, w_out):
    qk_dim, v_dim = GDN_QK_HEADS * GDN_DK, GDN_V_HEADS * GDN_DV
    conv_dim = 2 * qk_dim + v_dim
    n_gate = 2 * GDN_V_HEADS
    w_main = _weight_parts(w_in[:, :conv_dim + v_dim], 1)
    w_gate = jnp.pad(w_in[:, conv_dim + v_dim:], ((0, 0), (0, _round_up(n_gate, LANE) - n_gate)))
    qkv_all = pmatmul(x, w_main, col0=0, ncols=conv_dim, mod=mod, sid=sid)
    z_all = pmatmul(x, w_main, col0=conv_dim, ncols=v_dim, mod=mod, sid=sid)
    gates_all = pmatmul(x, _weight_parts(w_gate, 1), tn=w_gate.shape[1], mod=mod, sid=sid)
    outs, states = [], []
    for row0, B, T, conv_state, ssm_state in streams:
        rows = slice(row0, row0 + B * T)
        gates = gates_all[rows].reshape(B, T, -1)
        beta_raw, a_raw = gates[..., :GDN_V_HEADS], gates[..., GDN_V_HEADS:n_gate]
        if conv_state is None:
            conv_state = jnp.zeros((B, CONV_W - 1, conv_dim), x.dtype)
        if ssm_state is None:
            ssm_state = jnp.zeros((B, GDN_V_HEADS, GDN_DK, GDN_DV), jnp.float32)
        tail = jnp.stack([qkv_all[row0 + (b + 1) * T - (CONV_W - 1):row0 + (b + 1) * T] for b in range(B)])
        new_conv = jnp.concatenate([conv_state, tail], axis=1)[:, -(CONV_W - 1):]
        beta = jax.nn.sigmoid(beta_raw)
        g = -jnp.exp(a_log) * jax.nn.softplus(a_raw + dt_bias)
        o, S = _gdn_core(qkv_all, z_all, row0, conv_w, conv_state, g, beta, norm_w, ssm_state)
        outs.append(o.reshape(B * T, v_dim))
        states.append((new_conv, S))
    x = pmatmul(jnp.concatenate(outs, axis=0), _weight_parts(w_out, 1), res=(x, gate), sid=sid)
    return x, states


NEG_INF = float("-inf")
POS_INF = float("inf")
_PEER_CAND_ROWS = tuple((i, PEER_TOPK // (i + 1)) for i in range(PEER_TOPK // 2))


def _split_bf16(x):
    hi = x.astype(jnp.bfloat16)
    lo = (x - hi.astype(jnp.float32)).astype(jnp.bfloat16)
    return hi, lo


def _dot3_nt(a, b):
    dn = (((1,), (1,)), ((), ()))
    ah, al = _split_bf16(a)
    bh, bl = _split_bf16(b)
    out = lax.dot_general(ah, bh, dn, preferred_element_type=jnp.float32)
    out = out + lax.dot_general(ah, bl, dn, preferred_element_type=jnp.float32)
    return out + lax.dot_general(al, bh, dn, preferred_element_type=jnp.float32)


def _top_rows_desc(s, n, with_rank=False):
    rows = []
    cur = s
    rank = jnp.full(s.shape, float(n), jnp.float32)
    for i in range(n):
        m = jnp.max(cur, axis=0, keepdims=True)
        rows.append(m)
        hit = cur == m
        if with_rank:
            rank = jnp.where(hit, float(i), rank)
        cur = jnp.where(hit, NEG_INF, cur)
    return (rows, rank) if with_rank else rows


def _stack_rows(rows, lanes):
    n = len(rows)
    rid = lax.broadcasted_iota(jnp.int32, (n, lanes), 0)
    out = jnp.zeros((n, lanes), jnp.float32)
    for i, r in enumerate(rows):
        out = jnp.where(rid == i, r, out)
    return out


def _peer_select_kernel(q_ref, keys_ref, rk2_ref, e2_ref, cnt_ref, e1_ref):
    tm = q_ref.shape[0]
    k = PEER_TOPK
    s1 = _dot3_nt(keys_ref[0], q_ref[:, :PEER_HALF])
    s2 = _dot3_nt(keys_ref[1], q_ref[:, PEER_HALF:])
    r1 = _top_rows_desc(s1, k)
    r2, rank2 = _top_rows_desc(s2, k, with_rank=True)
    v1 = _stack_rows(r1, tm)
    v2 = _stack_rows(r2, tm)
    v2h = v2[:k // 2]
    rid = lax.broadcasted_iota(jnp.int32, (k // 2, tm), 0)
    pieces = [r1[0] + v2]
    for i, n in _PEER_CAND_ROWS[1:]:
        pieces.append(jnp.where(rid < n, r1[i] + v2h, NEG_INF))
    pieces.append(v1[k // 2:] + r2[0])
    cand = jnp.concatenate(pieces, axis=0)
    tau = _top_rows_desc(cand, k)[-1]
    top = r1[0] + r2[0]
    z = jnp.sum(jnp.where(cand >= tau, jnp.exp(cand - top), 0.0), axis=0, keepdims=True)
    cnt = jnp.zeros(s1.shape, jnp.float32)
    for j in range(k):
        cnt = cnt + jnp.where(s1 + r2[j] >= tau, 1.0, 0.0)
    rk2_ref[0] = rank2.astype(rk2_ref.dtype)
    e2_ref[0] = (jnp.exp(s2 - r2[0]) / z).astype(e2_ref.dtype)
    cnt_ref[0] = cnt
    e1_ref[0] = jnp.exp(s1 - r1[0])


def _peer_select(q, keys, *, tm):
    M = q.shape[0]
    nk = keys.shape[1]
    ospec = pl.BlockSpec((1, nk, tm), lambda i, h: (h, 0, i))
    return pl.pallas_call(
        _peer_select_kernel,
        grid=(M // tm, PEER_HEADS),
        in_specs=[pl.BlockSpec((tm, PEER_KEY_DIM), lambda i, h: (i, h)),
                  pl.BlockSpec(keys.shape, lambda i, h: (0, 0, 0))],
        out_specs=[ospec] * 4,
        out_shape=[jax.ShapeDtypeStruct((PEER_HEADS, nk, M), dt)
                   for dt in (jnp.bfloat16, jnp.bfloat16, jnp.float32, jnp.float32)],
        compiler_params=pltpu.CompilerParams(
            dimension_semantics=("parallel", "arbitrary"),
            vmem_limit_bytes=V7X_VMEM_LIMIT_BYTES),
        name="peer_select",
    )(q, keys)


BF16_SUBLANES = 16
PEER_SUB_EXPERTS = 512


def _peer_main_kernel(sid_ref, x_ref, gain_ref, shift_ref, scale_ref, gate_ref, fnorm_ref,
                      u_ref, vT_ref, rk2_ref, e2_ref, cnt_ref, e1_ref, *refs, final_norm):
    out_refs, (hT_ref, coef_ref, acc_ref) = refs[:-3], refs[-3:]
    e = pl.program_id(1)
    te, tm = coef_ref.shape
    nk = rk2_ref.shape[1]
    rows = BF16_SUBLANES
    sub = PEER_SUB_EXPERTS
    blk0 = pl.program_id(0) * (tm // SEQ_BLOCK)

    @pl.when(e == 0)
    def _():
        acc_ref[...] = jnp.zeros_like(acc_ref)
        per_lane_tile = LANE // SEQ_BLOCK
        for t in range(tm // LANE):
            hs = []
            for r in range(t * per_lane_tile, (t + 1) * per_lane_tile):
                rs = slice(r * SEQ_BLOCK, (r + 1) * SEQ_BLOCK)
                hs.append(_modulated_rows(x_ref[rs, :], gain_ref[...], shift_ref, scale_ref, sid_ref[blk0 + r]))
            hT_ref[:, t * LANE:(t + 1) * LANE] = jnp.concatenate(hs, axis=0).T.astype(jnp.bfloat16)

    def expert_acts(sb):
        return jnp.dot(u_ref[sb * sub:(sb + 1) * sub, :], hT_ref[...], preferred_element_type=jnp.float32)

    def weigh(sb, act):
        for al in range(sub // nk):
            a = sb * (sub // nk) + al
            for lg in range(tm // LANE):
                lanes = slice(lg * LANE, (lg + 1) * LANE)
                cnts = [jnp.broadcast_to(cnt_ref[h, a:a + 1, lanes], (rows, LANE)).astype(jnp.bfloat16)
                        for h in range(PEER_HEADS)]
                e1s = [jnp.broadcast_to(e1_ref[h, a:a + 1, lanes], (rows, LANE)).astype(jnp.bfloat16)
                       for h in range(PEER_HEADS)]
                for r in range(nk // rows):
                    rs = slice(r * rows, (r + 1) * rows)
                    w = None
                    for h in range(PEER_HEADS):
                        e2 = e2_ref[h, rs, lanes]
                        t = jnp.where(rk2_ref[h, rs, lanes] < cnts[h], e2, jnp.zeros_like(e2)) * e1s[h]
                        w = t if w is None else w + t
                    x = act[al * nk + r * rows:al * nk + (r + 1) * rows, lanes]
                    g = 0.5 * x * (1.0 + lax.erf(x * (2.0 ** -0.5)))
                    coef_ref[sb * sub + al * nk + r * rows:sb * sub + al * nk + (r + 1) * rows, lanes] = (
                        w * g.astype(jnp.bfloat16))

    def accumulate(sb):
        acc_ref[...] += lax.dot_general(vT_ref[sb * sub:(sb + 1) * sub, :], coef_ref[sb * sub:(sb + 1) * sub, :],
                                        (((0,), (0,)), ((), ())), preferred_element_type=jnp.float32)

    n_sub = te // sub
    act = expert_acts(0)
    for sb in range(n_sub):
        nxt = expert_acts(sb + 1) if sb + 1 < n_sub else None
        weigh(sb, act)
        accumulate(sb)
        act = nxt

    @pl.when(e == pl.num_programs(1) - 1)
    def _():
        out = acc_ref[...].T
        for r in range(tm // SEQ_BLOCK):
            rs = slice(r * SEQ_BLOCK, (r + 1) * SEQ_BLOCK)
            y = x_ref[rs, :] + gate_ref[pl.ds(sid_ref[blk0 + r], 1), :] * out[rs, :]
            out_refs[0][rs, :] = y
            if final_norm:
                out_refs[1][rs, :] = y * lax.rsqrt(jnp.mean(y * y, axis=-1, keepdims=True) + EPS) * fnorm_ref[...]


def _peer_main(x, mod, gate, fnorm, sid, u, vT, rk2, e2, cnt, e1, *, tm, te, final_norm):
    M, D = x.shape
    E = u.shape[0]
    nk = rk2.shape[1]
    const = lambda a: pl.BlockSpec(a.shape, lambda i, e, s: (0,) * a.ndim)
    col_spec = pl.BlockSpec((PEER_HEADS, nk, tm), lambda i, e, s: (0, 0, i))
    row_spec = pl.BlockSpec((PEER_HEADS, te // nk, tm), lambda i, e, s: (0, e, i))
    tok_spec = pl.BlockSpec((tm, D), lambda i, e, s: (i, 0))
    n_out = 2 if final_norm else 1
    return pl.pallas_call(
        functools.partial(_peer_main_kernel, final_norm=final_norm),
        grid_spec=pltpu.PrefetchScalarGridSpec(
            num_scalar_prefetch=1,
            grid=(M // tm, E // te),
            in_specs=[tok_spec, const(mod[0]), const(mod[1]), const(mod[2]), const(gate), const(fnorm),
                      pl.BlockSpec((te, D), lambda i, e, s: (e, 0)),
                      pl.BlockSpec((te, D), lambda i, e, s: (e, 0)),
                      col_spec, col_spec, row_spec, row_spec],
            out_specs=[tok_spec] * n_out,
            scratch_shapes=[pltpu.VMEM((D, tm), jnp.bfloat16),
                            pltpu.VMEM((te, tm), jnp.bfloat16),
                            pltpu.VMEM((D, tm), jnp.float32)]),
        out_shape=[jax.ShapeDtypeStruct((M, D), jnp.float32)] * n_out,
        compiler_params=pltpu.CompilerParams(
            dimension_semantics=("parallel", "arbitrary"),
            vmem_limit_bytes=V7X_VMEM_LIMIT_PEER_BYTES),
        name="peer_main",
    )(sid, x, *mod, gate, fnorm, u, vT, rk2, e2, cnt, e1)


def _peer(x, mod, gate, fnorm, sid, w_query, sub_keys, expert_u, expert_v, *, final_norm,
          tm_sel=256, tm=512, te=1024):
    M = x.shape[0]
    q = pmatmul(x, _weight_parts(w_query, 3), mod=mod, sid=sid)
    rk2, e2, cnt, e1 = _peer_select(q, sub_keys, tm=min(tm_sel, M))
    u = expert_u.astype(jnp.bfloat16)
    vT = expert_v.astype(jnp.bfloat16)
    return _peer_main(x, mod, gate, fnorm, sid, u, vT, rk2, e2, cnt, e1, tm=min(tm, M), te=te,
                      final_norm=final_norm)


def kernel(x_prompt, x_sample, c_prompt, c_sample, cache_k_l0, cache_v_l0, cache_kidx_l0, state_conv_l1, state_ssm_l1, norm1_l0, norm2_l0, ada_w_l0, ada_b_l0, attn_in_l0, attn_out_l0, peer_query_l0, peer_keys_l0, peer_u_l0, peer_v_l0, norm1_l1, norm2_l1, ada_w_l1, ada_b_l1, gdn_in_l1, gdn_conv_l1, gdn_a_log_l1, gdn_dt_bias_l1, gdn_norm_l1, gdn_out_l1, peer_query_l1, peer_keys_l1, peer_u_l1, peer_v_l1, final_norm):
    past_len = cache_k_l0.shape[1]
    norm1 = (norm1_l0, norm1_l1)
    norm2 = (norm2_l0, norm2_l1)
    ada_w = (ada_w_l0, ada_w_l1)
    ada_b = (ada_b_l0, ada_b_l1)
    peer_query = (peer_query_l0, peer_query_l1)
    peer_keys = (peer_keys_l0, peer_keys_l1)
    peer_u = (peer_u_l0, peer_u_l1)
    peer_v = (peer_v_l0, peer_v_l1)
    Bp, Tp, D = x_prompt.shape
    Bs, Ts, _ = x_sample.shape
    n_p, n_s = Bp * Tp, Bs * Ts
    x = jnp.concatenate([x_prompt.reshape(n_p, D), x_sample.reshape(n_s, D)], axis=0)
    sid = jnp.concatenate([jnp.repeat(jnp.arange(Bp, dtype=jnp.int32), Tp // SEQ_BLOCK),
                           Bp + jnp.repeat(jnp.arange(Bs, dtype=jnp.int32), Ts // SEQ_BLOCK)])
    c_all = jnp.concatenate([c_prompt, c_sample], axis=0)
    fnorm = final_norm.reshape(1, D)
    for i in range(2):
        mods = _adaln(c_all, ada_w[i], ada_b[i])
        mod1 = (norm1[i].reshape(1, D), mods[0], mods[1])
        if i == 0:
            x, ((nkp, nvp, nkip), (nks, nvs, nkis)) = _dsa_layer(
                x, mod1, mods[2], sid,
                [(0, Bp, Tp, 0, None, None, None), (n_p, Bs, Ts, past_len, cache_k_l0, cache_v_l0, cache_kidx_l0)],
                attn_in_l0, attn_out_l0)
        else:
            x, ((ncp, nsp), (ncs, nss)) = _gdn_layer(
                x, mod1, mods[2], sid, [(0, Bp, Tp, None, None), (n_p, Bs, Ts, state_conv_l1, state_ssm_l1)],
                gdn_in_l1, gdn_conv_l1, gdn_a_log_l1, gdn_dt_bias_l1, gdn_norm_l1, gdn_out_l1)
        mod2 = (norm2[i].reshape(1, D), mods[3], mods[4])
        outs = _peer(x, mod2, mods[5], fnorm, sid, peer_query[i], peer_keys[i], peer_u[i], peer_v[i],
                     final_norm=(i == 1))
        x = outs[0]
    y = outs[1]
    y_prompt = y[:n_p].reshape(Bp, Tp, D)
    y_sample = y[n_p:].reshape(Bs, Ts, D)
    return (y_prompt, y_sample, nkp, nvp, nkip, nks, nvs, nkis, ncp, nsp, ncs, nss)
```

```python
import functools

import jax
import jax.numpy as jnp
from jax import lax
from jax.experimental import pallas as pl
from jax.experimental.pallas import tpu as pltpu

CHUNK = 64
EPS = 1e-6
ROPE_THETA = 500000.0
ROPE_FRACTION = 4
A_HEADS = 16
A_KV_HEADS = 4
IDX_HEADS = 8
IDX_DIM = 64
TOPK_MAX = 256
Q_BLOCK = 128
GDN_QK_HEADS = 16
GDN_V_HEADS = 32
GDN_DK = 128
GDN_DV = 128
CONV_W = 4
PEER_HEADS = 8
PEER_NKEYS = 128
PEER_KEY_DIM = 256
PEER_HALF = PEER_KEY_DIM // 2
PEER_TOPK = 16
PEER_BLOCK = 64

V7X_VMEM_LIMIT_BYTES = 48 * 1024 * 1024
V7X_VMEM_LIMIT_PEER_BYTES = 56 * 1024 * 1024
LANE = 128
SUBLANE = 8


def _round_up(n, m):
    return (n + m - 1) // m * m


SEQ_BLOCK = CHUNK


def _modulated_rows(x, gain, shift_ref, scale_ref, sid):
    y = x * lax.rsqrt(jnp.mean(x * x, axis=-1, keepdims=True) + EPS)
    return y * gain * (1.0 + scale_ref[pl.ds(sid, 1), :]) + shift_ref[pl.ds(sid, 1), :]


def _matmul_kernel(sid_ref, x_ref, *refs, passes, has_mod, has_res):
    refs = list(refs)
    mod_refs = [refs.pop(0) for _ in range(3)] if has_mod else None
    nparts = 1 + passes // 2
    w_refs = [refs.pop(0) for _ in range(nparts)]
    res_refs = [refs.pop(0) for _ in range(2)] if has_res else None
    o_ref, x_parts = refs[0], refs[1:]
    tm = x_ref.shape[0]
    blocks = range(tm // SEQ_BLOCK) if (has_mod or has_res) else ()
    blk0 = pl.program_id(0) * (tm // SEQ_BLOCK)

    @pl.when(pl.program_id(1) == 0)
    def _():
        def put(rows, x):
            hi = x.astype(jnp.bfloat16)
            x_parts[0][rows, :] = hi
            if passes == 3:
                x_parts[1][rows, :] = (x - hi.astype(jnp.float32)).astype(jnp.bfloat16)

        if has_mod:
            gain_ref, shift_ref, scale_ref = mod_refs
            for r in blocks:
                rows = slice(r * SEQ_BLOCK, (r + 1) * SEQ_BLOCK)
                put(rows, _modulated_rows(x_ref[rows, :], gain_ref[...], shift_ref, scale_ref, sid_ref[blk0 + r]))
        else:
            put(slice(None), x_ref[...].astype(jnp.float32))

    acc = jnp.dot(x_parts[0][...], w_refs[0][...], preferred_element_type=jnp.float32)
    if passes == 3:
        acc = acc + jnp.dot(x_parts[0][...], w_refs[1][...], preferred_element_type=jnp.float32)
        acc = acc + jnp.dot(x_parts[1][...], w_refs[0][...], preferred_element_type=jnp.float32)
    if has_res:
        res_ref, gate_ref = res_refs
        for r in blocks:
            rows = slice(r * SEQ_BLOCK, (r + 1) * SEQ_BLOCK)
            gate = gate_ref[pl.ds(sid_ref[blk0 + r], 1), :]
            o_ref[rows, :] = res_ref[rows, :] + gate * acc[rows, :]
    else:
        o_ref[...] = acc.astype(o_ref.dtype)


def _weight_parts(w, passes):
    hi = w.astype(jnp.bfloat16)
    if passes == 1:
        return (hi,)
    return (hi, (w - hi.astype(jnp.float32)).astype(jnp.bfloat16))


def pmatmul(x, w_parts, *, col0=0, ncols=None, tm=512, tn=1024, out_dtype=jnp.float32,
            mod=None, res=None, sid=None):
    passes = 1 if len(w_parts) == 1 else 3
    M, K = x.shape
    n_total = w_parts[0].shape[1]
    ncols = n_total - col0 if ncols is None else ncols
    tm = min(tm, _round_up(M, 2 * SUBLANE))
    tn = min(tn, ncols)
    assert ncols % tn == 0 and col0 % tn == 0 and (tn % LANE == 0 or tn == n_total)
    Mp = _round_up(M, tm)
    if mod is not None or res is not None:
        assert Mp == M and tm % SEQ_BLOCK == 0 and sid is not None
    else:
        sid = jnp.zeros((1,), jnp.int32)
    if Mp != M:
        x = jnp.pad(x, ((0, Mp - M), (0, 0)))
    c0 = col0 // tn
    in_specs = [pl.BlockSpec((tm, K), lambda i, j, s: (i, 0))]
    args = [x]
    if mod is not None:
        in_specs += [pl.BlockSpec(a.shape, lambda i, j, s: (0, 0)) for a in mod]
        args += list(mod)
    in_specs += [pl.BlockSpec((K, tn), lambda i, j, s: (0, c0 + j))] * len(w_parts)
    args += list(w_parts)
    if res is not None:
        in_specs += [pl.BlockSpec((tm, tn), lambda i, j, s: (i, j)),
                     pl.BlockSpec((res[1].shape[0], tn), lambda i, j, s: (0, j))]
        args += list(res)
        out_dtype = jnp.float32
    out = pl.pallas_call(
        functools.partial(_matmul_kernel, passes=passes, has_mod=mod is not None, has_res=res is not None),
        grid_spec=pltpu.PrefetchScalarGridSpec(
            num_scalar_prefetch=1,
            grid=(Mp // tm, ncols // tn),
            in_specs=in_specs,
            out_specs=pl.BlockSpec((tm, tn), lambda i, j, s: (i, j)),
            scratch_shapes=[pltpu.VMEM((tm, K), jnp.bfloat16)] * (1 + passes // 2)),
        out_shape=jax.ShapeDtypeStruct((Mp, ncols), out_dtype),
        compiler_params=pltpu.CompilerParams(
            dimension_semantics=("parallel", "arbitrary"),
            vmem_limit_bytes=V7X_VMEM_LIMIT_BYTES),
        name="matmul",
    )(sid, *args)
    return out[:M]


def _adaln_kernel(c_ref, w_ref, b_ref, o_ref):
    c = c_ref[...]
    o_ref[...] = _dot3(c * jax.nn.sigmoid(c), w_ref[...]) + b_ref[...]


def _adaln(c, w, b, *, tn=512):
    n, D = c.shape
    N = w.shape[1]
    rows = _round_up(n, SUBLANE)
    mod = pl.pallas_call(
        _adaln_kernel,
        grid=(N // tn,),
        in_specs=[pl.BlockSpec((rows, D), lambda j: (0, 0)),
                  pl.BlockSpec((D, tn), lambda j: (0, j)),
                  pl.BlockSpec((1, tn), lambda j: (0, j))],
        out_specs=pl.BlockSpec((rows, tn), lambda j: (0, j)),
        out_shape=jax.ShapeDtypeStruct((rows, N), jnp.float32),
        compiler_params=pltpu.CompilerParams(
            dimension_semantics=("parallel",), vmem_limit_bytes=V7X_VMEM_LIMIT_BYTES),
        name="adaln",
    )(jnp.pad(c, ((0, rows - n), (0, 0))), w, b.reshape(1, N))
    return jnp.split(mod, 6, axis=-1)


_NT_DIMS = (((1,), (1,)), ((), ()))
INT32_MIN = -2 ** 31
LOG2_E = 1.4426950408889634
_NEG_INF_KEY = -2139095041
IDX_PACK = 4 * IDX_DIM


def _ordered_key(x):
    bits = pltpu.bitcast(x, jnp.int32)
    return bits ^ ((bits >> 31) & 0x7FFFFFFF)


def _lane_tile_sum(x, width=LANE):
    out = x[:, :width]
    for c in range(1, x.shape[1] // width):
        out = out + x[:, c * width:(c + 1) * width]
    return out


def _dsa_select_bias(qi_ref, wi_ref, kidx_ref, key_ref, bias_ref, *, first, n_tiles, topk, tk):
    tq = wi_ref.shape[1]
    row = lax.broadcasted_iota(jnp.int32, (tq, 1), 0)
    lim = (((first + row) >> 6) + 1) * CHUNK
    w = wi_ref[0]

    def score_tile(j, c):
        off = pl.multiple_of(j * tk, tk)
        kt = kidx_ref[0, pl.ds(off, tk), :]
        sc = lax.dot_general(qi_ref[0, 0], kt, _NT_DIMS,
                             preferred_element_type=jnp.float32)
        s = jnp.zeros((tq, tk), jnp.float32)
        for h in range(IDX_HEADS):
            s = s + w[:, h:h + 1] * jnp.maximum(sc[h * tq:(h + 1) * tq], 0.0)
        col = off + lax.broadcasted_iota(jnp.int32, (tq, tk), 1)
        s = jnp.where(col < lim, s + 0.0, NEG_INF)
        key_ref[:, pl.ds(off, tk)] = _ordered_key(s)
        return c

    lax.fori_loop(0, n_tiles, score_tile, 0)

    def bit_step(b, thr):
        cand = thr + lax.shift_left(jnp.int32(1), 31 - b)

        def count_tile(j, c):
            off = pl.multiple_of(j * tk, tk)
            ge = jnp.where(key_ref[:, pl.ds(off, tk)] >= cand, 1.0, 0.0)
            return c + _lane_tile_sum(ge)

        c = lax.fori_loop(0, n_tiles, count_tile, jnp.zeros((tq, LANE), jnp.float32))
        cnt = jnp.sum(c, axis=1, keepdims=True)
        return jnp.where(cnt >= topk, cand, thr)

    thr = lax.fori_loop(0, 32, bit_step, jnp.full((tq, 1), INT32_MIN, jnp.int32))
    thr = jnp.maximum(thr, _NEG_INF_KEY + 1)

    def bias_tile(j, c):
        off = pl.multiple_of(j * tk, tk)
        bias_ref[:, pl.ds(off, tk)] = jnp.where(key_ref[:, pl.ds(off, tk)] >= thr, 0.0, NEG_INF)
        return c

    lax.fori_loop(0, n_tiles, bias_tile, 0)


def _dsa_kernel(qi_ref, wi_ref, kidx_ref, q_ref, k_ref, v_ref, o_ref,
                key_ref, bias_ref, qg_ref, m_ref, l_ref, acc_ref, *, pos0, topk, tk):
    i = pl.program_id(1)
    tq = q_ref.shape[1]
    hd = acc_ref.shape[2]
    groups = acc_ref.shape[0]
    rep = q_ref.shape[2] // (groups * hd)
    first = pos0 + i * tq
    n_valid = (((first + tq - 1) >> 6) + 1) * CHUNK
    n_tiles = (n_valid + tk - 1) // tk
    _dsa_select_bias(qi_ref, wi_ref, kidx_ref, key_ref, bias_ref, first=first, n_tiles=n_tiles, topk=topk, tk=tk)

    for g in range(groups):
        for r in range(rep):
            c0 = (g * rep + r) * hd
            qg_ref[g, r * tq:(r + 1) * tq, :] = q_ref[0, :, c0:c0 + hd]
    m_ref[...] = jnp.full(m_ref.shape, NEG_INF, jnp.float32)
    l_ref[...] = jnp.zeros(l_ref.shape, jnp.float32)
    acc_ref[...] = jnp.zeros(acc_ref.shape, jnp.float32)
    lane_reps = tk // LANE

    def att_tile(j, c):
        off = pl.multiple_of(j * tk, tk)
        b = bias_ref[:, pl.ds(off, tk)]
        bias = jnp.concatenate([b] * rep, axis=0)
        for g in range(groups):
            kt = k_ref[0, pl.ds(off, tk), g * hd:(g + 1) * hd]
            vt = v_ref[0, pl.ds(off, tk), g * hd:(g + 1) * hd]
            lg = lax.dot_general(qg_ref[g], kt, _NT_DIMS, preferred_element_type=jnp.float32) + bias
            m_old = m_ref[g]
            m_new = jnp.maximum(m_old, jnp.max(lg, axis=1, keepdims=True))
            m_safe = jnp.where(m_new == NEG_INF, 0.0, m_new)
            p = jnp.exp2(lg - jnp.tile(m_safe, (1, lane_reps)))
            alpha = jnp.exp2(m_old - m_safe)
            l_ref[g] = alpha * l_ref[g] + jnp.sum(p, axis=1, keepdims=True)
            acc_ref[g] = alpha * acc_ref[g] + jnp.dot(p.astype(jnp.bfloat16), vt,
                                                      preferred_element_type=jnp.float32)
            m_ref[g] = m_new
        return c

    lax.fori_loop(0, n_tiles, att_tile, 0)
    for g in range(groups):
        out = acc_ref[g] / l_ref[g]
        for r in range(rep):
            c0 = (g * rep + r) * hd
            o_ref[0, :, c0:c0 + hd] = out[r * tq:(r + 1) * tq].astype(o_ref.dtype)


def _dsa_core(q, qi3, wi, kb, vb, kidx3, pos0, topk, *, tq, tk):
    B, T, qd = q.shape
    Sp = kb.shape[1]
    hd = qd // A_HEADS
    gw = qd // A_KV_HEADS
    nb = T // tq
    assert Sp % tk == 0 and hd == LANE
    rows = (gw // hd) * tq
    kvw = A_KV_HEADS * hd
    return pl.pallas_call(
        functools.partial(_dsa_kernel, pos0=pos0, topk=topk, tk=tk),
        grid=(B, nb),
        in_specs=[pl.BlockSpec((1, 1, IDX_HEADS * tq, IDX_PACK), lambda b, i: (b, i, 0, 0)),
                  pl.BlockSpec((1, tq, IDX_HEADS), lambda b, i: (b, i, 0)),
                  pl.BlockSpec((1, Sp, IDX_PACK), lambda b, i: (b, 0, 0)),
                  pl.BlockSpec((1, tq, qd), lambda b, i: (b, i, 0)),
                  pl.BlockSpec((1, Sp, kvw), lambda b, i: (b, 0, 0)),
                  pl.BlockSpec((1, Sp, kvw), lambda b, i: (b, 0, 0))],
        out_specs=pl.BlockSpec((1, tq, qd), lambda b, i: (b, i, 0)),
        out_shape=jax.ShapeDtypeStruct((B, T, qd), jnp.bfloat16),
        scratch_shapes=[pltpu.VMEM((tq, Sp), jnp.int32),
                        pltpu.VMEM((tq, Sp), jnp.float32),
                        pltpu.VMEM((A_KV_HEADS, rows, hd), jnp.bfloat16),
                        pltpu.VMEM((A_KV_HEADS, rows, LANE), jnp.float32),
                        pltpu.VMEM((A_KV_HEADS, rows, LANE), jnp.float32),
                        pltpu.VMEM((A_KV_HEADS, rows, hd), jnp.float32)],
        compiler_params=pltpu.CompilerParams(
            dimension_semantics=("parallel", "arbitrary"),
            vmem_limit_bytes=V7X_VMEM_LIMIT_BYTES),
        name="dsa_core",
    )(qi3, wi, kidx3, q, kb, vb)


def _rope_tables(pos, width, period):
    half = period // ROPE_FRACTION // 2
    inv_freq = ROPE_THETA ** (-jnp.arange(half, dtype=jnp.float32) / half)
    ang = pos.astype(jnp.float32)[:, None] * inv_freq[None, :]
    cos, sin = jnp.cos(ang), jnp.sin(ang)
    T = pos.shape[0]
    rest = period - 2 * half
    c = jnp.concatenate([cos, cos, jnp.ones((T, rest), jnp.float32)], axis=1)
    s_next = jnp.concatenate([-sin, jnp.zeros((T, period - half), jnp.float32)], axis=1)
    s_prev = jnp.concatenate([jnp.zeros((T, half), jnp.float32), sin, jnp.zeros((T, rest), jnp.float32)], axis=1)
    return jnp.stack([jnp.tile(t, (1, width // period)) for t in (c, s_next, s_prev)])


def _rope_lanes(x, tab_ref, half):
    return (x * tab_ref[0] + pltpu.roll(x, LANE - half, 1) * tab_ref[1] + pltpu.roll(x, half, 1) * tab_ref[2])


def _dsa_prep_kernel(q_ref, kv_ref, idx_ref, tq_ref, ti_ref,
                     qa_ref, kn_ref, vn_ref, kin_ref, kb_ref, vb_ref, qi3_ref, kidx3_ref, wi_ref, *, q_scale):
    tq = q_ref.shape[0]
    hd = LANE
    kvw = kv_ref.shape[1] // 2
    half_q = hd // ROPE_FRACTION // 2
    half_i = IDX_DIM // ROPE_FRACTION // 2
    left = lax.broadcasted_iota(jnp.int32, (tq, LANE), 1) < IDX_DIM
    zero = jnp.zeros((tq, LANE), jnp.float32)

    for h in range(q_ref.shape[1] // hd):
        cols = slice(h * hd, (h + 1) * hd)
        qa_ref[0, :, cols] = (_rope_lanes(q_ref[:, cols], tq_ref, half_q) * q_scale).astype(qa_ref.dtype)
    for h in range(kvw // hd):
        cols = slice(h * hd, (h + 1) * hd)
        k = _rope_lanes(kv_ref[:, cols], tq_ref, half_q)
        kn_ref[0, :, cols] = k
        kb_ref[0, :, cols] = k.astype(kb_ref.dtype)
    v = kv_ref[:, kvw:]
    vn_ref[0] = v
    vb_ref[0] = v.astype(vb_ref.dtype)

    def hi_lo(x):
        hi = x.astype(jnp.bfloat16).astype(jnp.float32)
        return hi, x - hi

    for t in range(IDX_HEADS * IDX_DIM // LANE):
        hi, lo = hi_lo(_rope_lanes(idx_ref[:, t * LANE:(t + 1) * LANE], ti_ref, half_i))
        hi_sw, lo_sw = pltpu.roll(hi, IDX_DIM, 1), pltpu.roll(lo, IDX_DIM, 1)
        even = jnp.concatenate([jnp.where(left, hi, lo_sw), jnp.where(left, hi, zero)], axis=1)
        odd = jnp.concatenate([jnp.where(left, hi_sw, lo), jnp.where(left, hi_sw, zero)], axis=1)
        qi3_ref[0, 0, (2 * t) * tq:(2 * t + 1) * tq, :] = even.astype(qi3_ref.dtype)
        qi3_ref[0, 0, (2 * t + 1) * tq:(2 * t + 2) * tq, :] = odd.astype(qi3_ref.dtype)
    c0 = IDX_HEADS * IDX_DIM
    x = idx_ref[:, c0:c0 + LANE]
    r = jnp.where(left, _rope_lanes(x, ti_ref, half_i), x)
    kin_ref[0] = r[:, :IDX_DIM]
    hi, lo = hi_lo(r)
    kidx3 = jnp.concatenate([jnp.where(left, hi, pltpu.roll(hi, IDX_DIM, 1)), jnp.where(left, lo, zero)], axis=1)
    kidx3_ref[0] = kidx3.astype(kidx3_ref.dtype)
    wi_ref[0] = x[:, IDX_DIM:IDX_DIM + IDX_HEADS] * (IDX_HEADS ** -0.5 * IDX_DIM ** -0.5)


def _dsa_prep(q_all, kv_all, idx_all, row0, B, T, pos0, *, tq):
    qd, kv2 = A_HEADS * LANE, 2 * A_KV_HEADS * LANE
    kv_col = (kv_all.shape[1] - kv2) // kv2
    kvw = kv2 // 2
    nb = T // tq
    blk0 = row0 // tq
    assert row0 % tq == 0 and qd // A_HEADS == LANE
    pos = pos0 + jnp.arange(T, dtype=jnp.int32)
    tab_q = _rope_tables(pos, LANE, LANE)
    tab_i = _rope_tables(pos, LANE, IDX_DIM)
    row = lambda w: pl.BlockSpec((tq, w), lambda b, i: (blk0 + b * nb + i, 0))
    tab = pl.BlockSpec((3, tq, LANE), lambda b, i: (0, i, 0))
    out = lambda w: pl.BlockSpec((1, tq, w), lambda b, i: (b, i, 0))
    f32, bf16 = jnp.float32, jnp.bfloat16
    shapes = [((B, T, qd), bf16), ((B, T, kvw), f32), ((B, T, kvw), f32), ((B, T, IDX_DIM), f32),
              ((B, T, kvw), bf16), ((B, T, kvw), bf16), ((B, nb, IDX_HEADS * tq, IDX_PACK), bf16),
              ((B, T, IDX_PACK), bf16), ((B, T, IDX_HEADS), f32)]
    return pl.pallas_call(
        functools.partial(_dsa_prep_kernel, q_scale=LANE ** -0.5 * LOG2_E),
        grid=(B, nb),
        in_specs=[row(qd), pl.BlockSpec((tq, kv2), lambda b, i: (blk0 + b * nb + i, kv_col)),
                  row(idx_all.shape[1]), tab, tab],
        out_specs=[out(qd), out(kvw), out(kvw), out(IDX_DIM), out(kvw), out(kvw),
                   pl.BlockSpec((1, 1, IDX_HEADS * tq, IDX_PACK), lambda b, i: (b, i, 0, 0)),
                   out(IDX_PACK), out(IDX_HEADS)],
        out_shape=[jax.ShapeDtypeStruct(s, d) for s, d in shapes],
        compiler_params=pltpu.CompilerParams(
            dimension_semantics=("parallel", "parallel"), vmem_limit_bytes=V7X_VMEM_LIMIT_BYTES),
        name="dsa_prep",
    )(q_all, kv_all, idx_all, tab_q, tab_i)


def _dsa_layer(x, mod, gate, sid, streams, w_in, w_out):
    D = x.shape[1]
    hd = D // A_HEADS
    q_dim, kv_dim = A_HEADS * hd, A_KV_HEADS * hd
    n_idx = IDX_HEADS * IDX_DIM + IDX_DIM + IDX_HEADS
    w_main = _weight_parts(w_in[:, :q_dim + 2 * kv_dim], 1)
    w_idx = jnp.pad(w_in[:, q_dim + 2 * kv_dim:], ((0, 0), (0, _round_up(n_idx, LANE) - n_idx)))
    q_all = kv_all = pmatmul(x, w_main, mod=mod, sid=sid)
    idx_all = pmatmul(x, _weight_parts(w_idx, 3), tn=w_idx.shape[1], mod=mod, sid=sid)
    outs, caches = [], []
    for row0, B, T, pos0, past_k, past_v, past_kidx in streams:
        o, k, v, ki = _dsa_stream(q_all, kv_all, idx_all, row0, B, T, pos0, past_k, past_v, past_kidx)
        outs.append(o.reshape(B * T, q_dim))
        caches.append((k, v, ki))
    x = pmatmul(jnp.concatenate(outs, axis=0), _weight_parts(w_out, 1), res=(x, gate), sid=sid)
    return x, caches


DSA_KEY_TILE = 512


def _dsa_stream(q_all, kv_all, idx_all, row0, B, T, pos0, past_k, past_v, past_kidx):
    tq = min(Q_BLOCK, T)
    q, k, v, ki, kb, vb, qi3, kidx3, wi = _dsa_prep(q_all, kv_all, idx_all, row0, B, T, pos0, tq=tq)
    n_keys = T
    if past_k is not None:
        P = past_k.shape[1]
        n_keys = P + T
        kb = jnp.concatenate([past_k.reshape(B, P, -1).astype(jnp.bfloat16), kb], axis=1)
        vb = jnp.concatenate([past_v.reshape(B, P, -1).astype(jnp.bfloat16), vb], axis=1)
        ph, pl_ = _split_bf16(past_kidx)
        kidx3 = jnp.concatenate([jnp.concatenate([ph, ph, pl_, jnp.zeros_like(ph)], axis=-1), kidx3], axis=1)
    pad = ((0, 0), (0, _round_up(n_keys, DSA_KEY_TILE) - n_keys), (0, 0))
    kb, vb, kidx3 = (jnp.pad(a, pad) for a in (kb, vb, kidx3))
    topk = min(TOPK_MAX, n_keys // 4)
    o = _dsa_core(q, qi3, wi, kb, vb, kidx3, pos0, topk, tq=tq, tk=DSA_KEY_TILE)
    hd = q.shape[2] // A_HEADS
    return o, k.reshape(B, T, A_KV_HEADS, hd), v.reshape(B, T, A_KV_HEADS, hd), ki


GDN_HEAD_GROUP = 32


def _bf16_dot(a, b):
    return jnp.dot(a.astype(jnp.bfloat16), b.astype(jnp.bfloat16), preferred_element_type=jnp.float32)


def _dot3(a, b):
    ah, al = _split_bf16(a)
    bh, bl = _split_bf16(b)
    out = jnp.dot(ah, bh, preferred_element_type=jnp.float32)
    out = out + jnp.dot(ah, bl, preferred_element_type=jnp.float32)
    return out + jnp.dot(al, bh, preferred_element_type=jnp.float32)


def _conv_silu(x_ref, w_ref, xe_ref):
    C = x_ref.shape[0]
    taps = w_ref.shape[0]
    xe_ref[SUBLANE:, :] = x_ref[...]
    first = SUBLANE - (taps - 1)
    acc = xe_ref[first:first + C, :] * w_ref[0:1, :]
    for j in range(1, taps):
        acc = acc + xe_ref[first + j:first + j + C, :] * w_ref[j:j + 1, :]
    xe_ref[:SUBLANE, :] = xe_ref[C:, :]
    return acc * jax.nn.sigmoid(acc)


def _gdn_kernel(xq_ref, xk_ref, xv_ref, wq_ref, wk_ref, wv_ref, cq_ref, ck_ref, cv_ref,
                z_ref, g_ref, gt_ref, beta_ref, nw_ref, s0_ref,
                o_ref, s_out_ref, s_ref, eq_ref, ek_ref, ev_ref):
    n = pl.program_id(2)
    C = xq_ref.shape[0]
    hg = g_ref.shape[3]
    dk = s_ref.shape[1]
    dv = s_ref.shape[2]
    rep = hg // (xk_ref.shape[1] // dk)

    @pl.when(n == 0)
    def _():
        s_ref[...] = s0_ref[0]
        for e_ref, c_ref in ((eq_ref, cq_ref), (ek_ref, ck_ref), (ev_ref, cv_ref)):
            e_ref[:SUBLANE, :] = jnp.zeros((SUBLANE, e_ref.shape[1]), jnp.float32)
            e_ref[SUBLANE - c_ref.shape[1]:SUBLANE, :] = c_ref[0]

    qc = _conv_silu(xq_ref, wq_ref, eq_ref)
    kc = _conv_silu(xk_ref, wk_ref, ek_ref)
    vc = _conv_silu(xv_ref, wv_ref, ev_ref)
    q_heads, k_heads = [], []
    for i in range(hg // rep):
        qh = qc[:, i * dk:(i + 1) * dk]
        kh = kc[:, i * dk:(i + 1) * dk]
        q_heads.append(qh * (lax.rsqrt(jnp.sum(qh * qh, axis=1, keepdims=True) + EPS) * dk ** -0.5))
        k_heads.append(kh * lax.rsqrt(jnp.sum(kh * kh, axis=1, keepdims=True) + EPS))

    ri = lax.broadcasted_iota(jnp.int32, (C, C), 0)
    ci = lax.broadcasted_iota(jnp.int32, (C, C), 1)
    causal = ri >= ci
    strict = ri > ci
    eye = jnp.where(ri == ci, 1.0, 0.0)
    g = g_ref[0, 0]
    gc_all = _dot3(jnp.where(causal, 1.0, 0.0), g)
    gr_all = _dot3(gt_ref[0, 0, 0], jnp.where(ri <= ci, 1.0, 0.0))
    beta = beta_ref[0, 0]
    nw = nw_ref[...]

    heads = range(hg)
    qs = [q_heads[h // rep] for h in heads]
    ks = [k_heads[h // rep] for h in heads]
    gcs = [gc_all[:, h:h + 1] for h in heads]
    bcols = [beta[:, h:h + 1] for h in heads]
    decays = [jnp.where(causal, jnp.exp(jnp.where(causal, gcs[h] - gr_all[h:h + 1, :], 0.0)), 0.0)
              for h in heads]
    kbs = [ks[h] * bcols[h] for h in heads]
    kks = [lax.dot_general(kbs[h].astype(jnp.bfloat16), ks[h].astype(jnp.bfloat16), _NT_DIMS,
                           preferred_element_type=jnp.float32) for h in heads]
    bms = [jnp.where(strict, -(kks[h] * decays[h]), 0.0) for h in heads]
    egs = [jnp.exp(gcs[h]) for h in heads]
    rhss = [jnp.concatenate([vc[:, h * dv:(h + 1) * dv] * bcols[h], kbs[h] * egs[h]], axis=1)
            for h in heads]
    pairs = range(hg // 2)
    left = lax.broadcasted_iota(jnp.int32, (C, 2 * C), 1) < C
    zero16 = jnp.zeros((C, 2 * C), jnp.bfloat16)

    def blockdiag(part):
        return jnp.concatenate([jnp.where(left, part, zero16), jnp.where(left, zero16, part)], axis=0)

    def dot3_pairs(x_parts, y_parts):
        xh, xl = x_parts
        yh, yl = blockdiag(y_parts[0]), blockdiag(y_parts[1])
        out = jnp.dot(xh, yh, preferred_element_type=jnp.float32)
        out = out + jnp.dot(xh, yl, preferred_element_type=jnp.float32)
        return out + jnp.dot(xl, yh, preferred_element_type=jnp.float32)

    b2 = [jnp.concatenate([bms[2 * i], bms[2 * i + 1]], axis=1) for i in pairs]
    eye2 = jnp.concatenate([eye, eye], axis=1)
    p2 = [eye2 + b2[i] for i in pairs]
    b2_parts = [_split_bf16(b2[i]) for i in pairs]
    step = 2
    while step < C:
        b2 = [dot3_pairs(b2_parts[i], b2_parts[i]) for i in pairs]
        b2_parts = [_split_bf16(b2[i]) for i in pairs]
        p2 = [p2[i] + dot3_pairs(_split_bf16(p2[i]), b2_parts[i]) for i in pairs]
        step *= 2
    ps = [p2[h // 2][:, (h % 2) * C:(h % 2 + 1) * C] for h in heads]
    ws = [_dot3(ps[h], rhss[h]) for h in heads]
    qks = [lax.dot_general(qs[h].astype(jnp.bfloat16), ks[h].astype(jnp.bfloat16), _NT_DIMS,
                           preferred_element_type=jnp.float32) * decays[h] for h in heads]
    g_lasts = [gcs[h][C - 1:C, :] for h in heads]
    ss = [s_ref[h] for h in heads]
    us = [ws[h][:, :dv] - _bf16_dot(ws[h][:, dv:], ss[h]) for h in heads]
    os_ = [_bf16_dot(qs[h] * egs[h], ss[h]) + _bf16_dot(qks[h], us[h]) for h in heads]
    for h in heads:
        ke = ks[h] * jnp.exp(g_lasts[h] - gcs[h])
        s_ref[h] = ss[h] * jnp.exp(g_lasts[h]) + _bf16_dot(ke.T, us[h])
    for h in heads:
        o = os_[h]
        o = o * lax.rsqrt(jnp.mean(o * o, axis=1, keepdims=True) + EPS) * nw
        z = z_ref[:, h * dv:(h + 1) * dv]
        o_ref[0, :, h * dv:(h + 1) * dv] = (o * (z * jax.nn.sigmoid(z))).astype(o_ref.dtype)

    @pl.when(n == pl.num_programs(2) - 1)
    def _():
        s_out_ref[0] = s_ref[...]


def _gdn_core(qkv, z, row0, conv_w, conv_state, g, beta, norm_w, s0):
    B, T, VH = g.shape
    dk, dv = s0.shape[2], s0.shape[3]
    vd = VH * dv
    qd = (conv_w.shape[1] - vd) // 2
    C = min(CHUNK, T)
    N = T // C
    assert row0 % C == 0
    blk0 = row0 // C
    hg = min(GDN_HEAD_GROUP, VH)
    ng = VH // hg
    qw = qd // ng
    vw = hg * dv
    assert qd % qw == 0 and (2 * qd) % vw == 0
    k0, v0 = qd // qw, (2 * qd) // vw
    z0 = (z.shape[1] - vd) // vw
    taps = conv_w.shape[0]

    def grouped(a):
        return a.reshape(B, T, ng, hg).transpose(0, 2, 1, 3)

    gg, bg = grouped(g), grouped(beta)
    gt = gg.reshape(B, ng, N, C, hg).transpose(0, 1, 2, 4, 3)
    small = pl.BlockSpec((1, 1, C, hg), lambda b, j, n: (b, j, n, 0))
    o, s = pl.pallas_call(
        _gdn_kernel,
        grid=(B, ng, N),
        in_specs=[pl.BlockSpec((C, qw), lambda b, j, n: (blk0 + b * N + n, j)),
                  pl.BlockSpec((C, qw), lambda b, j, n: (blk0 + b * N + n, k0 + j)),
                  pl.BlockSpec((C, vw), lambda b, j, n: (blk0 + b * N + n, v0 + j)),
                  pl.BlockSpec((taps, qw), lambda b, j, n: (0, j)),
                  pl.BlockSpec((taps, qw), lambda b, j, n: (0, k0 + j)),
                  pl.BlockSpec((taps, vw), lambda b, j, n: (0, v0 + j)),
                  pl.BlockSpec((1, taps - 1, qw), lambda b, j, n: (b, 0, j)),
                  pl.BlockSpec((1, taps - 1, qw), lambda b, j, n: (b, 0, k0 + j)),
                  pl.BlockSpec((1, taps - 1, vw), lambda b, j, n: (b, 0, v0 + j)),
                  pl.BlockSpec((C, vw), lambda b, j, n: (blk0 + b * N + n, z0 + j)),
                  small,
                  pl.BlockSpec((1, 1, 1, hg, C), lambda b, j, n: (b, j, n, 0, 0)),
                  small,
                  pl.BlockSpec((1, dv), lambda b, j, n: (0, 0)),
                  pl.BlockSpec((1, hg, dk, dv), lambda b, j, n: (b, j, 0, 0))],
        out_specs=[pl.BlockSpec((1, C, hg * dv), lambda b, j, n: (b, n, j)),
                   pl.BlockSpec((1, hg, dk, dv), lambda b, j, n: (b, j, 0, 0))],
        out_shape=[jax.ShapeDtypeStruct((B, T, vd), jnp.bfloat16),
                   jax.ShapeDtypeStruct(s0.shape, jnp.float32)],
        scratch_shapes=[pltpu.VMEM((hg, dk, dv), jnp.float32),
                        pltpu.VMEM((SUBLANE + C, qw), jnp.float32),
                        pltpu.VMEM((SUBLANE + C, qw), jnp.float32),
                        pltpu.VMEM((SUBLANE + C, vw), jnp.float32)],
        compiler_params=pltpu.CompilerParams(
            dimension_semantics=("parallel", "parallel", "arbitrary"),
            vmem_limit_bytes=V7X_VMEM_LIMIT_BYTES),
        name="gdn_core",
    )(qkv, qkv, qkv, conv_w, conv_w, conv_w, conv_state, conv_state, conv_state,
      z, gg, gt, bg, norm_w.reshape(1, dv), s0)
    return o, s


def _gdn_layer(x, mod, gate, sid, streams, w_in, conv_w, a_log, dt_bias, norm_w, w_out):
    qk_dim, v_dim = GDN_QK_HEADS * GDN_DK, GDN_V_HEADS * GDN_DV
    conv_dim = 2 * qk_dim + v_dim
    n_gate = 2 * GDN_V_HEADS
    w_main = _weight_parts(w_in[:, :conv_dim + v_dim], 1)
    w_gate = jnp.pad(w_in[:, conv_dim + v_dim:], ((0, 0), (0, _round_up(n_gate, LANE) - n_gate)))
    qkv_all = z_all = pmatmul(x, w_main, mod=mod, sid=sid)
    gates_all = pmatmul(x, _weight_parts(w_gate, 1), tn=w_gate.shape[1], mod=mod, sid=sid)
    outs, states = [], []
    for row0, B, T, conv_state, ssm_state in streams:
        rows = slice(row0, row0 + B * T)
        gates = gates_all[rows].reshape(B, T, -1)
        beta_raw, a_raw = gates[..., :GDN_V_HEADS], gates[..., GDN_V_HEADS:n_gate]
        if conv_state is None:
            conv_state = jnp.zeros((B, CONV_W - 1, conv_dim), x.dtype)
        if ssm_state is None:
            ssm_state = jnp.zeros((B, GDN_V_HEADS, GDN_DK, GDN_DV), jnp.float32)
        tail = jnp.stack([qkv_all[row0 + (b + 1) * T - (CONV_W - 1):row0 + (b + 1) * T, :conv_dim]
                          for b in range(B)])
        new_conv = jnp.concatenate([conv_state, tail], axis=1)[:, -(CONV_W - 1):]
        beta = jax.nn.sigmoid(beta_raw)
        g = -jnp.exp(a_log) * jax.nn.softplus(a_raw + dt_bias)
        o, S = _gdn_core(qkv_all, z_all, row0, conv_w, conv_state, g, beta, norm_w, ssm_state)
        outs.append(o.reshape(B * T, v_dim))
        states.append((new_conv, S))
    x = pmatmul(jnp.concatenate(outs, axis=0), _weight_parts(w_out, 1), res=(x, gate), sid=sid)
    return x, states


NEG_INF = float("-inf")
POS_INF = float("inf")
_PEER_CAND_ROWS = tuple((i, PEER_TOPK // (i + 1)) for i in range(PEER_TOPK // 2))


def _split_bf16(x):
    hi = x.astype(jnp.bfloat16)
    lo = (x - hi.astype(jnp.float32)).astype(jnp.bfloat16)
    return hi, lo


def _dot3_nt(a, b):
    dn = (((1,), (1,)), ((), ()))
    ah, al = _split_bf16(a)
    bh, bl = _split_bf16(b)
    out = lax.dot_general(ah, bh, dn, preferred_element_type=jnp.float32)
    out = out + lax.dot_general(ah, bl, dn, preferred_element_type=jnp.float32)
    return out + lax.dot_general(al, bh, dn, preferred_element_type=jnp.float32)


def _top_rows_desc(s, n, with_rank=False):
    rows = []
    cur = s
    rank = jnp.full(s.shape, float(n), jnp.float32)
    for i in range(n):
        m = jnp.max(cur, axis=0, keepdims=True)
        rows.append(m)
        hit = cur == m
        if with_rank:
            rank = jnp.where(hit, float(i), rank)
        cur = jnp.where(hit, NEG_INF, cur)
    return (rows, rank) if with_rank else rows


def _stack_rows(rows, lanes):
    n = len(rows)
    rid = lax.broadcasted_iota(jnp.int32, (n, lanes), 0)
    out = jnp.zeros((n, lanes), jnp.float32)
    for i, r in enumerate(rows):
        out = jnp.where(rid == i, r, out)
    return out


def _peer_select_kernel(q_ref, keys_ref, rk2_ref, e2_ref, cnt_ref, e1_ref):
    tm = q_ref.shape[0]
    k = PEER_TOPK
    s1 = _dot3_nt(keys_ref[0], q_ref[:, :PEER_HALF])
    s2 = _dot3_nt(keys_ref[1], q_ref[:, PEER_HALF:])
    r1 = _top_rows_desc(s1, k)
    r2, rank2 = _top_rows_desc(s2, k, with_rank=True)
    v1 = _stack_rows(r1, tm)
    v2 = _stack_rows(r2, tm)
    v2h = v2[:k // 2]
    rid = lax.broadcasted_iota(jnp.int32, (k // 2, tm), 0)
    pieces = [r1[0] + v2]
    for i, n in _PEER_CAND_ROWS[1:]:
        pieces.append(jnp.where(rid < n, r1[i] + v2h, NEG_INF))
    pieces.append(v1[k // 2:] + r2[0])
    cand = jnp.concatenate(pieces, axis=0)
    tau = _top_rows_desc(cand, k)[-1]
    top = r1[0] + r2[0]
    z = jnp.sum(jnp.where(cand >= tau, jnp.exp(cand - top), 0.0), axis=0, keepdims=True)
    cnt = jnp.zeros(s1.shape, jnp.float32)
    for j in range(k):
        cnt = cnt + jnp.where(s1 + r2[j] >= tau, 1.0, 0.0)
    rk2_ref[0] = pltpu.bitcast(rank2.astype(jnp.bfloat16), rk2_ref.dtype)
    e2_ref[0] = pltpu.bitcast((jnp.exp(s2 - r2[0]) / z).astype(jnp.bfloat16), e2_ref.dtype)
    cnt_ref[0] = cnt
    e1_ref[0] = jnp.exp(s1 - r1[0])


def _peer_select(q, keys, *, tm):
    M = q.shape[0]
    nk = keys.shape[1]
    ospec = pl.BlockSpec((1, nk, tm), lambda i, h: (h, 0, i))
    pspec = pl.BlockSpec((1, nk // 2, tm), lambda i, h: (h, 0, i))
    return pl.pallas_call(
        _peer_select_kernel,
        grid=(M // tm, PEER_HEADS),
        in_specs=[pl.BlockSpec((tm, PEER_KEY_DIM), lambda i, h: (i, h)),
                  pl.BlockSpec(keys.shape, lambda i, h: (0, 0, 0))],
        out_specs=[pspec, pspec, ospec, ospec],
        out_shape=[jax.ShapeDtypeStruct((PEER_HEADS, nk // 2, M), jnp.int32)] * 2
        + [jax.ShapeDtypeStruct((PEER_HEADS, nk, M), jnp.float32)] * 2,
        compiler_params=pltpu.CompilerParams(
            dimension_semantics=("parallel", "arbitrary"),
            vmem_limit_bytes=V7X_VMEM_LIMIT_BYTES),
        name="peer_select",
    )(q, keys)


BF16_SUBLANES = 16
PEER_SUB_EXPERTS = 512


def _peer_main_kernel(sid_ref, x_ref, gain_ref, shift_ref, scale_ref, gate_ref, fnorm_ref,
                      u_ref, vT_ref, rk2_ref, e2_ref, cnt_ref, e1_ref, *refs, final_norm):
    out_refs, (hT_ref, coef_ref, acc_ref) = refs[:-3], refs[-3:]
    e = pl.program_id(1)
    te, tm = coef_ref.shape
    nk = 2 * rk2_ref.shape[1]
    rows = BF16_SUBLANES
    sub = PEER_SUB_EXPERTS
    blk0 = pl.program_id(0) * (tm // SEQ_BLOCK)

    @pl.when(e == 0)
    def _():
        acc_ref[...] = jnp.zeros_like(acc_ref)
        per_lane_tile = LANE // SEQ_BLOCK
        for t in range(tm // LANE):
            hs = []
            for r in range(t * per_lane_tile, (t + 1) * per_lane_tile):
                rs = slice(r * SEQ_BLOCK, (r + 1) * SEQ_BLOCK)
                hs.append(_modulated_rows(x_ref[rs, :], gain_ref[...], shift_ref, scale_ref, sid_ref[blk0 + r]))
            hT_ref[:, t * LANE:(t + 1) * LANE] = jnp.concatenate(hs, axis=0).T.astype(jnp.bfloat16)

    def expert_acts(sb):
        return jnp.dot(u_ref[sb * sub:(sb + 1) * sub, :], hT_ref[...], preferred_element_type=jnp.float32)

    def weigh(sb, act):
        for al in range(sub // nk):
            a = sb * (sub // nk) + al
            for lg in range(tm // LANE):
                lanes = slice(lg * LANE, (lg + 1) * LANE)
                cnts = [jnp.broadcast_to(cnt_ref[h, a:a + 1, lanes], (rows, LANE)).astype(jnp.bfloat16)
                        for h in range(PEER_HEADS)]
                e1s = [jnp.broadcast_to(e1_ref[h, a:a + 1, lanes], (rows, LANE)).astype(jnp.bfloat16)
                       for h in range(PEER_HEADS)]
                for r in range(nk // rows):
                    words = slice(r * rows // 2, (r + 1) * rows // 2)
                    w = None
                    for h in range(PEER_HEADS):
                        e2 = pltpu.bitcast(e2_ref[h, words, lanes], jnp.bfloat16)
                        rk2 = pltpu.bitcast(rk2_ref[h, words, lanes], jnp.bfloat16)
                        t = jnp.where(rk2 < cnts[h], e2, jnp.zeros_like(e2)) * e1s[h]
                        w = t if w is None else w + t
                    x = act[al * nk + r * rows:al * nk + (r + 1) * rows, lanes]
                    g = 0.5 * x * (1.0 + lax.erf(x * (2.0 ** -0.5)))
                    coef_ref[sb * sub + al * nk + r * rows:sb * sub + al * nk + (r + 1) * rows, lanes] = (
                        w * g.astype(jnp.bfloat16))

    def accumulate(sb):
        acc_ref[...] += lax.dot_general(vT_ref[sb * sub:(sb + 1) * sub, :], coef_ref[sb * sub:(sb + 1) * sub, :],
                                        (((0,), (0,)), ((), ())), preferred_element_type=jnp.float32)

    n_sub = te // sub
    act = expert_acts(0)
    for sb in range(n_sub):
        nxt = expert_acts(sb + 1) if sb + 1 < n_sub else None
        weigh(sb, act)
        accumulate(sb)
        act = nxt

    @pl.when(e == pl.num_programs(1) - 1)
    def _():
        out = acc_ref[...].T
        for r in range(tm // SEQ_BLOCK):
            rs = slice(r * SEQ_BLOCK, (r + 1) * SEQ_BLOCK)
            y = x_ref[rs, :] + gate_ref[pl.ds(sid_ref[blk0 + r], 1), :] * out[rs, :]
            out_refs[0][rs, :] = y
            if final_norm:
                out_refs[1][rs, :] = y * lax.rsqrt(jnp.mean(y * y, axis=-1, keepdims=True) + EPS) * fnorm_ref[...]


def _peer_main(x, mod, gate, fnorm, sid, u, vT, rk2, e2, cnt, e1, *, tm, te, final_norm):
    M, D = x.shape
    E = u.shape[0]
    nk = cnt.shape[1]
    const = lambda a: pl.BlockSpec(a.shape, lambda i, e, s: (0,) * a.ndim)
    col_spec = pl.BlockSpec((PEER_HEADS, nk // 2, tm), lambda i, e, s: (0, 0, i))
    row_spec = pl.BlockSpec((PEER_HEADS, te // nk, tm), lambda i, e, s: (0, e, i))
    tok_spec = pl.BlockSpec((tm, D), lambda i, e, s: (i, 0))
    n_out = 2 if final_norm else 1
    return pl.pallas_call(
        functools.partial(_peer_main_kernel, final_norm=final_norm),
        grid_spec=pltpu.PrefetchScalarGridSpec(
            num_scalar_prefetch=1,
            grid=(M // tm, E // te),
            in_specs=[tok_spec, const(mod[0]), const(mod[1]), const(mod[2]), const(gate), const(fnorm),
                      pl.BlockSpec((te, D), lambda i, e, s: (e, 0)),
                      pl.BlockSpec((te, D), lambda i, e, s: (e, 0)),
                      col_spec, col_spec, row_spec, row_spec],
            out_specs=[tok_spec] * n_out,
            scratch_shapes=[pltpu.VMEM((D, tm), jnp.bfloat16),
                            pltpu.VMEM((te, tm), jnp.bfloat16),
                            pltpu.VMEM((D, tm), jnp.float32)]),
        out_shape=[jax.ShapeDtypeStruct((M, D), jnp.float32)] * n_out,
        compiler_params=pltpu.CompilerParams(
            dimension_semantics=("parallel", "arbitrary"),
            vmem_limit_bytes=V7X_VMEM_LIMIT_PEER_BYTES),
        name="peer_main",
    )(sid, x, *mod, gate, fnorm, u, vT, rk2, e2, cnt, e1)


def _peer(x, mod, gate, fnorm, sid, w_query, sub_keys, expert_u, expert_v, *, final_norm,
          tm_sel=256, tm=512, te=1024):
    M = x.shape[0]
    q = pmatmul(x, _weight_parts(w_query, 3), mod=mod, sid=sid)
    rk2, e2, cnt, e1 = _peer_select(q, sub_keys, tm=min(tm_sel, M))
    u = expert_u.astype(jnp.bfloat16)
    vT = expert_v.astype(jnp.bfloat16)
    return _peer_main(x, mod, gate, fnorm, sid, u, vT, rk2, e2, cnt, e1, tm=min(tm, M), te=te,
                      final_norm=final_norm)


def kernel(x_prompt, x_sample, c_prompt, c_sample, cache_k_l0, cache_v_l0, cache_kidx_l0, state_conv_l1, state_ssm_l1, norm1_l0, norm2_l0, ada_w_l0, ada_b_l0, attn_in_l0, attn_out_l0, peer_query_l0, peer_keys_l0, peer_u_l0, peer_v_l0, norm1_l1, norm2_l1, ada_w_l1, ada_b_l1, gdn_in_l1, gdn_conv_l1, gdn_a_log_l1, gdn_dt_bias_l1, gdn_norm_l1, gdn_out_l1, peer_query_l1, peer_keys_l1, peer_u_l1, peer_v_l1, final_norm):
    past_len = cache_k_l0.shape[1]
    norm1 = (norm1_l0, norm1_l1)
    norm2 = (norm2_l0, norm2_l1)
    ada_w = (ada_w_l0, ada_w_l1)
    ada_b = (ada_b_l0, ada_b_l1)
    peer_query = (peer_query_l0, peer_query_l1)
    peer_keys = (peer_keys_l0, peer_keys_l1)
    peer_u = (peer_u_l0, peer_u_l1)
    peer_v = (peer_v_l0, peer_v_l1)
    Bp, Tp, D = x_prompt.shape
    Bs, Ts, _ = x_sample.shape
    n_p, n_s = Bp * Tp, Bs * Ts
    x = jnp.concatenate([x_prompt.reshape(n_p, D), x_sample.reshape(n_s, D)], axis=0)
    sid = jnp.concatenate([jnp.repeat(jnp.arange(Bp, dtype=jnp.int32), Tp // SEQ_BLOCK),
                           Bp + jnp.repeat(jnp.arange(Bs, dtype=jnp.int32), Ts // SEQ_BLOCK)])
    c_all = jnp.concatenate([c_prompt, c_sample], axis=0)
    fnorm = final_norm.reshape(1, D)
    for i in range(2):
        mods = _adaln(c_all, ada_w[i], ada_b[i])
        mod1 = (norm1[i].reshape(1, D), mods[0], mods[1])
        if i == 0:
            x, ((nkp, nvp, nkip), (nks, nvs, nkis)) = _dsa_layer(
                x, mod1, mods[2], sid,
                [(0, Bp, Tp, 0, None, None, None), (n_p, Bs, Ts, past_len, cache_k_l0, cache_v_l0, cache_kidx_l0)],
                attn_in_l0, attn_out_l0)
        else:
            x, ((ncp, nsp), (ncs, nss)) = _gdn_layer(
                x, mod1, mods[2], sid, [(0, Bp, Tp, None, None), (n_p, Bs, Ts, state_conv_l1, state_ssm_l1)],
                gdn_in_l1, gdn_conv_l1, gdn_a_log_l1, gdn_dt_bias_l1, gdn_norm_l1, gdn_out_l1)
        mod2 = (norm2[i].reshape(1, D), mods[3], mods[4])
        outs = _peer(x, mod2, mods[5], fnorm, sid, peer_query[i], peer_keys[i], peer_u[i], peer_v[i],
                     final_norm=(i == 1))
        x = outs[0]
    y = outs[1]
    y_prompt = y[:n_p].reshape(Bp, Tp, D)
    y_sample = y[n_p:].reshape(Bs, Ts, D)
    return (y_prompt, y_sample, nkp, nvp, nkip, nks, nvs, nkis, ncp, nsp, ncs, nss)
```

```python
import functools

import jax
import jax.numpy as jnp
from jax import lax
from jax.experimental import pallas as pl
from jax.experimental.pallas import tpu as pltpu

CHUNK = 64
EPS = 1e-6
ROPE_THETA = 500000.0
ROPE_FRACTION = 4
A_HEADS = 16
A_KV_HEADS = 4
IDX_HEADS = 8
IDX_DIM = 64
TOPK_MAX = 256
Q_BLOCK = 128
GDN_QK_HEADS = 16
GDN_V_HEADS = 32
GDN_DK = 128
GDN_DV = 128
CONV_W = 4
PEER_HEADS = 8
PEER_NKEYS = 128
PEER_KEY_DIM = 256
PEER_HALF = PEER_KEY_DIM // 2
PEER_TOPK = 16
PEER_BLOCK = 64

V7X_VMEM_LIMIT_BYTES = 48 * 1024 * 1024
V7X_VMEM_LIMIT_PEER_BYTES = 56 * 1024 * 1024
LANE = 128
SUBLANE = 8


def _round_up(n, m):
    return (n + m - 1) // m * m


SEQ_BLOCK = CHUNK


def _modulated_rows(x, gain, shift_ref, scale_ref, sid):
    y = x * lax.rsqrt(jnp.mean(x * x, axis=-1, keepdims=True) + EPS)
    return y * gain * (1.0 + scale_ref[pl.ds(sid, 1), :]) + shift_ref[pl.ds(sid, 1), :]


def _matmul_kernel(sid_ref, x_ref, *refs, passes, has_mod, has_res):
    refs = list(refs)
    mod_refs = [refs.pop(0) for _ in range(3)] if has_mod else None
    nparts = 1 + passes // 2
    w_refs = [refs.pop(0) for _ in range(nparts)]
    res_refs = [refs.pop(0) for _ in range(2)] if has_res else None
    o_ref, x_parts = refs[0], refs[1:]
    tm = x_ref.shape[0]
    blocks = range(tm // SEQ_BLOCK) if (has_mod or has_res) else ()
    blk0 = pl.program_id(0) * (tm // SEQ_BLOCK)

    @pl.when(pl.program_id(1) == 0)
    def _():
        def put(rows, x):
            hi = x.astype(jnp.bfloat16)
            x_parts[0][rows, :] = hi
            if passes == 3:
                x_parts[1][rows, :] = (x - hi.astype(jnp.float32)).astype(jnp.bfloat16)

        if has_mod:
            gain_ref, shift_ref, scale_ref = mod_refs
            for r in blocks:
                rows = slice(r * SEQ_BLOCK, (r + 1) * SEQ_BLOCK)
                put(rows, _modulated_rows(x_ref[rows, :], gain_ref[...], shift_ref, scale_ref, sid_ref[blk0 + r]))
        else:
            put(slice(None), x_ref[...].astype(jnp.float32))

    acc = jnp.dot(x_parts[0][...], w_refs[0][...], preferred_element_type=jnp.float32)
    if passes == 3:
        acc = acc + jnp.dot(x_parts[0][...], w_refs[1][...], preferred_element_type=jnp.float32)
        acc = acc + jnp.dot(x_parts[1][...], w_refs[0][...], preferred_element_type=jnp.float32)
    if has_res:
        res_ref, gate_ref = res_refs
        for r in blocks:
            rows = slice(r * SEQ_BLOCK, (r + 1) * SEQ_BLOCK)
            gate = gate_ref[pl.ds(sid_ref[blk0 + r], 1), :]
            o_ref[rows, :] = res_ref[rows, :] + gate * acc[rows, :]
    else:
        o_ref[...] = acc.astype(o_ref.dtype)


def _weight_parts(w, passes):
    hi = w.astype(jnp.bfloat16)
    if passes == 1:
        return (hi,)
    return (hi, (w - hi.astype(jnp.float32)).astype(jnp.bfloat16))


def pmatmul(x, w_parts, *, col0=0, ncols=None, tm=512, tn=1024, out_dtype=jnp.float32,
            mod=None, res=None, sid=None):
    passes = 1 if len(w_parts) == 1 else 3
    M, K = x.shape
    n_total = w_parts[0].shape[1]
    ncols = n_total - col0 if ncols is None else ncols
    tm = min(tm, _round_up(M, 2 * SUBLANE))
    tn = min(tn, ncols)
    assert ncols % tn == 0 and col0 % tn == 0 and (tn % LANE == 0 or tn == n_total)
    Mp = _round_up(M, tm)
    if mod is not None or res is not None:
        assert Mp == M and tm % SEQ_BLOCK == 0 and sid is not None
    else:
        sid = jnp.zeros((1,), jnp.int32)
    if Mp != M:
        x = jnp.pad(x, ((0, Mp - M), (0, 0)))
    c0 = col0 // tn
    in_specs = [pl.BlockSpec((tm, K), lambda i, j, s: (i, 0))]
    args = [x]
    if mod is not None:
        in_specs += [pl.BlockSpec(a.shape, lambda i, j, s: (0, 0)) for a in mod]
        args += list(mod)
    in_specs += [pl.BlockSpec((K, tn), lambda i, j, s: (0, c0 + j))] * len(w_parts)
    args += list(w_parts)
    if res is not None:
        in_specs += [pl.BlockSpec((tm, tn), lambda i, j, s: (i, j)),
                     pl.BlockSpec((res[1].shape[0], tn), lambda i, j, s: (0, j))]
        args += list(res)
        out_dtype = jnp.float32
    out = pl.pallas_call(
        functools.partial(_matmul_kernel, passes=passes, has_mod=mod is not None, has_res=res is not None),
        grid_spec=pltpu.PrefetchScalarGridSpec(
            num_scalar_prefetch=1,
            grid=(Mp // tm, ncols // tn),
            in_specs=in_specs,
            out_specs=pl.BlockSpec((tm, tn), lambda i, j, s: (i, j)),
            scratch_shapes=[pltpu.VMEM((tm, K), jnp.bfloat16)] * (1 + passes // 2)),
        out_shape=jax.ShapeDtypeStruct((Mp, ncols), out_dtype),
        compiler_params=pltpu.CompilerParams(
            dimension_semantics=("parallel", "arbitrary"),
            vmem_limit_bytes=V7X_VMEM_LIMIT_BYTES),
        name="matmul",
    )(sid, *args)
    return out[:M]


def _adaln_kernel(c_ref, w_ref, b_ref, o_ref):
    c = c_ref[...]
    o_ref[...] = _dot3(c * jax.nn.sigmoid(c), w_ref[...]) + b_ref[...]


def _adaln(c, w, b, *, tn=512):
    n, D = c.shape
    N = w.shape[1]
    rows = _round_up(n, SUBLANE)
    mod = pl.pallas_call(
        _adaln_kernel,
        grid=(N // tn,),
        in_specs=[pl.BlockSpec((rows, D), lambda j: (0, 0)),
                  pl.BlockSpec((D, tn), lambda j: (0, j)),
                  pl.BlockSpec((1, tn), lambda j: (0, j))],
        out_specs=pl.BlockSpec((rows, tn), lambda j: (0, j)),
        out_shape=jax.ShapeDtypeStruct((rows, N), jnp.float32),
        compiler_params=pltpu.CompilerParams(
            dimension_semantics=("parallel",), vmem_limit_bytes=V7X_VMEM_LIMIT_BYTES),
        name="adaln",
    )(jnp.pad(c, ((0, rows - n), (0, 0))), w, b.reshape(1, N))
    return jnp.split(mod, 6, axis=-1)


_NT_DIMS = (((1,), (1,)), ((), ()))
INT32_MIN = -2 ** 31
LOG2_E = 1.4426950408889634
_NEG_INF_KEY = -2139095041
IDX_PACK = 4 * IDX_DIM


def _ordered_key(x):
    bits = pltpu.bitcast(x, jnp.int32)
    return bits ^ ((bits >> 31) & 0x7FFFFFFF)


def _lane_tile_sum(x, width=LANE):
    out = x[:, :width]
    for c in range(1, x.shape[1] // width):
        out = out + x[:, c * width:(c + 1) * width]
    return out


def _dsa_select_bias(qi_ref, wi_ref, kidx_ref, key_ref, bias_ref, *, first, n_tiles, topk, tk):
    tq = wi_ref.shape[1]
    row = lax.broadcasted_iota(jnp.int32, (tq, 1), 0)
    lim = (((first + row) >> 6) + 1) * CHUNK
    w = wi_ref[0]

    def score_tile(j, c):
        off = pl.multiple_of(j * tk, tk)
        kt = kidx_ref[0, pl.ds(off, tk), :]
        sc = lax.dot_general(qi_ref[0, 0], kt, _NT_DIMS,
                             preferred_element_type=jnp.float32)
        s = jnp.zeros((tq, tk), jnp.float32)
        for h in range(IDX_HEADS):
            s = s + w[:, h:h + 1] * jnp.maximum(sc[h * tq:(h + 1) * tq], 0.0)
        col = off + lax.broadcasted_iota(jnp.int32, (tq, tk), 1)
        s = jnp.where(col < lim, s + 0.0, NEG_INF)
        key_ref[:, pl.ds(off, tk)] = _ordered_key(s)
        return c

    lax.fori_loop(0, n_tiles, score_tile, 0)

    def bit_step(b, thr):
        cand = thr + lax.shift_left(jnp.int32(1), 31 - b)

        def count_tile(j, c):
            off = pl.multiple_of(j * tk, tk)
            ge = jnp.where(key_ref[:, pl.ds(off, tk)] >= cand, 1.0, 0.0)
            return c + _lane_tile_sum(ge)

        c = lax.fori_loop(0, n_tiles, count_tile, jnp.zeros((tq, LANE), jnp.float32))
        cnt = jnp.sum(c, axis=1, keepdims=True)
        return jnp.where(cnt >= topk, cand, thr)

    thr = lax.fori_loop(0, 32, bit_step, jnp.full((tq, 1), INT32_MIN, jnp.int32))
    thr = jnp.maximum(thr, _NEG_INF_KEY + 1)

    def bias_tile(j, c):
        off = pl.multiple_of(j * tk, tk)
        bias_ref[:, pl.ds(off, tk)] = jnp.where(key_ref[:, pl.ds(off, tk)] >= thr, 0.0, NEG_INF)
        return c

    lax.fori_loop(0, n_tiles, bias_tile, 0)


def _dsa_kernel(qi_ref, wi_ref, kidx_ref, q_ref, k_ref, v_ref, o_ref,
                key_ref, bias_ref, qg_ref, m_ref, l_ref, acc_ref, *, pos0, topk, tk):
    i = pl.program_id(1)
    tq = q_ref.shape[1]
    hd = acc_ref.shape[2]
    groups = acc_ref.shape[0]
    rep = q_ref.shape[2] // (groups * hd)
    first = pos0 + i * tq
    n_valid = (((first + tq - 1) >> 6) + 1) * CHUNK
    n_tiles = (n_valid + tk - 1) // tk
    _dsa_select_bias(qi_ref, wi_ref, kidx_ref, key_ref, bias_ref, first=first, n_tiles=n_tiles, topk=topk, tk=tk)

    for g in range(groups):
        for r in range(rep):
            c0 = (g * rep + r) * hd
            qg_ref[g, r * tq:(r + 1) * tq, :] = q_ref[0, :, c0:c0 + hd]
    m_ref[...] = jnp.full(m_ref.shape, NEG_INF, jnp.float32)
    l_ref[...] = jnp.zeros(l_ref.shape, jnp.float32)
    acc_ref[...] = jnp.zeros(acc_ref.shape, jnp.float32)
    lane_reps = tk // LANE

    def att_tile(j, c):
        off = pl.multiple_of(j * tk, tk)
        b = bias_ref[:, pl.ds(off, tk)]
        bias = jnp.concatenate([b] * rep, axis=0)
        for g in range(groups):
            kt = k_ref[0, pl.ds(off, tk), g * hd:(g + 1) * hd]
            vt = v_ref[0, pl.ds(off, tk), g * hd:(g + 1) * hd]
            lg = lax.dot_general(qg_ref[g], kt, _NT_DIMS, preferred_element_type=jnp.float32) + bias
            m_old = m_ref[g]
            m_new = jnp.maximum(m_old, jnp.max(lg, axis=1, keepdims=True))
            m_safe = jnp.where(m_new == NEG_INF, 0.0, m_new)
            p = jnp.exp2(lg - jnp.tile(m_safe, (1, lane_reps)))
            alpha = jnp.exp2(m_old - m_safe)
            l_ref[g] = alpha * l_ref[g] + jnp.sum(p, axis=1, keepdims=True)
            acc_ref[g] = alpha * acc_ref[g] + jnp.dot(p.astype(jnp.bfloat16), vt,
                                                      preferred_element_type=jnp.float32)
            m_ref[g] = m_new
        return c

    lax.fori_loop(0, n_tiles, att_tile, 0)
    for g in range(groups):
        out = acc_ref[g] / l_ref[g]
        for r in range(rep):
            c0 = (g * rep + r) * hd
            o_ref[0, :, c0:c0 + hd] = out[r * tq:(r + 1) * tq].astype(o_ref.dtype)


def _dsa_core(q, qi3, wi, kb, vb, kidx3, pos0, topk, *, tq, tk):
    B, T, qd = q.shape
    Sp = kb.shape[1]
    hd = qd // A_HEADS
    gw = qd // A_KV_HEADS
    nb = T // tq
    assert Sp % tk == 0 and hd == LANE
    rows = (gw // hd) * tq
    kvw = A_KV_HEADS * hd
    return pl.pallas_call(
        functools.partial(_dsa_kernel, pos0=pos0, topk=topk, tk=tk),
        grid=(B, nb),
        in_specs=[pl.BlockSpec((1, 1, IDX_HEADS * tq, IDX_PACK), lambda b, i: (b, i, 0, 0)),
                  pl.BlockSpec((1, tq, IDX_HEADS), lambda b, i: (b, i, 0)),
                  pl.BlockSpec((1, Sp, IDX_PACK), lambda b, i: (b, 0, 0)),
                  pl.BlockSpec((1, tq, qd), lambda b, i: (b, i, 0)),
                  pl.BlockSpec((1, Sp, kvw), lambda b, i: (b, 0, 0)),
                  pl.BlockSpec((1, Sp, kvw), lambda b, i: (b, 0, 0))],
        out_specs=pl.BlockSpec((1, tq, qd), lambda b, i: (b, i, 0)),
        out_shape=jax.ShapeDtypeStruct((B, T, qd), jnp.bfloat16),
        scratch_shapes=[pltpu.VMEM((tq, Sp), jnp.int32),
                        pltpu.VMEM((tq, Sp), jnp.float32),
                        pltpu.VMEM((A_KV_HEADS, rows, hd), jnp.bfloat16),
                        pltpu.VMEM((A_KV_HEADS, rows, LANE), jnp.float32),
                        pltpu.VMEM((A_KV_HEADS, rows, LANE), jnp.float32),
                        pltpu.VMEM((A_KV_HEADS, rows, hd), jnp.float32)],
        compiler_params=pltpu.CompilerParams(
            dimension_semantics=("parallel", "arbitrary"),
            vmem_limit_bytes=V7X_VMEM_LIMIT_BYTES),
        name="dsa_core",
    )(qi3, wi, kidx3, q, kb, vb)


def _rope_tables(pos, width, period):
    half = period // ROPE_FRACTION // 2
    inv_freq = ROPE_THETA ** (-jnp.arange(half, dtype=jnp.float32) / half)
    ang = pos.astype(jnp.float32)[:, None] * inv_freq[None, :]
    cos, sin = jnp.cos(ang), jnp.sin(ang)
    T = pos.shape[0]
    rest = period - 2 * half
    c = jnp.concatenate([cos, cos, jnp.ones((T, rest), jnp.float32)], axis=1)
    s_next = jnp.concatenate([-sin, jnp.zeros((T, period - half), jnp.float32)], axis=1)
    s_prev = jnp.concatenate([jnp.zeros((T, half), jnp.float32), sin, jnp.zeros((T, rest), jnp.float32)], axis=1)
    return jnp.stack([jnp.tile(t, (1, width // period)) for t in (c, s_next, s_prev)])


def _rope_lanes(x, tab_ref, half):
    return (x * tab_ref[0] + pltpu.roll(x, LANE - half, 1) * tab_ref[1] + pltpu.roll(x, half, 1) * tab_ref[2])


def _dsa_prep_kernel(q_ref, kv_ref, idx_ref, tq_ref, ti_ref,
                     qa_ref, kn_ref, vn_ref, kin_ref, kb_ref, vb_ref, qi3_ref, kidx3_ref, wi_ref, *, q_scale):
    tq = q_ref.shape[0]
    hd = LANE
    kvw = kv_ref.shape[1] // 2
    half_q = hd // ROPE_FRACTION // 2
    half_i = IDX_DIM // ROPE_FRACTION // 2
    left = lax.broadcasted_iota(jnp.int32, (tq, LANE), 1) < IDX_DIM
    zero = jnp.zeros((tq, LANE), jnp.float32)

    for h in range(q_ref.shape[1] // hd):
        cols = slice(h * hd, (h + 1) * hd)
        qa_ref[0, :, cols] = (_rope_lanes(q_ref[:, cols], tq_ref, half_q) * q_scale).astype(qa_ref.dtype)
    for h in range(kvw // hd):
        cols = slice(h * hd, (h + 1) * hd)
        k = _rope_lanes(kv_ref[:, cols], tq_ref, half_q)
        kn_ref[0, :, h, :] = k
        kb_ref[0, :, cols] = k.astype(kb_ref.dtype)
        vn_ref[0, :, h, :] = kv_ref[:, kvw + h * hd:kvw + (h + 1) * hd]
    vb_ref[0] = kv_ref[:, kvw:].astype(vb_ref.dtype)

    def hi_lo(x):
        hi = x.astype(jnp.bfloat16).astype(jnp.float32)
        return hi, x - hi

    for t in range(IDX_HEADS * IDX_DIM // LANE):
        hi, lo = hi_lo(_rope_lanes(idx_ref[:, t * LANE:(t + 1) * LANE], ti_ref, half_i))
        hi_sw, lo_sw = pltpu.roll(hi, IDX_DIM, 1), pltpu.roll(lo, IDX_DIM, 1)
        even = jnp.concatenate([jnp.where(left, hi, lo_sw), jnp.where(left, hi, zero)], axis=1)
        odd = jnp.concatenate([jnp.where(left, hi_sw, lo), jnp.where(left, hi_sw, zero)], axis=1)
        qi3_ref[0, 0, (2 * t) * tq:(2 * t + 1) * tq, :] = even.astype(qi3_ref.dtype)
        qi3_ref[0, 0, (2 * t + 1) * tq:(2 * t + 2) * tq, :] = odd.astype(qi3_ref.dtype)
    c0 = IDX_HEADS * IDX_DIM
    x = idx_ref[:, c0:c0 + LANE]
    r = jnp.where(left, _rope_lanes(x, ti_ref, half_i), x)
    kin_ref[0] = r[:, :IDX_DIM]
    hi, lo = hi_lo(r)
    kidx3 = jnp.concatenate([jnp.where(left, hi, pltpu.roll(hi, IDX_DIM, 1)), jnp.where(left, lo, zero)], axis=1)
    kidx3_ref[0] = kidx3.astype(kidx3_ref.dtype)
    wi_ref[0] = x[:, IDX_DIM:IDX_DIM + IDX_HEADS] * (IDX_HEADS ** -0.5 * IDX_DIM ** -0.5)


def _dsa_prep(q_all, kv_all, idx_all, row0, B, T, pos0, *, tq):
    qd, kv2 = A_HEADS * LANE, 2 * A_KV_HEADS * LANE
    kv_col = (kv_all.shape[1] - kv2) // kv2
    kvw = kv2 // 2
    nb = T // tq
    blk0 = row0 // tq
    assert row0 % tq == 0 and qd // A_HEADS == LANE
    pos = pos0 + jnp.arange(T, dtype=jnp.int32)
    tab_q = _rope_tables(pos, LANE, LANE)
    tab_i = _rope_tables(pos, LANE, IDX_DIM)
    row = lambda w: pl.BlockSpec((tq, w), lambda b, i: (blk0 + b * nb + i, 0))
    tab = pl.BlockSpec((3, tq, LANE), lambda b, i: (0, i, 0))
    out = lambda w: pl.BlockSpec((1, tq, w), lambda b, i: (b, i, 0))
    f32, bf16 = jnp.float32, jnp.bfloat16
    heads4 = pl.BlockSpec((1, tq, A_KV_HEADS, LANE), lambda b, i: (b, i, 0, 0))
    shapes = [((B, T, qd), bf16), ((B, T, A_KV_HEADS, LANE), f32), ((B, T, A_KV_HEADS, LANE), f32),
              ((B, T, IDX_DIM), f32),
              ((B, T, kvw), bf16), ((B, T, kvw), bf16), ((B, nb, IDX_HEADS * tq, IDX_PACK), bf16),
              ((B, T, IDX_PACK), bf16), ((B, T, IDX_HEADS), f32)]
    return pl.pallas_call(
        functools.partial(_dsa_prep_kernel, q_scale=LANE ** -0.5 * LOG2_E),
        grid=(B, nb),
        in_specs=[row(qd), pl.BlockSpec((tq, kv2), lambda b, i: (blk0 + b * nb + i, kv_col)),
                  row(idx_all.shape[1]), tab, tab],
        out_specs=[out(qd), heads4, heads4, out(IDX_DIM), out(kvw), out(kvw),
                   pl.BlockSpec((1, 1, IDX_HEADS * tq, IDX_PACK), lambda b, i: (b, i, 0, 0)),
                   out(IDX_PACK), out(IDX_HEADS)],
        out_shape=[jax.ShapeDtypeStruct(s, d) for s, d in shapes],
        compiler_params=pltpu.CompilerParams(
            dimension_semantics=("parallel", "parallel"), vmem_limit_bytes=V7X_VMEM_LIMIT_BYTES),
        name="dsa_prep",
    )(q_all, kv_all, idx_all, tab_q, tab_i)


def _dsa_layer(x, mod, gate, sid, streams, w_in, w_out):
    D = x.shape[1]
    hd = D // A_HEADS
    q_dim, kv_dim = A_HEADS * hd, A_KV_HEADS * hd
    n_idx = IDX_HEADS * IDX_DIM + IDX_DIM + IDX_HEADS
    w_main = _weight_parts(w_in[:, :q_dim + 2 * kv_dim], 1)
    w_idx = jnp.pad(w_in[:, q_dim + 2 * kv_dim:], ((0, 0), (0, _round_up(n_idx, LANE) - n_idx)))
    q_all = kv_all = pmatmul(x, w_main, mod=mod, sid=sid)
    idx_all = pmatmul(x, _weight_parts(w_idx, 3), tn=w_idx.shape[1], mod=mod, sid=sid)
    outs, caches = [], []
    for row0, B, T, pos0, past_k, past_v, past_kidx in streams:
        o, k, v, ki = _dsa_stream(q_all, kv_all, idx_all, row0, B, T, pos0, past_k, past_v, past_kidx)
        outs.append(o.reshape(B * T, q_dim))
        caches.append((k, v, ki))
    x = pmatmul(jnp.concatenate(outs, axis=0), _weight_parts(w_out, 1), res=(x, gate), sid=sid)
    return x, caches


DSA_KEY_TILE = 512


def _dsa_stream(q_all, kv_all, idx_all, row0, B, T, pos0, past_k, past_v, past_kidx):
    tq = min(Q_BLOCK, T)
    q, k, v, ki, kb, vb, qi3, kidx3, wi = _dsa_prep(q_all, kv_all, idx_all, row0, B, T, pos0, tq=tq)
    n_keys = T
    if past_k is not None:
        P = past_k.shape[1]
        n_keys = P + T
        kb = jnp.concatenate([past_k.reshape(B, P, -1).astype(jnp.bfloat16), kb], axis=1)
        vb = jnp.concatenate([past_v.reshape(B, P, -1).astype(jnp.bfloat16), vb], axis=1)
        ph, pl_ = _split_bf16(past_kidx)
        kidx3 = jnp.concatenate([jnp.concatenate([ph, ph, pl_, jnp.zeros_like(ph)], axis=-1), kidx3], axis=1)
    pad = ((0, 0), (0, _round_up(n_keys, DSA_KEY_TILE) - n_keys), (0, 0))
    kb, vb, kidx3 = (jnp.pad(a, pad) for a in (kb, vb, kidx3))
    topk = min(TOPK_MAX, n_keys // 4)
    o = _dsa_core(q, qi3, wi, kb, vb, kidx3, pos0, topk, tq=tq, tk=DSA_KEY_TILE)
    return o, k, v, ki


GDN_HEAD_GROUP = 32


def _bf16_dot(a, b):
    return jnp.dot(a.astype(jnp.bfloat16), b.astype(jnp.bfloat16), preferred_element_type=jnp.float32)


def _dot3(a, b):
    ah, al = _split_bf16(a)
    bh, bl = _split_bf16(b)
    out = jnp.dot(ah, bh, preferred_element_type=jnp.float32)
    out = out + jnp.dot(ah, bl, preferred_element_type=jnp.float32)
    return out + jnp.dot(al, bh, preferred_element_type=jnp.float32)


def _conv_silu(x_ref, w_ref, xe_ref):
    C = x_ref.shape[0]
    taps = w_ref.shape[0]
    xe_ref[SUBLANE:, :] = x_ref[...]
    first = SUBLANE - (taps - 1)
    acc = xe_ref[first:first + C, :] * w_ref[0:1, :]
    for j in range(1, taps):
        acc = acc + xe_ref[first + j:first + j + C, :] * w_ref[j:j + 1, :]
    xe_ref[:SUBLANE, :] = xe_ref[C:, :]
    return acc * jax.nn.sigmoid(acc)


def _gdn_kernel(xq_ref, xk_ref, xv_ref, wq_ref, wk_ref, wv_ref, cq_ref, ck_ref, cv_ref,
                z_ref, g_ref, gt_ref, beta_ref, nw_ref, s0_ref,
                o_ref, s_out_ref, s_ref, eq_ref, ek_ref, ev_ref):
    n = pl.program_id(2)
    C = xq_ref.shape[0]
    hg = g_ref.shape[3]
    dk = s_ref.shape[1]
    dv = s_ref.shape[2]
    rep = hg // (xk_ref.shape[1] // dk)

    @pl.when(n == 0)
    def _():
        s_ref[...] = s0_ref[0]
        for e_ref, c_ref in ((eq_ref, cq_ref), (ek_ref, ck_ref), (ev_ref, cv_ref)):
            e_ref[:SUBLANE, :] = jnp.zeros((SUBLANE, e_ref.shape[1]), jnp.float32)
            e_ref[SUBLANE - c_ref.shape[1]:SUBLANE, :] = c_ref[0]

    qc = _conv_silu(xq_ref, wq_ref, eq_ref)
    kc = _conv_silu(xk_ref, wk_ref, ek_ref)
    vc = _conv_silu(xv_ref, wv_ref, ev_ref)
    q_heads, k_heads = [], []
    for i in range(hg // rep):
        qh = qc[:, i * dk:(i + 1) * dk]
        kh = kc[:, i * dk:(i + 1) * dk]
        q_heads.append(qh * (lax.rsqrt(jnp.sum(qh * qh, axis=1, keepdims=True) + EPS) * dk ** -0.5))
        k_heads.append(kh * lax.rsqrt(jnp.sum(kh * kh, axis=1, keepdims=True) + EPS))

    ri = lax.broadcasted_iota(jnp.int32, (C, C), 0)
    ci = lax.broadcasted_iota(jnp.int32, (C, C), 1)
    causal = ri >= ci
    strict = ri > ci
    eye = jnp.where(ri == ci, 1.0, 0.0)
    g = g_ref[0, 0]
    gc_all = _dot3(jnp.where(causal, 1.0, 0.0), g)
    gr_all = _dot3(gt_ref[0, 0, 0], jnp.where(ri <= ci, 1.0, 0.0))
    beta = beta_ref[0, 0]
    nw = nw_ref[...]

    heads = range(hg)
    qs = [q_heads[h // rep] for h in heads]
    ks = [k_heads[h // rep] for h in heads]
    gcs = [gc_all[:, h:h + 1] for h in heads]
    bcols = [beta[:, h:h + 1] for h in heads]
    decays = [jnp.where(causal, jnp.exp(jnp.where(causal, gcs[h] - gr_all[h:h + 1, :], 0.0)), 0.0)
              for h in heads]
    kbs = [ks[h] * bcols[h] for h in heads]
    kks = [lax.dot_general(kbs[h].astype(jnp.bfloat16), ks[h].astype(jnp.bfloat16), _NT_DIMS,
                           preferred_element_type=jnp.float32) for h in heads]
    bms = [jnp.where(strict, -(kks[h] * decays[h]), 0.0) for h in heads]
    egs = [jnp.exp(gcs[h]) for h in heads]
    rhss = [jnp.concatenate([vc[:, h * dv:(h + 1) * dv] * bcols[h], kbs[h] * egs[h]], axis=1)
            for h in heads]
    pairs = range(hg // 2)
    left = lax.broadcasted_iota(jnp.int32, (C, 2 * C), 1) < C
    zero16 = jnp.zeros((C, 2 * C), jnp.bfloat16)

    def blockdiag(part):
        return jnp.concatenate([jnp.where(left, part, zero16), jnp.where(left, zero16, part)], axis=0)

    def dot3_pairs(x_parts, y_parts):
        xh, xl = x_parts
        yh, yl = blockdiag(y_parts[0]), blockdiag(y_parts[1])
        out = jnp.dot(xh, yh, preferred_element_type=jnp.float32)
        out = out + jnp.dot(xh, yl, preferred_element_type=jnp.float32)
        return out + jnp.dot(xl, yh, preferred_element_type=jnp.float32)

    b2 = [jnp.concatenate([bms[2 * i], bms[2 * i + 1]], axis=1) for i in pairs]
    eye2 = jnp.concatenate([eye, eye], axis=1)
    p2 = [eye2 + b2[i] for i in pairs]
    b2_parts = [_split_bf16(b2[i]) for i in pairs]
    step = 2
    while step < C:
        b2 = [dot3_pairs(b2_parts[i], b2_parts[i]) for i in pairs]
        b2_parts = [_split_bf16(b2[i]) for i in pairs]
        p2 = [p2[i] + dot3_pairs(_split_bf16(p2[i]), b2_parts[i]) for i in pairs]
        step *= 2
    ps = [p2[h // 2][:, (h % 2) * C:(h % 2 + 1) * C] for h in heads]
    ws = [_dot3(ps[h], rhss[h]) for h in heads]
    qks = [lax.dot_general(qs[h].astype(jnp.bfloat16), ks[h].astype(jnp.bfloat16), _NT_DIMS,
                           preferred_element_type=jnp.float32) * decays[h] for h in heads]
    g_lasts = [gcs[h][C - 1:C, :] for h in heads]
    ss = [s_ref[h] for h in heads]
    us = [ws[h][:, :dv] - _bf16_dot(ws[h][:, dv:], ss[h]) for h in heads]
    os_ = [_bf16_dot(qs[h] * egs[h], ss[h]) + _bf16_dot(qks[h], us[h]) for h in heads]
    for h in heads:
        ke = ks[h] * jnp.exp(g_lasts[h] - gcs[h])
        s_ref[h] = ss[h] * jnp.exp(g_lasts[h]) + _bf16_dot(ke.T, us[h])
    for h in heads:
        o = os_[h]
        o = o * lax.rsqrt(jnp.mean(o * o, axis=1, keepdims=True) + EPS) * nw
        z = z_ref[:, h * dv:(h + 1) * dv]
        o_ref[0, :, h * dv:(h + 1) * dv] = (o * (z * jax.nn.sigmoid(z))).astype(o_ref.dtype)

    @pl.when(n == pl.num_programs(2) - 1)
    def _():
        s_out_ref[0] = s_ref[...]


def _gdn_core(qkv, z, row0, conv_w, conv_state, g, beta, norm_w, s0):
    B, T, VH = g.shape
    dk, dv = s0.shape[2], s0.shape[3]
    vd = VH * dv
    qd = (conv_w.shape[1] - vd) // 2
    C = min(CHUNK, T)
    N = T // C
    assert row0 % C == 0
    blk0 = row0 // C
    hg = min(GDN_HEAD_GROUP, VH)
    ng = VH // hg
    qw = qd // ng
    vw = hg * dv
    assert qd % qw == 0 and (2 * qd) % vw == 0
    k0, v0 = qd // qw, (2 * qd) // vw
    z0 = (z.shape[1] - vd) // vw
    taps = conv_w.shape[0]

    def grouped(a):
        return a.reshape(B, T, ng, hg).transpose(0, 2, 1, 3)

    gg, bg = grouped(g), grouped(beta)
    gt = gg.reshape(B, ng, N, C, hg).transpose(0, 1, 2, 4, 3)
    small = pl.BlockSpec((1, 1, C, hg), lambda b, j, n: (b, j, n, 0))
    o, s = pl.pallas_call(
        _gdn_kernel,
        grid=(B, ng, N),
        in_specs=[pl.BlockSpec((C, qw), lambda b, j, n: (blk0 + b * N + n, j)),
                  pl.BlockSpec((C, qw), lambda b, j, n: (blk0 + b * N + n, k0 + j)),
                  pl.BlockSpec((C, vw), lambda b, j, n: (blk0 + b * N + n, v0 + j)),
                  pl.BlockSpec((taps, qw), lambda b, j, n: (0, j)),
                  pl.BlockSpec((taps, qw), lambda b, j, n: (0, k0 + j)),
                  pl.BlockSpec((taps, vw), lambda b, j, n: (0, v0 + j)),
                  pl.BlockSpec((1, taps - 1, qw), lambda b, j, n: (b, 0, j)),
                  pl.BlockSpec((1, taps - 1, qw), lambda b, j, n: (b, 0, k0 + j)),
                  pl.BlockSpec((1, taps - 1, vw), lambda b, j, n: (b, 0, v0 + j)),
                  pl.BlockSpec((C, vw), lambda b, j, n: (blk0 + b * N + n, z0 + j)),
                  small,
                  pl.BlockSpec((1, 1, 1, hg, C), lambda b, j, n: (b, j, n, 0, 0)),
                  small,
                  pl.BlockSpec((1, dv), lambda b, j, n: (0, 0)),
                  pl.BlockSpec((1, hg, dk, dv), lambda b, j, n: (b, j, 0, 0))],
        out_specs=[pl.BlockSpec((1, C, hg * dv), lambda b, j, n: (b, n, j)),
                   pl.BlockSpec((1, hg, dk, dv), lambda b, j, n: (b, j, 0, 0))],
        out_shape=[jax.ShapeDtypeStruct((B, T, vd), jnp.bfloat16),
                   jax.ShapeDtypeStruct(s0.shape, jnp.float32)],
        scratch_shapes=[pltpu.VMEM((hg, dk, dv), jnp.float32),
                        pltpu.VMEM((SUBLANE + C, qw), jnp.float32),
                        pltpu.VMEM((SUBLANE + C, qw), jnp.float32),
                        pltpu.VMEM((SUBLANE + C, vw), jnp.float32)],
        compiler_params=pltpu.CompilerParams(
            dimension_semantics=("parallel", "parallel", "arbitrary"),
            vmem_limit_bytes=V7X_VMEM_LIMIT_BYTES),
        name="gdn_core",
    )(qkv, qkv, qkv, conv_w, conv_w, conv_w, conv_state, conv_state, conv_state,
      z, gg, gt, bg, norm_w.reshape(1, dv), s0)
    return o, s


def _gdn_layer(x, mod, gate, sid, streams, w_in, conv_w, a_log, dt_bias, norm_w, w_out):
    qk_dim, v_dim = GDN_QK_HEADS * GDN_DK, GDN_V_HEADS * GDN_DV
    conv_dim = 2 * qk_dim + v_dim
    n_gate = 2 * GDN_V_HEADS
    w_main = _weight_parts(w_in[:, :conv_dim + v_dim], 1)
    w_gate = jnp.pad(w_in[:, conv_dim + v_dim:], ((0, 0), (0, _round_up(n_gate, LANE) - n_gate)))
    qkv_all = z_all = pmatmul(x, w_main, mod=mod, sid=sid)
    gates_all = pmatmul(x, _weight_parts(w_gate, 1), tn=w_gate.shape[1], mod=mod, sid=sid)
    outs, states = [], []
    for row0, B, T, conv_state, ssm_state in streams:
        rows = slice(row0, row0 + B * T)
        gates = gates_all[rows].reshape(B, T, -1)
        beta_raw, a_raw = gates[..., :GDN_V_HEADS], gates[..., GDN_V_HEADS:n_gate]
        if conv_state is None:
            conv_state = jnp.zeros((B, CONV_W - 1, conv_dim), x.dtype)
        if ssm_state is None:
            ssm_state = jnp.zeros((B, GDN_V_HEADS, GDN_DK, GDN_DV), jnp.float32)
        tail = jnp.stack([qkv_all[row0 + (b + 1) * T - (CONV_W - 1):row0 + (b + 1) * T, :conv_dim]
                          for b in range(B)])
        new_conv = jnp.concatenate([conv_state, tail], axis=1)[:, -(CONV_W - 1):]
        beta = jax.nn.sigmoid(beta_raw)
        g = -jnp.exp(a_log) * jax.nn.softplus(a_raw + dt_bias)
        o, S = _gdn_core(qkv_all, z_all, row0, conv_w, conv_state, g, beta, norm_w, ssm_state)
        outs.append(o.reshape(B * T, v_dim))
        states.append((new_conv, S))
    x = pmatmul(jnp.concatenate(outs, axis=0), _weight_parts(w_out, 1), res=(x, gate), sid=sid)
    return x, states


NEG_INF = float("-inf")
POS_INF = float("inf")
_PEER_CAND_ROWS = tuple((i, PEER_TOPK // (i + 1)) for i in range(PEER_TOPK // 2))


def _split_bf16(x):
    hi = x.astype(jnp.bfloat16)
    lo = (x - hi.astype(jnp.float32)).astype(jnp.bfloat16)
    return hi, lo


def _dot3_nt(a, b):
    dn = (((1,), (1,)), ((), ()))
    ah, al = _split_bf16(a)
    bh, bl = _split_bf16(b)
    out = lax.dot_general(ah, bh, dn, preferred_element_type=jnp.float32)
    out = out + lax.dot_general(ah, bl, dn, preferred_element_type=jnp.float32)
    return out + lax.dot_general(al, bh, dn, preferred_element_type=jnp.float32)


def _top_rows_desc(s, n, with_rank=False):
    rows = []
    cur = s
    rank = jnp.full(s.shape, float(n), jnp.float32)
    for i in range(n):
        m = jnp.max(cur, axis=0, keepdims=True)
        rows.append(m)
        hit = cur == m
        if with_rank:
            rank = jnp.where(hit, float(i), rank)
        cur = jnp.where(hit, NEG_INF, cur)
    return (rows, rank) if with_rank else rows


def _stack_rows(rows, lanes):
    n = len(rows)
    rid = lax.broadcasted_iota(jnp.int32, (n, lanes), 0)
    out = jnp.zeros((n, lanes), jnp.float32)
    for i, r in enumerate(rows):
        out = jnp.where(rid == i, r, out)
    return out


def _peer_select_kernel(q_ref, keys_ref, rk2_ref, e2_ref, cnt_ref, e1_ref):
    tm = q_ref.shape[0]
    k = PEER_TOPK
    s1 = _dot3_nt(keys_ref[0], q_ref[:, :PEER_HALF])
    s2 = _dot3_nt(keys_ref[1], q_ref[:, PEER_HALF:])
    r1 = _top_rows_desc(s1, k)
    r2, rank2 = _top_rows_desc(s2, k, with_rank=True)
    v1 = _stack_rows(r1, tm)
    v2 = _stack_rows(r2, tm)
    v2h = v2[:k // 2]
    rid = lax.broadcasted_iota(jnp.int32, (k // 2, tm), 0)
    pieces = [r1[0] + v2]
    for i, n in _PEER_CAND_ROWS[1:]:
        pieces.append(jnp.where(rid < n, r1[i] + v2h, NEG_INF))
    pieces.append(v1[k // 2:] + r2[0])
    cand = jnp.concatenate(pieces, axis=0)
    tau = _top_rows_desc(cand, k)[-1]
    top = r1[0] + r2[0]
    z = jnp.sum(jnp.where(cand >= tau, jnp.exp(cand - top), 0.0), axis=0, keepdims=True)
    cnt = jnp.zeros(s1.shape, jnp.float32)
    for j in range(k):
        cnt = cnt + jnp.where(s1 + r2[j] >= tau, 1.0, 0.0)
    rk2_ref[0] = pltpu.bitcast(rank2.astype(jnp.bfloat16), rk2_ref.dtype)
    e2_ref[0] = pltpu.bitcast((jnp.exp(s2 - r2[0]) / z).astype(jnp.bfloat16), e2_ref.dtype)
    cnt_ref[0] = cnt
    e1_ref[0] = jnp.exp(s1 - r1[0])


def _peer_select(q, keys, *, tm):
    M = q.shape[0]
    nk = keys.shape[1]
    ospec = pl.BlockSpec((1, nk, tm), lambda i, h: (h, 0, i))
    pspec = pl.BlockSpec((1, nk // 2, tm), lambda i, h: (h, 0, i))
    return pl.pallas_call(
        _peer_select_kernel,
        grid=(M // tm, PEER_HEADS),
        in_specs=[pl.BlockSpec((tm, PEER_KEY_DIM), lambda i, h: (i, h)),
                  pl.BlockSpec(keys.shape, lambda i, h: (0, 0, 0))],
        out_specs=[pspec, pspec, ospec, ospec],
        out_shape=[jax.ShapeDtypeStruct((PEER_HEADS, nk // 2, M), jnp.int32)] * 2
        + [jax.ShapeDtypeStruct((PEER_HEADS, nk, M), jnp.float32)] * 2,
        compiler_params=pltpu.CompilerParams(
            dimension_semantics=("parallel", "arbitrary"),
            vmem_limit_bytes=V7X_VMEM_LIMIT_BYTES),
        name="peer_select",
    )(q, keys)


BF16_SUBLANES = 16
PEER_SUB_EXPERTS = 512


def _peer_main_kernel(sid_ref, x_ref, gain_ref, shift_ref, scale_ref, gate_ref, fnorm_ref,
                      u_ref, vT_ref, rk2_ref, e2_ref, cnt_ref, e1_ref, *refs, final_norm):
    out_refs, (hT_ref, coef_ref, acc_ref) = refs[:-3], refs[-3:]
    e = pl.program_id(1)
    te, tm = coef_ref.shape
    nk = 2 * rk2_ref.shape[1]
    rows = BF16_SUBLANES
    sub = PEER_SUB_EXPERTS
    blk0 = pl.program_id(0) * (tm // SEQ_BLOCK)

    @pl.when(e == 0)
    def _():
        acc_ref[...] = jnp.zeros_like(acc_ref)
        per_lane_tile = LANE // SEQ_BLOCK
        for t in range(tm // LANE):
            hs = []
            for r in range(t * per_lane_tile, (t + 1) * per_lane_tile):
                rs = slice(r * SEQ_BLOCK, (r + 1) * SEQ_BLOCK)
                hs.append(_modulated_rows(x_ref[rs, :], gain_ref[...], shift_ref, scale_ref, sid_ref[blk0 + r]))
            hT_ref[:, t * LANE:(t + 1) * LANE] = jnp.concatenate(hs, axis=0).T.astype(jnp.bfloat16)

    def expert_acts(sb):
        return jnp.dot(u_ref[sb * sub:(sb + 1) * sub, :], hT_ref[...], preferred_element_type=jnp.float32)

    def weigh(sb, act):
        for al in range(sub // nk):
            a = sb * (sub // nk) + al
            for lg in range(tm // LANE):
                lanes = slice(lg * LANE, (lg + 1) * LANE)
                cnts = [jnp.broadcast_to(cnt_ref[h, a:a + 1, lanes], (rows, LANE)).astype(jnp.bfloat16)
                        for h in range(PEER_HEADS)]
                e1s = [jnp.broadcast_to(e1_ref[h, a:a + 1, lanes], (rows, LANE)).astype(jnp.bfloat16)
                       for h in range(PEER_HEADS)]
                for r in range(nk // rows):
                    words = slice(r * rows // 2, (r + 1) * rows // 2)
                    w = None
                    for h in range(PEER_HEADS):
                        e2 = pltpu.bitcast(e2_ref[h, words, lanes], jnp.bfloat16)
                        rk2 = pltpu.bitcast(rk2_ref[h, words, lanes], jnp.bfloat16)
                        t = jnp.where(rk2 < cnts[h], e2, jnp.zeros_like(e2)) * e1s[h]
                        w = t if w is None else w + t
                    x = act[al * nk + r * rows:al * nk + (r + 1) * rows, lanes]
                    g = 0.5 * x * (1.0 + lax.erf(x * (2.0 ** -0.5)))
                    coef_ref[sb * sub + al * nk + r * rows:sb * sub + al * nk + (r + 1) * rows, lanes] = (
                        w * g.astype(jnp.bfloat16))

    def accumulate(sb):
        acc_ref[...] += lax.dot_general(vT_ref[sb * sub:(sb + 1) * sub, :], coef_ref[sb * sub:(sb + 1) * sub, :],
                                        (((0,), (0,)), ((), ())), preferred_element_type=jnp.float32)

    n_sub = te // sub
    act = expert_acts(0)
    for sb in range(n_sub):
        nxt = expert_acts(sb + 1) if sb + 1 < n_sub else None
        weigh(sb, act)
        accumulate(sb)
        act = nxt

    @pl.when(e == pl.num_programs(1) - 1)
    def _():
        out = acc_ref[...].T
        for r in range(tm // SEQ_BLOCK):
            rs = slice(r * SEQ_BLOCK, (r + 1) * SEQ_BLOCK)
            y = x_ref[rs, :] + gate_ref[pl.ds(sid_ref[blk0 + r], 1), :] * out[rs, :]
            out_refs[0][rs, :] = y
            if final_norm:
                out_refs[1][rs, :] = y * lax.rsqrt(jnp.mean(y * y, axis=-1, keepdims=True) + EPS) * fnorm_ref[...]


def _peer_main(x, mod, gate, fnorm, sid, u, vT, rk2, e2, cnt, e1, *, tm, te, final_norm):
    M, D = x.shape
    E = u.shape[0]
    nk = cnt.shape[1]
    const = lambda a: pl.BlockSpec(a.shape, lambda i, e, s: (0,) * a.ndim)
    col_spec = pl.BlockSpec((PEER_HEADS, nk // 2, tm), lambda i, e, s: (0, 0, i))
    row_spec = pl.BlockSpec((PEER_HEADS, te // nk, tm), lambda i, e, s: (0, e, i))
    tok_spec = pl.BlockSpec((tm, D), lambda i, e, s: (i, 0))
    n_out = 2 if final_norm else 1
    return pl.pallas_call(
        functools.partial(_peer_main_kernel, final_norm=final_norm),
        grid_spec=pltpu.PrefetchScalarGridSpec(
            num_scalar_prefetch=1,
            grid=(M // tm, E // te),
            in_specs=[tok_spec, const(mod[0]), const(mod[1]), const(mod[2]), const(gate), const(fnorm),
                      pl.BlockSpec((te, D), lambda i, e, s: (e, 0)),
                      pl.BlockSpec((te, D), lambda i, e, s: (e, 0)),
                      col_spec, col_spec, row_spec, row_spec],
            out_specs=[tok_spec] * n_out,
            scratch_shapes=[pltpu.VMEM((D, tm), jnp.bfloat16),
                            pltpu.VMEM((te, tm), jnp.bfloat16),
                            pltpu.VMEM((D, tm), jnp.float32)]),
        out_shape=[jax.ShapeDtypeStruct((M, D), jnp.float32)] * n_out,
        compiler_params=pltpu.CompilerParams(
            dimension_semantics=("parallel", "arbitrary"),
            vmem_limit_bytes=V7X_VMEM_LIMIT_PEER_BYTES),
        name="peer_main",
    )(sid, x, *mod, gate, fnorm, u, vT, rk2, e2, cnt, e1)


def _peer(x, mod, gate, fnorm, sid, w_query, sub_keys, expert_u, expert_v, *, final_norm,
          tm_sel=256, tm=512, te=1024):
    M = x.shape[0]
    q = pmatmul(x, _weight_parts(w_query, 1), mod=mod, sid=sid)
    rk2, e2, cnt, e1 = _peer_select(q, sub_keys, tm=min(tm_sel, M))
    u = expert_u.astype(jnp.bfloat16)
    vT = expert_v.astype(jnp.bfloat16)
    return _peer_main(x, mod, gate, fnorm, sid, u, vT, rk2, e2, cnt, e1, tm=min(tm, M), te=te,
                      final_norm=final_norm)


def kernel(x_prompt, x_sample, c_prompt, c_sample, cache_k_l0, cache_v_l0, cache_kidx_l0, state_conv_l1, state_ssm_l1, norm1_l0, norm2_l0, ada_w_l0, ada_b_l0, attn_in_l0, attn_out_l0, peer_query_l0, peer_keys_l0, peer_u_l0, peer_v_l0, norm1_l1, norm2_l1, ada_w_l1, ada_b_l1, gdn_in_l1, gdn_conv_l1, gdn_a_log_l1, gdn_dt_bias_l1, gdn_norm_l1, gdn_out_l1, peer_query_l1, peer_keys_l1, peer_u_l1, peer_v_l1, final_norm):
    past_len = cache_k_l0.shape[1]
    norm1 = (norm1_l0, norm1_l1)
    norm2 = (norm2_l0, norm2_l1)
    ada_w = (ada_w_l0, ada_w_l1)
    ada_b = (ada_b_l0, ada_b_l1)
    peer_query = (peer_query_l0, peer_query_l1)
    peer_keys = (peer_keys_l0, peer_keys_l1)
    peer_u = (peer_u_l0, peer_u_l1)
    peer_v = (peer_v_l0, peer_v_l1)
    Bp, Tp, D = x_prompt.shape
    Bs, Ts, _ = x_sample.shape
    n_p, n_s = Bp * Tp, Bs * Ts
    x = jnp.concatenate([x_prompt.reshape(n_p, D), x_sample.reshape(n_s, D)], axis=0)
    sid = jnp.concatenate([jnp.repeat(jnp.arange(Bp, dtype=jnp.int32), Tp // SEQ_BLOCK),
                           Bp + jnp.repeat(jnp.arange(Bs, dtype=jnp.int32), Ts // SEQ_BLOCK)])
    c_all = jnp.concatenate([c_prompt, c_sample], axis=0)
    fnorm = final_norm.reshape(1, D)
    for i in range(2):
        mods = _adaln(c_all, ada_w[i], ada_b[i])
        mod1 = (norm1[i].reshape(1, D), mods[0], mods[1])
        if i == 0:
            x, ((nkp, nvp, nkip), (nks, nvs, nkis)) = _dsa_layer(
                x, mod1, mods[2], sid,
                [(0, Bp, Tp, 0, None, None, None), (n_p, Bs, Ts, past_len, cache_k_l0, cache_v_l0, cache_kidx_l0)],
                attn_in_l0, attn_out_l0)
        else:
            x, ((ncp, nsp), (ncs, nss)) = _gdn_layer(
                x, mod1, mods[2], sid, [(0, Bp, Tp, None, None), (n_p, Bs, Ts, state_conv_l1, state_ssm_l1)],
                gdn_in_l1, gdn_conv_l1, gdn_a_log_l1, gdn_dt_bias_l1, gdn_norm_l1, gdn_out_l1)
        mod2 = (norm2[i].reshape(1, D), mods[3], mods[4])
        outs = _peer(x, mod2, mods[5], fnorm, sid, peer_query[i], peer_keys[i], peer_u[i], peer_v[i],
                     final_norm=(i == 1))
        x = outs[0]
    y = outs[1]
    y_prompt = y[:n_p].reshape(Bp, Tp, D)
    y_sample = y[n_p:].reshape(Bs, Ts, D)
    return (y_prompt, y_sample, nkp, nvp, nkip, nks, nvs, nkis, ncp, nsp, ncs, nss)
```

```python
import functools

import jax
import jax.numpy as jnp
from jax import lax
from jax.experimental import pallas as pl
from jax.experimental.pallas import tpu as pltpu

CHUNK = 64
CHUNK_SHIFT = CHUNK.bit_length() - 1
EPS = 1e-6
ROPE_THETA = 500000.0
ROPE_FRACTION = 4
A_HEADS = 16
A_KV_HEADS = 4
IDX_HEADS = 8
IDX_DIM = 64
TOPK_MAX = 256
Q_BLOCK = 128
GDN_QK_HEADS = 16
GDN_V_HEADS = 32
GDN_DK = 128
GDN_DV = 128
CONV_W = 4
PEER_HEADS = 8
PEER_NKEYS = 128
PEER_KEY_DIM = 256
PEER_HALF = PEER_KEY_DIM // 2
PEER_TOPK = 16

V7X_VMEM_LIMIT_BYTES = 48 * 1024 * 1024
V7X_VMEM_LIMIT_PEER_BYTES = 56 * 1024 * 1024
LANE = 128
SUBLANE = 8


def _round_up(n, m):
    return (n + m - 1) // m * m


SEQ_BLOCK = CHUNK


def _modulated_rows(x, gain, shift_ref, scale_ref, sid):
    y = x * lax.rsqrt(jnp.mean(x * x, axis=-1, keepdims=True) + EPS)
    return y * gain * (1.0 + scale_ref[pl.ds(sid, 1), :]) + shift_ref[pl.ds(sid, 1), :]


def _matmul_kernel(sid_ref, x_ref, *refs, passes, has_mod, has_res):
    refs = list(refs)
    mod_refs = [refs.pop(0) for _ in range(3)] if has_mod else None
    nparts = 1 + passes // 2
    w_refs = [refs.pop(0) for _ in range(nparts)]
    res_refs = [refs.pop(0) for _ in range(2)] if has_res else None
    o_ref, x_parts = refs[0], refs[1:]
    tm = x_ref.shape[0]
    blocks = range(tm // SEQ_BLOCK) if (has_mod or has_res) else ()
    blk0 = pl.program_id(0) * (tm // SEQ_BLOCK)

    @pl.when(pl.program_id(1) == 0)
    def _():
        def put(rows, x):
            hi = x.astype(jnp.bfloat16)
            x_parts[0][rows, :] = hi
            if passes == 3:
                x_parts[1][rows, :] = (x - hi.astype(jnp.float32)).astype(jnp.bfloat16)

        if has_mod:
            gain_ref, shift_ref, scale_ref = mod_refs
            for r in blocks:
                rows = slice(r * SEQ_BLOCK, (r + 1) * SEQ_BLOCK)
                put(rows, _modulated_rows(x_ref[rows, :], gain_ref[...], shift_ref, scale_ref, sid_ref[blk0 + r]))
        else:
            put(slice(None), x_ref[...].astype(jnp.float32))

    acc = jnp.dot(x_parts[0][...], w_refs[0][...], preferred_element_type=jnp.float32)
    if passes == 3:
        acc = acc + jnp.dot(x_parts[0][...], w_refs[1][...], preferred_element_type=jnp.float32)
        acc = acc + jnp.dot(x_parts[1][...], w_refs[0][...], preferred_element_type=jnp.float32)
    if has_res:
        res_ref, gate_ref = res_refs
        for r in blocks:
            rows = slice(r * SEQ_BLOCK, (r + 1) * SEQ_BLOCK)
            gate = gate_ref[pl.ds(sid_ref[blk0 + r], 1), :]
            o_ref[rows, :] = res_ref[rows, :] + gate * acc[rows, :]
    else:
        o_ref[...] = acc.astype(o_ref.dtype)


def _weight_parts(w, passes):
    hi = w.astype(jnp.bfloat16)
    if passes == 1:
        return (hi,)
    return (hi, (w - hi.astype(jnp.float32)).astype(jnp.bfloat16))


def pmatmul(x, w_parts, *, col0=0, ncols=None, tm=512, tn=1024, out_dtype=jnp.float32,
            mod=None, res=None, sid=None):
    passes = 1 if len(w_parts) == 1 else 3
    M, K = x.shape
    n_total = w_parts[0].shape[1]
    ncols = n_total - col0 if ncols is None else ncols
    tm = min(tm, _round_up(M, 2 * SUBLANE))
    tn = min(tn, ncols)
    assert ncols % tn == 0 and col0 % tn == 0 and (tn % LANE == 0 or tn == n_total)
    Mp = _round_up(M, tm)
    if mod is not None or res is not None:
        assert Mp == M and tm % SEQ_BLOCK == 0 and sid is not None
    else:
        sid = jnp.zeros((1,), jnp.int32)
    if Mp != M:
        x = jnp.pad(x, ((0, Mp - M), (0, 0)))
    c0 = col0 // tn
    in_specs = [pl.BlockSpec((tm, K), lambda i, j, s: (i, 0))]
    args = [x]
    if mod is not None:
        in_specs += [pl.BlockSpec(a.shape, lambda i, j, s: (0, 0)) for a in mod]
        args += list(mod)
    in_specs += [pl.BlockSpec((K, tn), lambda i, j, s: (0, c0 + j))] * len(w_parts)
    args += list(w_parts)
    if res is not None:
        in_specs += [pl.BlockSpec((tm, tn), lambda i, j, s: (i, j)),
                     pl.BlockSpec((res[1].shape[0], tn), lambda i, j, s: (0, j))]
        args += list(res)
        out_dtype = jnp.float32
    out = pl.pallas_call(
        functools.partial(_matmul_kernel, passes=passes, has_mod=mod is not None, has_res=res is not None),
        grid_spec=pltpu.PrefetchScalarGridSpec(
            num_scalar_prefetch=1,
            grid=(Mp // tm, ncols // tn),
            in_specs=in_specs,
            out_specs=pl.BlockSpec((tm, tn), lambda i, j, s: (i, j)),
            scratch_shapes=[pltpu.VMEM((tm, K), jnp.bfloat16)] * (1 + passes // 2)),
        out_shape=jax.ShapeDtypeStruct((Mp, ncols), out_dtype),
        compiler_params=pltpu.CompilerParams(
            dimension_semantics=("parallel", "arbitrary"),
            vmem_limit_bytes=V7X_VMEM_LIMIT_BYTES),
        name="matmul",
    )(sid, *args)
    return out[:M]


def _adaln_kernel(c_ref, w_ref, b_ref, o_ref):
    c = c_ref[...]
    o_ref[...] = _dot3(c * jax.nn.sigmoid(c), w_ref[...]) + b_ref[...]


def _adaln(c, w, b, *, tn=512):
    n, D = c.shape
    N = w.shape[1]
    rows = _round_up(n, SUBLANE)
    mod = pl.pallas_call(
        _adaln_kernel,
        grid=(N // tn,),
        in_specs=[pl.BlockSpec((rows, D), lambda j: (0, 0)),
                  pl.BlockSpec((D, tn), lambda j: (0, j)),
                  pl.BlockSpec((1, tn), lambda j: (0, j))],
        out_specs=pl.BlockSpec((rows, tn), lambda j: (0, j)),
        out_shape=jax.ShapeDtypeStruct((rows, N), jnp.float32),
        compiler_params=pltpu.CompilerParams(
            dimension_semantics=("parallel",), vmem_limit_bytes=V7X_VMEM_LIMIT_BYTES),
        name="adaln",
    )(jnp.pad(c, ((0, rows - n), (0, 0))), w, b.reshape(1, N))
    return jnp.split(mod, 6, axis=-1)


_NT_DIMS = (((1,), (1,)), ((), ()))
INT32_MIN = -2 ** 31
LOG2_E = 1.4426950408889634
_NEG_INF_KEY = -2139095041
IDX_PACK = 4 * IDX_DIM


def _ordered_key(x):
    bits = pltpu.bitcast(x, jnp.int32)
    return bits ^ ((bits >> 31) & 0x7FFFFFFF)


def _lane_tile_sum(x, width=LANE):
    out = x[:, :width]
    for c in range(1, x.shape[1] // width):
        out = out + x[:, c * width:(c + 1) * width]
    return out


def _dsa_select_bias(qi_ref, wi_ref, kidx_ref, key_ref, bias_ref, *, first, n_tiles, topk, tk):
    tq = wi_ref.shape[1]
    row = lax.broadcasted_iota(jnp.int32, (tq, 1), 0)
    lim = (((first + row) >> CHUNK_SHIFT) + 1) * CHUNK
    w = wi_ref[0]

    def score_tile(j, c):
        off = pl.multiple_of(j * tk, tk)
        kt = kidx_ref[0, pl.ds(off, tk), :]
        sc = lax.dot_general(qi_ref[0, 0], kt, _NT_DIMS,
                             preferred_element_type=jnp.float32)
        s = jnp.zeros((tq, tk), jnp.float32)
        for h in range(IDX_HEADS):
            s = s + w[:, h:h + 1] * jnp.maximum(sc[h * tq:(h + 1) * tq], 0.0)
        col = off + lax.broadcasted_iota(jnp.int32, (tq, tk), 1)
        s = jnp.where(col < lim, s + 0.0, NEG_INF)
        key_ref[:, pl.ds(off, tk)] = _ordered_key(s)
        return c

    lax.fori_loop(0, n_tiles, score_tile, 0)

    def bit_step(b, thr):
        cand = thr + lax.shift_left(jnp.int32(1), 31 - b)

        def count_tile(j, c):
            off = pl.multiple_of(j * tk, tk)
            ge = jnp.where(key_ref[:, pl.ds(off, tk)] >= cand, 1.0, 0.0)
            return c + _lane_tile_sum(ge)

        c = lax.fori_loop(0, n_tiles, count_tile, jnp.zeros((tq, LANE), jnp.float32))
        cnt = jnp.sum(c, axis=1, keepdims=True)
        return jnp.where(cnt >= topk, cand, thr)

    thr = lax.fori_loop(0, 32, bit_step, jnp.full((tq, 1), INT32_MIN, jnp.int32))
    thr = jnp.maximum(thr, _NEG_INF_KEY + 1)

    def bias_tile(j, c):
        off = pl.multiple_of(j * tk, tk)
        bias_ref[:, pl.ds(off, tk)] = jnp.where(key_ref[:, pl.ds(off, tk)] >= thr, 0.0, NEG_INF)
        return c

    lax.fori_loop(0, n_tiles, bias_tile, 0)


def _dsa_kernel(qi_ref, wi_ref, kidx_ref, q_ref, k_ref, v_ref, o_ref,
                key_ref, bias_ref, qg_ref, m_ref, l_ref, acc_ref, *, pos0, topk, tk):
    i = pl.program_id(1)
    tq = q_ref.shape[1]
    hd = acc_ref.shape[2]
    groups = acc_ref.shape[0]
    rep = q_ref.shape[2] // (groups * hd)
    first = pos0 + i * tq
    n_valid = (((first + tq - 1) >> CHUNK_SHIFT) + 1) * CHUNK
    n_tiles = (n_valid + tk - 1) // tk
    _dsa_select_bias(qi_ref, wi_ref, kidx_ref, key_ref, bias_ref, first=first, n_tiles=n_tiles, topk=topk, tk=tk)

    for g in range(groups):
        for r in range(rep):
            c0 = (g * rep + r) * hd
            qg_ref[g, r * tq:(r + 1) * tq, :] = q_ref[0, :, c0:c0 + hd]
    m_ref[...] = jnp.full(m_ref.shape, NEG_INF, jnp.float32)
    l_ref[...] = jnp.zeros(l_ref.shape, jnp.float32)
    acc_ref[...] = jnp.zeros(acc_ref.shape, jnp.float32)
    lane_reps = tk // LANE

    def att_tile(j, c):
        off = pl.multiple_of(j * tk, tk)
        b = bias_ref[:, pl.ds(off, tk)]
        bias = jnp.concatenate([b] * rep, axis=0)
        for g in range(groups):
            kt = k_ref[0, pl.ds(off, tk), g * hd:(g + 1) * hd]
            vt = v_ref[0, pl.ds(off, tk), g * hd:(g + 1) * hd]
            lg = lax.dot_general(qg_ref[g], kt, _NT_DIMS, preferred_element_type=jnp.float32) + bias
            m_old = m_ref[g]
            m_new = jnp.maximum(m_old, jnp.max(lg, axis=1, keepdims=True))
            m_safe = jnp.where(m_new == NEG_INF, 0.0, m_new)
            p = jnp.exp2(lg - jnp.tile(m_safe, (1, lane_reps)))
            alpha = jnp.exp2(m_old - m_safe)
            l_ref[g] = alpha * l_ref[g] + jnp.sum(p, axis=1, keepdims=True)
            acc_ref[g] = alpha * acc_ref[g] + jnp.dot(p.astype(jnp.bfloat16), vt,
                                                      preferred_element_type=jnp.float32)
            m_ref[g] = m_new
        return c

    lax.fori_loop(0, n_tiles, att_tile, 0)
    for g in range(groups):
        out = acc_ref[g] / l_ref[g]
        for r in range(rep):
            c0 = (g * rep + r) * hd
            o_ref[0, :, c0:c0 + hd] = out[r * tq:(r + 1) * tq].astype(o_ref.dtype)


def _dsa_core(q, qi3, wi, kb, vb, kidx3, pos0, topk, *, tq, tk):
    B, T, qd = q.shape
    Sp = kb.shape[1]
    hd = qd // A_HEADS
    gw = qd // A_KV_HEADS
    nb = T // tq
    assert Sp % tk == 0 and hd == LANE
    rows = (gw // hd) * tq
    kvw = A_KV_HEADS * hd
    return pl.pallas_call(
        functools.partial(_dsa_kernel, pos0=pos0, topk=topk, tk=tk),
        grid=(B, nb),
        in_specs=[pl.BlockSpec((1, 1, IDX_HEADS * tq, IDX_PACK), lambda b, i: (b, i, 0, 0)),
                  pl.BlockSpec((1, tq, IDX_HEADS), lambda b, i: (b, i, 0)),
                  pl.BlockSpec((1, Sp, IDX_PACK), lambda b, i: (b, 0, 0)),
                  pl.BlockSpec((1, tq, qd), lambda b, i: (b, i, 0)),
                  pl.BlockSpec((1, Sp, kvw), lambda b, i: (b, 0, 0)),
                  pl.BlockSpec((1, Sp, kvw), lambda b, i: (b, 0, 0))],
        out_specs=pl.BlockSpec((1, tq, qd), lambda b, i: (b, i, 0)),
        out_shape=jax.ShapeDtypeStruct((B, T, qd), jnp.bfloat16),
        scratch_shapes=[pltpu.VMEM((tq, Sp), jnp.int32),
                        pltpu.VMEM((tq, Sp), jnp.float32),
                        pltpu.VMEM((A_KV_HEADS, rows, hd), jnp.bfloat16),
                        pltpu.VMEM((A_KV_HEADS, rows, LANE), jnp.float32),
                        pltpu.VMEM((A_KV_HEADS, rows, LANE), jnp.float32),
                        pltpu.VMEM((A_KV_HEADS, rows, hd), jnp.float32)],
        compiler_params=pltpu.CompilerParams(
            dimension_semantics=("parallel", "arbitrary"),
            vmem_limit_bytes=V7X_VMEM_LIMIT_BYTES),
        name="dsa_core",
    )(qi3, wi, kidx3, q, kb, vb)


def _rope_tables(pos, width, period):
    half = period // ROPE_FRACTION // 2
    inv_freq = ROPE_THETA ** (-jnp.arange(half, dtype=jnp.float32) / half)
    ang = pos.astype(jnp.float32)[:, None] * inv_freq[None, :]
    cos, sin = jnp.cos(ang), jnp.sin(ang)
    T = pos.shape[0]
    rest = period - 2 * half
    c = jnp.concatenate([cos, cos, jnp.ones((T, rest), jnp.float32)], axis=1)
    s_next = jnp.concatenate([-sin, jnp.zeros((T, period - half), jnp.float32)], axis=1)
    s_prev = jnp.concatenate([jnp.zeros((T, half), jnp.float32), sin, jnp.zeros((T, rest), jnp.float32)], axis=1)
    return jnp.stack([jnp.tile(t, (1, width // period)) for t in (c, s_next, s_prev)])


def _rope_lanes(x, tab_ref, half):
    return (x * tab_ref[0] + pltpu.roll(x, LANE - half, 1) * tab_ref[1] + pltpu.roll(x, half, 1) * tab_ref[2])


def _dsa_prep_kernel(q_ref, kv_ref, idx_ref, tq_ref, ti_ref,
                     qa_ref, kn_ref, vn_ref, kin_ref, kb_ref, vb_ref, qi3_ref, kidx3_ref, wi_ref, *, q_scale):
    tq = q_ref.shape[0]
    hd = LANE
    kvw = kv_ref.shape[1] // 2
    half_q = hd // ROPE_FRACTION // 2
    half_i = IDX_DIM // ROPE_FRACTION // 2
    left = lax.broadcasted_iota(jnp.int32, (tq, LANE), 1) < IDX_DIM
    zero = jnp.zeros((tq, LANE), jnp.float32)

    for h in range(q_ref.shape[1] // hd):
        cols = slice(h * hd, (h + 1) * hd)
        qa_ref[0, :, cols] = (_rope_lanes(q_ref[:, cols], tq_ref, half_q) * q_scale).astype(qa_ref.dtype)
    for h in range(kvw // hd):
        cols = slice(h * hd, (h + 1) * hd)
        k = _rope_lanes(kv_ref[:, cols], tq_ref, half_q)
        kn_ref[0, :, h, :] = k
        kb_ref[0, :, cols] = k.astype(kb_ref.dtype)
        vn_ref[0, :, h, :] = kv_ref[:, kvw + h * hd:kvw + (h + 1) * hd]
    vb_ref[0] = kv_ref[:, kvw:].astype(vb_ref.dtype)

    def hi_lo(x):
        hi = x.astype(jnp.bfloat16).astype(jnp.float32)
        return hi, x - hi

    for t in range(IDX_HEADS * IDX_DIM // LANE):
        hi, lo = hi_lo(_rope_lanes(idx_ref[:, t * LANE:(t + 1) * LANE], ti_ref, half_i))
        hi_sw, lo_sw = pltpu.roll(hi, IDX_DIM, 1), pltpu.roll(lo, IDX_DIM, 1)
        even = jnp.concatenate([jnp.where(left, hi, lo_sw), jnp.where(left, hi, zero)], axis=1)
        odd = jnp.concatenate([jnp.where(left, hi_sw, lo), jnp.where(left, hi_sw, zero)], axis=1)
        qi3_ref[0, 0, (2 * t) * tq:(2 * t + 1) * tq, :] = even.astype(qi3_ref.dtype)
        qi3_ref[0, 0, (2 * t + 1) * tq:(2 * t + 2) * tq, :] = odd.astype(qi3_ref.dtype)
    c0 = IDX_HEADS * IDX_DIM
    x = idx_ref[:, c0:c0 + LANE]
    r = jnp.where(left, _rope_lanes(x, ti_ref, half_i), x)
    kin_ref[0] = r[:, :IDX_DIM]
    hi, lo = hi_lo(r)
    kidx3 = jnp.concatenate([jnp.where(left, hi, pltpu.roll(hi, IDX_DIM, 1)), jnp.where(left, lo, zero)], axis=1)
    kidx3_ref[0] = kidx3.astype(kidx3_ref.dtype)
    wi_ref[0] = x[:, IDX_DIM:IDX_DIM + IDX_HEADS] * (IDX_HEADS ** -0.5 * IDX_DIM ** -0.5)


def _dsa_prep(q_all, kv_all, idx_all, row0, B, T, pos0, *, tq):
    qd, kv2 = A_HEADS * LANE, 2 * A_KV_HEADS * LANE
    kv_col = (kv_all.shape[1] - kv2) // kv2
    kvw = kv2 // 2
    nb = T // tq
    blk0 = row0 // tq
    assert row0 % tq == 0 and qd // A_HEADS == LANE
    pos = pos0 + jnp.arange(T, dtype=jnp.int32)
    tab_q = _rope_tables(pos, LANE, LANE)
    tab_i = _rope_tables(pos, LANE, IDX_DIM)
    row = lambda w: pl.BlockSpec((tq, w), lambda b, i: (blk0 + b * nb + i, 0))
    tab = pl.BlockSpec((3, tq, LANE), lambda b, i: (0, i, 0))
    out = lambda w: pl.BlockSpec((1, tq, w), lambda b, i: (b, i, 0))
    f32, bf16 = jnp.float32, jnp.bfloat16
    heads4 = pl.BlockSpec((1, tq, A_KV_HEADS, LANE), lambda b, i: (b, i, 0, 0))
    shapes = [((B, T, qd), bf16), ((B, T, A_KV_HEADS, LANE), f32), ((B, T, A_KV_HEADS, LANE), f32),
              ((B, T, IDX_DIM), f32),
              ((B, T, kvw), bf16), ((B, T, kvw), bf16), ((B, nb, IDX_HEADS * tq, IDX_PACK), bf16),
              ((B, T, IDX_PACK), bf16), ((B, T, IDX_HEADS), f32)]
    return pl.pallas_call(
        functools.partial(_dsa_prep_kernel, q_scale=LANE ** -0.5 * LOG2_E),
        grid=(B, nb),
        in_specs=[row(qd), pl.BlockSpec((tq, kv2), lambda b, i: (blk0 + b * nb + i, kv_col)),
                  row(idx_all.shape[1]), tab, tab],
        out_specs=[out(qd), heads4, heads4, out(IDX_DIM), out(kvw), out(kvw),
                   pl.BlockSpec((1, 1, IDX_HEADS * tq, IDX_PACK), lambda b, i: (b, i, 0, 0)),
                   out(IDX_PACK), out(IDX_HEADS)],
        out_shape=[jax.ShapeDtypeStruct(s, d) for s, d in shapes],
        compiler_params=pltpu.CompilerParams(
            dimension_semantics=("parallel", "parallel"), vmem_limit_bytes=V7X_VMEM_LIMIT_BYTES),
        name="dsa_prep",
    )(q_all, kv_all, idx_all, tab_q, tab_i)


def _dsa_layer(x, mod, gate, sid, streams, w_in, w_out):
    D = x.shape[1]
    hd = D // A_HEADS
    q_dim, kv_dim = A_HEADS * hd, A_KV_HEADS * hd
    n_idx = IDX_HEADS * IDX_DIM + IDX_DIM + IDX_HEADS
    w_main = _weight_parts(w_in[:, :q_dim + 2 * kv_dim], 1)
    w_idx = jnp.pad(w_in[:, q_dim + 2 * kv_dim:], ((0, 0), (0, _round_up(n_idx, LANE) - n_idx)))
    q_all = kv_all = pmatmul(x, w_main, mod=mod, sid=sid)
    idx_all = pmatmul(x, _weight_parts(w_idx, 3), tn=w_idx.shape[1], mod=mod, sid=sid)
    outs, caches = [], []
    for row0, B, T, pos0, past_k, past_v, past_kidx in streams:
        o, k, v, ki = _dsa_stream(q_all, kv_all, idx_all, row0, B, T, pos0, past_k, past_v, past_kidx)
        outs.append(o.reshape(B * T, q_dim))
        caches.append((k, v, ki))
    x = pmatmul(jnp.concatenate(outs, axis=0), _weight_parts(w_out, 1), res=(x, gate), sid=sid)
    return x, caches


DSA_KEY_TILE = 512
DSA_QUERY_TILE = 256


def _dsa_stream(q_all, kv_all, idx_all, row0, B, T, pos0, past_k, past_v, past_kidx):
    tq = min(DSA_QUERY_TILE, T)
    q, k, v, ki, kb, vb, qi3, kidx3, wi = _dsa_prep(q_all, kv_all, idx_all, row0, B, T, pos0, tq=tq)
    n_keys = T
    if past_k is not None:
        P = past_k.shape[1]
        n_keys = P + T
        kb = jnp.concatenate([past_k.reshape(B, P, -1).astype(jnp.bfloat16), kb], axis=1)
        vb = jnp.concatenate([past_v.reshape(B, P, -1).astype(jnp.bfloat16), vb], axis=1)
        ph, pl_ = _split_bf16(past_kidx)
        kidx3 = jnp.concatenate([jnp.concatenate([ph, ph, pl_, jnp.zeros_like(ph)], axis=-1), kidx3], axis=1)
    pad = ((0, 0), (0, _round_up(n_keys, DSA_KEY_TILE) - n_keys), (0, 0))
    kb, vb, kidx3 = (jnp.pad(a, pad) for a in (kb, vb, kidx3))
    topk = min(TOPK_MAX, n_keys // 4)
    o = _dsa_core(q, qi3, wi, kb, vb, kidx3, pos0, topk, tq=tq, tk=DSA_KEY_TILE)
    return o, k, v, ki


GDN_HEAD_GROUP = 32


def _bf16_dot(a, b):
    return jnp.dot(a.astype(jnp.bfloat16), b.astype(jnp.bfloat16), preferred_element_type=jnp.float32)


def _dot3(a, b):
    ah, al = _split_bf16(a)
    bh, bl = _split_bf16(b)
    out = jnp.dot(ah, bh, preferred_element_type=jnp.float32)
    out = out + jnp.dot(ah, bl, preferred_element_type=jnp.float32)
    return out + jnp.dot(al, bh, preferred_element_type=jnp.float32)


def _conv_silu(x_ref, w_ref, xe_ref):
    C = x_ref.shape[0]
    taps = w_ref.shape[0]
    xe_ref[SUBLANE:, :] = x_ref[...]
    first = SUBLANE - (taps - 1)
    acc = xe_ref[first:first + C, :] * w_ref[0:1, :]
    for j in range(1, taps):
        acc = acc + xe_ref[first + j:first + j + C, :] * w_ref[j:j + 1, :]
    xe_ref[:SUBLANE, :] = xe_ref[C:, :]
    return acc * jax.nn.sigmoid(acc)


def _gdn_kernel(xq_ref, xk_ref, xv_ref, wq_ref, wk_ref, wv_ref, cq_ref, ck_ref, cv_ref,
                z_ref, g_ref, gt_ref, beta_ref, nw_ref, s0_ref,
                o_ref, s_out_ref, s_ref, eq_ref, ek_ref, ev_ref):
    n = pl.program_id(2)
    C = xq_ref.shape[0]
    hg = g_ref.shape[3]
    dk = s_ref.shape[1]
    dv = s_ref.shape[2]
    rep = hg // (xk_ref.shape[1] // dk)

    @pl.when(n == 0)
    def _():
        s_ref[...] = s0_ref[0]
        for e_ref, c_ref in ((eq_ref, cq_ref), (ek_ref, ck_ref), (ev_ref, cv_ref)):
            e_ref[:SUBLANE, :] = jnp.zeros((SUBLANE, e_ref.shape[1]), jnp.float32)
            e_ref[SUBLANE - c_ref.shape[1]:SUBLANE, :] = c_ref[0]

    qc = _conv_silu(xq_ref, wq_ref, eq_ref)
    kc = _conv_silu(xk_ref, wk_ref, ek_ref)
    vc = _conv_silu(xv_ref, wv_ref, ev_ref)
    q_heads, k_heads = [], []
    for i in range(hg // rep):
        qh = qc[:, i * dk:(i + 1) * dk]
        kh = kc[:, i * dk:(i + 1) * dk]
        q_heads.append(qh * (lax.rsqrt(jnp.sum(qh * qh, axis=1, keepdims=True) + EPS) * dk ** -0.5))
        k_heads.append(kh * lax.rsqrt(jnp.sum(kh * kh, axis=1, keepdims=True) + EPS))

    ri = lax.broadcasted_iota(jnp.int32, (C, C), 0)
    ci = lax.broadcasted_iota(jnp.int32, (C, C), 1)
    causal = ri >= ci
    strict = ri > ci
    eye = jnp.where(ri == ci, 1.0, 0.0)
    g = g_ref[0, 0]
    gc_all = _dot3(jnp.where(causal, 1.0, 0.0), g)
    gr_all = _dot3(gt_ref[0, 0, 0], jnp.where(ri <= ci, 1.0, 0.0))
    beta = beta_ref[0, 0]
    nw = nw_ref[...]

    heads = range(hg)
    qs = [q_heads[h // rep] for h in heads]
    ks = [k_heads[h // rep] for h in heads]
    gcs = [gc_all[:, h:h + 1] for h in heads]
    bcols = [beta[:, h:h + 1] for h in heads]
    decays = [jnp.where(causal, jnp.exp(jnp.where(causal, gcs[h] - gr_all[h:h + 1, :], 0.0)), 0.0)
              for h in heads]
    kbs = [ks[h] * bcols[h] for h in heads]
    kks = [lax.dot_general(kbs[h].astype(jnp.bfloat16), ks[h].astype(jnp.bfloat16), _NT_DIMS,
                           preferred_element_type=jnp.float32) for h in heads]
    bms = [jnp.where(strict, -(kks[h] * decays[h]), 0.0) for h in heads]
    egs = [jnp.exp(gcs[h]) for h in heads]
    rhss = [jnp.concatenate([vc[:, h * dv:(h + 1) * dv] * bcols[h], kbs[h] * egs[h]], axis=1)
            for h in heads]
    pairs = range(hg // 2)
    left = lax.broadcasted_iota(jnp.int32, (C, 2 * C), 1) < C
    zero16 = jnp.zeros((C, 2 * C), jnp.bfloat16)

    def blockdiag(part):
        return jnp.concatenate([jnp.where(left, part, zero16), jnp.where(left, zero16, part)], axis=0)

    def dot3_pairs(x_parts, y_parts):
        xh, xl = x_parts
        yh, yl = blockdiag(y_parts[0]), blockdiag(y_parts[1])
        out = jnp.dot(xh, yh, preferred_element_type=jnp.float32)
        out = out + jnp.dot(xh, yl, preferred_element_type=jnp.float32)
        return out + jnp.dot(xl, yh, preferred_element_type=jnp.float32)

    b2 = [jnp.concatenate([bms[2 * i], bms[2 * i + 1]], axis=1) for i in pairs]
    eye2 = jnp.concatenate([eye, eye], axis=1)
    p2 = [eye2 + b2[i] for i in pairs]
    b2_parts = [_split_bf16(b2[i]) for i in pairs]
    step = 2
    while step < C:
        b2 = [dot3_pairs(b2_parts[i], b2_parts[i]) for i in pairs]
        b2_parts = [_split_bf16(b2[i]) for i in pairs]
        p2 = [p2[i] + dot3_pairs(_split_bf16(p2[i]), b2_parts[i]) for i in pairs]
        step *= 2
    ps = [p2[h // 2][:, (h % 2) * C:(h % 2 + 1) * C] for h in heads]
    ws = [_dot3(ps[h], rhss[h]) for h in heads]
    qks = [lax.dot_general(qs[h].astype(jnp.bfloat16), ks[h].astype(jnp.bfloat16), _NT_DIMS,
                           preferred_element_type=jnp.float32) * decays[h] for h in heads]
    g_lasts = [gcs[h][C - 1:C, :] for h in heads]
    ss = [s_ref[h] for h in heads]
    us = [ws[h][:, :dv] - _bf16_dot(ws[h][:, dv:], ss[h]) for h in heads]
    os_ = [_bf16_dot(qs[h] * egs[h], ss[h]) + _bf16_dot(qks[h], us[h]) for h in heads]
    for h in heads:
        ke = ks[h] * jnp.exp(g_lasts[h] - gcs[h])
        s_ref[h] = ss[h] * jnp.exp(g_lasts[h]) + _bf16_dot(ke.T, us[h])
    for h in heads:
        o = os_[h]
        o = o * lax.rsqrt(jnp.mean(o * o, axis=1, keepdims=True) + EPS) * nw
        z = z_ref[:, h * dv:(h + 1) * dv]
        o_ref[0, :, h * dv:(h + 1) * dv] = (o * (z * jax.nn.sigmoid(z))).astype(o_ref.dtype)

    @pl.when(n == pl.num_programs(2) - 1)
    def _():
        s_out_ref[0] = s_ref[...]


def _gdn_core(qkv, z, row0, conv_w, conv_state, g, beta, norm_w, s0):
    B, T, VH = g.shape
    dk, dv = s0.shape[2], s0.shape[3]
    vd = VH * dv
    qd = (conv_w.shape[1] - vd) // 2
    C = min(CHUNK, T)
    N = T // C
    assert row0 % C == 0
    blk0 = row0 // C
    hg = min(GDN_HEAD_GROUP, VH)
    ng = VH // hg
    qw = qd // ng
    vw = hg * dv
    assert qd % qw == 0 and (2 * qd) % vw == 0
    k0, v0 = qd // qw, (2 * qd) // vw
    z0 = (z.shape[1] - vd) // vw
    taps = conv_w.shape[0]

    def grouped(a):
        return a.reshape(B, T, ng, hg).transpose(0, 2, 1, 3)

    gg, bg = grouped(g), grouped(beta)
    gt = gg.reshape(B, ng, N, C, hg).transpose(0, 1, 2, 4, 3)
    small = pl.BlockSpec((1, 1, C, hg), lambda b, j, n: (b, j, n, 0))
    o, s = pl.pallas_call(
        _gdn_kernel,
        grid=(B, ng, N),
        in_specs=[pl.BlockSpec((C, qw), lambda b, j, n: (blk0 + b * N + n, j)),
                  pl.BlockSpec((C, qw), lambda b, j, n: (blk0 + b * N + n, k0 + j)),
                  pl.BlockSpec((C, vw), lambda b, j, n: (blk0 + b * N + n, v0 + j)),
                  pl.BlockSpec((taps, qw), lambda b, j, n: (0, j)),
                  pl.BlockSpec((taps, qw), lambda b, j, n: (0, k0 + j)),
                  pl.BlockSpec((taps, vw), lambda b, j, n: (0, v0 + j)),
                  pl.BlockSpec((1, taps - 1, qw), lambda b, j, n: (b, 0, j)),
                  pl.BlockSpec((1, taps - 1, qw), lambda b, j, n: (b, 0, k0 + j)),
                  pl.BlockSpec((1, taps - 1, vw), lambda b, j, n: (b, 0, v0 + j)),
                  pl.BlockSpec((C, vw), lambda b, j, n: (blk0 + b * N + n, z0 + j)),
                  small,
                  pl.BlockSpec((1, 1, 1, hg, C), lambda b, j, n: (b, j, n, 0, 0)),
                  small,
                  pl.BlockSpec((1, dv), lambda b, j, n: (0, 0)),
                  pl.BlockSpec((1, hg, dk, dv), lambda b, j, n: (b, j, 0, 0))],
        out_specs=[pl.BlockSpec((1, C, hg * dv), lambda b, j, n: (b, n, j)),
                   pl.BlockSpec((1, hg, dk, dv), lambda b, j, n: (b, j, 0, 0))],
        out_shape=[jax.ShapeDtypeStruct((B, T, vd), jnp.bfloat16),
                   jax.ShapeDtypeStruct(s0.shape, jnp.float32)],
        scratch_shapes=[pltpu.VMEM((hg, dk, dv), jnp.float32),
                        pltpu.VMEM((SUBLANE + C, qw), jnp.float32),
                        pltpu.VMEM((SUBLANE + C, qw), jnp.float32),
                        pltpu.VMEM((SUBLANE + C, vw), jnp.float32)],
        compiler_params=pltpu.CompilerParams(
            dimension_semantics=("parallel", "parallel", "arbitrary"),
            vmem_limit_bytes=V7X_VMEM_LIMIT_BYTES),
        name="gdn_core",
    )(qkv, qkv, qkv, conv_w, conv_w, conv_w, conv_state, conv_state, conv_state,
      z, gg, gt, bg, norm_w.reshape(1, dv), s0)
    return o, s


def _gdn_layer(x, mod, gate, sid, streams, w_in, conv_w, a_log, dt_bias, norm_w, w_out):
    qk_dim, v_dim = GDN_QK_HEADS * GDN_DK, GDN_V_HEADS * GDN_DV
    conv_dim = 2 * qk_dim + v_dim
    n_gate = 2 * GDN_V_HEADS
    w_main = _weight_parts(w_in[:, :conv_dim + v_dim], 1)
    w_gate = jnp.pad(w_in[:, conv_dim + v_dim:], ((0, 0), (0, _round_up(n_gate, LANE) - n_gate)))
    qkv_all = z_all = pmatmul(x, w_main, mod=mod, sid=sid)
    gates_all = pmatmul(x, _weight_parts(w_gate, 1), tn=w_gate.shape[1], mod=mod, sid=sid)
    outs, states = [], []
    for row0, B, T, conv_state, ssm_state in streams:
        rows = slice(row0, row0 + B * T)
        gates = gates_all[rows].reshape(B, T, -1)
        beta_raw, a_raw = gates[..., :GDN_V_HEADS], gates[..., GDN_V_HEADS:n_gate]
        if conv_state is None:
            conv_state = jnp.zeros((B, CONV_W - 1, conv_dim), x.dtype)
        if ssm_state is None:
            ssm_state = jnp.zeros((B, GDN_V_HEADS, GDN_DK, GDN_DV), jnp.float32)
        tail = jnp.stack([qkv_all[row0 + (b + 1) * T - (CONV_W - 1):row0 + (b + 1) * T, :conv_dim]
                          for b in range(B)])
        new_conv = jnp.concatenate([conv_state, tail], axis=1)[:, -(CONV_W - 1):]
        beta = jax.nn.sigmoid(beta_raw)
        g = -jnp.exp(a_log) * jax.nn.softplus(a_raw + dt_bias)
        o, S = _gdn_core(qkv_all, z_all, row0, conv_w, conv_state, g, beta, norm_w, ssm_state)
        outs.append(o.reshape(B * T, v_dim))
        states.append((new_conv, S))
    x = pmatmul(jnp.concatenate(outs, axis=0), _weight_parts(w_out, 1), res=(x, gate), sid=sid)
    return x, states


NEG_INF = float("-inf")
_PEER_CAND_ROWS = tuple((i, PEER_TOPK // (i + 1)) for i in range(PEER_TOPK // 2))


def _split_bf16(x):
    hi = x.astype(jnp.bfloat16)
    lo = (x - hi.astype(jnp.float32)).astype(jnp.bfloat16)
    return hi, lo


def _dot3_nt(a, b):
    dn = (((1,), (1,)), ((), ()))
    ah, al = _split_bf16(a)
    bh, bl = _split_bf16(b)
    out = lax.dot_general(ah, bh, dn, preferred_element_type=jnp.float32)
    out = out + lax.dot_general(ah, bl, dn, preferred_element_type=jnp.float32)
    return out + lax.dot_general(al, bh, dn, preferred_element_type=jnp.float32)


def _top_rows_desc(s, n, with_rank=False):
    rows = []
    cur = s
    rank = jnp.full(s.shape, float(n), jnp.float32)
    for i in range(n):
        m = jnp.max(cur, axis=0, keepdims=True)
        rows.append(m)
        hit = cur == m
        if with_rank:
            rank = jnp.where(hit, float(i), rank)
        cur = jnp.where(hit, NEG_INF, cur)
    return (rows, rank) if with_rank else rows


def _stack_rows(rows, lanes):
    n = len(rows)
    rid = lax.broadcasted_iota(jnp.int32, (n, lanes), 0)
    out = jnp.zeros((n, lanes), jnp.float32)
    for i, r in enumerate(rows):
        out = jnp.where(rid == i, r, out)
    return out


def _peer_select_kernel(q_ref, keys_ref, rk2_ref, e2_ref, cnt_ref, e1_ref):
    tm = q_ref.shape[0]
    k = PEER_TOPK
    s1 = _dot3_nt(keys_ref[0], q_ref[:, :PEER_HALF])
    s2 = _dot3_nt(keys_ref[1], q_ref[:, PEER_HALF:])
    r1 = _top_rows_desc(s1, k)
    r2, rank2 = _top_rows_desc(s2, k, with_rank=True)
    v1 = _stack_rows(r1, tm)
    v2 = _stack_rows(r2, tm)
    v2h = v2[:k // 2]
    rid = lax.broadcasted_iota(jnp.int32, (k // 2, tm), 0)
    pieces = [r1[0] + v2]
    for i, n in _PEER_CAND_ROWS[1:]:
        pieces.append(jnp.where(rid < n, r1[i] + v2h, NEG_INF))
    pieces.append(v1[k // 2:] + r2[0])
    cand = jnp.concatenate(pieces, axis=0)
    tau = _top_rows_desc(cand, k)[-1]
    top = r1[0] + r2[0]
    z = jnp.sum(jnp.where(cand >= tau, jnp.exp(cand - top), 0.0), axis=0, keepdims=True)
    cnt = jnp.zeros(s1.shape, jnp.float32)
    for j in range(k):
        cnt = cnt + jnp.where(s1 + r2[j] >= tau, 1.0, 0.0)
    rk2_ref[0] = pltpu.bitcast(rank2.astype(jnp.bfloat16), rk2_ref.dtype)
    e2_ref[0] = pltpu.bitcast((jnp.exp(s2 - r2[0]) / z).astype(jnp.bfloat16), e2_ref.dtype)
    cnt_ref[0] = cnt
    e1_ref[0] = jnp.exp(s1 - r1[0])


def _peer_select(q, keys, *, tm):
    M = q.shape[0]
    nk = keys.shape[1]
    ospec = pl.BlockSpec((1, nk, tm), lambda i, h: (h, 0, i))
    pspec = pl.BlockSpec((1, nk // 2, tm), lambda i, h: (h, 0, i))
    return pl.pallas_call(
        _peer_select_kernel,
        grid=(M // tm, PEER_HEADS),
        in_specs=[pl.BlockSpec((tm, PEER_KEY_DIM), lambda i, h: (i, h)),
                  pl.BlockSpec(keys.shape, lambda i, h: (0, 0, 0))],
        out_specs=[pspec, pspec, ospec, ospec],
        out_shape=[jax.ShapeDtypeStruct((PEER_HEADS, nk // 2, M), jnp.int32)] * 2
        + [jax.ShapeDtypeStruct((PEER_HEADS, nk, M), jnp.float32)] * 2,
        compiler_params=pltpu.CompilerParams(
            dimension_semantics=("parallel", "arbitrary"),
            vmem_limit_bytes=V7X_VMEM_LIMIT_BYTES),
        name="peer_select",
    )(q, keys)


BF16_SUBLANES = 16
PEER_SUB_SPLIT = (1, 2, 1)


def _peer_main_kernel(sid_ref, x_ref, gain_ref, shift_ref, scale_ref, gate_ref, fnorm_ref,
                      u_ref, vT_ref, rk2_ref, e2_ref, cnt_ref, e1_ref, *refs, final_norm):
    out_refs, (hT_ref, coef_ref, acc_ref) = refs[:-3], refs[-3:]
    e = pl.program_id(1)
    te, tm = coef_ref.shape
    nk = 2 * rk2_ref.shape[1]
    rows = BF16_SUBLANES
    blk0 = pl.program_id(0) * (tm // SEQ_BLOCK)

    @pl.when(e == 0)
    def _():
        acc_ref[...] = jnp.zeros_like(acc_ref)
        per_lane_tile = LANE // SEQ_BLOCK
        for t in range(tm // LANE):
            hs = []
            for r in range(t * per_lane_tile, (t + 1) * per_lane_tile):
                rs = slice(r * SEQ_BLOCK, (r + 1) * SEQ_BLOCK)
                hs.append(_modulated_rows(x_ref[rs, :], gain_ref[...], shift_ref, scale_ref, sid_ref[blk0 + r]))
            hT_ref[:, t * LANE:(t + 1) * LANE] = jnp.concatenate(hs, axis=0).T.astype(jnp.bfloat16)

    def expert_acts(span):
        off, size = span
        return jnp.dot(u_ref[off:off + size, :], hT_ref[...], preferred_element_type=jnp.float32)

    def weigh(span, act):
        off, size = span
        for al in range(size // nk):
            a = off // nk + al
            for lg in range(tm // LANE):
                lanes = slice(lg * LANE, (lg + 1) * LANE)
                cnts = [jnp.broadcast_to(cnt_ref[h, a:a + 1, lanes], (rows, LANE)).astype(jnp.bfloat16)
                        for h in range(PEER_HEADS)]
                e1s = [jnp.broadcast_to(e1_ref[h, a:a + 1, lanes], (rows, LANE)).astype(jnp.bfloat16)
                       for h in range(PEER_HEADS)]
                for r in range(nk // rows):
                    words = slice(r * rows // 2, (r + 1) * rows // 2)
                    w = None
                    for h in range(PEER_HEADS):
                        e2 = pltpu.bitcast(e2_ref[h, words, lanes], jnp.bfloat16)
                        rk2 = pltpu.bitcast(rk2_ref[h, words, lanes], jnp.bfloat16)
                        t = jnp.where(rk2 < cnts[h], e2, jnp.zeros_like(e2)) * e1s[h]
                        w = t if w is None else w + t
                    x = act[al * nk + r * rows:al * nk + (r + 1) * rows, lanes]
                    g = 0.5 * x * (1.0 + lax.erf(x * (2.0 ** -0.5)))
                    coef_ref[off + al * nk + r * rows:off + al * nk + (r + 1) * rows, lanes] = (
                        w * g.astype(jnp.bfloat16))

    def accumulate(span):
        off, size = span
        acc_ref[...] += lax.dot_general(vT_ref[off:off + size, :], coef_ref[off:off + size, :],
                                        (((0,), (0,)), ((), ())), preferred_element_type=jnp.float32)

    spans, off = [], 0
    for frac in PEER_SUB_SPLIT:
        spans.append((off, te * frac // sum(PEER_SUB_SPLIT)))
        off += spans[-1][1]
    act = expert_acts(spans[0])
    for i, span in enumerate(spans):
        nxt = expert_acts(spans[i + 1]) if i + 1 < len(spans) else None
        weigh(span, act)
        accumulate(span)
        act = nxt

    @pl.when(e == pl.num_programs(1) - 1)
    def _():
        out = acc_ref[...].T
        for r in range(tm // SEQ_BLOCK):
            rs = slice(r * SEQ_BLOCK, (r + 1) * SEQ_BLOCK)
            y = x_ref[rs, :] + gate_ref[pl.ds(sid_ref[blk0 + r], 1), :] * out[rs, :]
            out_refs[0][rs, :] = y
            if final_norm:
                out_refs[1][rs, :] = y * lax.rsqrt(jnp.mean(y * y, axis=-1, keepdims=True) + EPS) * fnorm_ref[...]


def _peer_main(x, mod, gate, fnorm, sid, u, vT, rk2, e2, cnt, e1, *, tm, te, final_norm):
    M, D = x.shape
    E = u.shape[0]
    nk = cnt.shape[1]
    const = lambda a: pl.BlockSpec(a.shape, lambda i, e, s: (0,) * a.ndim)
    col_spec = pl.BlockSpec((PEER_HEADS, nk // 2, tm), lambda i, e, s: (0, 0, i))
    row_spec = pl.BlockSpec((PEER_HEADS, te // nk, tm), lambda i, e, s: (0, e, i))
    tok_spec = pl.BlockSpec((tm, D), lambda i, e, s: (i, 0))
    n_out = 2 if final_norm else 1
    return pl.pallas_call(
        functools.partial(_peer_main_kernel, final_norm=final_norm),
        grid_spec=pltpu.PrefetchScalarGridSpec(
            num_scalar_prefetch=1,
            grid=(M // tm, E // te),
            in_specs=[tok_spec, const(mod[0]), const(mod[1]), const(mod[2]), const(gate), const(fnorm),
                      pl.BlockSpec((te, D), lambda i, e, s: (e, 0)),
                      pl.BlockSpec((te, D), lambda i, e, s: (e, 0)),
                      col_spec, col_spec, row_spec, row_spec],
            out_specs=[tok_spec] * n_out,
            scratch_shapes=[pltpu.VMEM((D, tm), jnp.bfloat16),
                            pltpu.VMEM((te, tm), jnp.bfloat16),
                            pltpu.VMEM((D, tm), jnp.float32)]),
        out_shape=[jax.ShapeDtypeStruct((M, D), jnp.float32)] * n_out,
        compiler_params=pltpu.CompilerParams(
            dimension_semantics=("parallel", "arbitrary"),
            vmem_limit_bytes=V7X_VMEM_LIMIT_PEER_BYTES),
        name="peer_main",
    )(sid, x, *mod, gate, fnorm, u, vT, rk2, e2, cnt, e1)


def _peer(x, mod, gate, fnorm, sid, w_query, sub_keys, expert_u, expert_v, *, final_norm,
          tm_sel=256, tm=512, te=1024):
    M = x.shape[0]
    q = pmatmul(x, _weight_parts(w_query, 1), mod=mod, sid=sid)
    rk2, e2, cnt, e1 = _peer_select(q, sub_keys, tm=min(tm_sel, M))
    u = expert_u.astype(jnp.bfloat16)
    vT = expert_v.astype(jnp.bfloat16)
    return _peer_main(x, mod, gate, fnorm, sid, u, vT, rk2, e2, cnt, e1, tm=min(tm, M), te=te,
                      final_norm=final_norm)


def kernel(x_prompt, x_sample, c_prompt, c_sample, cache_k_l0, cache_v_l0, cache_kidx_l0, state_conv_l1, state_ssm_l1, norm1_l0, norm2_l0, ada_w_l0, ada_b_l0, attn_in_l0, attn_out_l0, peer_query_l0, peer_keys_l0, peer_u_l0, peer_v_l0, norm1_l1, norm2_l1, ada_w_l1, ada_b_l1, gdn_in_l1, gdn_conv_l1, gdn_a_log_l1, gdn_dt_bias_l1, gdn_norm_l1, gdn_out_l1, peer_query_l1, peer_keys_l1, peer_u_l1, peer_v_l1, final_norm):
    past_len = cache_k_l0.shape[1]
    norm1 = (norm1_l0, norm1_l1)
    norm2 = (norm2_l0, norm2_l1)
    ada_w = (ada_w_l0, ada_w_l1)
    ada_b = (ada_b_l0, ada_b_l1)
    peer_query = (peer_query_l0, peer_query_l1)
    peer_keys = (peer_keys_l0, peer_keys_l1)
    peer_u = (peer_u_l0, peer_u_l1)
    peer_v = (peer_v_l0, peer_v_l1)
    Bp, Tp, D = x_prompt.shape
    Bs, Ts, _ = x_sample.shape
    n_p, n_s = Bp * Tp, Bs * Ts
    x = jnp.concatenate([x_prompt.reshape(n_p, D), x_sample.reshape(n_s, D)], axis=0)
    sid = jnp.concatenate([jnp.repeat(jnp.arange(Bp, dtype=jnp.int32), Tp // SEQ_BLOCK),
                           Bp + jnp.repeat(jnp.arange(Bs, dtype=jnp.int32), Ts // SEQ_BLOCK)])
    c_all = jnp.concatenate([c_prompt, c_sample], axis=0)
    fnorm = final_norm.reshape(1, D)
    for i in range(2):
        mods = _adaln(c_all, ada_w[i], ada_b[i])
        mod1 = (norm1[i].reshape(1, D), mods[0], mods[1])
        if i == 0:
            x, ((nkp, nvp, nkip), (nks, nvs, nkis)) = _dsa_layer(
                x, mod1, mods[2], sid,
                [(0, Bp, Tp, 0, None, None, None), (n_p, Bs, Ts, past_len, cache_k_l0, cache_v_l0, cache_kidx_l0)],
                attn_in_l0, attn_out_l0)
        else:
            x, ((ncp, nsp), (ncs, nss)) = _gdn_layer(
                x, mod1, mods[2], sid, [(0, Bp, Tp, None, None), (n_p, Bs, Ts, state_conv_l1, state_ssm_l1)],
                gdn_in_l1, gdn_conv_l1, gdn_a_log_l1, gdn_dt_bias_l1, gdn_norm_l1, gdn_out_l1)
        mod2 = (norm2[i].reshape(1, D), mods[3], mods[4])
        outs = _peer(x, mod2, mods[5], fnorm, sid, peer_query[i], peer_keys[i], peer_u[i], peer_v[i],
                     final_norm=(i == 1))
        x = outs[0]
    y = outs[1]
    y_prompt = y[:n_p].reshape(Bp, Tp, D)
    y_sample = y[n_p:].reshape(Bs, Ts, D)
    return (y_prompt, y_sample, nkp, nvp, nkip, nks, nvs, nkis, ncp, nsp, ncs, nss)
```

```python
import functools

import jax
import jax.numpy as jnp
from jax import lax
from jax.experimental import pallas as pl
from jax.experimental.pallas import tpu as pltpu

CHUNK = 64
CHUNK_SHIFT = CHUNK.bit_length() - 1
EPS = 1e-6
ROPE_THETA = 500000.0
ROPE_FRACTION = 4
A_HEADS = 16
A_KV_HEADS = 4
IDX_HEADS = 8
IDX_DIM = 64
TOPK_MAX = 256
Q_BLOCK = 128
GDN_QK_HEADS = 16
GDN_V_HEADS = 32
GDN_DK = 128
GDN_DV = 128
CONV_W = 4
PEER_HEADS = 8
PEER_NKEYS = 128
PEER_KEY_DIM = 256
PEER_HALF = PEER_KEY_DIM // 2
PEER_TOPK = 16

V7X_VMEM_LIMIT_BYTES = 48 * 1024 * 1024
V7X_VMEM_LIMIT_PEER_BYTES = 56 * 1024 * 1024
LANE = 128
SUBLANE = 8


def _round_up(n, m):
    return (n + m - 1) // m * m


SEQ_BLOCK = CHUNK


def _modulated_rows(x, gain, shift_ref, scale_ref, sid):
    y = x * lax.rsqrt(jnp.mean(x * x, axis=-1, keepdims=True) + EPS)
    return y * gain * (1.0 + scale_ref[pl.ds(sid, 1), :]) + shift_ref[pl.ds(sid, 1), :]


def _matmul_kernel(sid_ref, x_ref, *refs, passes, has_mod, has_res):
    refs = list(refs)
    mod_refs = [refs.pop(0) for _ in range(3)] if has_mod else None
    nparts = 1 + passes // 2
    w_refs = [refs.pop(0) for _ in range(nparts)]
    res_refs = [refs.pop(0) for _ in range(2)] if has_res else None
    o_ref, x_parts = refs[0], refs[1:]
    tm = x_ref.shape[0]
    blocks = range(tm // SEQ_BLOCK) if (has_mod or has_res) else ()
    blk0 = pl.program_id(0) * (tm // SEQ_BLOCK)

    @pl.when(pl.program_id(1) == 0)
    def _():
        def put(rows, x):
            hi = x.astype(jnp.bfloat16)
            x_parts[0][rows, :] = hi
            if passes == 3:
                x_parts[1][rows, :] = (x - hi.astype(jnp.float32)).astype(jnp.bfloat16)

        if has_mod:
            gain_ref, shift_ref, scale_ref = mod_refs
            for r in blocks:
                rows = slice(r * SEQ_BLOCK, (r + 1) * SEQ_BLOCK)
                put(rows, _modulated_rows(x_ref[rows, :], gain_ref[...], shift_ref, scale_ref, sid_ref[blk0 + r]))
        else:
            put(slice(None), x_ref[...].astype(jnp.float32))

    acc = jnp.dot(x_parts[0][...], w_refs[0][...], preferred_element_type=jnp.float32)
    if passes == 3:
        acc = acc + jnp.dot(x_parts[0][...], w_refs[1][...], preferred_element_type=jnp.float32)
        acc = acc + jnp.dot(x_parts[1][...], w_refs[0][...], preferred_element_type=jnp.float32)
    if has_res:
        res_ref, gate_ref = res_refs
        for r in blocks:
            rows = slice(r * SEQ_BLOCK, (r + 1) * SEQ_BLOCK)
            gate = gate_ref[pl.ds(sid_ref[blk0 + r], 1), :]
            o_ref[rows, :] = res_ref[rows, :] + gate * acc[rows, :]
    else:
        o_ref[...] = acc.astype(o_ref.dtype)


def _weight_parts(w, passes):
    hi = w.astype(jnp.bfloat16)
    if passes == 1:
        return (hi,)
    return (hi, (w - hi.astype(jnp.float32)).astype(jnp.bfloat16))


def pmatmul(x, w_parts, *, col0=0, ncols=None, tm=512, tn=1024, out_dtype=jnp.float32,
            mod=None, res=None, sid=None):
    passes = 1 if len(w_parts) == 1 else 3
    M, K = x.shape
    n_total = w_parts[0].shape[1]
    ncols = n_total - col0 if ncols is None else ncols
    tm = min(tm, _round_up(M, 2 * SUBLANE))
    tn = min(tn, ncols)
    assert ncols % tn == 0 and col0 % tn == 0 and (tn % LANE == 0 or tn == n_total)
    Mp = _round_up(M, tm)
    if mod is not None or res is not None:
        assert Mp == M and tm % SEQ_BLOCK == 0 and sid is not None
    else:
        sid = jnp.zeros((1,), jnp.int32)
    if Mp != M:
        x = jnp.pad(x, ((0, Mp - M), (0, 0)))
    c0 = col0 // tn
    in_specs = [pl.BlockSpec((tm, K), lambda i, j, s: (i, 0))]
    args = [x]
    if mod is not None:
        in_specs += [pl.BlockSpec(a.shape, lambda i, j, s: (0, 0)) for a in mod]
        args += list(mod)
    in_specs += [pl.BlockSpec((K, tn), lambda i, j, s: (0, c0 + j))] * len(w_parts)
    args += list(w_parts)
    if res is not None:
        in_specs += [pl.BlockSpec((tm, tn), lambda i, j, s: (i, j)),
                     pl.BlockSpec((res[1].shape[0], tn), lambda i, j, s: (0, j))]
        args += list(res)
        out_dtype = jnp.float32
    out = pl.pallas_call(
        functools.partial(_matmul_kernel, passes=passes, has_mod=mod is not None, has_res=res is not None),
        grid_spec=pltpu.PrefetchScalarGridSpec(
            num_scalar_prefetch=1,
            grid=(Mp // tm, ncols // tn),
            in_specs=in_specs,
            out_specs=pl.BlockSpec((tm, tn), lambda i, j, s: (i, j)),
            scratch_shapes=[pltpu.VMEM((tm, K), jnp.bfloat16)] * (1 + passes // 2)),
        out_shape=jax.ShapeDtypeStruct((Mp, ncols), out_dtype),
        compiler_params=pltpu.CompilerParams(
            dimension_semantics=("parallel", "arbitrary"),
            vmem_limit_bytes=V7X_VMEM_LIMIT_BYTES),
        name="matmul",
    )(sid, *args)
    return out[:M]


def _adaln_kernel(c_ref, w_ref, b_ref, o_ref):
    c = c_ref[...]
    o_ref[...] = _dot3(c * jax.nn.sigmoid(c), w_ref[...]) + b_ref[...]


def _adaln(c, w, b, *, tn=512):
    n, D = c.shape
    N = w.shape[1]
    rows = _round_up(n, SUBLANE)
    mod = pl.pallas_call(
        _adaln_kernel,
        grid=(N // tn,),
        in_specs=[pl.BlockSpec((rows, D), lambda j: (0, 0)),
                  pl.BlockSpec((D, tn), lambda j: (0, j)),
                  pl.BlockSpec((1, tn), lambda j: (0, j))],
        out_specs=pl.BlockSpec((rows, tn), lambda j: (0, j)),
        out_shape=jax.ShapeDtypeStruct((rows, N), jnp.float32),
        compiler_params=pltpu.CompilerParams(
            dimension_semantics=("parallel",), vmem_limit_bytes=V7X_VMEM_LIMIT_BYTES),
        name="adaln",
    )(jnp.pad(c, ((0, rows - n), (0, 0))), w, b.reshape(1, N))
    return jnp.split(mod, 6, axis=-1)


_NT_DIMS = (((1,), (1,)), ((), ()))
INT32_MIN = -2 ** 31
LOG2_E = 1.4426950408889634
_NEG_INF_KEY = -2139095041
IDX_PACK = 4 * IDX_DIM


def _ordered_key(x):
    bits = pltpu.bitcast(x, jnp.int32)
    return bits ^ ((bits >> 31) & 0x7FFFFFFF)


def _lane_tile_sum(x, width=LANE):
    out = x[:, :width]
    for c in range(1, x.shape[1] // width):
        out = out + x[:, c * width:(c + 1) * width]
    return out


def _dsa_select_bias(qi_ref, wi_ref, kidx_ref, key_ref, bias_ref, *, first, n_tiles, topk, tk):
    tq = wi_ref.shape[1]
    row = lax.broadcasted_iota(jnp.int32, (tq, 1), 0)
    lim = (((first + row) >> CHUNK_SHIFT) + 1) * CHUNK
    w = wi_ref[0]

    def score_tile(j, c):
        off = pl.multiple_of(j * tk, tk)
        kt = kidx_ref[0, pl.ds(off, tk), :]
        sc = lax.dot_general(qi_ref[0, 0], kt, _NT_DIMS,
                             preferred_element_type=jnp.float32)
        s = jnp.zeros((tq, tk), jnp.float32)
        for h in range(IDX_HEADS):
            s = s + w[:, h:h + 1] * jnp.maximum(sc[h * tq:(h + 1) * tq], 0.0)
        col = off + lax.broadcasted_iota(jnp.int32, (tq, tk), 1)
        s = jnp.where(col < lim, s + 0.0, NEG_INF)
        key_ref[:, pl.ds(off, tk)] = _ordered_key(s)
        return c

    lax.fori_loop(0, n_tiles, score_tile, 0)

    def bit_step(b, thr):
        cand = thr + lax.shift_left(jnp.int32(1), 31 - b)

        def count_tile(j, c):
            off = pl.multiple_of(j * tk, tk)
            ge = jnp.where(key_ref[:, pl.ds(off, tk)] >= cand, 1.0, 0.0)
            return c + _lane_tile_sum(ge)

        c = lax.fori_loop(0, n_tiles, count_tile, jnp.zeros((tq, LANE), jnp.float32))
        cnt = jnp.sum(c, axis=1, keepdims=True)
        return jnp.where(cnt >= topk, cand, thr)

    thr = lax.fori_loop(0, 32, bit_step, jnp.full((tq, 1), INT32_MIN, jnp.int32))
    thr = jnp.maximum(thr, _NEG_INF_KEY + 1)

    def bias_tile(j, c):
        off = pl.multiple_of(j * tk, tk)
        bias_ref[:, pl.ds(off, tk)] = jnp.where(key_ref[:, pl.ds(off, tk)] >= thr, 0.0, NEG_INF)
        return c

    lax.fori_loop(0, n_tiles, bias_tile, 0)


def _dsa_kernel(qi_ref, wi_ref, kidx_ref, q_ref, k_ref, v_ref, o_ref,
                key_ref, bias_ref, qg_ref, m_ref, l_ref, acc_ref, *, pos0, topk, tk):
    i = pl.program_id(1)
    tq = q_ref.shape[1]
    hd = acc_ref.shape[2]
    groups = acc_ref.shape[0]
    rep = q_ref.shape[2] // (groups * hd)
    first = pos0 + i * tq
    n_valid = (((first + tq - 1) >> CHUNK_SHIFT) + 1) * CHUNK
    n_tiles = (n_valid + tk - 1) // tk
    _dsa_select_bias(qi_ref, wi_ref, kidx_ref, key_ref, bias_ref, first=first, n_tiles=n_tiles, topk=topk, tk=tk)

    for g in range(groups):
        for r in range(rep):
            c0 = (g * rep + r) * hd
            qg_ref[g, r * tq:(r + 1) * tq, :] = q_ref[0, :, c0:c0 + hd]
    m_ref[...] = jnp.full(m_ref.shape, NEG_INF, jnp.float32)
    l_ref[...] = jnp.zeros(l_ref.shape, jnp.float32)
    acc_ref[...] = jnp.zeros(acc_ref.shape, jnp.float32)
    lane_reps = tk // LANE

    def att_tile(j, c):
        off = pl.multiple_of(j * tk, tk)
        b = bias_ref[:, pl.ds(off, tk)]
        bias = jnp.concatenate([b] * rep, axis=0)
        for g in range(groups):
            kt = k_ref[0, pl.ds(off, tk), g * hd:(g + 1) * hd]
            vt = v_ref[0, pl.ds(off, tk), g * hd:(g + 1) * hd]
            lg = lax.dot_general(qg_ref[g], kt, _NT_DIMS, preferred_element_type=jnp.float32) + bias
            m_old = m_ref[g]
            m_new = jnp.maximum(m_old, jnp.max(lg, axis=1, keepdims=True))
            m_safe = jnp.where(m_new == NEG_INF, 0.0, m_new)
            p = jnp.exp2(lg - jnp.tile(m_safe, (1, lane_reps)))
            alpha = jnp.exp2(m_old - m_safe)
            l_ref[g] = alpha * l_ref[g] + jnp.sum(p, axis=1, keepdims=True)
            acc_ref[g] = alpha * acc_ref[g] + jnp.dot(p.astype(jnp.bfloat16), vt,
                                                      preferred_element_type=jnp.float32)
            m_ref[g] = m_new
        return c

    lax.fori_loop(0, n_tiles, att_tile, 0)
    for g in range(groups):
        out = acc_ref[g] / l_ref[g]
        for r in range(rep):
            c0 = (g * rep + r) * hd
            o_ref[0, :, c0:c0 + hd] = out[r * tq:(r + 1) * tq].astype(o_ref.dtype)


def _dsa_core(q, qi3, wi, kb, vb, kidx3, pos0, topk, *, tq, tk):
    B, T, qd = q.shape
    Sp = kb.shape[1]
    hd = qd // A_HEADS
    gw = qd // A_KV_HEADS
    nb = T // tq
    assert Sp % tk == 0 and hd == LANE
    rows = (gw // hd) * tq
    kvw = A_KV_HEADS * hd
    return pl.pallas_call(
        functools.partial(_dsa_kernel, pos0=pos0, topk=topk, tk=tk),
        grid=(B, nb),
        in_specs=[pl.BlockSpec((1, 1, IDX_HEADS * tq, IDX_PACK), lambda b, i: (b, i, 0, 0)),
                  pl.BlockSpec((1, tq, IDX_HEADS), lambda b, i: (b, i, 0)),
                  pl.BlockSpec((1, Sp, IDX_PACK), lambda b, i: (b, 0, 0)),
                  pl.BlockSpec((1, tq, qd), lambda b, i: (b, i, 0)),
                  pl.BlockSpec((1, Sp, kvw), lambda b, i: (b, 0, 0)),
                  pl.BlockSpec((1, Sp, kvw), lambda b, i: (b, 0, 0))],
        out_specs=pl.BlockSpec((1, tq, qd), lambda b, i: (b, i, 0)),
        out_shape=jax.ShapeDtypeStruct((B, T, qd), jnp.bfloat16),
        scratch_shapes=[pltpu.VMEM((tq, Sp), jnp.int32),
                        pltpu.VMEM((tq, Sp), jnp.float32),
                        pltpu.VMEM((A_KV_HEADS, rows, hd), jnp.bfloat16),
                        pltpu.VMEM((A_KV_HEADS, rows, LANE), jnp.float32),
                        pltpu.VMEM((A_KV_HEADS, rows, LANE), jnp.float32),
                        pltpu.VMEM((A_KV_HEADS, rows, hd), jnp.float32)],
        compiler_params=pltpu.CompilerParams(
            dimension_semantics=("parallel", "arbitrary"),
            vmem_limit_bytes=V7X_VMEM_LIMIT_BYTES),
        name="dsa_core",
    )(qi3, wi, kidx3, q, kb, vb)


def _rope_tables(pos, width, period):
    half = period // ROPE_FRACTION // 2
    inv_freq = ROPE_THETA ** (-jnp.arange(half, dtype=jnp.float32) / half)
    ang = pos.astype(jnp.float32)[:, None] * inv_freq[None, :]
    cos, sin = jnp.cos(ang), jnp.sin(ang)
    T = pos.shape[0]
    rest = period - 2 * half
    c = jnp.concatenate([cos, cos, jnp.ones((T, rest), jnp.float32)], axis=1)
    s_next = jnp.concatenate([-sin, jnp.zeros((T, period - half), jnp.float32)], axis=1)
    s_prev = jnp.concatenate([jnp.zeros((T, half), jnp.float32), sin, jnp.zeros((T, rest), jnp.float32)], axis=1)
    return jnp.stack([jnp.tile(t, (1, width // period)) for t in (c, s_next, s_prev)])


def _rope_lanes(x, tab_ref, half):
    return (x * tab_ref[0] + pltpu.roll(x, LANE - half, 1) * tab_ref[1] + pltpu.roll(x, half, 1) * tab_ref[2])


def _dsa_prep_kernel(q_ref, kv_ref, idx_ref, tq_ref, ti_ref,
                     qa_ref, kn_ref, vn_ref, kin_ref, kb_ref, vb_ref, qi3_ref, kidx3_ref, wi_ref, *, q_scale):
    tq = q_ref.shape[0]
    hd = LANE
    kvw = kv_ref.shape[1] // 2
    half_q = hd // ROPE_FRACTION // 2
    half_i = IDX_DIM // ROPE_FRACTION // 2
    left = lax.broadcasted_iota(jnp.int32, (tq, LANE), 1) < IDX_DIM
    zero = jnp.zeros((tq, LANE), jnp.float32)

    for h in range(q_ref.shape[1] // hd):
        cols = slice(h * hd, (h + 1) * hd)
        qa_ref[0, :, cols] = (_rope_lanes(q_ref[:, cols], tq_ref, half_q) * q_scale).astype(qa_ref.dtype)
    for h in range(kvw // hd):
        cols = slice(h * hd, (h + 1) * hd)
        k = _rope_lanes(kv_ref[:, cols], tq_ref, half_q)
        kn_ref[0, :, h, :] = k
        kb_ref[0, :, cols] = k.astype(kb_ref.dtype)
        vn_ref[0, :, h, :] = kv_ref[:, kvw + h * hd:kvw + (h + 1) * hd]
    vb_ref[0] = kv_ref[:, kvw:].astype(vb_ref.dtype)

    def hi_lo(x):
        hi = x.astype(jnp.bfloat16).astype(jnp.float32)
        return hi, x - hi

    for t in range(IDX_HEADS * IDX_DIM // LANE):
        hi, lo = hi_lo(_rope_lanes(idx_ref[:, t * LANE:(t + 1) * LANE], ti_ref, half_i))
        hi_sw, lo_sw = pltpu.roll(hi, IDX_DIM, 1), pltpu.roll(lo, IDX_DIM, 1)
        even = jnp.concatenate([jnp.where(left, hi, lo_sw), jnp.where(left, hi, zero)], axis=1)
        odd = jnp.concatenate([jnp.where(left, hi_sw, lo), jnp.where(left, hi_sw, zero)], axis=1)
        qi3_ref[0, 0, (2 * t) * tq:(2 * t + 1) * tq, :] = even.astype(qi3_ref.dtype)
        qi3_ref[0, 0, (2 * t + 1) * tq:(2 * t + 2) * tq, :] = odd.astype(qi3_ref.dtype)
    c0 = IDX_HEADS * IDX_DIM
    x = idx_ref[:, c0:c0 + LANE]
    r = jnp.where(left, _rope_lanes(x, ti_ref, half_i), x)
    kin_ref[0] = r[:, :IDX_DIM]
    hi, lo = hi_lo(r)
    kidx3 = jnp.concatenate([jnp.where(left, hi, pltpu.roll(hi, IDX_DIM, 1)), jnp.where(left, lo, zero)], axis=1)
    kidx3_ref[0] = kidx3.astype(kidx3_ref.dtype)
    wi_ref[0] = x[:, IDX_DIM:IDX_DIM + IDX_HEADS] * (IDX_HEADS ** -0.5 * IDX_DIM ** -0.5)


def _dsa_prep(q_all, kv_all, idx_all, row0, B, T, pos0, *, tq):
    qd, kv2 = A_HEADS * LANE, 2 * A_KV_HEADS * LANE
    kv_col = (kv_all.shape[1] - kv2) // kv2
    kvw = kv2 // 2
    nb = T // tq
    blk0 = row0 // tq
    assert row0 % tq == 0 and qd // A_HEADS == LANE
    pos = pos0 + jnp.arange(T, dtype=jnp.int32)
    tab_q = _rope_tables(pos, LANE, LANE)
    tab_i = _rope_tables(pos, LANE, IDX_DIM)
    row = lambda w: pl.BlockSpec((tq, w), lambda b, i: (blk0 + b * nb + i, 0))
    tab = pl.BlockSpec((3, tq, LANE), lambda b, i: (0, i, 0))
    out = lambda w: pl.BlockSpec((1, tq, w), lambda b, i: (b, i, 0))
    f32, bf16 = jnp.float32, jnp.bfloat16
    heads4 = pl.BlockSpec((1, tq, A_KV_HEADS, LANE), lambda b, i: (b, i, 0, 0))
    shapes = [((B, T, qd), bf16), ((B, T, A_KV_HEADS, LANE), f32), ((B, T, A_KV_HEADS, LANE), f32),
              ((B, T, IDX_DIM), f32),
              ((B, T, kvw), bf16), ((B, T, kvw), bf16), ((B, nb, IDX_HEADS * tq, IDX_PACK), bf16),
              ((B, T, IDX_PACK), bf16), ((B, T, IDX_HEADS), f32)]
    return pl.pallas_call(
        functools.partial(_dsa_prep_kernel, q_scale=LANE ** -0.5 * LOG2_E),
        grid=(B, nb),
        in_specs=[row(qd), pl.BlockSpec((tq, kv2), lambda b, i: (blk0 + b * nb + i, kv_col)),
                  row(idx_all.shape[1]), tab, tab],
        out_specs=[out(qd), heads4, heads4, out(IDX_DIM), out(kvw), out(kvw),
                   pl.BlockSpec((1, 1, IDX_HEADS * tq, IDX_PACK), lambda b, i: (b, i, 0, 0)),
                   out(IDX_PACK), out(IDX_HEADS)],
        out_shape=[jax.ShapeDtypeStruct(s, d) for s, d in shapes],
        compiler_params=pltpu.CompilerParams(
            dimension_semantics=("parallel", "parallel"), vmem_limit_bytes=V7X_VMEM_LIMIT_BYTES),
        name="dsa_prep",
    )(q_all, kv_all, idx_all, tab_q, tab_i)


def _dsa_layer(x, mod, gate, sid, streams, w_in, w_out):
    D = x.shape[1]
    hd = D // A_HEADS
    q_dim, kv_dim = A_HEADS * hd, A_KV_HEADS * hd
    n_idx = IDX_HEADS * IDX_DIM + IDX_DIM + IDX_HEADS
    w_main = _weight_parts(w_in[:, :q_dim + 2 * kv_dim], 1)
    w_idx = jnp.pad(w_in[:, q_dim + 2 * kv_dim:], ((0, 0), (0, _round_up(n_idx, LANE) - n_idx)))
    q_all = kv_all = pmatmul(x, w_main, mod=mod, sid=sid)
    idx_all = pmatmul(x, _weight_parts(w_idx, 3), tn=w_idx.shape[1], mod=mod, sid=sid)
    outs, caches = [], []
    for row0, B, T, pos0, past_k, past_v, past_kidx in streams:
        o, k, v, ki = _dsa_stream(q_all, kv_all, idx_all, row0, B, T, pos0, past_k, past_v, past_kidx)
        outs.append(o.reshape(B * T, q_dim))
        caches.append((k, v, ki))
    x = pmatmul(jnp.concatenate(outs, axis=0), _weight_parts(w_out, 1), res=(x, gate), sid=sid)
    return x, caches


DSA_KEY_TILE = 512
DSA_QUERY_TILE = 256


def _dsa_stream(q_all, kv_all, idx_all, row0, B, T, pos0, past_k, past_v, past_kidx):
    tq = min(DSA_QUERY_TILE, T)
    q, k, v, ki, kb, vb, qi3, kidx3, wi = _dsa_prep(q_all, kv_all, idx_all, row0, B, T, pos0, tq=tq)
    n_keys = T
    if past_k is not None:
        P = past_k.shape[1]
        n_keys = P + T
        kb = jnp.concatenate([past_k.reshape(B, P, -1).astype(jnp.bfloat16), kb], axis=1)
        vb = jnp.concatenate([past_v.reshape(B, P, -1).astype(jnp.bfloat16), vb], axis=1)
        ph, pl_ = _split_bf16(past_kidx)
        kidx3 = jnp.concatenate([jnp.concatenate([ph, ph, pl_, jnp.zeros_like(ph)], axis=-1), kidx3], axis=1)
    pad = ((0, 0), (0, _round_up(n_keys, DSA_KEY_TILE) - n_keys), (0, 0))
    kb, vb, kidx3 = (jnp.pad(a, pad) for a in (kb, vb, kidx3))
    topk = min(TOPK_MAX, n_keys // 4)
    o = _dsa_core(q, qi3, wi, kb, vb, kidx3, pos0, topk, tq=tq, tk=DSA_KEY_TILE)
    return o, k, v, ki


GDN_HEAD_GROUP = 32


def _bf16_dot(a, b):
    return jnp.dot(a.astype(jnp.bfloat16), b.astype(jnp.bfloat16), preferred_element_type=jnp.float32)


def _dot3(a, b):
    ah, al = _split_bf16(a)
    bh, bl = _split_bf16(b)
    out = jnp.dot(ah, bh, preferred_element_type=jnp.float32)
    out = out + jnp.dot(ah, bl, preferred_element_type=jnp.float32)
    return out + jnp.dot(al, bh, preferred_element_type=jnp.float32)


def _conv_silu(x_ref, w_ref, xe_ref):
    C = x_ref.shape[0]
    taps = w_ref.shape[0]
    xe_ref[SUBLANE:, :] = x_ref[...]
    first = SUBLANE - (taps - 1)
    acc = xe_ref[first:first + C, :] * w_ref[0:1, :]
    for j in range(1, taps):
        acc = acc + xe_ref[first + j:first + j + C, :] * w_ref[j:j + 1, :]
    xe_ref[:SUBLANE, :] = xe_ref[C:, :]
    return acc * jax.nn.sigmoid(acc)


def _gdn_kernel(xq_ref, xk_ref, xv_ref, wq_ref, wk_ref, wv_ref, cq_ref, ck_ref, cv_ref,
                z_ref, g_ref, gt_ref, beta_ref, nw_ref, s0_ref,
                o_ref, s_out_ref, s_ref, eq_ref, ek_ref, ev_ref):
    n = pl.program_id(2)
    C = xq_ref.shape[0]
    hg = g_ref.shape[3]
    dk = s_ref.shape[1]
    dv = s_ref.shape[2]
    rep = hg // (xk_ref.shape[1] // dk)

    @pl.when(n == 0)
    def _():
        s_ref[...] = s0_ref[0]
        for e_ref, c_ref in ((eq_ref, cq_ref), (ek_ref, ck_ref), (ev_ref, cv_ref)):
            e_ref[:SUBLANE, :] = jnp.zeros((SUBLANE, e_ref.shape[1]), jnp.float32)
            e_ref[SUBLANE - c_ref.shape[1]:SUBLANE, :] = c_ref[0]

    qc = _conv_silu(xq_ref, wq_ref, eq_ref)
    kc = _conv_silu(xk_ref, wk_ref, ek_ref)
    vc = _conv_silu(xv_ref, wv_ref, ev_ref)
    q_heads, k_heads = [], []
    for i in range(hg // rep):
        qh = qc[:, i * dk:(i + 1) * dk]
        kh = kc[:, i * dk:(i + 1) * dk]
        q_heads.append(qh * (lax.rsqrt(jnp.sum(qh * qh, axis=1, keepdims=True) + EPS) * dk ** -0.5))
        k_heads.append(kh * lax.rsqrt(jnp.sum(kh * kh, axis=1, keepdims=True) + EPS))

    ri = lax.broadcasted_iota(jnp.int32, (C, C), 0)
    ci = lax.broadcasted_iota(jnp.int32, (C, C), 1)
    causal = ri >= ci
    strict = ri > ci
    eye = jnp.where(ri == ci, 1.0, 0.0)
    g = g_ref[0, 0]
    gc_all = _dot3(jnp.where(causal, 1.0, 0.0), g)
    gr_all = _dot3(gt_ref[0, 0, 0], jnp.where(ri <= ci, 1.0, 0.0))
    beta = beta_ref[0, 0]
    nw = nw_ref[...]

    heads = range(hg)
    qs = [q_heads[h // rep] for h in heads]
    ks = [k_heads[h // rep] for h in heads]
    gcs = [gc_all[:, h:h + 1] for h in heads]
    bcols = [beta[:, h:h + 1] for h in heads]
    decays = [jnp.where(causal, jnp.exp(jnp.where(causal, gcs[h] - gr_all[h:h + 1, :], 0.0)), 0.0)
              for h in heads]
    kbs = [ks[h] * bcols[h] for h in heads]
    kks = [lax.dot_general(kbs[h].astype(jnp.bfloat16), ks[h].astype(jnp.bfloat16), _NT_DIMS,
                           preferred_element_type=jnp.float32) for h in heads]
    bms = [jnp.where(strict, -(kks[h] * decays[h]), 0.0) for h in heads]
    egs = [jnp.exp(gcs[h]) for h in heads]
    rhss = [jnp.concatenate([vc[:, h * dv:(h + 1) * dv] * bcols[h], kbs[h] * egs[h]], axis=1)
            for h in heads]
    pairs = range(hg // 2)
    left = lax.broadcasted_iota(jnp.int32, (C, 2 * C), 1) < C
    zero16 = jnp.zeros((C, 2 * C), jnp.bfloat16)

    def blockdiag(part):
        return jnp.concatenate([jnp.where(left, part, zero16), jnp.where(left, zero16, part)], axis=0)

    def dot3_pairs(x_parts, y_parts):
        xh, xl = x_parts
        yh, yl = blockdiag(y_parts[0]), blockdiag(y_parts[1])
        out = jnp.dot(xh, yh, preferred_element_type=jnp.float32)
        out = out + jnp.dot(xh, yl, preferred_element_type=jnp.float32)
        return out + jnp.dot(xl, yh, preferred_element_type=jnp.float32)

    b2 = [jnp.concatenate([bms[2 * i], bms[2 * i + 1]], axis=1) for i in pairs]
    eye2 = jnp.concatenate([eye, eye], axis=1)
    p2 = [eye2 + b2[i] for i in pairs]
    b2_parts = [_split_bf16(b2[i]) for i in pairs]
    step = 2
    while step < C:
        b2 = [dot3_pairs(b2_parts[i], b2_parts[i]) for i in pairs]
        b2_parts = [_split_bf16(b2[i]) for i in pairs]
        p2 = [p2[i] + dot3_pairs(_split_bf16(p2[i]), b2_parts[i]) for i in pairs]
        step *= 2
    ps = [p2[h // 2][:, (h % 2) * C:(h % 2 + 1) * C] for h in heads]
    ws = [_dot3(ps[h], rhss[h]) for h in heads]
    qks = [lax.dot_general(qs[h].astype(jnp.bfloat16), ks[h].astype(jnp.bfloat16), _NT_DIMS,
                           preferred_element_type=jnp.float32) * decays[h] for h in heads]
    g_lasts = [gcs[h][C - 1:C, :] for h in heads]
    ss = [s_ref[h] for h in heads]
    us = [ws[h][:, :dv] - _bf16_dot(ws[h][:, dv:], ss[h]) for h in heads]
    os_ = [_bf16_dot(qs[h] * egs[h], ss[h]) + _bf16_dot(qks[h], us[h]) for h in heads]
    for h in heads:
        ke = ks[h] * jnp.exp(g_lasts[h] - gcs[h])
        s_ref[h] = ss[h] * jnp.exp(g_lasts[h]) + _bf16_dot(ke.T, us[h])
    for h in heads:
        o = os_[h]
        o = o * lax.rsqrt(jnp.mean(o * o, axis=1, keepdims=True) + EPS) * nw
        z = z_ref[:, h * dv:(h + 1) * dv]
        o_ref[0, :, h * dv:(h + 1) * dv] = (o * (z * jax.nn.sigmoid(z))).astype(o_ref.dtype)

    @pl.when(n == pl.num_programs(2) - 1)
    def _():
        s_out_ref[0] = s_ref[...]


def _gdn_core(qkv, z, row0, conv_w, conv_state, g, beta, norm_w, s0):
    B, T, VH = g.shape
    dk, dv = s0.shape[2], s0.shape[3]
    vd = VH * dv
    qd = (conv_w.shape[1] - vd) // 2
    C = min(CHUNK, T)
    N = T // C
    assert row0 % C == 0
    blk0 = row0 // C
    hg = min(GDN_HEAD_GROUP, VH)
    ng = VH // hg
    qw = qd // ng
    vw = hg * dv
    assert qd % qw == 0 and (2 * qd) % vw == 0
    k0, v0 = qd // qw, (2 * qd) // vw
    z0 = (z.shape[1] - vd) // vw
    taps = conv_w.shape[0]

    def grouped(a):
        return a.reshape(B, T, ng, hg).transpose(0, 2, 1, 3)

    gg, bg = grouped(g), grouped(beta)
    gt = gg.reshape(B, ng, N, C, hg).transpose(0, 1, 2, 4, 3)
    small = pl.BlockSpec((1, 1, C, hg), lambda b, j, n: (b, j, n, 0))
    o, s = pl.pallas_call(
        _gdn_kernel,
        grid=(B, ng, N),
        in_specs=[pl.BlockSpec((C, qw), lambda b, j, n: (blk0 + b * N + n, j)),
                  pl.BlockSpec((C, qw), lambda b, j, n: (blk0 + b * N + n, k0 + j)),
                  pl.BlockSpec((C, vw), lambda b, j, n: (blk0 + b * N + n, v0 + j)),
                  pl.BlockSpec((taps, qw), lambda b, j, n: (0, j)),
                  pl.BlockSpec((taps, qw), lambda b, j, n: (0, k0 + j)),
                  pl.BlockSpec((taps, vw), lambda b, j, n: (0, v0 + j)),
                  pl.BlockSpec((1, taps - 1, qw), lambda b, j, n: (b, 0, j)),
                  pl.BlockSpec((1, taps - 1, qw), lambda b, j, n: (b, 0, k0 + j)),
                  pl.BlockSpec((1, taps - 1, vw), lambda b, j, n: (b, 0, v0 + j)),
                  pl.BlockSpec((C, vw), lambda b, j, n: (blk0 + b * N + n, z0 + j)),
                  small,
                  pl.BlockSpec((1, 1, 1, hg, C), lambda b, j, n: (b, j, n, 0, 0)),
                  small,
                  pl.BlockSpec((1, dv), lambda b, j, n: (0, 0)),
                  pl.BlockSpec((1, hg, dk, dv), lambda b, j, n: (b, j, 0, 0))],
        out_specs=[pl.BlockSpec((1, C, hg * dv), lambda b, j, n: (b, n, j)),
                   pl.BlockSpec((1, hg, dk, dv), lambda b, j, n: (b, j, 0, 0))],
        out_shape=[jax.ShapeDtypeStruct((B, T, vd), jnp.bfloat16),
                   jax.ShapeDtypeStruct(s0.shape, jnp.float32)],
        scratch_shapes=[pltpu.VMEM((hg, dk, dv), jnp.float32),
                        pltpu.VMEM((SUBLANE + C, qw), jnp.float32),
                        pltpu.VMEM((SUBLANE + C, qw), jnp.float32),
                        pltpu.VMEM((SUBLANE + C, vw), jnp.float32)],
        compiler_params=pltpu.CompilerParams(
            dimension_semantics=("parallel", "parallel", "arbitrary"),
            vmem_limit_bytes=V7X_VMEM_LIMIT_BYTES),
        name="gdn_core",
    )(qkv, qkv, qkv, conv_w, conv_w, conv_w, conv_state, conv_state, conv_state,
      z, gg, gt, bg, norm_w.reshape(1, dv), s0)
    return o, s


def _gdn_layer(x, mod, gate, sid, streams, w_in, conv_w, a_log, dt_bias, norm_w, w_out):
    qk_dim, v_dim = GDN_QK_HEADS * GDN_DK, GDN_V_HEADS * GDN_DV
    conv_dim = 2 * qk_dim + v_dim
    n_gate = 2 * GDN_V_HEADS
    w_main = _weight_parts(w_in[:, :conv_dim + v_dim], 1)
    w_gate = jnp.pad(w_in[:, conv_dim + v_dim:], ((0, 0), (0, _round_up(n_gate, LANE) - n_gate)))
    qkv_all = z_all = pmatmul(x, w_main, mod=mod, sid=sid)
    gates_all = pmatmul(x, _weight_parts(w_gate, 1), tn=w_gate.shape[1], mod=mod, sid=sid)
    outs, states = [], []
    for row0, B, T, conv_state, ssm_state in streams:
        rows = slice(row0, row0 + B * T)
        gates = gates_all[rows].reshape(B, T, -1)
        beta_raw, a_raw = gates[..., :GDN_V_HEADS], gates[..., GDN_V_HEADS:n_gate]
        if conv_state is None:
            conv_state = jnp.zeros((B, CONV_W - 1, conv_dim), x.dtype)
        if ssm_state is None:
            ssm_state = jnp.zeros((B, GDN_V_HEADS, GDN_DK, GDN_DV), jnp.float32)
        tail = jnp.stack([qkv_all[row0 + (b + 1) * T - (CONV_W - 1):row0 + (b + 1) * T, :conv_dim]
                          for b in range(B)])
        new_conv = jnp.concatenate([conv_state, tail], axis=1)[:, -(CONV_W - 1):]
        beta = jax.nn.sigmoid(beta_raw)
        g = -jnp.exp(a_log) * jax.nn.softplus(a_raw + dt_bias)
        o, S = _gdn_core(qkv_all, z_all, row0, conv_w, conv_state, g, beta, norm_w, ssm_state)
        outs.append(o.reshape(B * T, v_dim))
        states.append((new_conv, S))
    x = pmatmul(jnp.concatenate(outs, axis=0), _weight_parts(w_out, 1), res=(x, gate), sid=sid)
    return x, states


NEG_INF = float("-inf")
_PEER_CAND_ROWS = tuple((i, PEER_TOPK // (i + 1)) for i in range(PEER_TOPK // 2))


def _split_bf16(x):
    hi = x.astype(jnp.bfloat16)
    lo = (x - hi.astype(jnp.float32)).astype(jnp.bfloat16)
    return hi, lo


def _dot3_nt(a, b):
    dn = (((1,), (1,)), ((), ()))
    ah, al = _split_bf16(a)
    bh, bl = _split_bf16(b)
    out = lax.dot_general(ah, bh, dn, preferred_element_type=jnp.float32)
    out = out + lax.dot_general(ah, bl, dn, preferred_element_type=jnp.float32)
    return out + lax.dot_general(al, bh, dn, preferred_element_type=jnp.float32)


def _top_rows_desc(s, n, with_rank=False):
    rows = []
    cur = s
    rank = jnp.full(s.shape, float(n), jnp.float32)
    for i in range(n):
        m = jnp.max(cur, axis=0, keepdims=True)
        rows.append(m)
        hit = cur == m
        if with_rank:
            rank = jnp.where(hit, float(i), rank)
        cur = jnp.where(hit, NEG_INF, cur)
    return (rows, rank) if with_rank else rows


def _stack_rows(rows, lanes):
    n = len(rows)
    rid = lax.broadcasted_iota(jnp.int32, (n, lanes), 0)
    out = jnp.zeros((n, lanes), jnp.float32)
    for i, r in enumerate(rows):
        out = jnp.where(rid == i, r, out)
    return out


def _peer_select_kernel(q_ref, keys_ref, rk2_ref, e2_ref, cnt_ref, e1_ref):
    tm = q_ref.shape[0]
    k = PEER_TOPK
    s1 = _dot3_nt(keys_ref[0], q_ref[:, :PEER_HALF])
    s2 = _dot3_nt(keys_ref[1], q_ref[:, PEER_HALF:])
    r1 = _top_rows_desc(s1, k)
    r2, rank2 = _top_rows_desc(s2, k, with_rank=True)
    v1 = _stack_rows(r1, tm)
    v2 = _stack_rows(r2, tm)
    v2h = v2[:k // 2]
    rid = lax.broadcasted_iota(jnp.int32, (k // 2, tm), 0)
    pieces = [r1[0] + v2]
    for i, n in _PEER_CAND_ROWS[1:]:
        pieces.append(jnp.where(rid < n, r1[i] + v2h, NEG_INF))
    pieces.append(v1[k // 2:] + r2[0])
    cand = jnp.concatenate(pieces, axis=0)
    tau = _top_rows_desc(cand, k)[-1]
    top = r1[0] + r2[0]
    z = jnp.sum(jnp.where(cand >= tau, jnp.exp(cand - top), 0.0), axis=0, keepdims=True)
    cnt = jnp.zeros(s1.shape, jnp.float32)
    for j in range(k):
        cnt = cnt + jnp.where(s1 + r2[j] >= tau, 1.0, 0.0)
    rk2_ref[0] = pltpu.bitcast(rank2.astype(jnp.bfloat16), rk2_ref.dtype)
    e2_ref[0] = pltpu.bitcast((jnp.exp(s2 - r2[0]) / z).astype(jnp.bfloat16), e2_ref.dtype)
    cnt_ref[0] = cnt
    e1_ref[0] = jnp.exp(s1 - r1[0])


def _peer_select(q, keys, *, tm):
    M = q.shape[0]
    nk = keys.shape[1]
    ospec = pl.BlockSpec((1, nk, tm), lambda i, h: (h, 0, i))
    pspec = pl.BlockSpec((1, nk // 2, tm), lambda i, h: (h, 0, i))
    return pl.pallas_call(
        _peer_select_kernel,
        grid=(M // tm, PEER_HEADS),
        in_specs=[pl.BlockSpec((tm, PEER_KEY_DIM), lambda i, h: (i, h)),
                  pl.BlockSpec(keys.shape, lambda i, h: (0, 0, 0))],
        out_specs=[pspec, pspec, ospec, ospec],
        out_shape=[jax.ShapeDtypeStruct((PEER_HEADS, nk // 2, M), jnp.int32)] * 2
        + [jax.ShapeDtypeStruct((PEER_HEADS, nk, M), jnp.float32)] * 2,
        compiler_params=pltpu.CompilerParams(
            dimension_semantics=("parallel", "arbitrary"),
            vmem_limit_bytes=V7X_VMEM_LIMIT_BYTES),
        name="peer_select",
    )(q, keys)


BF16_SUBLANES = 16
PEER_SUB_SPLIT = (1, 1)


def _peer_main_kernel(sid_ref, x_ref, gain_ref, shift_ref, scale_ref, gate_ref, fnorm_ref,
                      u_ref, vT_ref, rk2_ref, e2_ref, cnt_ref, e1_ref, *refs, final_norm):
    out_refs, (hT_ref, coef_ref, acc_ref) = refs[:-3], refs[-3:]
    e = pl.program_id(1)
    te, tm = coef_ref.shape
    nk = 2 * rk2_ref.shape[1]
    rows = BF16_SUBLANES
    blk0 = pl.program_id(0) * (tm // SEQ_BLOCK)

    @pl.when(e == 0)
    def _():
        acc_ref[...] = jnp.zeros_like(acc_ref)
        per_lane_tile = LANE // SEQ_BLOCK
        for t in range(tm // LANE):
            hs = []
            for r in range(t * per_lane_tile, (t + 1) * per_lane_tile):
                rs = slice(r * SEQ_BLOCK, (r + 1) * SEQ_BLOCK)
                hs.append(_modulated_rows(x_ref[rs, :], gain_ref[...], shift_ref, scale_ref, sid_ref[blk0 + r]))
            hT_ref[:, t * LANE:(t + 1) * LANE] = jnp.concatenate(hs, axis=0).T.astype(jnp.bfloat16)

    def expert_acts(span):
        off, size = span
        return jnp.dot(u_ref[off:off + size, :], hT_ref[...], preferred_element_type=jnp.float32)

    def weigh(span, act):
        off, size = span
        for al in range(size // nk):
            a = off // nk + al
            for lg in range(tm // LANE):
                lanes = slice(lg * LANE, (lg + 1) * LANE)
                cnts = [jnp.broadcast_to(cnt_ref[h, a:a + 1, lanes], (rows, LANE)).astype(jnp.bfloat16)
                        for h in range(PEER_HEADS)]
                e1s = [jnp.broadcast_to(e1_ref[h, a:a + 1, lanes], (rows, LANE)).astype(jnp.bfloat16)
                       for h in range(PEER_HEADS)]
                for r in range(nk // rows):
                    words = slice(r * rows // 2, (r + 1) * rows // 2)
                    w = None
                    for h in range(PEER_HEADS):
                        e2 = pltpu.bitcast(e2_ref[h, words, lanes], jnp.bfloat16)
                        rk2 = pltpu.bitcast(rk2_ref[h, words, lanes], jnp.bfloat16)
                        t = jnp.where(rk2 < cnts[h], e2, jnp.zeros_like(e2)) * e1s[h]
                        w = t if w is None else w + t
                    x = act[al * nk + r * rows:al * nk + (r + 1) * rows, lanes]
                    g = 0.5 * x * (1.0 + lax.erf(x * (2.0 ** -0.5)))
                    coef_ref[off + al * nk + r * rows:off + al * nk + (r + 1) * rows, lanes] = (
                        w * g.astype(jnp.bfloat16))

    def accumulate(span):
        off, size = span
        acc_ref[...] += lax.dot_general(vT_ref[off:off + size, :], coef_ref[off:off + size, :],
                                        (((0,), (0,)), ((), ())), preferred_element_type=jnp.float32)

    spans, off = [], 0
    for frac in PEER_SUB_SPLIT:
        spans.append((off, te * frac // sum(PEER_SUB_SPLIT)))
        off += spans[-1][1]
    act = expert_acts(spans[0])
    for i, span in enumerate(spans):
        nxt = expert_acts(spans[i + 1]) if i + 1 < len(spans) else None
        weigh(span, act)
        accumulate(span)
        act = nxt

    @pl.when(e == pl.num_programs(1) - 1)
    def _():
        out = acc_ref[...].T
        for r in range(tm // SEQ_BLOCK):
            rs = slice(r * SEQ_BLOCK, (r + 1) * SEQ_BLOCK)
            y = x_ref[rs, :] + gate_ref[pl.ds(sid_ref[blk0 + r], 1), :] * out[rs, :]
            out_refs[0][rs, :] = y
            if final_norm:
                out_refs[1][rs, :] = y * lax.rsqrt(jnp.mean(y * y, axis=-1, keepdims=True) + EPS) * fnorm_ref[...]


def _peer_main(x, mod, gate, fnorm, sid, u, vT, rk2, e2, cnt, e1, *, tm, te, final_norm):
    M, D = x.shape
    E = u.shape[0]
    nk = cnt.shape[1]
    const = lambda a: pl.BlockSpec(a.shape, lambda i, e, s: (0,) * a.ndim)
    col_spec = pl.BlockSpec((PEER_HEADS, nk // 2, tm), lambda i, e, s: (0, 0, i))
    row_spec = pl.BlockSpec((PEER_HEADS, te // nk, tm), lambda i, e, s: (0, e, i))
    tok_spec = pl.BlockSpec((tm, D), lambda i, e, s: (i, 0))
    n_out = 2 if final_norm else 1
    return pl.pallas_call(
        functools.partial(_peer_main_kernel, final_norm=final_norm),
        grid_spec=pltpu.PrefetchScalarGridSpec(
            num_scalar_prefetch=1,
            grid=(M // tm, E // te),
            in_specs=[tok_spec, const(mod[0]), const(mod[1]), const(mod[2]), const(gate), const(fnorm),
                      pl.BlockSpec((te, D), lambda i, e, s: (e, 0)),
                      pl.BlockSpec((te, D), lambda i, e, s: (e, 0)),
                      col_spec, col_spec, row_spec, row_spec],
            out_specs=[tok_spec] * n_out,
            scratch_shapes=[pltpu.VMEM((D, tm), jnp.bfloat16),
                            pltpu.VMEM((te, tm), jnp.bfloat16),
                            pltpu.VMEM((D, tm), jnp.float32)]),
        out_shape=[jax.ShapeDtypeStruct((M, D), jnp.float32)] * n_out,
        compiler_params=pltpu.CompilerParams(
            dimension_semantics=("parallel", "arbitrary"),
            vmem_limit_bytes=V7X_VMEM_LIMIT_PEER_BYTES),
        name="peer_main",
    )(sid, x, *mod, gate, fnorm, u, vT, rk2, e2, cnt, e1)


def _peer(x, mod, gate, fnorm, sid, w_query, sub_keys, expert_u, expert_v, *, final_norm,
          tm_sel=256, tm=512, te=1024):
    M = x.shape[0]
    q = pmatmul(x, _weight_parts(w_query, 1), mod=mod, sid=sid)
    rk2, e2, cnt, e1 = _peer_select(q, sub_keys, tm=min(tm_sel, M))
    u = expert_u.astype(jnp.bfloat16)
    vT = expert_v.astype(jnp.bfloat16)
    return _peer_main(x, mod, gate, fnorm, sid, u, vT, rk2, e2, cnt, e1, tm=min(tm, M), te=te,
                      final_norm=final_norm)


def kernel(x_prompt, x_sample, c_prompt, c_sample, cache_k_l0, cache_v_l0, cache_kidx_l0, state_conv_l1, state_ssm_l1, norm1_l0, norm2_l0, ada_w_l0, ada_b_l0, attn_in_l0, attn_out_l0, peer_query_l0, peer_keys_l0, peer_u_l0, peer_v_l0, norm1_l1, norm2_l1, ada_w_l1, ada_b_l1, gdn_in_l1, gdn_conv_l1, gdn_a_log_l1, gdn_dt_bias_l1, gdn_norm_l1, gdn_out_l1, peer_query_l1, peer_keys_l1, peer_u_l1, peer_v_l1, final_norm):
    past_len = cache_k_l0.shape[1]
    norm1 = (norm1_l0, norm1_l1)
    norm2 = (norm2_l0, norm2_l1)
    ada_w = (ada_w_l0, ada_w_l1)
    ada_b = (ada_b_l0, ada_b_l1)
    peer_query = (peer_query_l0, peer_query_l1)
    peer_keys = (peer_keys_l0, peer_keys_l1)
    peer_u = (peer_u_l0, peer_u_l1)
    peer_v = (peer_v_l0, peer_v_l1)
    Bp, Tp, D = x_prompt.shape
    Bs, Ts, _ = x_sample.shape
    n_p, n_s = Bp * Tp, Bs * Ts
    x = jnp.concatenate([x_prompt.reshape(n_p, D), x_sample.reshape(n_s, D)], axis=0)
    sid = jnp.concatenate([jnp.repeat(jnp.arange(Bp, dtype=jnp.int32), Tp // SEQ_BLOCK),
                           Bp + jnp.repeat(jnp.arange(Bs, dtype=jnp.int32), Ts // SEQ_BLOCK)])
    c_all = jnp.concatenate([c_prompt, c_sample], axis=0)
    fnorm = final_norm.reshape(1, D)
    for i in range(2):
        mods = _adaln(c_all, ada_w[i], ada_b[i])
        mod1 = (norm1[i].reshape(1, D), mods[0], mods[1])
        if i == 0:
            x, ((nkp, nvp, nkip), (nks, nvs, nkis)) = _dsa_layer(
                x, mod1, mods[2], sid,
                [(0, Bp, Tp, 0, None, None, None), (n_p, Bs, Ts, past_len, cache_k_l0, cache_v_l0, cache_kidx_l0)],
                attn_in_l0, attn_out_l0)
        else:
            x, ((ncp, nsp), (ncs, nss)) = _gdn_layer(
                x, mod1, mods[2], sid, [(0, Bp, Tp, None, None), (n_p, Bs, Ts, state_conv_l1, state_ssm_l1)],
                gdn_in_l1, gdn_conv_l1, gdn_a_log_l1, gdn_dt_bias_l1, gdn_norm_l1, gdn_out_l1)
        mod2 = (norm2[i].reshape(1, D), mods[3], mods[4])
        outs = _peer(x, mod2, mods[5], fnorm, sid, peer_query[i], peer_keys[i], peer_u[i], peer_v[i],
                     final_norm=(i == 1))
        x = outs[0]
    y = outs[1]
    y_prompt = y[:n_p].reshape(Bp, Tp, D)
    y_sample = y[n_p:].reshape(Bs, Ts, D)
    return (y_prompt, y_sample, nkp, nvp, nkip, nks, nvs, nkis, ncp, nsp, ncs, nss)
```

```python
import functools

import jax
import jax.numpy as jnp
from jax import lax
from jax.experimental import pallas as pl
from jax.experimental.pallas import tpu as pltpu

CHUNK = 64
CHUNK_SHIFT = CHUNK.bit_length() - 1
EPS = 1e-6
ROPE_THETA = 500000.0
ROPE_FRACTION = 4
A_HEADS = 16
A_KV_HEADS = 4
IDX_HEADS = 8
IDX_DIM = 64
TOPK_MAX = 256
Q_BLOCK = 128
GDN_QK_HEADS = 16
GDN_V_HEADS = 32
GDN_DK = 128
GDN_DV = 128
CONV_W = 4
PEER_HEADS = 8
PEER_KEY_DIM = 256
PEER_HALF = PEER_KEY_DIM // 2
PEER_TOPK = 16

V7X_VMEM_LIMIT_BYTES = 48 * 1024 * 1024
V7X_VMEM_LIMIT_PEER_BYTES = 56 * 1024 * 1024
LANE = 128
SUBLANE = 8


def _round_up(n, m):
    return (n + m - 1) // m * m


SEQ_BLOCK = CHUNK


def _modulated_rows(x, gain, shift_ref, scale_ref, sid):
    y = x * lax.rsqrt(jnp.mean(x * x, axis=-1, keepdims=True) + EPS)
    return y * gain * (1.0 + scale_ref[pl.ds(sid, 1), :]) + shift_ref[pl.ds(sid, 1), :]


def _matmul_kernel(sid_ref, x_ref, *refs, passes, has_mod, has_res):
    refs = list(refs)
    mod_refs = [refs.pop(0) for _ in range(3)] if has_mod else None
    nparts = 1 + passes // 2
    w_refs = [refs.pop(0) for _ in range(nparts)]
    res_refs = [refs.pop(0) for _ in range(2)] if has_res else None
    o_ref, x_parts = refs[0], refs[1:]
    tm = x_ref.shape[0]
    blocks = range(tm // SEQ_BLOCK) if (has_mod or has_res) else ()
    blk0 = pl.program_id(0) * (tm // SEQ_BLOCK)

    @pl.when(pl.program_id(1) == 0)
    def _():
        def put(rows, x):
            hi = x.astype(jnp.bfloat16)
            x_parts[0][rows, :] = hi
            if passes == 3:
                x_parts[1][rows, :] = (x - hi.astype(jnp.float32)).astype(jnp.bfloat16)

        if has_mod:
            gain_ref, shift_ref, scale_ref = mod_refs
            for r in blocks:
                rows = slice(r * SEQ_BLOCK, (r + 1) * SEQ_BLOCK)
                put(rows, _modulated_rows(x_ref[rows, :], gain_ref[...], shift_ref, scale_ref, sid_ref[blk0 + r]))
        else:
            put(slice(None), x_ref[...].astype(jnp.float32))

    acc = jnp.dot(x_parts[0][...], w_refs[0][...], preferred_element_type=jnp.float32)
    if passes == 3:
        acc = acc + jnp.dot(x_parts[0][...], w_refs[1][...], preferred_element_type=jnp.float32)
        acc = acc + jnp.dot(x_parts[1][...], w_refs[0][...], preferred_element_type=jnp.float32)
    if has_res:
        res_ref, gate_ref = res_refs
        for r in blocks:
            rows = slice(r * SEQ_BLOCK, (r + 1) * SEQ_BLOCK)
            gate = gate_ref[pl.ds(sid_ref[blk0 + r], 1), :]
            o_ref[rows, :] = res_ref[rows, :] + gate * acc[rows, :]
    else:
        o_ref[...] = acc.astype(o_ref.dtype)


def _weight_parts(w, passes):
    hi = w.astype(jnp.bfloat16)
    if passes == 1:
        return (hi,)
    return (hi, (w - hi.astype(jnp.float32)).astype(jnp.bfloat16))


def pmatmul(x, w_parts, *, col0=0, ncols=None, tm=512, tn=1024, out_dtype=jnp.float32,
            mod=None, res=None, sid=None):
    passes = 1 if len(w_parts) == 1 else 3
    M, K = x.shape
    n_total = w_parts[0].shape[1]
    ncols = n_total - col0 if ncols is None else ncols
    tm = min(tm, _round_up(M, 2 * SUBLANE))
    tn = min(tn, ncols)
    assert ncols % tn == 0 and col0 % tn == 0 and (tn % LANE == 0 or tn == n_total)
    Mp = _round_up(M, tm)
    if mod is not None or res is not None:
        assert Mp == M and tm % SEQ_BLOCK == 0 and sid is not None
    else:
        sid = jnp.zeros((1,), jnp.int32)
    if Mp != M:
        x = jnp.pad(x, ((0, Mp - M), (0, 0)))
    c0 = col0 // tn
    in_specs = [pl.BlockSpec((tm, K), lambda i, j, s: (i, 0))]
    args = [x]
    if mod is not None:
        in_specs += [pl.BlockSpec(a.shape, lambda i, j, s: (0, 0)) for a in mod]
        args += list(mod)
    in_specs += [pl.BlockSpec((K, tn), lambda i, j, s: (0, c0 + j))] * len(w_parts)
    args += list(w_parts)
    if res is not None:
        in_specs += [pl.BlockSpec((tm, tn), lambda i, j, s: (i, j)),
                     pl.BlockSpec((res[1].shape[0], tn), lambda i, j, s: (0, j))]
        args += list(res)
        out_dtype = jnp.float32
    out = pl.pallas_call(
        functools.partial(_matmul_kernel, passes=passes, has_mod=mod is not None, has_res=res is not None),
        grid_spec=pltpu.PrefetchScalarGridSpec(
            num_scalar_prefetch=1,
            grid=(Mp // tm, ncols // tn),
            in_specs=in_specs,
            out_specs=pl.BlockSpec((tm, tn), lambda i, j, s: (i, j)),
            scratch_shapes=[pltpu.VMEM((tm, K), jnp.bfloat16)] * (1 + passes // 2)),
        out_shape=jax.ShapeDtypeStruct((Mp, ncols), out_dtype),
        compiler_params=pltpu.CompilerParams(
            dimension_semantics=("parallel", "arbitrary"),
            vmem_limit_bytes=V7X_VMEM_LIMIT_BYTES),
        name="matmul",
    )(sid, *args)
    return out[:M]


def _adaln_kernel(c_ref, w_ref, b_ref, o_ref):
    c = c_ref[...]
    o_ref[...] = _dot3(c * jax.nn.sigmoid(c), w_ref[...]) + b_ref[...]


def _adaln(c, w, b, *, tn=512):
    n, D = c.shape
    N = w.shape[1]
    rows = _round_up(n, SUBLANE)
    mod = pl.pallas_call(
        _adaln_kernel,
        grid=(N // tn,),
        in_specs=[pl.BlockSpec((rows, D), lambda j: (0, 0)),
                  pl.BlockSpec((D, tn), lambda j: (0, j)),
                  pl.BlockSpec((1, tn), lambda j: (0, j))],
        out_specs=pl.BlockSpec((rows, tn), lambda j: (0, j)),
        out_shape=jax.ShapeDtypeStruct((rows, N), jnp.float32),
        compiler_params=pltpu.CompilerParams(
            dimension_semantics=("parallel",), vmem_limit_bytes=V7X_VMEM_LIMIT_BYTES),
        name="adaln",
    )(jnp.pad(c, ((0, rows - n), (0, 0))), w, b.reshape(1, N))
    return jnp.split(mod, 6, axis=-1)


_NT_DIMS = (((1,), (1,)), ((), ()))
INT32_MIN = -2 ** 31
LOG2_E = 1.4426950408889634
_NEG_INF_KEY = -2139095041
IDX_PACK = 4 * IDX_DIM


def _ordered_key(x):
    bits = pltpu.bitcast(x, jnp.int32)
    return bits ^ ((bits >> 31) & 0x7FFFFFFF)


def _lane_tile_sum(x, width=LANE):
    out = x[:, :width]
    for c in range(1, x.shape[1] // width):
        out = out + x[:, c * width:(c + 1) * width]
    return out


def _dsa_select_bias(qi_ref, wi_ref, kidx_ref, key_ref, bias_ref, *, first, n_tiles, topk, tk):
    tq = wi_ref.shape[1]
    row = lax.broadcasted_iota(jnp.int32, (tq, 1), 0)
    lim = (((first + row) >> CHUNK_SHIFT) + 1) * CHUNK
    w = wi_ref[0]

    def score_tile(j, c):
        off = pl.multiple_of(j * tk, tk)
        kt = kidx_ref[0, pl.ds(off, tk), :]
        sc = lax.dot_general(qi_ref[0, 0], kt, _NT_DIMS,
                             preferred_element_type=jnp.float32)
        s = jnp.zeros((tq, tk), jnp.float32)
        for h in range(IDX_HEADS):
            s = s + w[:, h:h + 1] * jnp.maximum(sc[h * tq:(h + 1) * tq], 0.0)
        col = off + lax.broadcasted_iota(jnp.int32, (tq, tk), 1)
        s = jnp.where(col < lim, s + 0.0, NEG_INF)
        key_ref[:, pl.ds(off, tk)] = _ordered_key(s)
        return c

    lax.fori_loop(0, n_tiles, score_tile, 0)

    def bit_step(b, thr):
        cand = thr + lax.shift_left(jnp.int32(1), 31 - b)

        def count_tile(j, c):
            off = pl.multiple_of(j * tk, tk)
            ge = jnp.where(key_ref[:, pl.ds(off, tk)] >= cand, 1.0, 0.0)
            return c + _lane_tile_sum(ge)

        c = lax.fori_loop(0, n_tiles, count_tile, jnp.zeros((tq, LANE), jnp.float32))
        cnt = jnp.sum(c, axis=1, keepdims=True)
        return jnp.where(cnt >= topk, cand, thr)

    thr = lax.fori_loop(0, 32, bit_step, jnp.full((tq, 1), INT32_MIN, jnp.int32))
    thr = jnp.maximum(thr, _NEG_INF_KEY + 1)

    def bias_tile(j, c):
        off = pl.multiple_of(j * tk, tk)
        bias_ref[:, pl.ds(off, tk)] = jnp.where(key_ref[:, pl.ds(off, tk)] >= thr, 0.0, NEG_INF)
        return c

    lax.fori_loop(0, n_tiles, bias_tile, 0)


def _dsa_kernel(qi_ref, wi_ref, kidx_ref, q_ref, k_ref, v_ref, o_all_ref, o_ref,
                key_ref, bias_ref, qg_ref, m_ref, l_ref, acc_ref, *, pos0, topk, tk):
    del o_all_ref
    i = pl.program_id(1)
    tq = q_ref.shape[1]
    hd = acc_ref.shape[2]
    groups = acc_ref.shape[0]
    rep = q_ref.shape[2] // (groups * hd)
    first = pos0 + i * tq
    n_valid = (((first + tq - 1) >> CHUNK_SHIFT) + 1) * CHUNK
    n_tiles = (n_valid + tk - 1) // tk
    _dsa_select_bias(qi_ref, wi_ref, kidx_ref, key_ref, bias_ref, first=first, n_tiles=n_tiles, topk=topk, tk=tk)

    for g in range(groups):
        for r in range(rep):
            c0 = (g * rep + r) * hd
            qg_ref[g, r * tq:(r + 1) * tq, :] = q_ref[0, :, c0:c0 + hd]
    m_ref[...] = jnp.full(m_ref.shape, NEG_INF, jnp.float32)
    l_ref[...] = jnp.zeros(l_ref.shape, jnp.float32)
    acc_ref[...] = jnp.zeros(acc_ref.shape, jnp.float32)
    lane_reps = tk // LANE

    def att_tile(j, c):
        off = pl.multiple_of(j * tk, tk)
        b = bias_ref[:, pl.ds(off, tk)]
        bias = jnp.concatenate([b] * rep, axis=0)
        for g in range(groups):
            kt = k_ref[0, pl.ds(off, tk), g * hd:(g + 1) * hd]
            vt = v_ref[0, pl.ds(off, tk), g * hd:(g + 1) * hd]
            lg = lax.dot_general(qg_ref[g], kt, _NT_DIMS, preferred_element_type=jnp.float32) + bias
            m_old = m_ref[g]
            m_new = jnp.maximum(m_old, jnp.max(lg, axis=1, keepdims=True))
            m_safe = jnp.where(m_new == NEG_INF, 0.0, m_new)
            p = jnp.exp2(lg - jnp.tile(m_safe, (1, lane_reps)))
            alpha = jnp.exp2(m_old - m_safe)
            l_ref[g] = alpha * l_ref[g] + jnp.sum(p, axis=1, keepdims=True)
            acc_ref[g] = alpha * acc_ref[g] + jnp.dot(p.astype(jnp.bfloat16), vt,
                                                      preferred_element_type=jnp.float32)
            m_ref[g] = m_new
        return c

    lax.fori_loop(0, n_tiles, att_tile, 0)
    for g in range(groups):
        out = acc_ref[g] / l_ref[g]
        for r in range(rep):
            c0 = (g * rep + r) * hd
            o_ref[:, c0:c0 + hd] = out[r * tq:(r + 1) * tq].astype(o_ref.dtype)


def _dsa_core(q, qi3, wi, kb, vb, kidx3, pos0, topk, o_all, row0, *, tq, tk):
    B, T, qd = q.shape
    Sp = kb.shape[1]
    hd = qd // A_HEADS
    gw = qd // A_KV_HEADS
    nb = T // tq
    assert Sp % tk == 0 and hd == LANE and row0 % tq == 0
    blk0 = row0 // tq
    rows = (gw // hd) * tq
    kvw = A_KV_HEADS * hd
    return pl.pallas_call(
        functools.partial(_dsa_kernel, pos0=pos0, topk=topk, tk=tk),
        grid=(B, nb),
        in_specs=[pl.BlockSpec((1, 1, IDX_HEADS * tq, IDX_PACK), lambda b, i: (b, i, 0, 0)),
                  pl.BlockSpec((1, tq, IDX_HEADS), lambda b, i: (b, i, 0)),
                  pl.BlockSpec((1, Sp, IDX_PACK), lambda b, i: (b, 0, 0)),
                  pl.BlockSpec((1, tq, qd), lambda b, i: (b, i, 0)),
                  pl.BlockSpec((1, Sp, kvw), lambda b, i: (b, 0, 0)),
                  pl.BlockSpec((1, Sp, kvw), lambda b, i: (b, 0, 0)),
                  pl.BlockSpec(memory_space=pl.ANY)],
        out_specs=pl.BlockSpec((tq, qd), lambda b, i: (blk0 + b * nb + i, 0)),
        out_shape=jax.ShapeDtypeStruct(o_all.shape, o_all.dtype),
        input_output_aliases={6: 0},
        scratch_shapes=[pltpu.VMEM((tq, Sp), jnp.int32),
                        pltpu.VMEM((tq, Sp), jnp.float32),
                        pltpu.VMEM((A_KV_HEADS, rows, hd), jnp.bfloat16),
                        pltpu.VMEM((A_KV_HEADS, rows, LANE), jnp.float32),
                        pltpu.VMEM((A_KV_HEADS, rows, LANE), jnp.float32),
                        pltpu.VMEM((A_KV_HEADS, rows, hd), jnp.float32)],
        compiler_params=pltpu.CompilerParams(
            dimension_semantics=("parallel", "arbitrary"),
            vmem_limit_bytes=V7X_VMEM_LIMIT_BYTES),
        name="dsa_core",
    )(qi3, wi, kidx3, q, kb, vb, o_all)


def _rope_tables(pos, width, period):
    half = period // ROPE_FRACTION // 2
    inv_freq = ROPE_THETA ** (-jnp.arange(half, dtype=jnp.float32) / half)
    ang = pos.astype(jnp.float32)[:, None] * inv_freq[None, :]
    cos, sin = jnp.cos(ang), jnp.sin(ang)
    T = pos.shape[0]
    rest = period - 2 * half
    c = jnp.concatenate([cos, cos, jnp.ones((T, rest), jnp.float32)], axis=1)
    s_next = jnp.concatenate([-sin, jnp.zeros((T, period - half), jnp.float32)], axis=1)
    s_prev = jnp.concatenate([jnp.zeros((T, half), jnp.float32), sin, jnp.zeros((T, rest), jnp.float32)], axis=1)
    return jnp.stack([jnp.tile(t, (1, width // period)) for t in (c, s_next, s_prev)])


def _rope_lanes(x, tab_ref, half):
    return (x * tab_ref[0] + pltpu.roll(x, LANE - half, 1) * tab_ref[1] + pltpu.roll(x, half, 1) * tab_ref[2])


def _dsa_prep_kernel(q_ref, kv_ref, idx_ref, tq_ref, ti_ref,
                     qa_ref, kn_ref, vn_ref, kin_ref, kb_ref, vb_ref, qi3_ref, kidx3_ref, wi_ref, *, q_scale):
    tq = q_ref.shape[0]
    hd = LANE
    kvw = kv_ref.shape[1] // 2
    half_q = hd // ROPE_FRACTION // 2
    half_i = IDX_DIM // ROPE_FRACTION // 2
    left = lax.broadcasted_iota(jnp.int32, (tq, LANE), 1) < IDX_DIM
    zero = jnp.zeros((tq, LANE), jnp.float32)

    for h in range(q_ref.shape[1] // hd):
        cols = slice(h * hd, (h + 1) * hd)
        qa_ref[0, :, cols] = (_rope_lanes(q_ref[:, cols], tq_ref, half_q) * q_scale).astype(qa_ref.dtype)
    for h in range(kvw // hd):
        cols = slice(h * hd, (h + 1) * hd)
        k = _rope_lanes(kv_ref[:, cols], tq_ref, half_q)
        kn_ref[0, :, h, :] = k
        kb_ref[0, :, cols] = k.astype(kb_ref.dtype)
        vn_ref[0, :, h, :] = kv_ref[:, kvw + h * hd:kvw + (h + 1) * hd]
    vb_ref[0] = kv_ref[:, kvw:].astype(vb_ref.dtype)

    def hi_lo(x):
        hi = x.astype(jnp.bfloat16).astype(jnp.float32)
        return hi, x - hi

    for t in range(IDX_HEADS * IDX_DIM // LANE):
        hi, lo = hi_lo(_rope_lanes(idx_ref[:, t * LANE:(t + 1) * LANE], ti_ref, half_i))
        hi_sw, lo_sw = pltpu.roll(hi, IDX_DIM, 1), pltpu.roll(lo, IDX_DIM, 1)
        even = jnp.concatenate([jnp.where(left, hi, lo_sw), jnp.where(left, hi, zero)], axis=1)
        odd = jnp.concatenate([jnp.where(left, hi_sw, lo), jnp.where(left, hi_sw, zero)], axis=1)
        qi3_ref[0, 0, (2 * t) * tq:(2 * t + 1) * tq, :] = even.astype(qi3_ref.dtype)
        qi3_ref[0, 0, (2 * t + 1) * tq:(2 * t + 2) * tq, :] = odd.astype(qi3_ref.dtype)
    c0 = IDX_HEADS * IDX_DIM
    x = idx_ref[:, c0:c0 + LANE]
    r = jnp.where(left, _rope_lanes(x, ti_ref, half_i), x)
    kin_ref[0] = r[:, :IDX_DIM]
    hi, lo = hi_lo(r)
    kidx3 = jnp.concatenate([jnp.where(left, hi, pltpu.roll(hi, IDX_DIM, 1)), jnp.where(left, lo, zero)], axis=1)
    kidx3_ref[0] = kidx3.astype(kidx3_ref.dtype)
    wi_ref[0] = x[:, IDX_DIM:IDX_DIM + IDX_HEADS] * (IDX_HEADS ** -0.5 * IDX_DIM ** -0.5)


def _dsa_prep(q_all, kv_all, idx_all, row0, B, T, pos0, *, tq):
    qd, kv2 = A_HEADS * LANE, 2 * A_KV_HEADS * LANE
    kv_col = (kv_all.shape[1] - kv2) // kv2
    kvw = kv2 // 2
    nb = T // tq
    blk0 = row0 // tq
    assert row0 % tq == 0 and qd // A_HEADS == LANE
    pos = pos0 + jnp.arange(T, dtype=jnp.int32)
    tab_q = _rope_tables(pos, LANE, LANE)
    tab_i = _rope_tables(pos, LANE, IDX_DIM)
    row = lambda w: pl.BlockSpec((tq, w), lambda b, i: (blk0 + b * nb + i, 0))
    tab = pl.BlockSpec((3, tq, LANE), lambda b, i: (0, i, 0))
    out = lambda w: pl.BlockSpec((1, tq, w), lambda b, i: (b, i, 0))
    f32, bf16 = jnp.float32, jnp.bfloat16
    heads4 = pl.BlockSpec((1, tq, A_KV_HEADS, LANE), lambda b, i: (b, i, 0, 0))
    shapes = [((B, T, qd), bf16), ((B, T, A_KV_HEADS, LANE), f32), ((B, T, A_KV_HEADS, LANE), f32),
              ((B, T, IDX_DIM), f32),
              ((B, T, kvw), bf16), ((B, T, kvw), bf16), ((B, nb, IDX_HEADS * tq, IDX_PACK), bf16),
              ((B, T, IDX_PACK), bf16), ((B, T, IDX_HEADS), f32)]
    return pl.pallas_call(
        functools.partial(_dsa_prep_kernel, q_scale=LANE ** -0.5 * LOG2_E),
        grid=(B, nb),
        in_specs=[row(qd), pl.BlockSpec((tq, kv2), lambda b, i: (blk0 + b * nb + i, kv_col)),
                  row(idx_all.shape[1]), tab, tab],
        out_specs=[out(qd), heads4, heads4, out(IDX_DIM), out(kvw), out(kvw),
                   pl.BlockSpec((1, 1, IDX_HEADS * tq, IDX_PACK), lambda b, i: (b, i, 0, 0)),
                   out(IDX_PACK), out(IDX_HEADS)],
        out_shape=[jax.ShapeDtypeStruct(s, d) for s, d in shapes],
        compiler_params=pltpu.CompilerParams(
            dimension_semantics=("parallel", "parallel"), vmem_limit_bytes=V7X_VMEM_LIMIT_BYTES),
        name="dsa_prep",
    )(q_all, kv_all, idx_all, tab_q, tab_i)


def _dsa_layer(x, mod, gate, sid, streams, w_in, w_out):
    D = x.shape[1]
    hd = D // A_HEADS
    q_dim, kv_dim = A_HEADS * hd, A_KV_HEADS * hd
    n_idx = IDX_HEADS * IDX_DIM + IDX_DIM + IDX_HEADS
    w_main = _weight_parts(w_in[:, :q_dim + 2 * kv_dim], 1)
    w_idx = jnp.pad(w_in[:, q_dim + 2 * kv_dim:], ((0, 0), (0, _round_up(n_idx, LANE) - n_idx)))
    q_all = kv_all = pmatmul(x, w_main, mod=mod, sid=sid)
    idx_all = pmatmul(x, _weight_parts(w_idx, 3), tn=w_idx.shape[1], mod=mod, sid=sid)
    caches = []
    o_all = jnp.zeros((x.shape[0], q_dim), jnp.bfloat16)
    for row0, B, T, pos0, past_k, past_v, past_kidx in streams:
        o_all, k, v, ki = _dsa_stream(q_all, kv_all, idx_all, row0, B, T, pos0, past_k, past_v, past_kidx, o_all)
        caches.append((k, v, ki))
    x = pmatmul(o_all, _weight_parts(w_out, 1), res=(x, gate), sid=sid)
    return x, caches


DSA_KEY_TILE = 512
DSA_QUERY_TILE = 256


def _dsa_stream(q_all, kv_all, idx_all, row0, B, T, pos0, past_k, past_v, past_kidx, o_all):
    tq = min(DSA_QUERY_TILE, T)
    q, k, v, ki, kb, vb, qi3, kidx3, wi = _dsa_prep(q_all, kv_all, idx_all, row0, B, T, pos0, tq=tq)
    n_keys = T
    if past_k is not None:
        P = past_k.shape[1]
        n_keys = P + T
        kb = jnp.concatenate([past_k.reshape(B, P, -1).astype(jnp.bfloat16), kb], axis=1)
        vb = jnp.concatenate([past_v.reshape(B, P, -1).astype(jnp.bfloat16), vb], axis=1)
        ph, pl_ = _split_bf16(past_kidx)
        kidx3 = jnp.concatenate([jnp.concatenate([ph, ph, pl_, jnp.zeros_like(ph)], axis=-1), kidx3], axis=1)
    pad = ((0, 0), (0, _round_up(n_keys, DSA_KEY_TILE) - n_keys), (0, 0))
    kb, vb, kidx3 = (jnp.pad(a, pad) for a in (kb, vb, kidx3))
    topk = min(TOPK_MAX, n_keys // 4)
    o_all = _dsa_core(q, qi3, wi, kb, vb, kidx3, pos0, topk, o_all, row0, tq=tq, tk=DSA_KEY_TILE)
    return o_all, k, v, ki


GDN_HEAD_GROUP = 32


def _bf16_dot(a, b):
    return jnp.dot(a.astype(jnp.bfloat16), b.astype(jnp.bfloat16), preferred_element_type=jnp.float32)


def _dot3(a, b):
    ah, al = _split_bf16(a)
    bh, bl = _split_bf16(b)
    out = jnp.dot(ah, bh, preferred_element_type=jnp.float32)
    out = out + jnp.dot(ah, bl, preferred_element_type=jnp.float32)
    return out + jnp.dot(al, bh, preferred_element_type=jnp.float32)


def _conv_silu(x_ref, w_ref, xe_ref):
    C = x_ref.shape[0]
    taps = w_ref.shape[0]
    xe_ref[SUBLANE:, :] = x_ref[...]
    first = SUBLANE - (taps - 1)
    acc = xe_ref[first:first + C, :] * w_ref[0:1, :]
    for j in range(1, taps):
        acc = acc + xe_ref[first + j:first + j + C, :] * w_ref[j:j + 1, :]
    xe_ref[:SUBLANE, :] = xe_ref[C:, :]
    return acc * jax.nn.sigmoid(acc)


def _gdn_kernel(xq_ref, xk_ref, xv_ref, wq_ref, wk_ref, wv_ref, cq_ref, ck_ref, cv_ref,
                z_ref, g_ref, gt_ref, beta_ref, nw_ref, s0_ref, o_all_ref,
                o_ref, s_out_ref, s_ref, eq_ref, ek_ref, ev_ref):
    del o_all_ref
    n = pl.program_id(2)
    C = xq_ref.shape[0]
    hg = g_ref.shape[3]
    dk = s_ref.shape[1]
    dv = s_ref.shape[2]
    rep = hg // (xk_ref.shape[1] // dk)

    @pl.when(n == 0)
    def _():
        s_ref[...] = s0_ref[0]
        for e_ref, c_ref in ((eq_ref, cq_ref), (ek_ref, ck_ref), (ev_ref, cv_ref)):
            e_ref[:SUBLANE, :] = jnp.zeros((SUBLANE, e_ref.shape[1]), jnp.float32)
            e_ref[SUBLANE - c_ref.shape[1]:SUBLANE, :] = c_ref[0]

    qc = _conv_silu(xq_ref, wq_ref, eq_ref)
    kc = _conv_silu(xk_ref, wk_ref, ek_ref)
    vc = _conv_silu(xv_ref, wv_ref, ev_ref)
    q_heads, k_heads = [], []
    for i in range(hg // rep):
        qh = qc[:, i * dk:(i + 1) * dk]
        kh = kc[:, i * dk:(i + 1) * dk]
        q_heads.append(qh * (lax.rsqrt(jnp.sum(qh * qh, axis=1, keepdims=True) + EPS) * dk ** -0.5))
        k_heads.append(kh * lax.rsqrt(jnp.sum(kh * kh, axis=1, keepdims=True) + EPS))

    ri = lax.broadcasted_iota(jnp.int32, (C, C), 0)
    ci = lax.broadcasted_iota(jnp.int32, (C, C), 1)
    causal = ri >= ci
    strict = ri > ci
    eye = jnp.where(ri == ci, 1.0, 0.0)
    g = g_ref[0, 0]
    gc_all = _dot3(jnp.where(causal, 1.0, 0.0), g)
    gr_all = _dot3(gt_ref[0, 0, 0], jnp.where(ri <= ci, 1.0, 0.0))
    beta = beta_ref[0, 0]
    nw = nw_ref[...]

    heads = range(hg)
    qs = [q_heads[h // rep] for h in heads]
    ks = [k_heads[h // rep] for h in heads]
    gcs = [gc_all[:, h:h + 1] for h in heads]
    bcols = [beta[:, h:h + 1] for h in heads]
    decays = [jnp.where(causal, jnp.exp(jnp.where(causal, gcs[h] - gr_all[h:h + 1, :], 0.0)), 0.0)
              for h in heads]
    kbs = [ks[h] * bcols[h] for h in heads]
    kks = [lax.dot_general(kbs[h].astype(jnp.bfloat16), ks[h].astype(jnp.bfloat16), _NT_DIMS,
                           preferred_element_type=jnp.float32) for h in heads]
    bms = [jnp.where(strict, -(kks[h] * decays[h]), 0.0) for h in heads]
    egs = [jnp.exp(gcs[h]) for h in heads]
    rhss = [jnp.concatenate([vc[:, h * dv:(h + 1) * dv] * bcols[h], kbs[h] * egs[h]], axis=1)
            for h in heads]
    pairs = range(hg // 2)
    left = lax.broadcasted_iota(jnp.int32, (C, 2 * C), 1) < C
    zero16 = jnp.zeros((C, 2 * C), jnp.bfloat16)

    def blockdiag(part):
        return jnp.concatenate([jnp.where(left, part, zero16), jnp.where(left, zero16, part)], axis=0)

    def dot3_pairs(x_parts, y_parts):
        xh, xl = x_parts
        yh, yl = blockdiag(y_parts[0]), blockdiag(y_parts[1])
        out = jnp.dot(xh, yh, preferred_element_type=jnp.float32)
        out = out + jnp.dot(xh, yl, preferred_element_type=jnp.float32)
        return out + jnp.dot(xl, yh, preferred_element_type=jnp.float32)

    b2 = [jnp.concatenate([bms[2 * i], bms[2 * i + 1]], axis=1) for i in pairs]
    eye2 = jnp.concatenate([eye, eye], axis=1)
    p2 = [eye2 + b2[i] for i in pairs]
    b2_parts = [_split_bf16(b2[i]) for i in pairs]
    step = 2
    while step < C:
        b2 = [dot3_pairs(b2_parts[i], b2_parts[i]) for i in pairs]
        b2_parts = [_split_bf16(b2[i]) for i in pairs]
        p2 = [p2[i] + dot3_pairs(_split_bf16(p2[i]), b2_parts[i]) for i in pairs]
        step *= 2
    ps = [p2[h // 2][:, (h % 2) * C:(h % 2 + 1) * C] for h in heads]
    ws = [_dot3(ps[h], rhss[h]) for h in heads]
    qks = [lax.dot_general(qs[h].astype(jnp.bfloat16), ks[h].astype(jnp.bfloat16), _NT_DIMS,
                           preferred_element_type=jnp.float32) * decays[h] for h in heads]
    g_lasts = [gcs[h][C - 1:C, :] for h in heads]
    ss = [s_ref[h] for h in heads]
    us = [ws[h][:, :dv] - _bf16_dot(ws[h][:, dv:], ss[h]) for h in heads]
    os_ = [_bf16_dot(qs[h] * egs[h], ss[h]) + _bf16_dot(qks[h], us[h]) for h in heads]
    for h in heads:
        ke = ks[h] * jnp.exp(g_lasts[h] - gcs[h])
        s_ref[h] = ss[h] * jnp.exp(g_lasts[h]) + _bf16_dot(ke.T, us[h])
    for h in heads:
        o = os_[h]
        o = o * lax.rsqrt(jnp.mean(o * o, axis=1, keepdims=True) + EPS) * nw
        z = z_ref[:, h * dv:(h + 1) * dv]
        o_ref[:, h * dv:(h + 1) * dv] = (o * (z * jax.nn.sigmoid(z))).astype(o_ref.dtype)

    @pl.when(n == pl.num_programs(2) - 1)
    def _():
        s_out_ref[0] = s_ref[...]


def _gdn_core(qkv, z, row0, conv_w, conv_state, g, beta, norm_w, s0, o_all):
    B, T, VH = g.shape
    dk, dv = s0.shape[2], s0.shape[3]
    vd = VH * dv
    qd = (conv_w.shape[1] - vd) // 2
    C = min(CHUNK, T)
    N = T // C
    assert row0 % C == 0
    blk0 = row0 // C
    hg = min(GDN_HEAD_GROUP, VH)
    ng = VH // hg
    qw = qd // ng
    vw = hg * dv
    assert qd % qw == 0 and (2 * qd) % vw == 0
    k0, v0 = qd // qw, (2 * qd) // vw
    z0 = (z.shape[1] - vd) // vw
    taps = conv_w.shape[0]

    def grouped(a):
        return a.reshape(B, T, ng, hg).transpose(0, 2, 1, 3)

    gg, bg = grouped(g), grouped(beta)
    gt = gg.reshape(B, ng, N, C, hg).transpose(0, 1, 2, 4, 3)
    small = pl.BlockSpec((1, 1, C, hg), lambda b, j, n: (b, j, n, 0))
    o, s = pl.pallas_call(
        _gdn_kernel,
        grid=(B, ng, N),
        in_specs=[pl.BlockSpec((C, qw), lambda b, j, n: (blk0 + b * N + n, j)),
                  pl.BlockSpec((C, qw), lambda b, j, n: (blk0 + b * N + n, k0 + j)),
                  pl.BlockSpec((C, vw), lambda b, j, n: (blk0 + b * N + n, v0 + j)),
                  pl.BlockSpec((taps, qw), lambda b, j, n: (0, j)),
                  pl.BlockSpec((taps, qw), lambda b, j, n: (0, k0 + j)),
                  pl.BlockSpec((taps, vw), lambda b, j, n: (0, v0 + j)),
                  pl.BlockSpec((1, taps - 1, qw), lambda b, j, n: (b, 0, j)),
                  pl.BlockSpec((1, taps - 1, qw), lambda b, j, n: (b, 0, k0 + j)),
                  pl.BlockSpec((1, taps - 1, vw), lambda b, j, n: (b, 0, v0 + j)),
                  pl.BlockSpec((C, vw), lambda b, j, n: (blk0 + b * N + n, z0 + j)),
                  small,
                  pl.BlockSpec((1, 1, 1, hg, C), lambda b, j, n: (b, j, n, 0, 0)),
                  small,
                  pl.BlockSpec((1, dv), lambda b, j, n: (0, 0)),
                  pl.BlockSpec((1, hg, dk, dv), lambda b, j, n: (b, j, 0, 0)),
                  pl.BlockSpec(memory_space=pl.ANY)],
        out_specs=[pl.BlockSpec((C, hg * dv), lambda b, j, n: (blk0 + b * N + n, j)),
                   pl.BlockSpec((1, hg, dk, dv), lambda b, j, n: (b, j, 0, 0))],
        out_shape=[jax.ShapeDtypeStruct(o_all.shape, o_all.dtype),
                   jax.ShapeDtypeStruct(s0.shape, jnp.float32)],
        input_output_aliases={15: 0},
        scratch_shapes=[pltpu.VMEM((hg, dk, dv), jnp.float32),
                        pltpu.VMEM((SUBLANE + C, qw), jnp.float32),
                        pltpu.VMEM((SUBLANE + C, qw), jnp.float32),
                        pltpu.VMEM((SUBLANE + C, vw), jnp.float32)],
        compiler_params=pltpu.CompilerParams(
            dimension_semantics=("parallel", "parallel", "arbitrary"),
            vmem_limit_bytes=V7X_VMEM_LIMIT_BYTES),
        name="gdn_core",
    )(qkv, qkv, qkv, conv_w, conv_w, conv_w, conv_state, conv_state, conv_state,
      z, gg, gt, bg, norm_w.reshape(1, dv), s0, o_all)
    return o, s


def _gdn_layer(x, mod, gate, sid, streams, w_in, conv_w, a_log, dt_bias, norm_w, w_out):
    qk_dim, v_dim = GDN_QK_HEADS * GDN_DK, GDN_V_HEADS * GDN_DV
    conv_dim = 2 * qk_dim + v_dim
    n_gate = 2 * GDN_V_HEADS
    w_main = _weight_parts(w_in[:, :conv_dim + v_dim], 1)
    w_gate = jnp.pad(w_in[:, conv_dim + v_dim:], ((0, 0), (0, _round_up(n_gate, LANE) - n_gate)))
    qkv_all = z_all = pmatmul(x, w_main, mod=mod, sid=sid)
    gates_all = pmatmul(x, _weight_parts(w_gate, 1), tn=w_gate.shape[1], mod=mod, sid=sid)
    states = []
    o_all = jnp.zeros((x.shape[0], v_dim), jnp.bfloat16)
    for row0, B, T, conv_state, ssm_state in streams:
        rows = slice(row0, row0 + B * T)
        gates = gates_all[rows].reshape(B, T, -1)
        beta_raw, a_raw = gates[..., :GDN_V_HEADS], gates[..., GDN_V_HEADS:n_gate]
        if conv_state is None:
            conv_state = jnp.zeros((B, CONV_W - 1, conv_dim), x.dtype)
        if ssm_state is None:
            ssm_state = jnp.zeros((B, GDN_V_HEADS, GDN_DK, GDN_DV), jnp.float32)
        tail = jnp.stack([qkv_all[row0 + (b + 1) * T - (CONV_W - 1):row0 + (b + 1) * T, :conv_dim]
                          for b in range(B)])
        new_conv = jnp.concatenate([conv_state, tail], axis=1)[:, -(CONV_W - 1):]
        beta = jax.nn.sigmoid(beta_raw)
        g = -jnp.exp(a_log) * jax.nn.softplus(a_raw + dt_bias)
        o_all, S = _gdn_core(qkv_all, z_all, row0, conv_w, conv_state, g, beta, norm_w, ssm_state, o_all)
        states.append((new_conv, S))
    x = pmatmul(o_all, _weight_parts(w_out, 1), res=(x, gate), sid=sid)
    return x, states


NEG_INF = float("-inf")
_PEER_CAND_ROWS = tuple((i, PEER_TOPK // (i + 1)) for i in range(PEER_TOPK // 2))


def _split_bf16(x):
    hi = x.astype(jnp.bfloat16)
    lo = (x - hi.astype(jnp.float32)).astype(jnp.bfloat16)
    return hi, lo


def _dot3_nt(a, b):
    dn = (((1,), (1,)), ((), ()))
    ah, al = _split_bf16(a)
    bh, bl = _split_bf16(b)
    out = lax.dot_general(ah, bh, dn, preferred_element_type=jnp.float32)
    out = out + lax.dot_general(ah, bl, dn, preferred_element_type=jnp.float32)
    return out + lax.dot_general(al, bh, dn, preferred_element_type=jnp.float32)


def _top_rows_desc(s, n, with_rank=False):
    rows = []
    cur = s
    rank = jnp.full(s.shape, float(n), jnp.float32)
    for i in range(n):
        m = jnp.max(cur, axis=0, keepdims=True)
        rows.append(m)
        hit = cur == m
        if with_rank:
            rank = jnp.where(hit, float(i), rank)
        cur = jnp.where(hit, NEG_INF, cur)
    return (rows, rank) if with_rank else rows


def _stack_rows(rows, lanes):
    n = len(rows)
    rid = lax.broadcasted_iota(jnp.int32, (n, lanes), 0)
    out = jnp.zeros((n, lanes), jnp.float32)
    for i, r in enumerate(rows):
        out = jnp.where(rid == i, r, out)
    return out


def _peer_select_kernel(q_ref, keys_ref, rk2_ref, e2_ref, cnt_ref, e1_ref):
    tm = q_ref.shape[0]
    k = PEER_TOPK
    s1 = _dot3_nt(keys_ref[0], q_ref[:, :PEER_HALF])
    s2 = _dot3_nt(keys_ref[1], q_ref[:, PEER_HALF:])
    r1 = _top_rows_desc(s1, k)
    r2, rank2 = _top_rows_desc(s2, k, with_rank=True)
    v1 = _stack_rows(r1, tm)
    v2 = _stack_rows(r2, tm)
    v2h = v2[:k // 2]
    rid = lax.broadcasted_iota(jnp.int32, (k // 2, tm), 0)
    pieces = [r1[0] + v2]
    for i, n in _PEER_CAND_ROWS[1:]:
        pieces.append(jnp.where(rid < n, r1[i] + v2h, NEG_INF))
    pieces.append(v1[k // 2:] + r2[0])
    cand = jnp.concatenate(pieces, axis=0)
    tau = _top_rows_desc(cand, k)[-1]
    top = r1[0] + r2[0]
    z = jnp.sum(jnp.where(cand >= tau, jnp.exp(cand - top), 0.0), axis=0, keepdims=True)
    cnt = jnp.zeros(s1.shape, jnp.float32)
    for j in range(k):
        cnt = cnt + jnp.where(s1 + r2[j] >= tau, 1.0, 0.0)
    rk2_ref[0] = pltpu.bitcast(rank2.astype(jnp.bfloat16), rk2_ref.dtype)
    e2_ref[0] = pltpu.bitcast((jnp.exp(s2 - r2[0]) / z).astype(jnp.bfloat16), e2_ref.dtype)
    cnt_ref[0] = cnt
    e1_ref[0] = jnp.exp(s1 - r1[0])


def _peer_select(q, keys, *, tm):
    M = q.shape[0]
    nk = keys.shape[1]
    ospec = pl.BlockSpec((1, nk, tm), lambda i, h: (h, 0, i))
    pspec = pl.BlockSpec((1, nk // 2, tm), lambda i, h: (h, 0, i))
    return pl.pallas_call(
        _peer_select_kernel,
        grid=(M // tm, PEER_HEADS),
        in_specs=[pl.BlockSpec((tm, PEER_KEY_DIM), lambda i, h: (i, h)),
                  pl.BlockSpec(keys.shape, lambda i, h: (0, 0, 0))],
        out_specs=[pspec, pspec, ospec, ospec],
        out_shape=[jax.ShapeDtypeStruct((PEER_HEADS, nk // 2, M), jnp.int32)] * 2
        + [jax.ShapeDtypeStruct((PEER_HEADS, nk, M), jnp.float32)] * 2,
        compiler_params=pltpu.CompilerParams(
            dimension_semantics=("parallel", "arbitrary"),
            vmem_limit_bytes=V7X_VMEM_LIMIT_BYTES),
        name="peer_select",
    )(q, keys)


BF16_SUBLANES = 16
PEER_SUB_SPLIT = (1, 1)


def _peer_main_kernel(sid_ref, x_ref, gain_ref, shift_ref, scale_ref, gate_ref, fnorm_ref,
                      u_ref, vT_ref, rk2_ref, e2_ref, cnt_ref, e1_ref, *refs, final_norm):
    out_refs, (hT_ref, coef_ref, acc_ref) = refs[:-3], refs[-3:]
    e = pl.program_id(1)
    te, tm = coef_ref.shape
    nk = 2 * rk2_ref.shape[1]
    rows = BF16_SUBLANES
    blk0 = pl.program_id(0) * (tm // SEQ_BLOCK)

    @pl.when(e == 0)
    def _():
        acc_ref[...] = jnp.zeros_like(acc_ref)
        per_lane_tile = LANE // SEQ_BLOCK
        for t in range(tm // LANE):
            hs = []
            for r in range(t * per_lane_tile, (t + 1) * per_lane_tile):
                rs = slice(r * SEQ_BLOCK, (r + 1) * SEQ_BLOCK)
                hs.append(_modulated_rows(x_ref[rs, :], gain_ref[...], shift_ref, scale_ref, sid_ref[blk0 + r]))
            hT_ref[:, t * LANE:(t + 1) * LANE] = jnp.concatenate(hs, axis=0).T.astype(jnp.bfloat16)

    def expert_acts(span):
        off, size = span
        return jnp.dot(u_ref[off:off + size, :], hT_ref[...], preferred_element_type=jnp.float32)

    def weigh(span, act):
        off, size = span
        for al in range(size // nk):
            a = off // nk + al
            for lg in range(tm // LANE):
                lanes = slice(lg * LANE, (lg + 1) * LANE)
                cnts = [jnp.broadcast_to(cnt_ref[h, a:a + 1, lanes], (rows, LANE)).astype(jnp.bfloat16)
                        for h in range(PEER_HEADS)]
                e1s = [jnp.broadcast_to(e1_ref[h, a:a + 1, lanes], (rows, LANE)).astype(jnp.bfloat16)
                       for h in range(PEER_HEADS)]
                for r in range(nk // rows):
                    words = slice(r * rows // 2, (r + 1) * rows // 2)
                    w = None
                    for h in range(PEER_HEADS):
                        e2 = pltpu.bitcast(e2_ref[h, words, lanes], jnp.bfloat16)
                        rk2 = pltpu.bitcast(rk2_ref[h, words, lanes], jnp.bfloat16)
                        t = jnp.where(rk2 < cnts[h], e2, jnp.zeros_like(e2)) * e1s[h]
                        w = t if w is None else w + t
                    x = act[al * nk + r * rows:al * nk + (r + 1) * rows, lanes]
                    g = 0.5 * x * (1.0 + lax.erf(x * (2.0 ** -0.5)))
                    coef_ref[off + al * nk + r * rows:off + al * nk + (r + 1) * rows, lanes] = (
                        w * g.astype(jnp.bfloat16))

    def accumulate(span):
        off, size = span
        acc_ref[...] += lax.dot_general(vT_ref[off:off + size, :], coef_ref[off:off + size, :],
                                        (((0,), (0,)), ((), ())), preferred_element_type=jnp.float32)

    spans, off = [], 0
    for frac in PEER_SUB_SPLIT:
        spans.append((off, te * frac // sum(PEER_SUB_SPLIT)))
        off += spans[-1][1]
    act = expert_acts(spans[0])
    for i, span in enumerate(spans):
        nxt = expert_acts(spans[i + 1]) if i + 1 < len(spans) else None
        weigh(span, act)
        accumulate(span)
        act = nxt

    @pl.when(e == pl.num_programs(1) - 1)
    def _():
        out = acc_ref[...].T
        for r in range(tm // SEQ_BLOCK):
            rs = slice(r * SEQ_BLOCK, (r + 1) * SEQ_BLOCK)
            y = x_ref[rs, :] + gate_ref[pl.ds(sid_ref[blk0 + r], 1), :] * out[rs, :]
            out_refs[0][rs, :] = y
            if final_norm:
                out_refs[1][rs, :] = y * lax.rsqrt(jnp.mean(y * y, axis=-1, keepdims=True) + EPS) * fnorm_ref[...]


def _peer_main(x, mod, gate, fnorm, sid, u, vT, rk2, e2, cnt, e1, *, tm, te, final_norm):
    M, D = x.shape
    E = u.shape[0]
    nk = cnt.shape[1]
    const = lambda a: pl.BlockSpec(a.shape, lambda i, e, s: (0,) * a.ndim)
    col_spec = pl.BlockSpec((PEER_HEADS, nk // 2, tm), lambda i, e, s: (0, 0, i))
    row_spec = pl.BlockSpec((PEER_HEADS, te // nk, tm), lambda i, e, s: (0, e, i))
    tok_spec = pl.BlockSpec((tm, D), lambda i, e, s: (i, 0))
    n_out = 2 if final_norm else 1
    return pl.pallas_call(
        functools.partial(_peer_main_kernel, final_norm=final_norm),
        grid_spec=pltpu.PrefetchScalarGridSpec(
            num_scalar_prefetch=1,
            grid=(M // tm, E // te),
            in_specs=[tok_spec, const(mod[0]), const(mod[1]), const(mod[2]), const(gate), const(fnorm),
                      pl.BlockSpec((te, D), lambda i, e, s: (e, 0)),
                      pl.BlockSpec((te, D), lambda i, e, s: (e, 0)),
                      col_spec, col_spec, row_spec, row_spec],
            out_specs=[tok_spec] * n_out,
            scratch_shapes=[pltpu.VMEM((D, tm), jnp.bfloat16),
                            pltpu.VMEM((te, tm), jnp.bfloat16),
                            pltpu.VMEM((D, tm), jnp.float32)]),
        out_shape=[jax.ShapeDtypeStruct((M, D), jnp.float32)] * n_out,
        compiler_params=pltpu.CompilerParams(
            dimension_semantics=("parallel", "arbitrary"),
            vmem_limit_bytes=V7X_VMEM_LIMIT_PEER_BYTES),
        name="peer_main",
    )(sid, x, *mod, gate, fnorm, u, vT, rk2, e2, cnt, e1)


def _peer(x, mod, gate, fnorm, sid, w_query, sub_keys, expert_u, expert_v, *, final_norm,
          tm_sel=256, tm=512, te=1024):
    M = x.shape[0]
    q = pmatmul(x, _weight_parts(w_query, 1), mod=mod, sid=sid)
    rk2, e2, cnt, e1 = _peer_select(q, sub_keys, tm=min(tm_sel, M))
    u = expert_u.astype(jnp.bfloat16)
    vT = expert_v.astype(jnp.bfloat16)
    return _peer_main(x, mod, gate, fnorm, sid, u, vT, rk2, e2, cnt, e1, tm=min(tm, M), te=te,
                      final_norm=final_norm)


def kernel(x_prompt, x_sample, c_prompt, c_sample, cache_k_l0, cache_v_l0, cache_kidx_l0, state_conv_l1, state_ssm_l1, norm1_l0, norm2_l0, ada_w_l0, ada_b_l0, attn_in_l0, attn_out_l0, peer_query_l0, peer_keys_l0, peer_u_l0, peer_v_l0, norm1_l1, norm2_l1, ada_w_l1, ada_b_l1, gdn_in_l1, gdn_conv_l1, gdn_a_log_l1, gdn_dt_bias_l1, gdn_norm_l1, gdn_out_l1, peer_query_l1, peer_keys_l1, peer_u_l1, peer_v_l1, final_norm):
    past_len = cache_k_l0.shape[1]
    norm1 = (norm1_l0, norm1_l1)
    norm2 = (norm2_l0, norm2_l1)
    ada_w = (ada_w_l0, ada_w_l1)
    ada_b = (ada_b_l0, ada_b_l1)
    peer_query = (peer_query_l0, peer_query_l1)
    peer_keys = (peer_keys_l0, peer_keys_l1)
    peer_u = (peer_u_l0, peer_u_l1)
    peer_v = (peer_v_l0, peer_v_l1)
    Bp, Tp, D = x_prompt.shape
    Bs, Ts, _ = x_sample.shape
    n_p, n_s = Bp * Tp, Bs * Ts
    x = jnp.concatenate([x_prompt.reshape(n_p, D), x_sample.reshape(n_s, D)], axis=0)
    sid = jnp.concatenate([jnp.repeat(jnp.arange(Bp, dtype=jnp.int32), Tp // SEQ_BLOCK),
                           Bp + jnp.repeat(jnp.arange(Bs, dtype=jnp.int32), Ts // SEQ_BLOCK)])
    c_all = jnp.concatenate([c_prompt, c_sample], axis=0)
    fnorm = final_norm.reshape(1, D)
    for i in range(2):
        mods = _adaln(c_all, ada_w[i], ada_b[i])
        mod1 = (norm1[i].reshape(1, D), mods[0], mods[1])
        if i == 0:
            x, ((nkp, nvp, nkip), (nks, nvs, nkis)) = _dsa_layer(
                x, mod1, mods[2], sid,
                [(0, Bp, Tp, 0, None, None, None), (n_p, Bs, Ts, past_len, cache_k_l0, cache_v_l0, cache_kidx_l0)],
                attn_in_l0, attn_out_l0)
        else:
            x, ((ncp, nsp), (ncs, nss)) = _gdn_layer(
                x, mod1, mods[2], sid, [(0, Bp, Tp, None, None), (n_p, Bs, Ts, state_conv_l1, state_ssm_l1)],
                gdn_in_l1, gdn_conv_l1, gdn_a_log_l1, gdn_dt_bias_l1, gdn_norm_l1, gdn_out_l1)
        mod2 = (norm2[i].reshape(1, D), mods[3], mods[4])
        outs = _peer(x, mod2, mods[5], fnorm, sid, peer_query[i], peer_keys[i], peer_u[i], peer_v[i],
                     final_norm=(i == 1))
        x = outs[0]
    y = outs[1]
    y_prompt = y[:n_p].reshape(Bp, Tp, D)
    y_sample = y[n_p:].reshape(Bs, Ts, D)
    return (y_prompt, y_sample, nkp, nvp, nkip, nks, nvs, nkis, ncp, nsp, ncs, nss)
```

```python
import functools

import jax
import jax.numpy as jnp
from jax import lax
from jax.experimental import pallas as pl
from jax.experimental.pallas import tpu as pltpu

CHUNK = 64
CHUNK_SHIFT = CHUNK.bit_length() - 1
EPS = 1e-6
ROPE_THETA = 500000.0
ROPE_FRACTION = 4
A_HEADS = 16
A_KV_HEADS = 4
IDX_HEADS = 8
IDX_DIM = 64
TOPK_MAX = 256
Q_BLOCK = 128
GDN_QK_HEADS = 16
GDN_V_HEADS = 32
GDN_DK = 128
GDN_DV = 128
CONV_W = 4
PEER_HEADS = 8
PEER_KEY_DIM = 256
PEER_HALF = PEER_KEY_DIM // 2
PEER_TOPK = 16

V7X_VMEM_LIMIT_BYTES = 48 * 1024 * 1024
V7X_VMEM_LIMIT_PEER_BYTES = 56 * 1024 * 1024
LANE = 128
SUBLANE = 8


def _round_up(n, m):
    return (n + m - 1) // m * m


SEQ_BLOCK = CHUNK


def _modulated_rows(x, gain, shift_ref, scale_ref, sid):
    y = x * lax.rsqrt(jnp.mean(x * x, axis=-1, keepdims=True) + EPS)
    return y * gain * (1.0 + scale_ref[pl.ds(sid, 1), :]) + shift_ref[pl.ds(sid, 1), :]


def _matmul_kernel(sid_ref, x_ref, *refs, passes, has_mod, has_res):
    refs = list(refs)
    mod_refs = [refs.pop(0) for _ in range(3)] if has_mod else None
    nparts = 1 + passes // 2
    w_refs = [refs.pop(0) for _ in range(nparts)]
    res_refs = [refs.pop(0) for _ in range(2)] if has_res else None
    o_ref, x_parts = refs[0], refs[1:]
    tm = x_ref.shape[0]
    blocks = range(tm // SEQ_BLOCK) if (has_mod or has_res) else ()
    blk0 = pl.program_id(0) * (tm // SEQ_BLOCK)

    @pl.when(pl.program_id(1) == 0)
    def _():
        def put(rows, x):
            hi = x.astype(jnp.bfloat16)
            x_parts[0][rows, :] = hi
            if passes == 3:
                x_parts[1][rows, :] = (x - hi.astype(jnp.float32)).astype(jnp.bfloat16)

        if has_mod:
            gain_ref, shift_ref, scale_ref = mod_refs
            for r in blocks:
                rows = slice(r * SEQ_BLOCK, (r + 1) * SEQ_BLOCK)
                put(rows, _modulated_rows(x_ref[rows, :], gain_ref[...], shift_ref, scale_ref, sid_ref[blk0 + r]))
        else:
            put(slice(None), x_ref[...].astype(jnp.float32))

    acc = jnp.dot(x_parts[0][...], w_refs[0][...], preferred_element_type=jnp.float32)
    if passes == 3:
        acc = acc + jnp.dot(x_parts[0][...], w_refs[1][...], preferred_element_type=jnp.float32)
        acc = acc + jnp.dot(x_parts[1][...], w_refs[0][...], preferred_element_type=jnp.float32)
    if has_res:
        res_ref, gate_ref = res_refs
        for r in blocks:
            rows = slice(r * SEQ_BLOCK, (r + 1) * SEQ_BLOCK)
            gate = gate_ref[pl.ds(sid_ref[blk0 + r], 1), :]
            o_ref[rows, :] = res_ref[rows, :] + gate * acc[rows, :]
    else:
        o_ref[...] = acc.astype(o_ref.dtype)


def _weight_parts(w, passes):
    hi = w.astype(jnp.bfloat16)
    if passes == 1:
        return (hi,)
    return (hi, (w - hi.astype(jnp.float32)).astype(jnp.bfloat16))


def pmatmul(x, w_parts, *, col0=0, ncols=None, tm=512, tn=1024, out_dtype=jnp.float32,
            mod=None, res=None, sid=None):
    passes = 1 if len(w_parts) == 1 else 3
    M, K = x.shape
    n_total = w_parts[0].shape[1]
    ncols = n_total - col0 if ncols is None else ncols
    tm = min(tm, _round_up(M, 2 * SUBLANE))
    tn = min(tn, ncols)
    assert ncols % tn == 0 and col0 % tn == 0 and (tn % LANE == 0 or tn == n_total)
    Mp = _round_up(M, tm)
    if mod is not None or res is not None:
        assert Mp == M and tm % SEQ_BLOCK == 0 and sid is not None
    else:
        sid = jnp.zeros((1,), jnp.int32)
    if Mp != M:
        x = jnp.pad(x, ((0, Mp - M), (0, 0)))
    c0 = col0 // tn
    in_specs = [pl.BlockSpec((tm, K), lambda i, j, s: (i, 0))]
    args = [x]
    if mod is not None:
        in_specs += [pl.BlockSpec(a.shape, lambda i, j, s: (0, 0)) for a in mod]
        args += list(mod)
    in_specs += [pl.BlockSpec((K, tn), lambda i, j, s: (0, c0 + j))] * len(w_parts)
    args += list(w_parts)
    if res is not None:
        in_specs += [pl.BlockSpec((tm, tn), lambda i, j, s: (i, j)),
                     pl.BlockSpec((res[1].shape[0], tn), lambda i, j, s: (0, j))]
        args += list(res)
        out_dtype = jnp.float32
    out = pl.pallas_call(
        functools.partial(_matmul_kernel, passes=passes, has_mod=mod is not None, has_res=res is not None),
        grid_spec=pltpu.PrefetchScalarGridSpec(
            num_scalar_prefetch=1,
            grid=(Mp // tm, ncols // tn),
            in_specs=in_specs,
            out_specs=pl.BlockSpec((tm, tn), lambda i, j, s: (i, j)),
            scratch_shapes=[pltpu.VMEM((tm, K), jnp.bfloat16)] * (1 + passes // 2)),
        out_shape=jax.ShapeDtypeStruct((Mp, ncols), out_dtype),
        compiler_params=pltpu.CompilerParams(
            dimension_semantics=("parallel", "arbitrary"),
            vmem_limit_bytes=V7X_VMEM_LIMIT_BYTES),
        name="matmul",
    )(sid, *args)
    return out[:M]


def _adaln_kernel(c_ref, w_ref, b_ref, o_ref):
    c = c_ref[...]
    o_ref[...] = _dot3(c * jax.nn.sigmoid(c), w_ref[...]) + b_ref[...]


def _adaln(c, w, b, *, tn=512):
    n, D = c.shape
    N = w.shape[1]
    rows = _round_up(n, SUBLANE)
    mod = pl.pallas_call(
        _adaln_kernel,
        grid=(N // tn,),
        in_specs=[pl.BlockSpec((rows, D), lambda j: (0, 0)),
                  pl.BlockSpec((D, tn), lambda j: (0, j)),
                  pl.BlockSpec((1, tn), lambda j: (0, j))],
        out_specs=pl.BlockSpec((rows, tn), lambda j: (0, j)),
        out_shape=jax.ShapeDtypeStruct((rows, N), jnp.float32),
        compiler_params=pltpu.CompilerParams(
            dimension_semantics=("parallel",), vmem_limit_bytes=V7X_VMEM_LIMIT_BYTES),
        name="adaln",
    )(jnp.pad(c, ((0, rows - n), (0, 0))), w, b.reshape(1, N))
    return jnp.split(mod, 6, axis=-1)


_NT_DIMS = (((1,), (1,)), ((), ()))
INT32_MIN = -2 ** 31
LOG2_E = 1.4426950408889634
_NEG_INF_KEY = -2139095041
IDX_PACK = 4 * IDX_DIM


def _ordered_key(x):
    bits = pltpu.bitcast(x, jnp.int32)
    return bits ^ ((bits >> 31) & 0x7FFFFFFF)


def _lane_tile_sum(x, width=LANE):
    out = x[:, :width]
    for c in range(1, x.shape[1] // width):
        out = out + x[:, c * width:(c + 1) * width]
    return out


def _dsa_select_bias(qi_ref, wi_ref, kidx_ref, key_ref, bias_ref, *, first, n_tiles, topk, tk):
    tq = wi_ref.shape[1]
    row = lax.broadcasted_iota(jnp.int32, (tq, 1), 0)
    lim = (((first + row) >> CHUNK_SHIFT) + 1) * CHUNK
    w = wi_ref[0]

    def score_tile(j, c):
        off = pl.multiple_of(j * tk, tk)
        kt = kidx_ref[0, pl.ds(off, tk), :]
        sc = lax.dot_general(qi_ref[0, 0], kt, _NT_DIMS,
                             preferred_element_type=jnp.float32)
        s = jnp.zeros((tq, tk), jnp.float32)
        for h in range(IDX_HEADS):
            s = s + w[:, h:h + 1] * jnp.maximum(sc[h * tq:(h + 1) * tq], 0.0)
        col = off + lax.broadcasted_iota(jnp.int32, (tq, tk), 1)
        s = jnp.where(col < lim, s + 0.0, NEG_INF)
        key_ref[:, pl.ds(off, tk)] = _ordered_key(s)
        return c

    lax.fori_loop(0, n_tiles, score_tile, 0)

    def bit_step(b, thr):
        cand = thr + lax.shift_left(jnp.int32(1), 31 - b)

        def count_tile(j, c):
            off = pl.multiple_of(j * tk, tk)
            ge = jnp.where(key_ref[:, pl.ds(off, tk)] >= cand, 1.0, 0.0)
            return c + _lane_tile_sum(ge)

        c = lax.fori_loop(0, n_tiles, count_tile, jnp.zeros((tq, LANE), jnp.float32))
        cnt = jnp.sum(c, axis=1, keepdims=True)
        return jnp.where(cnt >= topk, cand, thr)

    thr = lax.fori_loop(0, 32, bit_step, jnp.full((tq, 1), INT32_MIN, jnp.int32))
    thr = jnp.maximum(thr, _NEG_INF_KEY + 1)

    def bias_tile(j, c):
        off = pl.multiple_of(j * tk, tk)
        bias_ref[:, pl.ds(off, tk)] = jnp.where(key_ref[:, pl.ds(off, tk)] >= thr, 0.0, NEG_INF)
        return c

    lax.fori_loop(0, n_tiles, bias_tile, 0)


def _dsa_kernel(qi_ref, wi_ref, kidx_ref, q_ref, k_ref, v_ref, o_all_ref, o_ref,
                key_ref, bias_ref, qg_ref, m_ref, l_ref, acc_ref, *, pos0, topk, tk):
    del o_all_ref
    i = pl.program_id(1)
    tq = q_ref.shape[1]
    hd = acc_ref.shape[2]
    groups = acc_ref.shape[0]
    rep = q_ref.shape[2] // (groups * hd)
    first = pos0 + i * tq
    n_valid = (((first + tq - 1) >> CHUNK_SHIFT) + 1) * CHUNK
    n_tiles = (n_valid + tk - 1) // tk
    _dsa_select_bias(qi_ref, wi_ref, kidx_ref, key_ref, bias_ref, first=first, n_tiles=n_tiles, topk=topk, tk=tk)

    for g in range(groups):
        for r in range(rep):
            c0 = (g * rep + r) * hd
            qg_ref[g, r * tq:(r + 1) * tq, :] = q_ref[0, :, c0:c0 + hd]
    m_ref[...] = jnp.full(m_ref.shape, NEG_INF, jnp.float32)
    l_ref[...] = jnp.zeros(l_ref.shape, jnp.float32)
    acc_ref[...] = jnp.zeros(acc_ref.shape, jnp.float32)
    lane_reps = tk // LANE

    def att_tile(j, c):
        off = pl.multiple_of(j * tk, tk)
        b = bias_ref[:, pl.ds(off, tk)]
        bias = jnp.concatenate([b] * rep, axis=0)
        for g in range(groups):
            kt = k_ref[0, pl.ds(off, tk), g * hd:(g + 1) * hd]
            vt = v_ref[0, pl.ds(off, tk), g * hd:(g + 1) * hd]
            lg = lax.dot_general(qg_ref[g], kt, _NT_DIMS, preferred_element_type=jnp.float32) + bias
            m_old = m_ref[g]
            m_new = jnp.maximum(m_old, jnp.max(lg, axis=1, keepdims=True))
            m_safe = jnp.where(m_new == NEG_INF, 0.0, m_new)
            p = jnp.exp2(lg - jnp.tile(m_safe, (1, lane_reps)))
            alpha = jnp.exp2(m_old - m_safe)
            l_ref[g] = alpha * l_ref[g] + jnp.sum(p, axis=1, keepdims=True)
            acc_ref[g] = alpha * acc_ref[g] + jnp.dot(p.astype(jnp.bfloat16), vt,
                                                      preferred_element_type=jnp.float32)
            m_ref[g] = m_new
        return c

    lax.fori_loop(0, n_tiles, att_tile, 0)
    for g in range(groups):
        out = acc_ref[g] / l_ref[g]
        for r in range(rep):
            c0 = (g * rep + r) * hd
            o_ref[:, c0:c0 + hd] = out[r * tq:(r + 1) * tq].astype(o_ref.dtype)


def _dsa_core(q, qi3, wi, kb, vb, kidx3, pos0, topk, o_all, row0, *, tq, tk):
    B, T, qd = q.shape
    Sp = kb.shape[1]
    hd = qd // A_HEADS
    gw = qd // A_KV_HEADS
    nb = T // tq
    assert Sp % tk == 0 and hd == LANE and row0 % tq == 0
    blk0 = row0 // tq
    rows = (gw // hd) * tq
    kvw = A_KV_HEADS * hd
    return pl.pallas_call(
        functools.partial(_dsa_kernel, pos0=pos0, topk=topk, tk=tk),
        grid=(B, nb),
        in_specs=[pl.BlockSpec((1, 1, IDX_HEADS * tq, IDX_PACK), lambda b, i: (b, i, 0, 0)),
                  pl.BlockSpec((1, tq, IDX_HEADS), lambda b, i: (b, i, 0)),
                  pl.BlockSpec((1, Sp, IDX_PACK), lambda b, i: (b, 0, 0)),
                  pl.BlockSpec((1, tq, qd), lambda b, i: (b, i, 0)),
                  pl.BlockSpec((1, Sp, kvw), lambda b, i: (b, 0, 0)),
                  pl.BlockSpec((1, Sp, kvw), lambda b, i: (b, 0, 0)),
                  pl.BlockSpec(memory_space=pl.ANY)],
        out_specs=pl.BlockSpec((tq, qd), lambda b, i: (blk0 + b * nb + i, 0)),
        out_shape=jax.ShapeDtypeStruct(o_all.shape, o_all.dtype),
        input_output_aliases={6: 0},
        scratch_shapes=[pltpu.VMEM((tq, Sp), jnp.int32),
                        pltpu.VMEM((tq, Sp), jnp.float32),
                        pltpu.VMEM((A_KV_HEADS, rows, hd), jnp.bfloat16),
                        pltpu.VMEM((A_KV_HEADS, rows, LANE), jnp.float32),
                        pltpu.VMEM((A_KV_HEADS, rows, LANE), jnp.float32),
                        pltpu.VMEM((A_KV_HEADS, rows, hd), jnp.float32)],
        compiler_params=pltpu.CompilerParams(
            dimension_semantics=("parallel", "arbitrary"),
            vmem_limit_bytes=V7X_VMEM_LIMIT_BYTES),
        name="dsa_core",
    )(qi3, wi, kidx3, q, kb, vb, o_all)


def _rope_tables(pos, width, period):
    half = period // ROPE_FRACTION // 2
    inv_freq = ROPE_THETA ** (-jnp.arange(half, dtype=jnp.float32) / half)
    ang = pos.astype(jnp.float32)[:, None] * inv_freq[None, :]
    cos, sin = jnp.cos(ang), jnp.sin(ang)
    T = pos.shape[0]
    rest = period - 2 * half
    c = jnp.concatenate([cos, cos, jnp.ones((T, rest), jnp.float32)], axis=1)
    s_next = jnp.concatenate([-sin, jnp.zeros((T, period - half), jnp.float32)], axis=1)
    s_prev = jnp.concatenate([jnp.zeros((T, half), jnp.float32), sin, jnp.zeros((T, rest), jnp.float32)], axis=1)
    return jnp.stack([jnp.tile(t, (1, width // period)) for t in (c, s_next, s_prev)])


def _rope_lanes(x, tab_ref, half):
    return (x * tab_ref[0] + pltpu.roll(x, LANE - half, 1) * tab_ref[1] + pltpu.roll(x, half, 1) * tab_ref[2])


def _dsa_prep_kernel(q_ref, kv_ref, idx_ref, tq_ref, ti_ref,
                     qa_ref, kn_ref, vn_ref, kin_ref, kb_ref, vb_ref, qi3_ref, kidx3_ref, wi_ref, *, q_scale):
    tq = q_ref.shape[0]
    hd = LANE
    kvw = kv_ref.shape[1] // 2
    half_q = hd // ROPE_FRACTION // 2
    half_i = IDX_DIM // ROPE_FRACTION // 2
    left = lax.broadcasted_iota(jnp.int32, (tq, LANE), 1) < IDX_DIM
    zero = jnp.zeros((tq, LANE), jnp.float32)

    for h in range(q_ref.shape[1] // hd):
        cols = slice(h * hd, (h + 1) * hd)
        qa_ref[0, :, cols] = (_rope_lanes(q_ref[:, cols], tq_ref, half_q) * q_scale).astype(qa_ref.dtype)
    for h in range(kvw // hd):
        cols = slice(h * hd, (h + 1) * hd)
        k = _rope_lanes(kv_ref[:, cols], tq_ref, half_q)
        kn_ref[0, :, h, :] = k
        kb_ref[0, :, cols] = k.astype(kb_ref.dtype)
        vn_ref[0, :, h, :] = kv_ref[:, kvw + h * hd:kvw + (h + 1) * hd]
    vb_ref[0] = kv_ref[:, kvw:].astype(vb_ref.dtype)

    def hi_lo(x):
        hi = x.astype(jnp.bfloat16).astype(jnp.float32)
        return hi, x - hi

    for t in range(IDX_HEADS * IDX_DIM // LANE):
        hi, lo = hi_lo(_rope_lanes(idx_ref[:, t * LANE:(t + 1) * LANE], ti_ref, half_i))
        hi_sw, lo_sw = pltpu.roll(hi, IDX_DIM, 1), pltpu.roll(lo, IDX_DIM, 1)
        even = jnp.concatenate([jnp.where(left, hi, lo_sw), jnp.where(left, hi, zero)], axis=1)
        odd = jnp.concatenate([jnp.where(left, hi_sw, lo), jnp.where(left, hi_sw, zero)], axis=1)
        qi3_ref[0, 0, (2 * t) * tq:(2 * t + 1) * tq, :] = even.astype(qi3_ref.dtype)
        qi3_ref[0, 0, (2 * t + 1) * tq:(2 * t + 2) * tq, :] = odd.astype(qi3_ref.dtype)
    c0 = IDX_HEADS * IDX_DIM
    x = idx_ref[:, c0:c0 + LANE]
    r = jnp.where(left, _rope_lanes(x, ti_ref, half_i), x)
    kin_ref[0] = r[:, :IDX_DIM]
    hi, lo = hi_lo(r)
    kidx3 = jnp.concatenate([jnp.where(left, hi, pltpu.roll(hi, IDX_DIM, 1)), jnp.where(left, lo, zero)], axis=1)
    kidx3_ref[0] = kidx3.astype(kidx3_ref.dtype)
    wi_ref[0] = x[:, IDX_DIM:IDX_DIM + IDX_HEADS] * (IDX_HEADS ** -0.5 * IDX_DIM ** -0.5)


def _dsa_prep(q_all, kv_all, idx_all, row0, B, T, pos0, *, tq):
    qd, kv2 = A_HEADS * LANE, 2 * A_KV_HEADS * LANE
    kv_col = (kv_all.shape[1] - kv2) // kv2
    kvw = kv2 // 2
    nb = T // tq
    blk0 = row0 // tq
    assert row0 % tq == 0 and qd // A_HEADS == LANE
    pos = pos0 + jnp.arange(T, dtype=jnp.int32)
    tab_q = _rope_tables(pos, LANE, LANE)
    tab_i = _rope_tables(pos, LANE, IDX_DIM)
    row = lambda w: pl.BlockSpec((tq, w), lambda b, i: (blk0 + b * nb + i, 0))
    tab = pl.BlockSpec((3, tq, LANE), lambda b, i: (0, i, 0))
    out = lambda w: pl.BlockSpec((1, tq, w), lambda b, i: (b, i, 0))
    f32, bf16 = jnp.float32, jnp.bfloat16
    heads4 = pl.BlockSpec((1, tq, A_KV_HEADS, LANE), lambda b, i: (b, i, 0, 0))
    shapes = [((B, T, qd), bf16), ((B, T, A_KV_HEADS, LANE), f32), ((B, T, A_KV_HEADS, LANE), f32),
              ((B, T, IDX_DIM), f32),
              ((B, T, kvw), bf16), ((B, T, kvw), bf16), ((B, nb, IDX_HEADS * tq, IDX_PACK), bf16),
              ((B, T, IDX_PACK), bf16), ((B, T, IDX_HEADS), f32)]
    return pl.pallas_call(
        functools.partial(_dsa_prep_kernel, q_scale=LANE ** -0.5 * LOG2_E),
        grid=(B, nb),
        in_specs=[row(qd), pl.BlockSpec((tq, kv2), lambda b, i: (blk0 + b * nb + i, kv_col)),
                  row(idx_all.shape[1]), tab, tab],
        out_specs=[out(qd), heads4, heads4, out(IDX_DIM), out(kvw), out(kvw),
                   pl.BlockSpec((1, 1, IDX_HEADS * tq, IDX_PACK), lambda b, i: (b, i, 0, 0)),
                   out(IDX_PACK), out(IDX_HEADS)],
        out_shape=[jax.ShapeDtypeStruct(s, d) for s, d in shapes],
        compiler_params=pltpu.CompilerParams(
            dimension_semantics=("parallel", "parallel"), vmem_limit_bytes=V7X_VMEM_LIMIT_BYTES),
        name="dsa_prep",
    )(q_all, kv_all, idx_all, tab_q, tab_i)


def _dsa_layer(x, mod, gate, sid, streams, w_in, w_out):
    D = x.shape[1]
    hd = D // A_HEADS
    q_dim, kv_dim = A_HEADS * hd, A_KV_HEADS * hd
    n_idx = IDX_HEADS * IDX_DIM + IDX_DIM + IDX_HEADS
    w_main = _weight_parts(w_in[:, :q_dim + 2 * kv_dim], 1)
    w_idx = jnp.pad(w_in[:, q_dim + 2 * kv_dim:], ((0, 0), (0, _round_up(n_idx, LANE) - n_idx)))
    q_all = kv_all = pmatmul(x, w_main, mod=mod, sid=sid)
    idx_all = pmatmul(x, _weight_parts(w_idx, 3), tn=w_idx.shape[1], mod=mod, sid=sid)
    caches = []
    o_all = jnp.zeros((x.shape[0], q_dim), jnp.bfloat16)
    for row0, B, T, pos0, past_k, past_v, past_kidx in streams:
        o_all, k, v, ki = _dsa_stream(q_all, kv_all, idx_all, row0, B, T, pos0, past_k, past_v, past_kidx, o_all)
        caches.append((k, v, ki))
    x = pmatmul(o_all, _weight_parts(w_out, 1), res=(x, gate), sid=sid)
    return x, caches


DSA_KEY_TILE = 512
DSA_QUERY_TILE = 256


def _dsa_stream(q_all, kv_all, idx_all, row0, B, T, pos0, past_k, past_v, past_kidx, o_all):
    tq = min(DSA_QUERY_TILE, T)
    q, k, v, ki, kb, vb, qi3, kidx3, wi = _dsa_prep(q_all, kv_all, idx_all, row0, B, T, pos0, tq=tq)
    n_keys = T
    if past_k is not None:
        P = past_k.shape[1]
        n_keys = P + T
        kb = jnp.concatenate([past_k.astype(jnp.bfloat16).reshape(B, P, -1), kb], axis=1)
        vb = jnp.concatenate([past_v.astype(jnp.bfloat16).reshape(B, P, -1), vb], axis=1)
        ph, pl_ = _split_bf16(past_kidx)
        kidx3 = jnp.concatenate([jnp.concatenate([ph, ph, pl_, jnp.zeros_like(ph)], axis=-1), kidx3], axis=1)
    pad = ((0, 0), (0, _round_up(n_keys, DSA_KEY_TILE) - n_keys), (0, 0))
    kb, vb, kidx3 = (jnp.pad(a, pad) for a in (kb, vb, kidx3))
    topk = min(TOPK_MAX, n_keys // 4)
    o_all = _dsa_core(q, qi3, wi, kb, vb, kidx3, pos0, topk, o_all, row0, tq=tq, tk=DSA_KEY_TILE)
    return o_all, k, v, ki


GDN_HEAD_GROUP = 32


def _bf16_dot(a, b):
    return jnp.dot(a.astype(jnp.bfloat16), b.astype(jnp.bfloat16), preferred_element_type=jnp.float32)


def _dot3(a, b):
    ah, al = _split_bf16(a)
    bh, bl = _split_bf16(b)
    out = jnp.dot(ah, bh, preferred_element_type=jnp.float32)
    out = out + jnp.dot(ah, bl, preferred_element_type=jnp.float32)
    return out + jnp.dot(al, bh, preferred_element_type=jnp.float32)


def _conv_silu(x_ref, w_ref, xe_ref):
    C = x_ref.shape[0]
    taps = w_ref.shape[0]
    xe_ref[SUBLANE:, :] = x_ref[...]
    first = SUBLANE - (taps - 1)
    acc = xe_ref[first:first + C, :] * w_ref[0:1, :]
    for j in range(1, taps):
        acc = acc + xe_ref[first + j:first + j + C, :] * w_ref[j:j + 1, :]
    xe_ref[:SUBLANE, :] = xe_ref[C:, :]
    return acc * jax.nn.sigmoid(acc)


def _gdn_kernel(xq_ref, xk_ref, xv_ref, wq_ref, wk_ref, wv_ref, cq_ref, ck_ref, cv_ref,
                z_ref, g_ref, gt_ref, beta_ref, nw_ref, s0_ref, o_all_ref,
                o_ref, s_out_ref, s_ref, eq_ref, ek_ref, ev_ref):
    del o_all_ref
    n = pl.program_id(2)
    C = xq_ref.shape[0]
    hg = g_ref.shape[3]
    dk = s_ref.shape[1]
    dv = s_ref.shape[2]
    rep = hg // (xk_ref.shape[1] // dk)

    @pl.when(n == 0)
    def _():
        s_ref[...] = s0_ref[0]
        for e_ref, c_ref in ((eq_ref, cq_ref), (ek_ref, ck_ref), (ev_ref, cv_ref)):
            e_ref[:SUBLANE, :] = jnp.zeros((SUBLANE, e_ref.shape[1]), jnp.float32)
            e_ref[SUBLANE - c_ref.shape[1]:SUBLANE, :] = c_ref[0]

    qc = _conv_silu(xq_ref, wq_ref, eq_ref)
    kc = _conv_silu(xk_ref, wk_ref, ek_ref)
    vc = _conv_silu(xv_ref, wv_ref, ev_ref)
    q_heads, k_heads = [], []
    for i in range(hg // rep):
        qh = qc[:, i * dk:(i + 1) * dk]
        kh = kc[:, i * dk:(i + 1) * dk]
        q_heads.append(qh * (lax.rsqrt(jnp.sum(qh * qh, axis=1, keepdims=True) + EPS) * dk ** -0.5))
        k_heads.append(kh * lax.rsqrt(jnp.sum(kh * kh, axis=1, keepdims=True) + EPS))

    ri = lax.broadcasted_iota(jnp.int32, (C, C), 0)
    ci = lax.broadcasted_iota(jnp.int32, (C, C), 1)
    causal = ri >= ci
    strict = ri > ci
    eye = jnp.where(ri == ci, 1.0, 0.0)
    g = g_ref[0, 0]
    gc_all = _dot3(jnp.where(causal, 1.0, 0.0), g)
    gr_all = _dot3(gt_ref[0, 0, 0], jnp.where(ri <= ci, 1.0, 0.0))
    beta = beta_ref[0, 0]
    nw = nw_ref[...]

    heads = range(hg)
    qs = [q_heads[h // rep] for h in heads]
    ks = [k_heads[h // rep] for h in heads]
    gcs = [gc_all[:, h:h + 1] for h in heads]
    bcols = [beta[:, h:h + 1] for h in heads]
    decays = [jnp.where(causal, jnp.exp(jnp.where(causal, gcs[h] - gr_all[h:h + 1, :], 0.0)), 0.0)
              for h in heads]
    kbs = [ks[h] * bcols[h] for h in heads]
    kks = [lax.dot_general(kbs[h].astype(jnp.bfloat16), ks[h].astype(jnp.bfloat16), _NT_DIMS,
                           preferred_element_type=jnp.float32) for h in heads]
    bms = [jnp.where(strict, -(kks[h] * decays[h]), 0.0) for h in heads]
    egs = [jnp.exp(gcs[h]) for h in heads]
    rhss = [jnp.concatenate([vc[:, h * dv:(h + 1) * dv] * bcols[h], kbs[h] * egs[h]], axis=1)
            for h in heads]
    pairs = range(hg // 2)
    left = lax.broadcasted_iota(jnp.int32, (C, 2 * C), 1) < C
    zero16 = jnp.zeros((C, 2 * C), jnp.bfloat16)

    def blockdiag(part):
        return jnp.concatenate([jnp.where(left, part, zero16), jnp.where(left, zero16, part)], axis=0)

    def dot3_pairs(x_parts, y_parts):
        xh, xl = x_parts
        yh, yl = blockdiag(y_parts[0]), blockdiag(y_parts[1])
        out = jnp.dot(xh, yh, preferred_element_type=jnp.float32)
        out = out + jnp.dot(xh, yl, preferred_element_type=jnp.float32)
        return out + jnp.dot(xl, yh, preferred_element_type=jnp.float32)

    b2 = [jnp.concatenate([bms[2 * i], bms[2 * i + 1]], axis=1) for i in pairs]
    eye2 = jnp.concatenate([eye, eye], axis=1)
    p2 = [eye2 + b2[i] for i in pairs]
    b2_parts = [_split_bf16(b2[i]) for i in pairs]
    step = 2
    while step < C:
        b2 = [dot3_pairs(b2_parts[i], b2_parts[i]) for i in pairs]
        b2_parts = [_split_bf16(b2[i]) for i in pairs]
        p2 = [p2[i] + dot3_pairs(_split_bf16(p2[i]), b2_parts[i]) for i in pairs]
        step *= 2
    ps = [p2[h // 2][:, (h % 2) * C:(h % 2 + 1) * C] for h in heads]
    ws = [_dot3(ps[h], rhss[h]) for h in heads]
    qks = [lax.dot_general(qs[h].astype(jnp.bfloat16), ks[h].astype(jnp.bfloat16), _NT_DIMS,
                           preferred_element_type=jnp.float32) * decays[h] for h in heads]
    g_lasts = [gcs[h][C - 1:C, :] for h in heads]
    ss = [s_ref[h] for h in heads]
    us = [ws[h][:, :dv] - _bf16_dot(ws[h][:, dv:], ss[h]) for h in heads]
    os_ = [_bf16_dot(qs[h] * egs[h], ss[h]) + _bf16_dot(qks[h], us[h]) for h in heads]
    for h in heads:
        ke = ks[h] * jnp.exp(g_lasts[h] - gcs[h])
        s_ref[h] = ss[h] * jnp.exp(g_lasts[h]) + _bf16_dot(ke.T, us[h])
    for h in heads:
        o = os_[h]
        o = o * lax.rsqrt(jnp.mean(o * o, axis=1, keepdims=True) + EPS) * nw
        z = z_ref[:, h * dv:(h + 1) * dv]
        o_ref[:, h * dv:(h + 1) * dv] = (o * (z * jax.nn.sigmoid(z))).astype(o_ref.dtype)

    @pl.when(n == pl.num_programs(2) - 1)
    def _():
        s_out_ref[0] = s_ref[...]


def _gdn_core(qkv, z, row0, conv_w, conv_state, g, beta, norm_w, s0, o_all):
    B, T, VH = g.shape
    dk, dv = s0.shape[2], s0.shape[3]
    vd = VH * dv
    qd = (conv_w.shape[1] - vd) // 2
    C = min(CHUNK, T)
    N = T // C
    assert row0 % C == 0
    blk0 = row0 // C
    hg = min(GDN_HEAD_GROUP, VH)
    ng = VH // hg
    qw = qd // ng
    vw = hg * dv
    assert qd % qw == 0 and (2 * qd) % vw == 0
    k0, v0 = qd // qw, (2 * qd) // vw
    z0 = (z.shape[1] - vd) // vw
    taps = conv_w.shape[0]

    def grouped(a):
        return a.reshape(B, T, ng, hg).transpose(0, 2, 1, 3)

    gg, bg = grouped(g), grouped(beta)
    gt = gg.reshape(B, ng, N, C, hg).transpose(0, 1, 2, 4, 3)
    small = pl.BlockSpec((1, 1, C, hg), lambda b, j, n: (b, j, n, 0))
    o, s = pl.pallas_call(
        _gdn_kernel,
        grid=(B, ng, N),
        in_specs=[pl.BlockSpec((C, qw), lambda b, j, n: (blk0 + b * N + n, j)),
                  pl.BlockSpec((C, qw), lambda b, j, n: (blk0 + b * N + n, k0 + j)),
                  pl.BlockSpec((C, vw), lambda b, j, n: (blk0 + b * N + n, v0 + j)),
                  pl.BlockSpec((taps, qw), lambda b, j, n: (0, j)),
                  pl.BlockSpec((taps, qw), lambda b, j, n: (0, k0 + j)),
                  pl.BlockSpec((taps, vw), lambda b, j, n: (0, v0 + j)),
                  pl.BlockSpec((1, taps - 1, qw), lambda b, j, n: (b, 0, j)),
                  pl.BlockSpec((1, taps - 1, qw), lambda b, j, n: (b, 0, k0 + j)),
                  pl.BlockSpec((1, taps - 1, vw), lambda b, j, n: (b, 0, v0 + j)),
                  pl.BlockSpec((C, vw), lambda b, j, n: (blk0 + b * N + n, z0 + j)),
                  small,
                  pl.BlockSpec((1, 1, 1, hg, C), lambda b, j, n: (b, j, n, 0, 0)),
                  small,
                  pl.BlockSpec((1, dv), lambda b, j, n: (0, 0)),
                  pl.BlockSpec((1, hg, dk, dv), lambda b, j, n: (b, j, 0, 0)),
                  pl.BlockSpec(memory_space=pl.ANY)],
        out_specs=[pl.BlockSpec((C, hg * dv), lambda b, j, n: (blk0 + b * N + n, j)),
                   pl.BlockSpec((1, hg, dk, dv), lambda b, j, n: (b, j, 0, 0))],
        out_shape=[jax.ShapeDtypeStruct(o_all.shape, o_all.dtype),
                   jax.ShapeDtypeStruct(s0.shape, jnp.float32)],
        input_output_aliases={15: 0},
        scratch_shapes=[pltpu.VMEM((hg, dk, dv), jnp.float32),
                        pltpu.VMEM((SUBLANE + C, qw), jnp.float32),
                        pltpu.VMEM((SUBLANE + C, qw), jnp.float32),
                        pltpu.VMEM((SUBLANE + C, vw), jnp.float32)],
        compiler_params=pltpu.CompilerParams(
            dimension_semantics=("parallel", "parallel", "arbitrary"),
            vmem_limit_bytes=V7X_VMEM_LIMIT_BYTES),
        name="gdn_core",
    )(qkv, qkv, qkv, conv_w, conv_w, conv_w, conv_state, conv_state, conv_state,
      z, gg, gt, bg, norm_w.reshape(1, dv), s0, o_all)
    return o, s


def _gdn_layer(x, mod, gate, sid, streams, w_in, conv_w, a_log, dt_bias, norm_w, w_out):
    qk_dim, v_dim = GDN_QK_HEADS * GDN_DK, GDN_V_HEADS * GDN_DV
    conv_dim = 2 * qk_dim + v_dim
    n_gate = 2 * GDN_V_HEADS
    w_main = _weight_parts(w_in[:, :conv_dim + v_dim], 1)
    w_gate = jnp.pad(w_in[:, conv_dim + v_dim:], ((0, 0), (0, _round_up(n_gate, LANE) - n_gate)))
    qkv_all = z_all = pmatmul(x, w_main, tn=2048, mod=mod, sid=sid)
    gates_all = pmatmul(x, _weight_parts(w_gate, 1), tn=w_gate.shape[1], mod=mod, sid=sid)
    states = []
    o_all = jnp.zeros((x.shape[0], v_dim), jnp.bfloat16)
    for row0, B, T, conv_state, ssm_state in streams:
        rows = slice(row0, row0 + B * T)
        gates = gates_all[rows].reshape(B, T, -1)
        beta_raw, a_raw = gates[..., :GDN_V_HEADS], gates[..., GDN_V_HEADS:n_gate]
        if conv_state is None:
            conv_state = jnp.zeros((B, CONV_W - 1, conv_dim), x.dtype)
        if ssm_state is None:
            ssm_state = jnp.zeros((B, GDN_V_HEADS, GDN_DK, GDN_DV), jnp.float32)
        tail = jnp.stack([qkv_all[row0 + (b + 1) * T - (CONV_W - 1):row0 + (b + 1) * T, :conv_dim]
                          for b in range(B)])
        new_conv = jnp.concatenate([conv_state, tail], axis=1)[:, -(CONV_W - 1):]
        beta = jax.nn.sigmoid(beta_raw)
        g = -jnp.exp(a_log) * jax.nn.softplus(a_raw + dt_bias)
        o_all, S = _gdn_core(qkv_all, z_all, row0, conv_w, conv_state, g, beta, norm_w, ssm_state, o_all)
        states.append((new_conv, S))
    x = pmatmul(o_all, _weight_parts(w_out, 1), res=(x, gate), sid=sid)
    return x, states


NEG_INF = float("-inf")
_PEER_CAND_ROWS = tuple((i, PEER_TOPK // (i + 1)) for i in range(PEER_TOPK // 2))


def _split_bf16(x):
    hi = x.astype(jnp.bfloat16)
    lo = (x - hi.astype(jnp.float32)).astype(jnp.bfloat16)
    return hi, lo


def _dot3_nt(a, b):
    dn = (((1,), (1,)), ((), ()))
    ah, al = _split_bf16(a)
    bh, bl = _split_bf16(b)
    out = lax.dot_general(ah, bh, dn, preferred_element_type=jnp.float32)
    out = out + lax.dot_general(ah, bl, dn, preferred_element_type=jnp.float32)
    return out + lax.dot_general(al, bh, dn, preferred_element_type=jnp.float32)


def _top_rows_desc(s, n, with_rank=False):
    rows = []
    cur = s
    rank = jnp.full(s.shape, float(n), jnp.float32)
    for i in range(n):
        m = jnp.max(cur, axis=0, keepdims=True)
        rows.append(m)
        hit = cur == m
        if with_rank:
            rank = jnp.where(hit, float(i), rank)
        cur = jnp.where(hit, NEG_INF, cur)
    return (rows, rank) if with_rank else rows


def _stack_rows(rows, lanes):
    n = len(rows)
    rid = lax.broadcasted_iota(jnp.int32, (n, lanes), 0)
    out = jnp.zeros((n, lanes), jnp.float32)
    for i, r in enumerate(rows):
        out = jnp.where(rid == i, r, out)
    return out


def _peer_select_kernel(q_ref, keys_ref, rk2_ref, e2_ref, cnt_ref, e1_ref):
    tm = q_ref.shape[0]
    k = PEER_TOPK
    s1 = _dot3_nt(keys_ref[0], q_ref[:, :PEER_HALF])
    s2 = _dot3_nt(keys_ref[1], q_ref[:, PEER_HALF:])
    r1 = _top_rows_desc(s1, k)
    r2, rank2 = _top_rows_desc(s2, k, with_rank=True)
    v1 = _stack_rows(r1, tm)
    v2 = _stack_rows(r2, tm)
    v2h = v2[:k // 2]
    rid = lax.broadcasted_iota(jnp.int32, (k // 2, tm), 0)
    pieces = [r1[0] + v2]
    for i, n in _PEER_CAND_ROWS[1:]:
        pieces.append(jnp.where(rid < n, r1[i] + v2h, NEG_INF))
    pieces.append(v1[k // 2:] + r2[0])
    cand = jnp.concatenate(pieces, axis=0)
    tau = _top_rows_desc(cand, k)[-1]
    top = r1[0] + r2[0]
    z = jnp.sum(jnp.where(cand >= tau, jnp.exp(cand - top), 0.0), axis=0, keepdims=True)
    cnt = jnp.zeros(s1.shape, jnp.float32)
    for j in range(k):
        cnt = cnt + jnp.where(s1 + r2[j] >= tau, 1.0, 0.0)
    rk2_ref[0] = pltpu.bitcast(rank2.astype(jnp.bfloat16), rk2_ref.dtype)
    e2_ref[0] = pltpu.bitcast((jnp.exp(s2 - r2[0]) / z).astype(jnp.bfloat16), e2_ref.dtype)
    cnt_ref[0] = cnt
    e1_ref[0] = jnp.exp(s1 - r1[0])


def _peer_select(q, keys, *, tm):
    M = q.shape[0]
    nk = keys.shape[1]
    ospec = pl.BlockSpec((1, nk, tm), lambda i, h: (h, 0, i))
    pspec = pl.BlockSpec((1, nk // 2, tm), lambda i, h: (h, 0, i))
    return pl.pallas_call(
        _peer_select_kernel,
        grid=(M // tm, PEER_HEADS),
        in_specs=[pl.BlockSpec((tm, PEER_KEY_DIM), lambda i, h: (i, h)),
                  pl.BlockSpec(keys.shape, lambda i, h: (0, 0, 0))],
        out_specs=[pspec, pspec, ospec, ospec],
        out_shape=[jax.ShapeDtypeStruct((PEER_HEADS, nk // 2, M), jnp.int32)] * 2
        + [jax.ShapeDtypeStruct((PEER_HEADS, nk, M), jnp.float32)] * 2,
        compiler_params=pltpu.CompilerParams(
            dimension_semantics=("parallel", "arbitrary"),
            vmem_limit_bytes=V7X_VMEM_LIMIT_BYTES),
        name="peer_select",
    )(q, keys)


BF16_SUBLANES = 16
PEER_SUB_SPLIT = (1, 1)


def _peer_main_kernel(sid_ref, x_ref, gain_ref, shift_ref, scale_ref, gate_ref, fnorm_ref,
                      u_ref, vT_ref, rk2_ref, e2_ref, cnt_ref, e1_ref, *refs, final_norm):
    out_refs, (hT_ref, coef_ref, acc_ref) = refs[:-3], refs[-3:]
    e = pl.program_id(1)
    te, tm = coef_ref.shape
    nk = 2 * rk2_ref.shape[1]
    rows = BF16_SUBLANES
    blk0 = pl.program_id(0) * (tm // SEQ_BLOCK)

    @pl.when(e == 0)
    def _():
        acc_ref[...] = jnp.zeros_like(acc_ref)
        per_lane_tile = LANE // SEQ_BLOCK
        for t in range(tm // LANE):
            hs = []
            for r in range(t * per_lane_tile, (t + 1) * per_lane_tile):
                rs = slice(r * SEQ_BLOCK, (r + 1) * SEQ_BLOCK)
                hs.append(_modulated_rows(x_ref[rs, :], gain_ref[...], shift_ref, scale_ref, sid_ref[blk0 + r]))
            hT_ref[:, t * LANE:(t + 1) * LANE] = jnp.concatenate(hs, axis=0).T.astype(jnp.bfloat16)

    def expert_acts(span):
        off, size = span
        return jnp.dot(u_ref[off:off + size, :], hT_ref[...], preferred_element_type=jnp.float32)

    def weigh(span, act):
        off, size = span
        for al in range(size // nk):
            a = off // nk + al
            for lg in range(tm // LANE):
                lanes = slice(lg * LANE, (lg + 1) * LANE)
                cnts = [jnp.broadcast_to(cnt_ref[h, a:a + 1, lanes], (rows, LANE)).astype(jnp.bfloat16)
                        for h in range(PEER_HEADS)]
                e1s = [jnp.broadcast_to(e1_ref[h, a:a + 1, lanes], (rows, LANE)).astype(jnp.bfloat16)
                       for h in range(PEER_HEADS)]
                for r in range(nk // rows):
                    words = slice(r * rows // 2, (r + 1) * rows // 2)
                    w = None
                    for h in range(PEER_HEADS):
                        e2 = pltpu.bitcast(e2_ref[h, words, lanes], jnp.bfloat16)
                        rk2 = pltpu.bitcast(rk2_ref[h, words, lanes], jnp.bfloat16)
                        t = jnp.where(rk2 < cnts[h], e2, jnp.zeros_like(e2)) * e1s[h]
                        w = t if w is None else w + t
                    x = act[al * nk + r * rows:al * nk + (r + 1) * rows, lanes]
                    g = 0.5 * x * (1.0 + lax.erf(x * (2.0 ** -0.5)))
                    coef_ref[off + al * nk + r * rows:off + al * nk + (r + 1) * rows, lanes] = (
                        w * g.astype(jnp.bfloat16))

    def accumulate(span):
        off, size = span
        acc_ref[...] += lax.dot_general(vT_ref[off:off + size, :], coef_ref[off:off + size, :],
                                        (((0,), (0,)), ((), ())), preferred_element_type=jnp.float32)

    spans, off = [], 0
    for frac in PEER_SUB_SPLIT:
        spans.append((off, te * frac // sum(PEER_SUB_SPLIT)))
        off += spans[-1][1]
    act = expert_acts(spans[0])
    for i, span in enumerate(spans):
        nxt = expert_acts(spans[i + 1]) if i + 1 < len(spans) else None
        weigh(span, act)
        accumulate(span)
        act = nxt

    @pl.when(e == pl.num_programs(1) - 1)
    def _():
        out = acc_ref[...].T
        for r in range(tm // SEQ_BLOCK):
            rs = slice(r * SEQ_BLOCK, (r + 1) * SEQ_BLOCK)
            y = x_ref[rs, :] + gate_ref[pl.ds(sid_ref[blk0 + r], 1), :] * out[rs, :]
            out_refs[0][rs, :] = y
            if final_norm:
                out_refs[1][rs, :] = y * lax.rsqrt(jnp.mean(y * y, axis=-1, keepdims=True) + EPS) * fnorm_ref[...]


def _peer_main(x, mod, gate, fnorm, sid, u, vT, rk2, e2, cnt, e1, *, tm, te, final_norm):
    M, D = x.shape
    E = u.shape[0]
    nk = cnt.shape[1]
    const = lambda a: pl.BlockSpec(a.shape, lambda i, e, s: (0,) * a.ndim)
    col_spec = pl.BlockSpec((PEER_HEADS, nk // 2, tm), lambda i, e, s: (0, 0, i))
    row_spec = pl.BlockSpec((PEER_HEADS, te // nk, tm), lambda i, e, s: (0, e, i))
    tok_spec = pl.BlockSpec((tm, D), lambda i, e, s: (i, 0))
    n_out = 2 if final_norm else 1
    return pl.pallas_call(
        functools.partial(_peer_main_kernel, final_norm=final_norm),
        grid_spec=pltpu.PrefetchScalarGridSpec(
            num_scalar_prefetch=1,
            grid=(M // tm, E // te),
            in_specs=[tok_spec, const(mod[0]), const(mod[1]), const(mod[2]), const(gate), const(fnorm),
                      pl.BlockSpec((te, D), lambda i, e, s: (e, 0)),
                      pl.BlockSpec((te, D), lambda i, e, s: (e, 0)),
                      col_spec, col_spec, row_spec, row_spec],
            out_specs=[tok_spec] * n_out,
            scratch_shapes=[pltpu.VMEM((D, tm), jnp.bfloat16),
                            pltpu.VMEM((te, tm), jnp.bfloat16),
                            pltpu.VMEM((D, tm), jnp.float32)]),
        out_shape=[jax.ShapeDtypeStruct((M, D), jnp.float32)] * n_out,
        compiler_params=pltpu.CompilerParams(
            dimension_semantics=("parallel", "arbitrary"),
            vmem_limit_bytes=V7X_VMEM_LIMIT_PEER_BYTES),
        name="peer_main",
    )(sid, x, *mod, gate, fnorm, u, vT, rk2, e2, cnt, e1)


def _peer(x, mod, gate, fnorm, sid, w_query, sub_keys, expert_u, expert_v, *, final_norm,
          tm_sel=256, tm=512, te=1024):
    M = x.shape[0]
    q = pmatmul(x, _weight_parts(w_query, 1), mod=mod, sid=sid)
    rk2, e2, cnt, e1 = _peer_select(q, sub_keys, tm=min(tm_sel, M))
    u = expert_u.astype(jnp.bfloat16)
    vT = expert_v.astype(jnp.bfloat16)
    return _peer_main(x, mod, gate, fnorm, sid, u, vT, rk2, e2, cnt, e1, tm=min(tm, M), te=te,
                      final_norm=final_norm)


def kernel(x_prompt, x_sample, c_prompt, c_sample, cache_k_l0, cache_v_l0, cache_kidx_l0, state_conv_l1, state_ssm_l1, norm1_l0, norm2_l0, ada_w_l0, ada_b_l0, attn_in_l0, attn_out_l0, peer_query_l0, peer_keys_l0, peer_u_l0, peer_v_l0, norm1_l1, norm2_l1, ada_w_l1, ada_b_l1, gdn_in_l1, gdn_conv_l1, gdn_a_log_l1, gdn_dt_bias_l1, gdn_norm_l1, gdn_out_l1, peer_query_l1, peer_keys_l1, peer_u_l1, peer_v_l1, final_norm):
    past_len = cache_k_l0.shape[1]
    norm1 = (norm1_l0, norm1_l1)
    norm2 = (norm2_l0, norm2_l1)
    ada_w = (ada_w_l0, ada_w_l1)
    ada_b = (ada_b_l0, ada_b_l1)
    peer_query = (peer_query_l0, peer_query_l1)
    peer_keys = (peer_keys_l0, peer_keys_l1)
    peer_u = (peer_u_l0, peer_u_l1)
    peer_v = (peer_v_l0, peer_v_l1)
    Bp, Tp, D = x_prompt.shape
    Bs, Ts, _ = x_sample.shape
    n_p, n_s = Bp * Tp, Bs * Ts
    x = jnp.concatenate([x_prompt.reshape(n_p, D), x_sample.reshape(n_s, D)], axis=0)
    sid = jnp.concatenate([jnp.repeat(jnp.arange(Bp, dtype=jnp.int32), Tp // SEQ_BLOCK),
                           Bp + jnp.repeat(jnp.arange(Bs, dtype=jnp.int32), Ts // SEQ_BLOCK)])
    c_all = jnp.concatenate([c_prompt, c_sample], axis=0)
    fnorm = final_norm.reshape(1, D)
    for i in range(2):
        mods = _adaln(c_all, ada_w[i], ada_b[i])
        mod1 = (norm1[i].reshape(1, D), mods[0], mods[1])
        if i == 0:
            x, ((nkp, nvp, nkip), (nks, nvs, nkis)) = _dsa_layer(
                x, mod1, mods[2], sid,
                [(0, Bp, Tp, 0, None, None, None), (n_p, Bs, Ts, past_len, cache_k_l0, cache_v_l0, cache_kidx_l0)],
                attn_in_l0, attn_out_l0)
        else:
            x, ((ncp, nsp), (ncs, nss)) = _gdn_layer(
                x, mod1, mods[2], sid, [(0, Bp, Tp, None, None), (n_p, Bs, Ts, state_conv_l1, state_ssm_l1)],
                gdn_in_l1, gdn_conv_l1, gdn_a_log_l1, gdn_dt_bias_l1, gdn_norm_l1, gdn_out_l1)
        mod2 = (norm2[i].reshape(1, D), mods[3], mods[4])
        outs = _peer(x, mod2, mods[5], fnorm, sid, peer_query[i], peer_keys[i], peer_u[i], peer_v[i],
                     final_norm=(i == 1))
        x = outs[0]
    y = outs[1]
    y_prompt = y[:n_p].reshape(Bp, Tp, D)
    y_sample = y[n_p:].reshape(Bs, Ts, D)
    return (y_prompt, y_sample, nkp, nvp, nkip, nks, nvs, nkis, ncp, nsp, ncs, nss)
```

```python
import functools

import jax
import jax.numpy as jnp
from jax import lax
from jax.experimental import pallas as pl
from jax.experimental.pallas import tpu as pltpu

CHUNK = 64
CHUNK_SHIFT = CHUNK.bit_length() - 1
EPS = 1e-6
ROPE_THETA = 500000.0
ROPE_FRACTION = 4
A_HEADS = 16
A_KV_HEADS = 4
IDX_HEADS = 8
IDX_DIM = 64
TOPK_MAX = 256
Q_BLOCK = 128
GDN_QK_HEADS = 16
GDN_V_HEADS = 32
GDN_DK = 128
GDN_DV = 128
CONV_W = 4
PEER_HEADS = 8
PEER_KEY_DIM = 256
PEER_HALF = PEER_KEY_DIM // 2
PEER_TOPK = 16

V7X_VMEM_LIMIT_BYTES = 48 * 1024 * 1024
V7X_VMEM_LIMIT_PEER_BYTES = 56 * 1024 * 1024
LANE = 128
SUBLANE = 8


def _round_up(n, m):
    return (n + m - 1) // m * m


SEQ_BLOCK = CHUNK


def _modulated_rows(x, gain, shift_ref, scale_ref, sid):
    y = x * lax.rsqrt(jnp.mean(x * x, axis=-1, keepdims=True) + EPS)
    return y * gain * (1.0 + scale_ref[pl.ds(sid, 1), :]) + shift_ref[pl.ds(sid, 1), :]


def _matmul_kernel(sid_ref, x_ref, *refs, passes, has_mod, has_res):
    refs = list(refs)
    mod_refs = [refs.pop(0) for _ in range(3)] if has_mod else None
    nparts = 1 + passes // 2
    w_refs = [refs.pop(0) for _ in range(nparts)]
    res_refs = [refs.pop(0) for _ in range(2)] if has_res else None
    o_ref, x_parts = refs[0], refs[1:]
    tm = x_ref.shape[0]
    blocks = range(tm // SEQ_BLOCK) if (has_mod or has_res) else ()
    blk0 = pl.program_id(0) * (tm // SEQ_BLOCK)

    @pl.when(pl.program_id(1) == 0)
    def _():
        def put(rows, x):
            hi = x.astype(jnp.bfloat16)
            x_parts[0][rows, :] = hi
            if passes == 3:
                x_parts[1][rows, :] = (x - hi.astype(jnp.float32)).astype(jnp.bfloat16)

        if has_mod:
            gain_ref, shift_ref, scale_ref = mod_refs
            for r in blocks:
                rows = slice(r * SEQ_BLOCK, (r + 1) * SEQ_BLOCK)
                put(rows, _modulated_rows(x_ref[rows, :], gain_ref[...], shift_ref, scale_ref, sid_ref[blk0 + r]))
        else:
            put(slice(None), x_ref[...].astype(jnp.float32))

    acc = jnp.dot(x_parts[0][...], w_refs[0][...], preferred_element_type=jnp.float32)
    if passes == 3:
        acc = acc + jnp.dot(x_parts[0][...], w_refs[1][...], preferred_element_type=jnp.float32)
        acc = acc + jnp.dot(x_parts[1][...], w_refs[0][...], preferred_element_type=jnp.float32)
    if has_res:
        res_ref, gate_ref = res_refs
        for r in blocks:
            rows = slice(r * SEQ_BLOCK, (r + 1) * SEQ_BLOCK)
            gate = gate_ref[pl.ds(sid_ref[blk0 + r], 1), :]
            o_ref[rows, :] = res_ref[rows, :] + gate * acc[rows, :]
    else:
        o_ref[...] = acc.astype(o_ref.dtype)


def _weight_parts(w, passes):
    hi = w.astype(jnp.bfloat16)
    if passes == 1:
        return (hi,)
    return (hi, (w - hi.astype(jnp.float32)).astype(jnp.bfloat16))


def pmatmul(x, w_parts, *, col0=0, ncols=None, tm=512, tn=1024, out_dtype=jnp.float32,
            mod=None, res=None, sid=None):
    passes = 1 if len(w_parts) == 1 else 3
    M, K = x.shape
    n_total = w_parts[0].shape[1]
    ncols = n_total - col0 if ncols is None else ncols
    tm = min(tm, _round_up(M, 2 * SUBLANE))
    tn = min(tn, ncols)
    assert ncols % tn == 0 and col0 % tn == 0 and (tn % LANE == 0 or tn == n_total)
    Mp = _round_up(M, tm)
    if mod is not None or res is not None:
        assert Mp == M and tm % SEQ_BLOCK == 0 and sid is not None
    else:
        sid = jnp.zeros((1,), jnp.int32)
    if Mp != M:
        x = jnp.pad(x, ((0, Mp - M), (0, 0)))
    c0 = col0 // tn
    in_specs = [pl.BlockSpec((tm, K), lambda i, j, s: (i, 0))]
    args = [x]
    if mod is not None:
        in_specs += [pl.BlockSpec(a.shape, lambda i, j, s: (0, 0)) for a in mod]
        args += list(mod)
    in_specs += [pl.BlockSpec((K, tn), lambda i, j, s: (0, c0 + j))] * len(w_parts)
    args += list(w_parts)
    if res is not None:
        in_specs += [pl.BlockSpec((tm, tn), lambda i, j, s: (i, j)),
                     pl.BlockSpec((res[1].shape[0], tn), lambda i, j, s: (0, j))]
        args += list(res)
        out_dtype = jnp.float32
    out = pl.pallas_call(
        functools.partial(_matmul_kernel, passes=passes, has_mod=mod is not None, has_res=res is not None),
        grid_spec=pltpu.PrefetchScalarGridSpec(
            num_scalar_prefetch=1,
            grid=(Mp // tm, ncols // tn),
            in_specs=in_specs,
            out_specs=pl.BlockSpec((tm, tn), lambda i, j, s: (i, j)),
            scratch_shapes=[pltpu.VMEM((tm, K), jnp.bfloat16)] * (1 + passes // 2)),
        out_shape=jax.ShapeDtypeStruct((Mp, ncols), out_dtype),
        compiler_params=pltpu.CompilerParams(
            dimension_semantics=("parallel", "arbitrary"),
            vmem_limit_bytes=V7X_VMEM_LIMIT_BYTES),
        name="matmul",
    )(sid, *args)
    return out[:M]


def _adaln_kernel(c_ref, w_ref, b_ref, o_ref):
    c = c_ref[...]
    o_ref[...] = _dot3(c * jax.nn.sigmoid(c), w_ref[...]) + b_ref[...]


def _adaln(c, w, b, *, tn=512):
    n, D = c.shape
    N = w.shape[1]
    rows = _round_up(n, SUBLANE)
    mod = pl.pallas_call(
        _adaln_kernel,
        grid=(N // tn,),
        in_specs=[pl.BlockSpec((rows, D), lambda j: (0, 0)),
                  pl.BlockSpec((D, tn), lambda j: (0, j)),
                  pl.BlockSpec((1, tn), lambda j: (0, j))],
        out_specs=pl.BlockSpec((rows, tn), lambda j: (0, j)),
        out_shape=jax.ShapeDtypeStruct((rows, N), jnp.float32),
        compiler_params=pltpu.CompilerParams(
            dimension_semantics=("parallel",), vmem_limit_bytes=V7X_VMEM_LIMIT_BYTES),
        name="adaln",
    )(jnp.pad(c, ((0, rows - n), (0, 0))), w, b.reshape(1, N))
    return jnp.split(mod, 6, axis=-1)


_NT_DIMS = (((1,), (1,)), ((), ()))
INT32_MIN = -2 ** 31
LOG2_E = 1.4426950408889634
_NEG_INF_KEY = -2139095041
IDX_PACK = 4 * IDX_DIM


def _ordered_key(x):
    bits = pltpu.bitcast(x, jnp.int32)
    return bits ^ ((bits >> 31) & 0x7FFFFFFF)


def _lane_tile_sum(x, width=LANE):
    out = x[:, :width]
    for c in range(1, x.shape[1] // width):
        out = out + x[:, c * width:(c + 1) * width]
    return out


def _dsa_select_bias(qi_ref, wi_ref, kidx_ref, key_ref, bias_ref, *, first, n_tiles, topk, tk):
    tq = wi_ref.shape[1]
    row = lax.broadcasted_iota(jnp.int32, (tq, 1), 0)
    lim = (((first + row) >> CHUNK_SHIFT) + 1) * CHUNK
    w = wi_ref[0]

    def score_tile(j, c):
        off = pl.multiple_of(j * tk, tk)
        kt = kidx_ref[0, pl.ds(off, tk), :]
        sc = lax.dot_general(qi_ref[0, 0], kt, _NT_DIMS,
                             preferred_element_type=jnp.float32)
        s = jnp.zeros((tq, tk), jnp.float32)
        for h in range(IDX_HEADS):
            s = s + w[:, h:h + 1] * jnp.maximum(sc[h * tq:(h + 1) * tq], 0.0)
        col = off + lax.broadcasted_iota(jnp.int32, (tq, tk), 1)
        s = jnp.where(col < lim, s + 0.0, NEG_INF)
        key_ref[:, pl.ds(off, tk)] = _ordered_key(s)
        return c

    lax.fori_loop(0, n_tiles, score_tile, 0)

    def bit_step(b, thr):
        cand = thr + lax.shift_left(jnp.int32(1), 31 - b)

        def count_tile(j, c):
            off = pl.multiple_of(j * tk, tk)
            ge = jnp.where(key_ref[:, pl.ds(off, tk)] >= cand, 1.0, 0.0)
            return c + _lane_tile_sum(ge)

        c = lax.fori_loop(0, n_tiles, count_tile, jnp.zeros((tq, LANE), jnp.float32))
        cnt = jnp.sum(c, axis=1, keepdims=True)
        return jnp.where(cnt >= topk, cand, thr)

    thr = lax.fori_loop(0, 32, bit_step, jnp.full((tq, 1), INT32_MIN, jnp.int32))
    thr = jnp.maximum(thr, _NEG_INF_KEY + 1)

    def bias_tile(j, c):
        off = pl.multiple_of(j * tk, tk)
        bias_ref[:, pl.ds(off, tk)] = jnp.where(key_ref[:, pl.ds(off, tk)] >= thr, 0.0, NEG_INF)
        return c

    lax.fori_loop(0, n_tiles, bias_tile, 0)


def _dsa_kernel(qi_ref, wi_ref, kidx_ref, q_ref, k_ref, v_ref, o_all_ref, o_ref,
                key_ref, bias_ref, qg_ref, m_ref, l_ref, acc_ref, *, pos0, topk, tk):
    del o_all_ref
    i = pl.program_id(1)
    tq = q_ref.shape[1]
    hd = acc_ref.shape[2]
    groups = acc_ref.shape[0]
    rep = q_ref.shape[2] // (groups * hd)
    first = pos0 + i * tq
    n_valid = (((first + tq - 1) >> CHUNK_SHIFT) + 1) * CHUNK
    n_tiles = (n_valid + tk - 1) // tk
    _dsa_select_bias(qi_ref, wi_ref, kidx_ref, key_ref, bias_ref, first=first, n_tiles=n_tiles, topk=topk, tk=tk)

    for g in range(groups):
        for r in range(rep):
            c0 = (g * rep + r) * hd
            qg_ref[g, r * tq:(r + 1) * tq, :] = q_ref[0, :, c0:c0 + hd]
    m_ref[...] = jnp.full(m_ref.shape, NEG_INF, jnp.float32)
    l_ref[...] = jnp.zeros(l_ref.shape, jnp.float32)
    acc_ref[...] = jnp.zeros(acc_ref.shape, jnp.float32)
    lane_reps = tk // LANE

    def att_tile(j, c):
        off = pl.multiple_of(j * tk, tk)
        b = bias_ref[:, pl.ds(off, tk)]
        bias = jnp.concatenate([b] * rep, axis=0)
        for g in range(groups):
            kt = k_ref[0, pl.ds(off, tk), g * hd:(g + 1) * hd]
            vt = v_ref[0, pl.ds(off, tk), g * hd:(g + 1) * hd]
            lg = lax.dot_general(qg_ref[g], kt, _NT_DIMS, preferred_element_type=jnp.float32) + bias
            m_old = m_ref[g]
            m_new = jnp.maximum(m_old, jnp.max(lg, axis=1, keepdims=True))
            m_safe = jnp.where(m_new == NEG_INF, 0.0, m_new)
            p = jnp.exp2(lg - jnp.tile(m_safe, (1, lane_reps)))
            alpha = jnp.exp2(m_old - m_safe)
            l_ref[g] = alpha * l_ref[g] + jnp.sum(p, axis=1, keepdims=True)
            acc_ref[g] = alpha * acc_ref[g] + jnp.dot(p.astype(jnp.bfloat16), vt,
                                                      preferred_element_type=jnp.float32)
            m_ref[g] = m_new
        return c

    lax.fori_loop(0, n_tiles, att_tile, 0)
    for g in range(groups):
        out = acc_ref[g] / l_ref[g]
        for r in range(rep):
            c0 = (g * rep + r) * hd
            o_ref[:, c0:c0 + hd] = out[r * tq:(r + 1) * tq].astype(o_ref.dtype)


def _dsa_core(q, qi3, wi, kb, vb, kidx3, pos0, topk, o_all, row0, *, tq, tk):
    B, T, qd = q.shape
    Sp = kb.shape[1]
    hd = qd // A_HEADS
    gw = qd // A_KV_HEADS
    nb = T // tq
    assert Sp % tk == 0 and hd == LANE and row0 % tq == 0
    blk0 = row0 // tq
    rows = (gw // hd) * tq
    kvw = A_KV_HEADS * hd
    return pl.pallas_call(
        functools.partial(_dsa_kernel, pos0=pos0, topk=topk, tk=tk),
        grid=(B, nb),
        in_specs=[pl.BlockSpec((1, 1, IDX_HEADS * tq, IDX_PACK), lambda b, i: (b, i, 0, 0)),
                  pl.BlockSpec((1, tq, IDX_HEADS), lambda b, i: (b, i, 0)),
                  pl.BlockSpec((1, Sp, IDX_PACK), lambda b, i: (b, 0, 0)),
                  pl.BlockSpec((1, tq, qd), lambda b, i: (b, i, 0)),
                  pl.BlockSpec((1, Sp, kvw), lambda b, i: (b, 0, 0)),
                  pl.BlockSpec((1, Sp, kvw), lambda b, i: (b, 0, 0)),
                  pl.BlockSpec(memory_space=pl.ANY)],
        out_specs=pl.BlockSpec((tq, qd), lambda b, i: (blk0 + b * nb + i, 0)),
        out_shape=jax.ShapeDtypeStruct(o_all.shape, o_all.dtype),
        input_output_aliases={6: 0},
        scratch_shapes=[pltpu.VMEM((tq, Sp), jnp.int32),
                        pltpu.VMEM((tq, Sp), jnp.float32),
                        pltpu.VMEM((A_KV_HEADS, rows, hd), jnp.bfloat16),
                        pltpu.VMEM((A_KV_HEADS, rows, LANE), jnp.float32),
                        pltpu.VMEM((A_KV_HEADS, rows, LANE), jnp.float32),
                        pltpu.VMEM((A_KV_HEADS, rows, hd), jnp.float32)],
        compiler_params=pltpu.CompilerParams(
            dimension_semantics=("parallel", "arbitrary"),
            vmem_limit_bytes=V7X_VMEM_LIMIT_BYTES),
        name="dsa_core",
    )(qi3, wi, kidx3, q, kb, vb, o_all)


def _rope_tables(pos, width, period):
    half = period // ROPE_FRACTION // 2
    inv_freq = ROPE_THETA ** (-jnp.arange(half, dtype=jnp.float32) / half)
    ang = pos.astype(jnp.float32)[:, None] * inv_freq[None, :]
    cos, sin = jnp.cos(ang), jnp.sin(ang)
    T = pos.shape[0]
    rest = period - 2 * half
    c = jnp.concatenate([cos, cos, jnp.ones((T, rest), jnp.float32)], axis=1)
    s_next = jnp.concatenate([-sin, jnp.zeros((T, period - half), jnp.float32)], axis=1)
    s_prev = jnp.concatenate([jnp.zeros((T, half), jnp.float32), sin, jnp.zeros((T, rest), jnp.float32)], axis=1)
    return jnp.stack([jnp.tile(t, (1, width // period)) for t in (c, s_next, s_prev)])


def _rope_lanes(x, tab_ref, half):
    return (x * tab_ref[0] + pltpu.roll(x, LANE - half, 1) * tab_ref[1] + pltpu.roll(x, half, 1) * tab_ref[2])


def _dsa_prep_kernel(q_ref, kv_ref, idx_ref, tq_ref, ti_ref,
                     qa_ref, kn_ref, vn_ref, kin_ref, kb_ref, vb_ref, qi3_ref, kidx3_ref, wi_ref, *, q_scale):
    tq = q_ref.shape[0]
    hd = LANE
    kvw = kv_ref.shape[1] // 2
    half_q = hd // ROPE_FRACTION // 2
    half_i = IDX_DIM // ROPE_FRACTION // 2
    left = lax.broadcasted_iota(jnp.int32, (tq, LANE), 1) < IDX_DIM
    zero = jnp.zeros((tq, LANE), jnp.float32)

    for h in range(q_ref.shape[1] // hd):
        cols = slice(h * hd, (h + 1) * hd)
        qa_ref[0, :, cols] = (_rope_lanes(q_ref[:, cols], tq_ref, half_q) * q_scale).astype(qa_ref.dtype)
    for h in range(kvw // hd):
        cols = slice(h * hd, (h + 1) * hd)
        k = _rope_lanes(kv_ref[:, cols], tq_ref, half_q)
        kn_ref[0, :, h, :] = k
        kb_ref[0, :, cols] = k.astype(kb_ref.dtype)
        vn_ref[0, :, h, :] = kv_ref[:, kvw + h * hd:kvw + (h + 1) * hd]
    vb_ref[0] = kv_ref[:, kvw:].astype(vb_ref.dtype)

    def hi_lo(x):
        hi = x.astype(jnp.bfloat16).astype(jnp.float32)
        return hi, x - hi

    for t in range(IDX_HEADS * IDX_DIM // LANE):
        hi, lo = hi_lo(_rope_lanes(idx_ref[:, t * LANE:(t + 1) * LANE], ti_ref, half_i))
        hi_sw, lo_sw = pltpu.roll(hi, IDX_DIM, 1), pltpu.roll(lo, IDX_DIM, 1)
        even = jnp.concatenate([jnp.where(left, hi, lo_sw), jnp.where(left, hi, zero)], axis=1)
        odd = jnp.concatenate([jnp.where(left, hi_sw, lo), jnp.where(left, hi_sw, zero)], axis=1)
        qi3_ref[0, 0, (2 * t) * tq:(2 * t + 1) * tq, :] = even.astype(qi3_ref.dtype)
        qi3_ref[0, 0, (2 * t + 1) * tq:(2 * t + 2) * tq, :] = odd.astype(qi3_ref.dtype)
    c0 = IDX_HEADS * IDX_DIM
    x = idx_ref[:, c0:c0 + LANE]
    r = jnp.where(left, _rope_lanes(x, ti_ref, half_i), x)
    kin_ref[0] = r[:, :IDX_DIM]
    hi, lo = hi_lo(r)
    kidx3 = jnp.concatenate([jnp.where(left, hi, pltpu.roll(hi, IDX_DIM, 1)), jnp.where(left, lo, zero)], axis=1)
    kidx3_ref[0] = kidx3.astype(kidx3_ref.dtype)
    wi_ref[0] = x[:, IDX_DIM:IDX_DIM + IDX_HEADS] * (IDX_HEADS ** -0.5 * IDX_DIM ** -0.5)


def _dsa_prep(q_all, kv_all, idx_all, row0, B, T, pos0, *, tq):
    qd, kv2 = A_HEADS * LANE, 2 * A_KV_HEADS * LANE
    kv_col = (kv_all.shape[1] - kv2) // kv2
    kvw = kv2 // 2
    nb = T // tq
    blk0 = row0 // tq
    assert row0 % tq == 0 and qd // A_HEADS == LANE
    pos = pos0 + jnp.arange(T, dtype=jnp.int32)
    tab_q = _rope_tables(pos, LANE, LANE)
    tab_i = _rope_tables(pos, LANE, IDX_DIM)
    row = lambda w: pl.BlockSpec((tq, w), lambda b, i: (blk0 + b * nb + i, 0))
    tab = pl.BlockSpec((3, tq, LANE), lambda b, i: (0, i, 0))
    out = lambda w: pl.BlockSpec((1, tq, w), lambda b, i: (b, i, 0))
    f32, bf16 = jnp.float32, jnp.bfloat16
    heads4 = pl.BlockSpec((1, tq, A_KV_HEADS, LANE), lambda b, i: (b, i, 0, 0))
    shapes = [((B, T, qd), bf16), ((B, T, A_KV_HEADS, LANE), f32), ((B, T, A_KV_HEADS, LANE), f32),
              ((B, T, IDX_DIM), f32),
              ((B, T, kvw), bf16), ((B, T, kvw), bf16), ((B, nb, IDX_HEADS * tq, IDX_PACK), bf16),
              ((B, T, IDX_PACK), bf16), ((B, T, IDX_HEADS), f32)]
    return pl.pallas_call(
        functools.partial(_dsa_prep_kernel, q_scale=LANE ** -0.5 * LOG2_E),
        grid=(B, nb),
        in_specs=[row(qd), pl.BlockSpec((tq, kv2), lambda b, i: (blk0 + b * nb + i, kv_col)),
                  row(idx_all.shape[1]), tab, tab],
        out_specs=[out(qd), heads4, heads4, out(IDX_DIM), out(kvw), out(kvw),
                   pl.BlockSpec((1, 1, IDX_HEADS * tq, IDX_PACK), lambda b, i: (b, i, 0, 0)),
                   out(IDX_PACK), out(IDX_HEADS)],
        out_shape=[jax.ShapeDtypeStruct(s, d) for s, d in shapes],
        compiler_params=pltpu.CompilerParams(
            dimension_semantics=("parallel", "parallel"), vmem_limit_bytes=V7X_VMEM_LIMIT_BYTES),
        name="dsa_prep",
    )(q_all, kv_all, idx_all, tab_q, tab_i)


def _dsa_layer(x, mod, gate, sid, streams, w_in, w_out):
    D = x.shape[1]
    hd = D // A_HEADS
    q_dim, kv_dim = A_HEADS * hd, A_KV_HEADS * hd
    n_idx = IDX_HEADS * IDX_DIM + IDX_DIM + IDX_HEADS
    w_main = _weight_parts(w_in[:, :q_dim + 2 * kv_dim], 1)
    w_idx = jnp.pad(w_in[:, q_dim + 2 * kv_dim:], ((0, 0), (0, _round_up(n_idx, LANE) - n_idx)))
    q_all = kv_all = pmatmul(x, w_main, tn=(q_dim + 2 * kv_dim) // 2, mod=mod, sid=sid)
    idx_all = pmatmul(x, _weight_parts(w_idx, 3), tn=w_idx.shape[1], mod=mod, sid=sid)
    caches = []
    o_all = jnp.zeros((x.shape[0], q_dim), jnp.bfloat16)
    for row0, B, T, pos0, past_k, past_v, past_kidx in streams:
        o_all, k, v, ki = _dsa_stream(q_all, kv_all, idx_all, row0, B, T, pos0, past_k, past_v, past_kidx, o_all)
        caches.append((k, v, ki))
    x = pmatmul(o_all, _weight_parts(w_out, 1), tn=2048, res=(x, gate), sid=sid)
    return x, caches


DSA_KEY_TILE = 512
DSA_QUERY_TILE = 256


def _dsa_stream(q_all, kv_all, idx_all, row0, B, T, pos0, past_k, past_v, past_kidx, o_all):
    tq = min(DSA_QUERY_TILE, T)
    q, k, v, ki, kb, vb, qi3, kidx3, wi = _dsa_prep(q_all, kv_all, idx_all, row0, B, T, pos0, tq=tq)
    n_keys = T
    if past_k is not None:
        P = past_k.shape[1]
        n_keys = P + T
        kb = jnp.concatenate([past_k.astype(jnp.bfloat16).reshape(B, P, -1), kb], axis=1)
        vb = jnp.concatenate([past_v.astype(jnp.bfloat16).reshape(B, P, -1), vb], axis=1)
        ph, pl_ = _split_bf16(past_kidx)
        kidx3 = jnp.concatenate([jnp.concatenate([ph, ph, pl_, jnp.zeros_like(ph)], axis=-1), kidx3], axis=1)
    pad = ((0, 0), (0, _round_up(n_keys, DSA_KEY_TILE) - n_keys), (0, 0))
    kb, vb, kidx3 = (jnp.pad(a, pad) for a in (kb, vb, kidx3))
    topk = min(TOPK_MAX, n_keys // 4)
    o_all = _dsa_core(q, qi3, wi, kb, vb, kidx3, pos0, topk, o_all, row0, tq=tq, tk=DSA_KEY_TILE)
    return o_all, k, v, ki


GDN_HEAD_GROUP = 32


def _bf16_dot(a, b):
    return jnp.dot(a.astype(jnp.bfloat16), b.astype(jnp.bfloat16), preferred_element_type=jnp.float32)


def _dot3(a, b):
    ah, al = _split_bf16(a)
    bh, bl = _split_bf16(b)
    out = jnp.dot(ah, bh, preferred_element_type=jnp.float32)
    out = out + jnp.dot(ah, bl, preferred_element_type=jnp.float32)
    return out + jnp.dot(al, bh, preferred_element_type=jnp.float32)


def _conv_silu(x_ref, w_ref, xe_ref):
    C = x_ref.shape[0]
    taps = w_ref.shape[0]
    xe_ref[SUBLANE:, :] = x_ref[...]
    first = SUBLANE - (taps - 1)
    acc = xe_ref[first:first + C, :] * w_ref[0:1, :]
    for j in range(1, taps):
        acc = acc + xe_ref[first + j:first + j + C, :] * w_ref[j:j + 1, :]
    xe_ref[:SUBLANE, :] = xe_ref[C:, :]
    return acc * jax.nn.sigmoid(acc)


def _gdn_kernel(xq_ref, xk_ref, xv_ref, wq_ref, wk_ref, wv_ref, cq_ref, ck_ref, cv_ref,
                z_ref, g_ref, gt_ref, beta_ref, nw_ref, s0_ref, o_all_ref,
                o_ref, s_out_ref, s_ref, eq_ref, ek_ref, ev_ref):
    del o_all_ref
    n = pl.program_id(2)
    C = xq_ref.shape[0]
    hg = g_ref.shape[3]
    dk = s_ref.shape[1]
    dv = s_ref.shape[2]
    rep = hg // (xk_ref.shape[1] // dk)

    @pl.when(n == 0)
    def _():
        s_ref[...] = s0_ref[0]
        for e_ref, c_ref in ((eq_ref, cq_ref), (ek_ref, ck_ref), (ev_ref, cv_ref)):
            e_ref[:SUBLANE, :] = jnp.zeros((SUBLANE, e_ref.shape[1]), jnp.float32)
            e_ref[SUBLANE - c_ref.shape[1]:SUBLANE, :] = c_ref[0]

    qc = _conv_silu(xq_ref, wq_ref, eq_ref)
    kc = _conv_silu(xk_ref, wk_ref, ek_ref)
    vc = _conv_silu(xv_ref, wv_ref, ev_ref)
    q_heads, k_heads = [], []
    for i in range(hg // rep):
        qh = qc[:, i * dk:(i + 1) * dk]
        kh = kc[:, i * dk:(i + 1) * dk]
        q_heads.append(qh * (lax.rsqrt(jnp.sum(qh * qh, axis=1, keepdims=True) + EPS) * dk ** -0.5))
        k_heads.append(kh * lax.rsqrt(jnp.sum(kh * kh, axis=1, keepdims=True) + EPS))

    ri = lax.broadcasted_iota(jnp.int32, (C, C), 0)
    ci = lax.broadcasted_iota(jnp.int32, (C, C), 1)
    causal = ri >= ci
    strict = ri > ci
    eye = jnp.where(ri == ci, 1.0, 0.0)
    g = g_ref[0, 0]
    gc_all = _dot3(jnp.where(causal, 1.0, 0.0), g)
    gr_all = _dot3(gt_ref[0, 0, 0], jnp.where(ri <= ci, 1.0, 0.0))
    beta = beta_ref[0, 0]
    nw = nw_ref[...]

    heads = range(hg)
    qs = [q_heads[h // rep] for h in heads]
    ks = [k_heads[h // rep] for h in heads]
    gcs = [gc_all[:, h:h + 1] for h in heads]
    bcols = [beta[:, h:h + 1] for h in heads]
    decays = [jnp.where(causal, jnp.exp(jnp.where(causal, gcs[h] - gr_all[h:h + 1, :], 0.0)), 0.0)
              for h in heads]
    kbs = [ks[h] * bcols[h] for h in heads]
    kks = [lax.dot_general(kbs[h].astype(jnp.bfloat16), ks[h].astype(jnp.bfloat16), _NT_DIMS,
                           preferred_element_type=jnp.float32) for h in heads]
    bms = [jnp.where(strict, -(kks[h] * decays[h]), 0.0) for h in heads]
    egs = [jnp.exp(gcs[h]) for h in heads]
    rhss = [jnp.concatenate([vc[:, h * dv:(h + 1) * dv] * bcols[h], kbs[h] * egs[h]], axis=1)
            for h in heads]
    pairs = range(hg // 2)
    left = lax.broadcasted_iota(jnp.int32, (C, 2 * C), 1) < C
    zero16 = jnp.zeros((C, 2 * C), jnp.bfloat16)

    def blockdiag(part):
        return jnp.concatenate([jnp.where(left, part, zero16), jnp.where(left, zero16, part)], axis=0)

    def dot3_pairs(x_parts, y_parts):
        xh, xl = x_parts
        yh, yl = blockdiag(y_parts[0]), blockdiag(y_parts[1])
        out = jnp.dot(xh, yh, preferred_element_type=jnp.float32)
        out = out + jnp.dot(xh, yl, preferred_element_type=jnp.float32)
        return out + jnp.dot(xl, yh, preferred_element_type=jnp.float32)

    b2 = [jnp.concatenate([bms[2 * i], bms[2 * i + 1]], axis=1) for i in pairs]
    eye2 = jnp.concatenate([eye, eye], axis=1)
    p2 = [eye2 + b2[i] for i in pairs]
    b2_parts = [_split_bf16(b2[i]) for i in pairs]
    step = 2
    while step < C:
        b2 = [dot3_pairs(b2_parts[i], b2_parts[i]) for i in pairs]
        b2_parts = [_split_bf16(b2[i]) for i in pairs]
        p2 = [p2[i] + dot3_pairs(_split_bf16(p2[i]), b2_parts[i]) for i in pairs]
        step *= 2
    ps = [p2[h // 2][:, (h % 2) * C:(h % 2 + 1) * C] for h in heads]
    ws = [_dot3(ps[h], rhss[h]) for h in heads]
    qks = [lax.dot_general(qs[h].astype(jnp.bfloat16), ks[h].astype(jnp.bfloat16), _NT_DIMS,
                           preferred_element_type=jnp.float32) * decays[h] for h in heads]
    g_lasts = [gcs[h][C - 1:C, :] for h in heads]
    ss = [s_ref[h] for h in heads]
    us = [ws[h][:, :dv] - _bf16_dot(ws[h][:, dv:], ss[h]) for h in heads]
    os_ = [_bf16_dot(qs[h] * egs[h], ss[h]) + _bf16_dot(qks[h], us[h]) for h in heads]
    for h in heads:
        ke = ks[h] * jnp.exp(g_lasts[h] - gcs[h])
        s_ref[h] = ss[h] * jnp.exp(g_lasts[h]) + _bf16_dot(ke.T, us[h])
    for h in heads:
        o = os_[h]
        o = o * lax.rsqrt(jnp.mean(o * o, axis=1, keepdims=True) + EPS) * nw
        z = z_ref[:, h * dv:(h + 1) * dv]
        o_ref[:, h * dv:(h + 1) * dv] = (o * (z * jax.nn.sigmoid(z))).astype(o_ref.dtype)

    @pl.when(n == pl.num_programs(2) - 1)
    def _():
        s_out_ref[0] = s_ref[...]


def _gdn_core(qkv, z, row0, conv_w, conv_state, g, beta, norm_w, s0, o_all):
    B, T, VH = g.shape
    dk, dv = s0.shape[2], s0.shape[3]
    vd = VH * dv
    qd = (conv_w.shape[1] - vd) // 2
    C = min(CHUNK, T)
    N = T // C
    assert row0 % C == 0
    blk0 = row0 // C
    hg = min(GDN_HEAD_GROUP, VH)
    ng = VH // hg
    qw = qd // ng
    vw = hg * dv
    assert qd % qw == 0 and (2 * qd) % vw == 0
    k0, v0 = qd // qw, (2 * qd) // vw
    z0 = (z.shape[1] - vd) // vw
    taps = conv_w.shape[0]

    def grouped(a):
        return a.reshape(B, T, ng, hg).transpose(0, 2, 1, 3)

    gg, bg = grouped(g), grouped(beta)
    gt = gg.reshape(B, ng, N, C, hg).transpose(0, 1, 2, 4, 3)
    small = pl.BlockSpec((1, 1, C, hg), lambda b, j, n: (b, j, n, 0))
    o, s = pl.pallas_call(
        _gdn_kernel,
        grid=(B, ng, N),
        in_specs=[pl.BlockSpec((C, qw), lambda b, j, n: (blk0 + b * N + n, j)),
                  pl.BlockSpec((C, qw), lambda b, j, n: (blk0 + b * N + n, k0 + j)),
                  pl.BlockSpec((C, vw), lambda b, j, n: (blk0 + b * N + n, v0 + j)),
                  pl.BlockSpec((taps, qw), lambda b, j, n: (0, j)),
                  pl.BlockSpec((taps, qw), lambda b, j, n: (0, k0 + j)),
                  pl.BlockSpec((taps, vw), lambda b, j, n: (0, v0 + j)),
                  pl.BlockSpec((1, taps - 1, qw), lambda b, j, n: (b, 0, j)),
                  pl.BlockSpec((1, taps - 1, qw), lambda b, j, n: (b, 0, k0 + j)),
                  pl.BlockSpec((1, taps - 1, vw), lambda b, j, n: (b, 0, v0 + j)),
                  pl.BlockSpec((C, vw), lambda b, j, n: (blk0 + b * N + n, z0 + j)),
                  small,
                  pl.BlockSpec((1, 1, 1, hg, C), lambda b, j, n: (b, j, n, 0, 0)),
                  small,
                  pl.BlockSpec((1, dv), lambda b, j, n: (0, 0)),
                  pl.BlockSpec((1, hg, dk, dv), lambda b, j, n: (b, j, 0, 0)),
                  pl.BlockSpec(memory_space=pl.ANY)],
        out_specs=[pl.BlockSpec((C, hg * dv), lambda b, j, n: (blk0 + b * N + n, j)),
                   pl.BlockSpec((1, hg, dk, dv), lambda b, j, n: (b, j, 0, 0))],
        out_shape=[jax.ShapeDtypeStruct(o_all.shape, o_all.dtype),
                   jax.ShapeDtypeStruct(s0.shape, jnp.float32)],
        input_output_aliases={15: 0},
        scratch_shapes=[pltpu.VMEM((hg, dk, dv), jnp.float32),
                        pltpu.VMEM((SUBLANE + C, qw), jnp.float32),
                        pltpu.VMEM((SUBLANE + C, qw), jnp.float32),
                        pltpu.VMEM((SUBLANE + C, vw), jnp.float32)],
        compiler_params=pltpu.CompilerParams(
            dimension_semantics=("parallel", "parallel", "arbitrary"),
            vmem_limit_bytes=V7X_VMEM_LIMIT_BYTES),
        name="gdn_core",
    )(qkv, qkv, qkv, conv_w, conv_w, conv_w, conv_state, conv_state, conv_state,
      z, gg, gt, bg, norm_w.reshape(1, dv), s0, o_all)
    return o, s


def _gdn_layer(x, mod, gate, sid, streams, w_in, conv_w, a_log, dt_bias, norm_w, w_out):
    qk_dim, v_dim = GDN_QK_HEADS * GDN_DK, GDN_V_HEADS * GDN_DV
    conv_dim = 2 * qk_dim + v_dim
    n_gate = 2 * GDN_V_HEADS
    w_main = _weight_parts(w_in[:, :conv_dim + v_dim], 1)
    w_gate = jnp.pad(w_in[:, conv_dim + v_dim:], ((0, 0), (0, _round_up(n_gate, LANE) - n_gate)))
    qkv_all = z_all = pmatmul(x, w_main, tn=2048, mod=mod, sid=sid)
    gates_all = pmatmul(x, _weight_parts(w_gate, 1), tn=w_gate.shape[1], mod=mod, sid=sid)
    states = []
    o_all = jnp.zeros((x.shape[0], v_dim), jnp.bfloat16)
    for row0, B, T, conv_state, ssm_state in streams:
        rows = slice(row0, row0 + B * T)
        gates = gates_all[rows].reshape(B, T, -1)
        beta_raw, a_raw = gates[..., :GDN_V_HEADS], gates[..., GDN_V_HEADS:n_gate]
        if conv_state is None:
            conv_state = jnp.zeros((B, CONV_W - 1, conv_dim), x.dtype)
        if ssm_state is None:
            ssm_state = jnp.zeros((B, GDN_V_HEADS, GDN_DK, GDN_DV), jnp.float32)
        tail = jnp.stack([qkv_all[row0 + (b + 1) * T - (CONV_W - 1):row0 + (b + 1) * T, :conv_dim]
                          for b in range(B)])
        new_conv = jnp.concatenate([conv_state, tail], axis=1)[:, -(CONV_W - 1):]
        beta = jax.nn.sigmoid(beta_raw)
        g = -jnp.exp(a_log) * jax.nn.softplus(a_raw + dt_bias)
        o_all, S = _gdn_core(qkv_all, z_all, row0, conv_w, conv_state, g, beta, norm_w, ssm_state, o_all)
        states.append((new_conv, S))
    x = pmatmul(o_all, _weight_parts(w_out, 1), res=(x, gate), sid=sid)
    return x, states


NEG_INF = float("-inf")
_PEER_CAND_ROWS = tuple((i, PEER_TOPK // (i + 1)) for i in range(PEER_TOPK // 2))


def _split_bf16(x):
    hi = x.astype(jnp.bfloat16)
    lo = (x - hi.astype(jnp.float32)).astype(jnp.bfloat16)
    return hi, lo


def _dot3_nt(a, b):
    dn = (((1,), (1,)), ((), ()))
    ah, al = _split_bf16(a)
    bh, bl = _split_bf16(b)
    out = lax.dot_general(ah, bh, dn, preferred_element_type=jnp.float32)
    out = out + lax.dot_general(ah, bl, dn, preferred_element_type=jnp.float32)
    return out + lax.dot_general(al, bh, dn, preferred_element_type=jnp.float32)


def _top_rows_desc(s, n, with_rank=False):
    rows = []
    cur = s
    rank = jnp.full(s.shape, float(n), jnp.float32)
    for i in range(n):
        m = jnp.max(cur, axis=0, keepdims=True)
        rows.append(m)
        hit = cur == m
        if with_rank:
            rank = jnp.where(hit, float(i), rank)
        cur = jnp.where(hit, NEG_INF, cur)
    return (rows, rank) if with_rank else rows


def _stack_rows(rows, lanes):
    n = len(rows)
    rid = lax.broadcasted_iota(jnp.int32, (n, lanes), 0)
    out = jnp.zeros((n, lanes), jnp.float32)
    for i, r in enumerate(rows):
        out = jnp.where(rid == i, r, out)
    return out


def _peer_select_kernel(q_ref, keys_ref, rk2_ref, e2_ref, cnt_ref, e1_ref):
    tm = q_ref.shape[0]
    k = PEER_TOPK
    s1 = _dot3_nt(keys_ref[0], q_ref[:, :PEER_HALF])
    s2 = _dot3_nt(keys_ref[1], q_ref[:, PEER_HALF:])
    r1 = _top_rows_desc(s1, k)
    r2, rank2 = _top_rows_desc(s2, k, with_rank=True)
    v1 = _stack_rows(r1, tm)
    v2 = _stack_rows(r2, tm)
    v2h = v2[:k // 2]
    rid = lax.broadcasted_iota(jnp.int32, (k // 2, tm), 0)
    pieces = [r1[0] + v2]
    for i, n in _PEER_CAND_ROWS[1:]:
        pieces.append(jnp.where(rid < n, r1[i] + v2h, NEG_INF))
    pieces.append(v1[k // 2:] + r2[0])
    cand = jnp.concatenate(pieces, axis=0)
    tau = _top_rows_desc(cand, k)[-1]
    top = r1[0] + r2[0]
    z = jnp.sum(jnp.where(cand >= tau, jnp.exp(cand - top), 0.0), axis=0, keepdims=True)
    cnt = jnp.zeros(s1.shape, jnp.float32)
    for j in range(k):
        cnt = cnt + jnp.where(s1 + r2[j] >= tau, 1.0, 0.0)
    rk2_ref[0] = pltpu.bitcast(rank2.astype(jnp.bfloat16), rk2_ref.dtype)
    e2_ref[0] = pltpu.bitcast((jnp.exp(s2 - r2[0]) / z).astype(jnp.bfloat16), e2_ref.dtype)
    cnt_ref[0] = cnt
    e1_ref[0] = jnp.exp(s1 - r1[0])


def _peer_select(q, keys, *, tm):
    M = q.shape[0]
    nk = keys.shape[1]
    ospec = pl.BlockSpec((1, nk, tm), lambda i, h: (h, 0, i))
    pspec = pl.BlockSpec((1, nk // 2, tm), lambda i, h: (h, 0, i))
    return pl.pallas_call(
        _peer_select_kernel,
        grid=(M // tm, PEER_HEADS),
        in_specs=[pl.BlockSpec((tm, PEER_KEY_DIM), lambda i, h: (i, h)),
                  pl.BlockSpec(keys.shape, lambda i, h: (0, 0, 0))],
        out_specs=[pspec, pspec, ospec, ospec],
        out_shape=[jax.ShapeDtypeStruct((PEER_HEADS, nk // 2, M), jnp.int32)] * 2
        + [jax.ShapeDtypeStruct((PEER_HEADS, nk, M), jnp.float32)] * 2,
        compiler_params=pltpu.CompilerParams(
            dimension_semantics=("parallel", "arbitrary"),
            vmem_limit_bytes=V7X_VMEM_LIMIT_BYTES),
        name="peer_select",
    )(q, keys)


BF16_SUBLANES = 16
PEER_SUB_SPLIT = (1, 1)


def _peer_main_kernel(sid_ref, x_ref, gain_ref, shift_ref, scale_ref, gate_ref, fnorm_ref,
                      u_ref, vT_ref, rk2_ref, e2_ref, cnt_ref, e1_ref, *refs, final_norm):
    out_refs, (hT_ref, coef_ref, acc_ref) = refs[:-3], refs[-3:]
    e = pl.program_id(1)
    te, tm = coef_ref.shape
    nk = 2 * rk2_ref.shape[1]
    rows = BF16_SUBLANES
    blk0 = pl.program_id(0) * (tm // SEQ_BLOCK)

    @pl.when(e == 0)
    def _():
        acc_ref[...] = jnp.zeros_like(acc_ref)
        per_lane_tile = LANE // SEQ_BLOCK
        for t in range(tm // LANE):
            hs = []
            for r in range(t * per_lane_tile, (t + 1) * per_lane_tile):
                rs = slice(r * SEQ_BLOCK, (r + 1) * SEQ_BLOCK)
                hs.append(_modulated_rows(x_ref[rs, :], gain_ref[...], shift_ref, scale_ref, sid_ref[blk0 + r]))
            hT_ref[:, t * LANE:(t + 1) * LANE] = jnp.concatenate(hs, axis=0).T.astype(jnp.bfloat16)

    def expert_acts(span):
        off, size = span
        return jnp.dot(u_ref[off:off + size, :], hT_ref[...], preferred_element_type=jnp.float32)

    def weigh(span, act):
        off, size = span
        for al in range(size // nk):
            a = off // nk + al
            for lg in range(tm // LANE):
                lanes = slice(lg * LANE, (lg + 1) * LANE)
                cnts = [jnp.broadcast_to(cnt_ref[h, a:a + 1, lanes], (rows, LANE)).astype(jnp.bfloat16)
                        for h in range(PEER_HEADS)]
                e1s = [jnp.broadcast_to(e1_ref[h, a:a + 1, lanes], (rows, LANE)).astype(jnp.bfloat16)
                       for h in range(PEER_HEADS)]
                for r in range(nk // rows):
                    words = slice(r * rows // 2, (r + 1) * rows // 2)
                    w = None
                    for h in range(PEER_HEADS):
                        e2 = pltpu.bitcast(e2_ref[h, words, lanes], jnp.bfloat16)
                        rk2 = pltpu.bitcast(rk2_ref[h, words, lanes], jnp.bfloat16)
                        t = jnp.where(rk2 < cnts[h], e2, jnp.zeros_like(e2)) * e1s[h]
                        w = t if w is None else w + t
                    x = act[al * nk + r * rows:al * nk + (r + 1) * rows, lanes]
                    g = 0.5 * x * (1.0 + lax.erf(x * (2.0 ** -0.5)))
                    coef_ref[off + al * nk + r * rows:off + al * nk + (r + 1) * rows, lanes] = (
                        w * g.astype(jnp.bfloat16))

    def accumulate(span):
        off, size = span
        acc_ref[...] += lax.dot_general(vT_ref[off:off + size, :], coef_ref[off:off + size, :],
                                        (((0,), (0,)), ((), ())), preferred_element_type=jnp.float32)

    spans, off = [], 0
    for frac in PEER_SUB_SPLIT:
        spans.append((off, te * frac // sum(PEER_SUB_SPLIT)))
        off += spans[-1][1]
    act = expert_acts(spans[0])
    for i, span in enumerate(spans):
        nxt = expert_acts(spans[i + 1]) if i + 1 < len(spans) else None
        weigh(span, act)
        accumulate(span)
        act = nxt

    @pl.when(e == pl.num_programs(1) - 1)
    def _():
        out = acc_ref[...].T
        for r in range(tm // SEQ_BLOCK):
            rs = slice(r * SEQ_BLOCK, (r + 1) * SEQ_BLOCK)
            y = x_ref[rs, :] + gate_ref[pl.ds(sid_ref[blk0 + r], 1), :] * out[rs, :]
            out_refs[0][rs, :] = y
            if final_norm:
                out_refs[1][rs, :] = y * lax.rsqrt(jnp.mean(y * y, axis=-1, keepdims=True) + EPS) * fnorm_ref[...]


def _peer_main(x, mod, gate, fnorm, sid, u, vT, rk2, e2, cnt, e1, *, tm, te, final_norm):
    M, D = x.shape
    E = u.shape[0]
    nk = cnt.shape[1]
    const = lambda a: pl.BlockSpec(a.shape, lambda i, e, s: (0,) * a.ndim)
    col_spec = pl.BlockSpec((PEER_HEADS, nk // 2, tm), lambda i, e, s: (0, 0, i))
    row_spec = pl.BlockSpec((PEER_HEADS, te // nk, tm), lambda i, e, s: (0, e, i))
    tok_spec = pl.BlockSpec((tm, D), lambda i, e, s: (i, 0))
    n_out = 2 if final_norm else 1
    return pl.pallas_call(
        functools.partial(_peer_main_kernel, final_norm=final_norm),
        grid_spec=pltpu.PrefetchScalarGridSpec(
            num_scalar_prefetch=1,
            grid=(M // tm, E // te),
            in_specs=[tok_spec, const(mod[0]), const(mod[1]), const(mod[2]), const(gate), const(fnorm),
                      pl.BlockSpec((te, D), lambda i, e, s: (e, 0)),
                      pl.BlockSpec((te, D), lambda i, e, s: (e, 0)),
                      col_spec, col_spec, row_spec, row_spec],
            out_specs=[tok_spec] * n_out,
            scratch_shapes=[pltpu.VMEM((D, tm), jnp.bfloat16),
                            pltpu.VMEM((te, tm), jnp.bfloat16),
                            pltpu.VMEM((D, tm), jnp.float32)]),
        out_shape=[jax.ShapeDtypeStruct((M, D), jnp.float32)] * n_out,
        compiler_params=pltpu.CompilerParams(
            dimension_semantics=("parallel", "arbitrary"),
            vmem_limit_bytes=V7X_VMEM_LIMIT_PEER_BYTES),
        name="peer_main",
    )(sid, x, *mod, gate, fnorm, u, vT, rk2, e2, cnt, e1)


def _peer(x, mod, gate, fnorm, sid, w_query, sub_keys, expert_u, expert_v, *, final_norm,
          tm_sel=256, tm=512, te=1024):
    M = x.shape[0]
    q = pmatmul(x, _weight_parts(w_query, 1), tn=2048, mod=mod, sid=sid)
    rk2, e2, cnt, e1 = _peer_select(q, sub_keys, tm=min(tm_sel, M))
    u = expert_u.astype(jnp.bfloat16)
    vT = expert_v.astype(jnp.bfloat16)
    return _peer_main(x, mod, gate, fnorm, sid, u, vT, rk2, e2, cnt, e1, tm=min(tm, M), te=te,
                      final_norm=final_norm)


def kernel(x_prompt, x_sample, c_prompt, c_sample, cache_k_l0, cache_v_l0, cache_kidx_l0, state_conv_l1, state_ssm_l1, norm1_l0, norm2_l0, ada_w_l0, ada_b_l0, attn_in_l0, attn_out_l0, peer_query_l0, peer_keys_l0, peer_u_l0, peer_v_l0, norm1_l1, norm2_l1, ada_w_l1, ada_b_l1, gdn_in_l1, gdn_conv_l1, gdn_a_log_l1, gdn_dt_bias_l1, gdn_norm_l1, gdn_out_l1, peer_query_l1, peer_keys_l1, peer_u_l1, peer_v_l1, final_norm):
    past_len = cache_k_l0.shape[1]
    norm1 = (norm1_l0, norm1_l1)
    norm2 = (norm2_l0, norm2_l1)
    ada_w = (ada_w_l0, ada_w_l1)
    ada_b = (ada_b_l0, ada_b_l1)
    peer_query = (peer_query_l0, peer_query_l1)
    peer_keys = (peer_keys_l0, peer_keys_l1)
    peer_u = (peer_u_l0, peer_u_l1)
    peer_v = (peer_v_l0, peer_v_l1)
    Bp, Tp, D = x_prompt.shape
    Bs, Ts, _ = x_sample.shape
    n_p, n_s = Bp * Tp, Bs * Ts
    x = jnp.concatenate([x_prompt.reshape(n_p, D), x_sample.reshape(n_s, D)], axis=0)
    sid = jnp.concatenate([jnp.repeat(jnp.arange(Bp, dtype=jnp.int32), Tp // SEQ_BLOCK),
                           Bp + jnp.repeat(jnp.arange(Bs, dtype=jnp.int32), Ts // SEQ_BLOCK)])
    c_all = jnp.concatenate([c_prompt, c_sample], axis=0)
    fnorm = final_norm.reshape(1, D)
    for i in range(2):
        mods = _adaln(c_all, ada_w[i], ada_b[i])
        mod1 = (norm1[i].reshape(1, D), mods[0], mods[1])
        if i == 0:
            x, ((nkp, nvp, nkip), (nks, nvs, nkis)) = _dsa_layer(
                x, mod1, mods[2], sid,
                [(0, Bp, Tp, 0, None, None, None), (n_p, Bs, Ts, past_len, cache_k_l0, cache_v_l0, cache_kidx_l0)],
                attn_in_l0, attn_out_l0)
        else:
            x, ((ncp, nsp), (ncs, nss)) = _gdn_layer(
                x, mod1, mods[2], sid, [(0, Bp, Tp, None, None), (n_p, Bs, Ts, state_conv_l1, state_ssm_l1)],
                gdn_in_l1, gdn_conv_l1, gdn_a_log_l1, gdn_dt_bias_l1, gdn_norm_l1, gdn_out_l1)
        mod2 = (norm2[i].reshape(1, D), mods[3], mods[4])
        outs = _peer(x, mod2, mods[5], fnorm, sid, peer_query[i], peer_keys[i], peer_u[i], peer_v[i],
                     final_norm=(i == 1))
        x = outs[0]
    y = outs[1]
    y_prompt = y[:n_p].reshape(Bp, Tp, D)
    y_sample = y[n_p:].reshape(Bs, Ts, D)
    return (y_prompt, y_sample, nkp, nvp, nkip, nks, nvs, nkis, ncp, nsp, ncs, nss)
```

```python
import functools

import jax
import jax.numpy as jnp
from jax import lax
from jax.experimental import pallas as pl
from jax.experimental.pallas import tpu as pltpu

CHUNK = 64
CHUNK_SHIFT = CHUNK.bit_length() - 1
EPS = 1e-6
ROPE_THETA = 500000.0
ROPE_FRACTION = 4
A_HEADS = 16
A_KV_HEADS = 4
IDX_HEADS = 8
IDX_DIM = 64
TOPK_MAX = 256
Q_BLOCK = 128
GDN_QK_HEADS = 16
GDN_V_HEADS = 32
GDN_DK = 128
GDN_DV = 128
CONV_W = 4
PEER_HEADS = 8
PEER_KEY_DIM = 256
PEER_HALF = PEER_KEY_DIM // 2
PEER_TOPK = 16

V7X_VMEM_LIMIT_BYTES = 48 * 1024 * 1024
V7X_VMEM_LIMIT_PEER_BYTES = 56 * 1024 * 1024
LANE = 128
SUBLANE = 8


def _round_up(n, m):
    return (n + m - 1) // m * m


SEQ_BLOCK = CHUNK


def _modulated_rows(x, gain, shift_ref, scale_ref, sid):
    y = x * lax.rsqrt(jnp.mean(x * x, axis=-1, keepdims=True) + EPS)
    return y * gain * (1.0 + scale_ref[pl.ds(sid, 1), :]) + shift_ref[pl.ds(sid, 1), :]


def _matmul_kernel(sid_ref, x_ref, *refs, passes, has_mod, has_res):
    refs = list(refs)
    mod_refs = [refs.pop(0) for _ in range(3)] if has_mod else None
    nparts = 1 + passes // 2
    w_refs = [refs.pop(0) for _ in range(nparts)]
    res_refs = [refs.pop(0) for _ in range(2)] if has_res else None
    o_ref, x_parts = refs[0], refs[1:]
    tm = x_ref.shape[0]
    blocks = range(tm // SEQ_BLOCK) if (has_mod or has_res) else ()
    blk0 = pl.program_id(0) * (tm // SEQ_BLOCK)

    @pl.when(pl.program_id(1) == 0)
    def _():
        def put(rows, x):
            hi = x.astype(jnp.bfloat16)
            x_parts[0][rows, :] = hi
            if passes == 3:
                x_parts[1][rows, :] = (x - hi.astype(jnp.float32)).astype(jnp.bfloat16)

        if has_mod:
            gain_ref, shift_ref, scale_ref = mod_refs
            for r in blocks:
                rows = slice(r * SEQ_BLOCK, (r + 1) * SEQ_BLOCK)
                put(rows, _modulated_rows(x_ref[rows, :], gain_ref[...], shift_ref, scale_ref, sid_ref[blk0 + r]))
        else:
            put(slice(None), x_ref[...].astype(jnp.float32))

    acc = jnp.dot(x_parts[0][...], w_refs[0][...], preferred_element_type=jnp.float32)
    if passes == 3:
        acc = acc + jnp.dot(x_parts[0][...], w_refs[1][...], preferred_element_type=jnp.float32)
        acc = acc + jnp.dot(x_parts[1][...], w_refs[0][...], preferred_element_type=jnp.float32)
    if has_res:
        res_ref, gate_ref = res_refs
        for r in blocks:
            rows = slice(r * SEQ_BLOCK, (r + 1) * SEQ_BLOCK)
            gate = gate_ref[pl.ds(sid_ref[blk0 + r], 1), :]
            o_ref[rows, :] = res_ref[rows, :] + gate * acc[rows, :]
    else:
        o_ref[...] = acc.astype(o_ref.dtype)


def _weight_parts(w, passes):
    hi = w.astype(jnp.bfloat16)
    if passes == 1:
        return (hi,)
    return (hi, (w - hi.astype(jnp.float32)).astype(jnp.bfloat16))


def pmatmul(x, w_parts, *, col0=0, ncols=None, tm=512, tn=1024, out_dtype=jnp.float32,
            mod=None, res=None, sid=None):
    passes = 1 if len(w_parts) == 1 else 3
    M, K = x.shape
    n_total = w_parts[0].shape[1]
    ncols = n_total - col0 if ncols is None else ncols
    tm = min(tm, _round_up(M, 2 * SUBLANE))
    tn = min(tn, ncols)
    assert ncols % tn == 0 and col0 % tn == 0 and (tn % LANE == 0 or tn == n_total)
    Mp = _round_up(M, tm)
    if mod is not None or res is not None:
        assert Mp == M and tm % SEQ_BLOCK == 0 and sid is not None
    else:
        sid = jnp.zeros((1,), jnp.int32)
    if Mp != M:
        x = jnp.pad(x, ((0, Mp - M), (0, 0)))
    c0 = col0 // tn
    in_specs = [pl.BlockSpec((tm, K), lambda i, j, s: (i, 0))]
    args = [x]
    if mod is not None:
        in_specs += [pl.BlockSpec(a.shape, lambda i, j, s: (0, 0)) for a in mod]
        args += list(mod)
    in_specs += [pl.BlockSpec((K, tn), lambda i, j, s: (0, c0 + j))] * len(w_parts)
    args += list(w_parts)
    if res is not None:
        in_specs += [pl.BlockSpec((tm, tn), lambda i, j, s: (i, j)),
                     pl.BlockSpec((res[1].shape[0], tn), lambda i, j, s: (0, j))]
        args += list(res)
        out_dtype = jnp.float32
    out = pl.pallas_call(
        functools.partial(_matmul_kernel, passes=passes, has_mod=mod is not None, has_res=res is not None),
        grid_spec=pltpu.PrefetchScalarGridSpec(
            num_scalar_prefetch=1,
            grid=(Mp // tm, ncols // tn),
            in_specs=in_specs,
            out_specs=pl.BlockSpec((tm, tn), lambda i, j, s: (i, j)),
            scratch_shapes=[pltpu.VMEM((tm, K), jnp.bfloat16)] * (1 + passes // 2)),
        out_shape=jax.ShapeDtypeStruct((Mp, ncols), out_dtype),
        compiler_params=pltpu.CompilerParams(
            dimension_semantics=("parallel", "arbitrary"),
            vmem_limit_bytes=V7X_VMEM_LIMIT_BYTES),
        name="matmul",
    )(sid, *args)
    return out[:M]


def _adaln_kernel(c_ref, w_ref, b_ref, o_ref):
    c = c_ref[...]
    o_ref[...] = _dot3(c * jax.nn.sigmoid(c), w_ref[...]) + b_ref[...]


def _adaln(c, w, b, *, tn=512):
    n, D = c.shape
    N = w.shape[1]
    rows = _round_up(n, SUBLANE)
    mod = pl.pallas_call(
        _adaln_kernel,
        grid=(N // tn,),
        in_specs=[pl.BlockSpec((rows, D), lambda j: (0, 0)),
                  pl.BlockSpec((D, tn), lambda j: (0, j)),
                  pl.BlockSpec((1, tn), lambda j: (0, j))],
        out_specs=pl.BlockSpec((rows, tn), lambda j: (0, j)),
        out_shape=jax.ShapeDtypeStruct((rows, N), jnp.float32),
        compiler_params=pltpu.CompilerParams(
            dimension_semantics=("parallel",), vmem_limit_bytes=V7X_VMEM_LIMIT_BYTES),
        name="adaln",
    )(jnp.pad(c, ((0, rows - n), (0, 0))), w, b.reshape(1, N))
    return jnp.split(mod, 6, axis=-1)


_NT_DIMS = (((1,), (1,)), ((), ()))
INT32_MIN = -2 ** 31
LOG2_E = 1.4426950408889634
_NEG_INF_KEY = -2139095041
IDX_PACK = 4 * IDX_DIM


def _ordered_key(x):
    bits = pltpu.bitcast(x, jnp.int32)
    return bits ^ ((bits >> 31) & 0x7FFFFFFF)


def _lane_tile_sum(x, width=LANE):
    out = x[:, :width]
    for c in range(1, x.shape[1] // width):
        out = out + x[:, c * width:(c + 1) * width]
    return out


def _dsa_select_bias(qi_ref, wi_ref, kidx_ref, key_ref, bias_ref, *, first, n_tiles, topk, tk):
    tq = wi_ref.shape[1]
    row = lax.broadcasted_iota(jnp.int32, (tq, 1), 0)
    lim = (((first + row) >> CHUNK_SHIFT) + 1) * CHUNK
    w = wi_ref[0]

    def score_tile(j, c):
        off = pl.multiple_of(j * tk, tk)
        kt = kidx_ref[0, pl.ds(off, tk), :]
        sc = lax.dot_general(qi_ref[0, 0], kt, _NT_DIMS,
                             preferred_element_type=jnp.float32)
        s = jnp.zeros((tq, tk), jnp.float32)
        for h in range(IDX_HEADS):
            s = s + w[:, h:h + 1] * jnp.maximum(sc[h * tq:(h + 1) * tq], 0.0)
        col = off + lax.broadcasted_iota(jnp.int32, (tq, tk), 1)
        s = jnp.where(col < lim, s + 0.0, NEG_INF)
        key_ref[:, pl.ds(off, tk)] = _ordered_key(s)
        return c

    lax.fori_loop(0, n_tiles, score_tile, 0)

    def bit_step(b, thr):
        cand = thr + lax.shift_left(jnp.int32(1), 31 - b)

        def count_tile(j, c):
            off = pl.multiple_of(j * tk, tk)
            ge = jnp.where(key_ref[:, pl.ds(off, tk)] >= cand, 1.0, 0.0)
            return c + _lane_tile_sum(ge)

        c = lax.fori_loop(0, n_tiles, count_tile, jnp.zeros((tq, LANE), jnp.float32))
        cnt = jnp.sum(c, axis=1, keepdims=True)
        return jnp.where(cnt >= topk, cand, thr)

    thr = lax.fori_loop(0, 32, bit_step, jnp.full((tq, 1), INT32_MIN, jnp.int32))
    thr = jnp.maximum(thr, _NEG_INF_KEY + 1)

    def bias_tile(j, c):
        off = pl.multiple_of(j * tk, tk)
        bias_ref[:, pl.ds(off, tk)] = jnp.where(key_ref[:, pl.ds(off, tk)] >= thr, 0.0, NEG_INF)
        return c

    lax.fori_loop(0, n_tiles, bias_tile, 0)


def _dsa_kernel(qi_ref, wi_ref, kidx_ref, q_ref, k_ref, v_ref, o_all_ref, o_ref,
                key_ref, bias_ref, qg_ref, m_ref, l_ref, acc_ref, *, pos0, topk, tk):
    del o_all_ref
    i = pl.program_id(1)
    tq = q_ref.shape[1]
    hd = acc_ref.shape[2]
    groups = acc_ref.shape[0]
    rep = q_ref.shape[2] // (groups * hd)
    first = pos0 + i * tq
    n_valid = (((first + tq - 1) >> CHUNK_SHIFT) + 1) * CHUNK
    n_tiles = (n_valid + tk - 1) // tk
    _dsa_select_bias(qi_ref, wi_ref, kidx_ref, key_ref, bias_ref, first=first, n_tiles=n_tiles, topk=topk, tk=tk)

    for g in range(groups):
        for r in range(rep):
            c0 = (g * rep + r) * hd
            qg_ref[g, r * tq:(r + 1) * tq, :] = q_ref[0, :, c0:c0 + hd]
    m_ref[...] = jnp.full(m_ref.shape, NEG_INF, jnp.float32)
    l_ref[...] = jnp.zeros(l_ref.shape, jnp.float32)
    acc_ref[...] = jnp.zeros(acc_ref.shape, jnp.float32)
    lane_reps = tk // LANE

    def att_tile(j, c):
        off = pl.multiple_of(j * tk, tk)
        b = bias_ref[:, pl.ds(off, tk)]
        bias = jnp.concatenate([b] * rep, axis=0)
        for g in range(groups):
            kt = k_ref[0, pl.ds(off, tk), g * hd:(g + 1) * hd]
            vt = v_ref[0, pl.ds(off, tk), g * hd:(g + 1) * hd]
            lg = lax.dot_general(qg_ref[g], kt, _NT_DIMS, preferred_element_type=jnp.float32) + bias
            m_old = m_ref[g]
            m_new = jnp.maximum(m_old, jnp.max(lg, axis=1, keepdims=True))
            m_safe = jnp.where(m_new == NEG_INF, 0.0, m_new)
            p = jnp.exp2(lg - jnp.tile(m_safe, (1, lane_reps)))
            alpha = jnp.exp2(m_old - m_safe)
            l_ref[g] = alpha * l_ref[g] + jnp.sum(p, axis=1, keepdims=True)
            acc_ref[g] = alpha * acc_ref[g] + jnp.dot(p.astype(jnp.bfloat16), vt,
                                                      preferred_element_type=jnp.float32)
            m_ref[g] = m_new
        return c

    lax.fori_loop(0, n_tiles, att_tile, 0)
    for g in range(groups):
        out = acc_ref[g] / l_ref[g]
        for r in range(rep):
            c0 = (g * rep + r) * hd
            o_ref[:, c0:c0 + hd] = out[r * tq:(r + 1) * tq].astype(o_ref.dtype)


def _dsa_core(q, qi3, wi, kb, vb, kidx3, pos0, topk, o_all, row0, *, tq, tk):
    B, T, qd = q.shape
    Sp = kb.shape[1]
    hd = qd // A_HEADS
    gw = qd // A_KV_HEADS
    nb = T // tq
    assert Sp % tk == 0 and hd == LANE and row0 % tq == 0
    blk0 = row0 // tq
    rows = (gw // hd) * tq
    kvw = A_KV_HEADS * hd
    return pl.pallas_call(
        functools.partial(_dsa_kernel, pos0=pos0, topk=topk, tk=tk),
        grid=(B, nb),
        in_specs=[pl.BlockSpec((1, 1, IDX_HEADS * tq, IDX_PACK), lambda b, i: (b, i, 0, 0)),
                  pl.BlockSpec((1, tq, IDX_HEADS), lambda b, i: (b, i, 0)),
                  pl.BlockSpec((1, Sp, IDX_PACK), lambda b, i: (b, 0, 0)),
                  pl.BlockSpec((1, tq, qd), lambda b, i: (b, i, 0)),
                  pl.BlockSpec((1, Sp, kvw), lambda b, i: (b, 0, 0)),
                  pl.BlockSpec((1, Sp, kvw), lambda b, i: (b, 0, 0)),
                  pl.BlockSpec(memory_space=pl.ANY)],
        out_specs=pl.BlockSpec((tq, qd), lambda b, i: (blk0 + b * nb + i, 0)),
        out_shape=jax.ShapeDtypeStruct(o_all.shape, o_all.dtype),
        input_output_aliases={6: 0},
        scratch_shapes=[pltpu.VMEM((tq, Sp), jnp.int32),
                        pltpu.VMEM((tq, Sp), jnp.float32),
                        pltpu.VMEM((A_KV_HEADS, rows, hd), jnp.bfloat16),
                        pltpu.VMEM((A_KV_HEADS, rows, LANE), jnp.float32),
                        pltpu.VMEM((A_KV_HEADS, rows, LANE), jnp.float32),
                        pltpu.VMEM((A_KV_HEADS, rows, hd), jnp.float32)],
        compiler_params=pltpu.CompilerParams(
            dimension_semantics=("parallel", "arbitrary"),
            vmem_limit_bytes=V7X_VMEM_LIMIT_BYTES),
        name="dsa_core",
    )(qi3, wi, kidx3, q, kb, vb, o_all)


def _rope_tables(pos, width, period):
    half = period // ROPE_FRACTION // 2
    inv_freq = ROPE_THETA ** (-jnp.arange(half, dtype=jnp.float32) / half)
    ang = pos.astype(jnp.float32)[:, None] * inv_freq[None, :]
    cos, sin = jnp.cos(ang), jnp.sin(ang)
    T = pos.shape[0]
    rest = period - 2 * half
    c = jnp.concatenate([cos, cos, jnp.ones((T, rest), jnp.float32)], axis=1)
    s_next = jnp.concatenate([-sin, jnp.zeros((T, period - half), jnp.float32)], axis=1)
    s_prev = jnp.concatenate([jnp.zeros((T, half), jnp.float32), sin, jnp.zeros((T, rest), jnp.float32)], axis=1)
    return jnp.stack([jnp.tile(t, (1, width // period)) for t in (c, s_next, s_prev)])


def _rope_lanes(x, tab_ref, half):
    return (x * tab_ref[0] + pltpu.roll(x, LANE - half, 1) * tab_ref[1] + pltpu.roll(x, half, 1) * tab_ref[2])


def _dsa_prep_kernel(q_ref, kv_ref, idx_ref, tq_ref, ti_ref,
                     qa_ref, kn_ref, vn_ref, kin_ref, kb_ref, vb_ref, qi3_ref, kidx3_ref, wi_ref, *, q_scale):
    tq = q_ref.shape[0]
    hd = LANE
    kvw = kv_ref.shape[1] // 2
    half_q = hd // ROPE_FRACTION // 2
    half_i = IDX_DIM // ROPE_FRACTION // 2
    left = lax.broadcasted_iota(jnp.int32, (tq, LANE), 1) < IDX_DIM
    zero = jnp.zeros((tq, LANE), jnp.float32)

    for h in range(q_ref.shape[1] // hd):
        cols = slice(h * hd, (h + 1) * hd)
        qa_ref[0, :, cols] = (_rope_lanes(q_ref[:, cols], tq_ref, half_q) * q_scale).astype(qa_ref.dtype)
    for h in range(kvw // hd):
        cols = slice(h * hd, (h + 1) * hd)
        k = _rope_lanes(kv_ref[:, cols], tq_ref, half_q)
        kn_ref[0, :, h, :] = k
        kb_ref[0, :, cols] = k.astype(kb_ref.dtype)
        vn_ref[0, :, h, :] = kv_ref[:, kvw + h * hd:kvw + (h + 1) * hd]
    vb_ref[0] = kv_ref[:, kvw:].astype(vb_ref.dtype)

    def hi_lo(x):
        hi = x.astype(jnp.bfloat16).astype(jnp.float32)
        return hi, x - hi

    for t in range(IDX_HEADS * IDX_DIM // LANE):
        hi, lo = hi_lo(_rope_lanes(idx_ref[:, t * LANE:(t + 1) * LANE], ti_ref, half_i))
        hi_sw, lo_sw = pltpu.roll(hi, IDX_DIM, 1), pltpu.roll(lo, IDX_DIM, 1)
        even = jnp.concatenate([jnp.where(left, hi, lo_sw), jnp.where(left, hi, zero)], axis=1)
        odd = jnp.concatenate([jnp.where(left, hi_sw, lo), jnp.where(left, hi_sw, zero)], axis=1)
        qi3_ref[0, 0, (2 * t) * tq:(2 * t + 1) * tq, :] = even.astype(qi3_ref.dtype)
        qi3_ref[0, 0, (2 * t + 1) * tq:(2 * t + 2) * tq, :] = odd.astype(qi3_ref.dtype)
    c0 = IDX_HEADS * IDX_DIM
    x = idx_ref[:, c0:c0 + LANE]
    r = jnp.where(left, _rope_lanes(x, ti_ref, half_i), x)
    kin_ref[0] = r[:, :IDX_DIM]
    hi, lo = hi_lo(r)
    kidx3 = jnp.concatenate([jnp.where(left, hi, pltpu.roll(hi, IDX_DIM, 1)), jnp.where(left, lo, zero)], axis=1)
    kidx3_ref[0] = kidx3.astype(kidx3_ref.dtype)
    wi_ref[0] = x[:, IDX_DIM:IDX_DIM + IDX_HEADS] * (IDX_HEADS ** -0.5 * IDX_DIM ** -0.5)


def _dsa_prep(q_all, kv_all, idx_all, row0, B, T, pos0, *, tq):
    qd, kv2 = A_HEADS * LANE, 2 * A_KV_HEADS * LANE
    kv_col = (kv_all.shape[1] - kv2) // kv2
    kvw = kv2 // 2
    nb = T // tq
    blk0 = row0 // tq
    assert row0 % tq == 0 and qd // A_HEADS == LANE
    pos = pos0 + jnp.arange(T, dtype=jnp.int32)
    tab_q = _rope_tables(pos, LANE, LANE)
    tab_i = _rope_tables(pos, LANE, IDX_DIM)
    row = lambda w: pl.BlockSpec((tq, w), lambda b, i: (blk0 + b * nb + i, 0))
    tab = pl.BlockSpec((3, tq, LANE), lambda b, i: (0, i, 0))
    out = lambda w: pl.BlockSpec((1, tq, w), lambda b, i: (b, i, 0))
    f32, bf16 = jnp.float32, jnp.bfloat16
    heads4 = pl.BlockSpec((1, tq, A_KV_HEADS, LANE), lambda b, i: (b, i, 0, 0))
    shapes = [((B, T, qd), bf16), ((B, T, A_KV_HEADS, LANE), f32), ((B, T, A_KV_HEADS, LANE), f32),
              ((B, T, IDX_DIM), f32),
              ((B, T, kvw), bf16), ((B, T, kvw), bf16), ((B, nb, IDX_HEADS * tq, IDX_PACK), bf16),
              ((B, T, IDX_PACK), bf16), ((B, T, IDX_HEADS), f32)]
    return pl.pallas_call(
        functools.partial(_dsa_prep_kernel, q_scale=LANE ** -0.5 * LOG2_E),
        grid=(B, nb),
        in_specs=[row(qd), pl.BlockSpec((tq, kv2), lambda b, i: (blk0 + b * nb + i, kv_col)),
                  row(idx_all.shape[1]), tab, tab],
        out_specs=[out(qd), heads4, heads4, out(IDX_DIM), out(kvw), out(kvw),
                   pl.BlockSpec((1, 1, IDX_HEADS * tq, IDX_PACK), lambda b, i: (b, i, 0, 0)),
                   out(IDX_PACK), out(IDX_HEADS)],
        out_shape=[jax.ShapeDtypeStruct(s, d) for s, d in shapes],
        compiler_params=pltpu.CompilerParams(
            dimension_semantics=("parallel", "parallel"), vmem_limit_bytes=V7X_VMEM_LIMIT_BYTES),
        name="dsa_prep",
    )(q_all, kv_all, idx_all, tab_q, tab_i)


def _dsa_layer(x, mod, gate, sid, streams, w_in, w_out):
    D = x.shape[1]
    hd = D // A_HEADS
    q_dim, kv_dim = A_HEADS * hd, A_KV_HEADS * hd
    n_idx = IDX_HEADS * IDX_DIM + IDX_DIM + IDX_HEADS
    w_main = _weight_parts(w_in[:, :q_dim + 2 * kv_dim], 1)
    w_idx = jnp.pad(w_in[:, q_dim + 2 * kv_dim:], ((0, 0), (0, _round_up(n_idx, LANE) - n_idx)))
    q_all = kv_all = pmatmul(x, w_main, tn=q_dim + 2 * kv_dim, mod=mod, sid=sid)
    idx_all = pmatmul(x, _weight_parts(w_idx, 3), tn=w_idx.shape[1], mod=mod, sid=sid)
    caches = []
    o_all = jnp.zeros((x.shape[0], q_dim), jnp.bfloat16)
    for row0, B, T, pos0, past_k, past_v, past_kidx in streams:
        o_all, k, v, ki = _dsa_stream(q_all, kv_all, idx_all, row0, B, T, pos0, past_k, past_v, past_kidx, o_all)
        caches.append((k, v, ki))
    x = pmatmul(o_all, _weight_parts(w_out, 1), tn=2048, res=(x, gate), sid=sid)
    return x, caches


DSA_KEY_TILE = 512
DSA_QUERY_TILE = 256


def _dsa_stream(q_all, kv_all, idx_all, row0, B, T, pos0, past_k, past_v, past_kidx, o_all):
    tq = min(DSA_QUERY_TILE, T)
    q, k, v, ki, kb, vb, qi3, kidx3, wi = _dsa_prep(q_all, kv_all, idx_all, row0, B, T, pos0, tq=tq)
    n_keys = T
    if past_k is not None:
        P = past_k.shape[1]
        n_keys = P + T
        kb = jnp.concatenate([past_k.astype(jnp.bfloat16).reshape(B, P, -1), kb], axis=1)
        vb = jnp.concatenate([past_v.astype(jnp.bfloat16).reshape(B, P, -1), vb], axis=1)
        ph, pl_ = _split_bf16(past_kidx)
        kidx3 = jnp.concatenate([jnp.concatenate([ph, ph, pl_, jnp.zeros_like(ph)], axis=-1), kidx3], axis=1)
    pad = ((0, 0), (0, _round_up(n_keys, DSA_KEY_TILE) - n_keys), (0, 0))
    kb, vb, kidx3 = (jnp.pad(a, pad) for a in (kb, vb, kidx3))
    topk = min(TOPK_MAX, n_keys // 4)
    o_all = _dsa_core(q, qi3, wi, kb, vb, kidx3, pos0, topk, o_all, row0, tq=tq, tk=DSA_KEY_TILE)
    return o_all, k, v, ki


GDN_HEAD_GROUP = 32


def _bf16_dot(a, b):
    return jnp.dot(a.astype(jnp.bfloat16), b.astype(jnp.bfloat16), preferred_element_type=jnp.float32)


def _dot3(a, b):
    ah, al = _split_bf16(a)
    bh, bl = _split_bf16(b)
    out = jnp.dot(ah, bh, preferred_element_type=jnp.float32)
    out = out + jnp.dot(ah, bl, preferred_element_type=jnp.float32)
    return out + jnp.dot(al, bh, preferred_element_type=jnp.float32)


def _conv_silu(x_ref, w_ref, xe_ref):
    C = x_ref.shape[0]
    taps = w_ref.shape[0]
    xe_ref[SUBLANE:, :] = x_ref[...]
    first = SUBLANE - (taps - 1)
    acc = xe_ref[first:first + C, :] * w_ref[0:1, :]
    for j in range(1, taps):
        acc = acc + xe_ref[first + j:first + j + C, :] * w_ref[j:j + 1, :]
    xe_ref[:SUBLANE, :] = xe_ref[C:, :]
    return acc * jax.nn.sigmoid(acc)


def _gdn_kernel(xq_ref, xk_ref, xv_ref, wq_ref, wk_ref, wv_ref, cq_ref, ck_ref, cv_ref,
                z_ref, g_ref, gt_ref, beta_ref, nw_ref, s0_ref, o_all_ref,
                o_ref, s_out_ref, s_ref, eq_ref, ek_ref, ev_ref):
    del o_all_ref
    n = pl.program_id(2)
    C = xq_ref.shape[0]
    hg = g_ref.shape[3]
    dk = s_ref.shape[1]
    dv = s_ref.shape[2]
    rep = hg // (xk_ref.shape[1] // dk)

    @pl.when(n == 0)
    def _():
        s_ref[...] = s0_ref[0]
        for e_ref, c_ref in ((eq_ref, cq_ref), (ek_ref, ck_ref), (ev_ref, cv_ref)):
            e_ref[:SUBLANE, :] = jnp.zeros((SUBLANE, e_ref.shape[1]), jnp.float32)
            e_ref[SUBLANE - c_ref.shape[1]:SUBLANE, :] = c_ref[0]

    qc = _conv_silu(xq_ref, wq_ref, eq_ref)
    kc = _conv_silu(xk_ref, wk_ref, ek_ref)
    vc = _conv_silu(xv_ref, wv_ref, ev_ref)
    q_heads, k_heads = [], []
    for i in range(hg // rep):
        qh = qc[:, i * dk:(i + 1) * dk]
        kh = kc[:, i * dk:(i + 1) * dk]
        q_heads.append(qh * (lax.rsqrt(jnp.sum(qh * qh, axis=1, keepdims=True) + EPS) * dk ** -0.5))
        k_heads.append(kh * lax.rsqrt(jnp.sum(kh * kh, axis=1, keepdims=True) + EPS))

    ri = lax.broadcasted_iota(jnp.int32, (C, C), 0)
    ci = lax.broadcasted_iota(jnp.int32, (C, C), 1)
    causal = ri >= ci
    strict = ri > ci
    eye = jnp.where(ri == ci, 1.0, 0.0)
    g = g_ref[0, 0]
    gc_all = _dot3(jnp.where(causal, 1.0, 0.0), g)
    gr_all = _dot3(gt_ref[0, 0, 0], jnp.where(ri <= ci, 1.0, 0.0))
    beta = beta_ref[0, 0]
    nw = nw_ref[...]

    heads = range(hg)
    qs = [q_heads[h // rep] for h in heads]
    ks = [k_heads[h // rep] for h in heads]
    gcs = [gc_all[:, h:h + 1] for h in heads]
    bcols = [beta[:, h:h + 1] for h in heads]
    decays = [jnp.where(causal, jnp.exp(jnp.where(causal, gcs[h] - gr_all[h:h + 1, :], 0.0)), 0.0)
              for h in heads]
    kbs = [ks[h] * bcols[h] for h in heads]
    kks = [lax.dot_general(kbs[h].astype(jnp.bfloat16), ks[h].astype(jnp.bfloat16), _NT_DIMS,
                           preferred_element_type=jnp.float32) for h in heads]
    bms = [jnp.where(strict, -(kks[h] * decays[h]), 0.0) for h in heads]
    egs = [jnp.exp(gcs[h]) for h in heads]
    rhss = [jnp.concatenate([vc[:, h * dv:(h + 1) * dv] * bcols[h], kbs[h] * egs[h]], axis=1)
            for h in heads]
    pairs = range(hg // 2)
    left = lax.broadcasted_iota(jnp.int32, (C, 2 * C), 1) < C
    zero16 = jnp.zeros((C, 2 * C), jnp.bfloat16)

    def blockdiag(part):
        return jnp.concatenate([jnp.where(left, part, zero16), jnp.where(left, zero16, part)], axis=0)

    def dot3_pairs(x_parts, y_parts):
        xh, xl = x_parts
        yh, yl = blockdiag(y_parts[0]), blockdiag(y_parts[1])
        out = jnp.dot(xh, yh, preferred_element_type=jnp.float32)
        out = out + jnp.dot(xh, yl, preferred_element_type=jnp.float32)
        return out + jnp.dot(xl, yh, preferred_element_type=jnp.float32)

    b2 = [jnp.concatenate([bms[2 * i], bms[2 * i + 1]], axis=1) for i in pairs]
    eye2 = jnp.concatenate([eye, eye], axis=1)
    p2 = [eye2 + b2[i] for i in pairs]
    b2_parts = [_split_bf16(b2[i]) for i in pairs]
    step = 2
    while step < C:
        b2 = [dot3_pairs(b2_parts[i], b2_parts[i]) for i in pairs]
        b2_parts = [_split_bf16(b2[i]) for i in pairs]
        p2 = [p2[i] + dot3_pairs(_split_bf16(p2[i]), b2_parts[i]) for i in pairs]
        step *= 2
    ps = [p2[h // 2][:, (h % 2) * C:(h % 2 + 1) * C] for h in heads]
    ws = [_dot3(ps[h], rhss[h]) for h in heads]
    qks = [lax.dot_general(qs[h].astype(jnp.bfloat16), ks[h].astype(jnp.bfloat16), _NT_DIMS,
                           preferred_element_type=jnp.float32) * decays[h] for h in heads]
    g_lasts = [gcs[h][C - 1:C, :] for h in heads]
    ss = [s_ref[h] for h in heads]
    us = [ws[h][:, :dv] - _bf16_dot(ws[h][:, dv:], ss[h]) for h in heads]
    os_ = [_bf16_dot(qs[h] * egs[h], ss[h]) + _bf16_dot(qks[h], us[h]) for h in heads]
    for h in heads:
        ke = ks[h] * jnp.exp(g_lasts[h] - gcs[h])
        s_ref[h] = ss[h] * jnp.exp(g_lasts[h]) + _bf16_dot(ke.T, us[h])
    for h in heads:
        o = os_[h]
        o = o * lax.rsqrt(jnp.mean(o * o, axis=1, keepdims=True) + EPS) * nw
        z = z_ref[:, h * dv:(h + 1) * dv]
        o_ref[:, h * dv:(h + 1) * dv] = (o * (z * jax.nn.sigmoid(z))).astype(o_ref.dtype)

    @pl.when(n == pl.num_programs(2) - 1)
    def _():
        s_out_ref[0] = s_ref[...]


def _gdn_core(qkv, z, row0, conv_w, conv_state, g, beta, norm_w, s0, o_all):
    B, T, VH = g.shape
    dk, dv = s0.shape[2], s0.shape[3]
    vd = VH * dv
    qd = (conv_w.shape[1] - vd) // 2
    C = min(CHUNK, T)
    N = T // C
    assert row0 % C == 0
    blk0 = row0 // C
    hg = min(GDN_HEAD_GROUP, VH)
    ng = VH // hg
    qw = qd // ng
    vw = hg * dv
    assert qd % qw == 0 and (2 * qd) % vw == 0
    k0, v0 = qd // qw, (2 * qd) // vw
    z0 = (z.shape[1] - vd) // vw
    taps = conv_w.shape[0]

    def grouped(a):
        return a.reshape(B, T, ng, hg).transpose(0, 2, 1, 3)

    gg, bg = grouped(g), grouped(beta)
    gt = gg.reshape(B, ng, N, C, hg).transpose(0, 1, 2, 4, 3)
    small = pl.BlockSpec((1, 1, C, hg), lambda b, j, n: (b, j, n, 0))
    o, s = pl.pallas_call(
        _gdn_kernel,
        grid=(B, ng, N),
        in_specs=[pl.BlockSpec((C, qw), lambda b, j, n: (blk0 + b * N + n, j)),
                  pl.BlockSpec((C, qw), lambda b, j, n: (blk0 + b * N + n, k0 + j)),
                  pl.BlockSpec((C, vw), lambda b, j, n: (blk0 + b * N + n, v0 + j)),
                  pl.BlockSpec((taps, qw), lambda b, j, n: (0, j)),
                  pl.BlockSpec((taps, qw), lambda b, j, n: (0, k0 + j)),
                  pl.BlockSpec((taps, vw), lambda b, j, n: (0, v0 + j)),
                  pl.BlockSpec((1, taps - 1, qw), lambda b, j, n: (b, 0, j)),
                  pl.BlockSpec((1, taps - 1, qw), lambda b, j, n: (b, 0, k0 + j)),
                  pl.BlockSpec((1, taps - 1, vw), lambda b, j, n: (b, 0, v0 + j)),
                  pl.BlockSpec((C, vw), lambda b, j, n: (blk0 + b * N + n, z0 + j)),
                  small,
                  pl.BlockSpec((1, 1, 1, hg, C), lambda b, j, n: (b, j, n, 0, 0)),
                  small,
                  pl.BlockSpec((1, dv), lambda b, j, n: (0, 0)),
                  pl.BlockSpec((1, hg, dk, dv), lambda b, j, n: (b, j, 0, 0)),
                  pl.BlockSpec(memory_space=pl.ANY)],
        out_specs=[pl.BlockSpec((C, hg * dv), lambda b, j, n: (blk0 + b * N + n, j)),
                   pl.BlockSpec((1, hg, dk, dv), lambda b, j, n: (b, j, 0, 0))],
        out_shape=[jax.ShapeDtypeStruct(o_all.shape, o_all.dtype),
                   jax.ShapeDtypeStruct(s0.shape, jnp.float32)],
        input_output_aliases={15: 0},
        scratch_shapes=[pltpu.VMEM((hg, dk, dv), jnp.float32),
                        pltpu.VMEM((SUBLANE + C, qw), jnp.float32),
                        pltpu.VMEM((SUBLANE + C, qw), jnp.float32),
                        pltpu.VMEM((SUBLANE + C, vw), jnp.float32)],
        compiler_params=pltpu.CompilerParams(
            dimension_semantics=("parallel", "parallel", "arbitrary"),
            vmem_limit_bytes=V7X_VMEM_LIMIT_BYTES),
        name="gdn_core",
    )(qkv, qkv, qkv, conv_w, conv_w, conv_w, conv_state, conv_state, conv_state,
      z, gg, gt, bg, norm_w.reshape(1, dv), s0, o_all)
    return o, s


def _gdn_layer(x, mod, gate, sid, streams, w_in, conv_w, a_log, dt_bias, norm_w, w_out):
    qk_dim, v_dim = GDN_QK_HEADS * GDN_DK, GDN_V_HEADS * GDN_DV
    conv_dim = 2 * qk_dim + v_dim
    n_gate = 2 * GDN_V_HEADS
    w_main = _weight_parts(w_in[:, :conv_dim + v_dim], 1)
    w_gate = jnp.pad(w_in[:, conv_dim + v_dim:], ((0, 0), (0, _round_up(n_gate, LANE) - n_gate)))
    qkv_all = z_all = pmatmul(x, w_main, tn=2048, mod=mod, sid=sid)
    gates_all = pmatmul(x, _weight_parts(w_gate, 1), tn=w_gate.shape[1], mod=mod, sid=sid)
    states = []
    o_all = jnp.zeros((x.shape[0], v_dim), jnp.bfloat16)
    for row0, B, T, conv_state, ssm_state in streams:
        rows = slice(row0, row0 + B * T)
        gates = gates_all[rows].reshape(B, T, -1)
        beta_raw, a_raw = gates[..., :GDN_V_HEADS], gates[..., GDN_V_HEADS:n_gate]
        if conv_state is None:
            conv_state = jnp.zeros((B, CONV_W - 1, conv_dim), x.dtype)
        if ssm_state is None:
            ssm_state = jnp.zeros((B, GDN_V_HEADS, GDN_DK, GDN_DV), jnp.float32)
        tail = jnp.stack([qkv_all[row0 + (b + 1) * T - (CONV_W - 1):row0 + (b + 1) * T, :conv_dim]
                          for b in range(B)])
        new_conv = jnp.concatenate([conv_state, tail], axis=1)[:, -(CONV_W - 1):]
        beta = jax.nn.sigmoid(beta_raw)
        g = -jnp.exp(a_log) * jax.nn.softplus(a_raw + dt_bias)
        o_all, S = _gdn_core(qkv_all, z_all, row0, conv_w, conv_state, g, beta, norm_w, ssm_state, o_all)
        states.append((new_conv, S))
    x = pmatmul(o_all, _weight_parts(w_out, 1), tm=256, tn=w_out.shape[1], res=(x, gate), sid=sid)
    return x, states


NEG_INF = float("-inf")
_PEER_CAND_ROWS = tuple((i, PEER_TOPK // (i + 1)) for i in range(PEER_TOPK // 2))


def _split_bf16(x):
    hi = x.astype(jnp.bfloat16)
    lo = (x - hi.astype(jnp.float32)).astype(jnp.bfloat16)
    return hi, lo


def _dot3_nt(a, b):
    dn = (((1,), (1,)), ((), ()))
    ah, al = _split_bf16(a)
    bh, bl = _split_bf16(b)
    out = lax.dot_general(ah, bh, dn, preferred_element_type=jnp.float32)
    out = out + lax.dot_general(ah, bl, dn, preferred_element_type=jnp.float32)
    return out + lax.dot_general(al, bh, dn, preferred_element_type=jnp.float32)


def _top_rows_desc(s, n, with_rank=False):
    rows = []
    cur = s
    rank = jnp.full(s.shape, float(n), jnp.float32)
    for i in range(n):
        m = jnp.max(cur, axis=0, keepdims=True)
        rows.append(m)
        hit = cur == m
        if with_rank:
            rank = jnp.where(hit, float(i), rank)
        cur = jnp.where(hit, NEG_INF, cur)
    return (rows, rank) if with_rank else rows


def _stack_rows(rows, lanes):
    n = len(rows)
    rid = lax.broadcasted_iota(jnp.int32, (n, lanes), 0)
    out = jnp.zeros((n, lanes), jnp.float32)
    for i, r in enumerate(rows):
        out = jnp.where(rid == i, r, out)
    return out


def _peer_select_kernel(q_ref, keys_ref, rk2_ref, e2_ref, cnt_ref, e1_ref):
    tm = q_ref.shape[0]
    k = PEER_TOPK
    s1 = _dot3_nt(keys_ref[0], q_ref[:, :PEER_HALF])
    s2 = _dot3_nt(keys_ref[1], q_ref[:, PEER_HALF:])
    r1 = _top_rows_desc(s1, k)
    r2, rank2 = _top_rows_desc(s2, k, with_rank=True)
    v1 = _stack_rows(r1, tm)
    v2 = _stack_rows(r2, tm)
    v2h = v2[:k // 2]
    rid = lax.broadcasted_iota(jnp.int32, (k // 2, tm), 0)
    pieces = [r1[0] + v2]
    for i, n in _PEER_CAND_ROWS[1:]:
        pieces.append(jnp.where(rid < n, r1[i] + v2h, NEG_INF))
    pieces.append(v1[k // 2:] + r2[0])
    cand = jnp.concatenate(pieces, axis=0)
    tau = _top_rows_desc(cand, k)[-1]
    top = r1[0] + r2[0]
    z = jnp.sum(jnp.where(cand >= tau, jnp.exp(cand - top), 0.0), axis=0, keepdims=True)
    cnt = jnp.zeros(s1.shape, jnp.float32)
    for j in range(k):
        cnt = cnt + jnp.where(s1 + r2[j] >= tau, 1.0, 0.0)
    rk2_ref[0] = pltpu.bitcast(rank2.astype(jnp.bfloat16), rk2_ref.dtype)
    e2_ref[0] = pltpu.bitcast((jnp.exp(s2 - r2[0]) / z).astype(jnp.bfloat16), e2_ref.dtype)
    cnt_ref[0] = cnt
    e1_ref[0] = jnp.exp(s1 - r1[0])


def _peer_select(q, keys, *, tm):
    M = q.shape[0]
    nk = keys.shape[1]
    ospec = pl.BlockSpec((1, nk, tm), lambda i, h: (h, 0, i))
    pspec = pl.BlockSpec((1, nk // 2, tm), lambda i, h: (h, 0, i))
    return pl.pallas_call(
        _peer_select_kernel,
        grid=(M // tm, PEER_HEADS),
        in_specs=[pl.BlockSpec((tm, PEER_KEY_DIM), lambda i, h: (i, h)),
                  pl.BlockSpec(keys.shape, lambda i, h: (0, 0, 0))],
        out_specs=[pspec, pspec, ospec, ospec],
        out_shape=[jax.ShapeDtypeStruct((PEER_HEADS, nk // 2, M), jnp.int32)] * 2
        + [jax.ShapeDtypeStruct((PEER_HEADS, nk, M), jnp.float32)] * 2,
        compiler_params=pltpu.CompilerParams(
            dimension_semantics=("parallel", "arbitrary"),
            vmem_limit_bytes=V7X_VMEM_LIMIT_BYTES),
        name="peer_select",
    )(q, keys)


BF16_SUBLANES = 16
PEER_SUB_SPLIT = (1, 1)


def _peer_main_kernel(sid_ref, x_ref, gain_ref, shift_ref, scale_ref, gate_ref, fnorm_ref,
                      u_ref, vT_ref, rk2_ref, e2_ref, cnt_ref, e1_ref, *refs, final_norm):
    out_refs, (hT_ref, coef_ref, acc_ref) = refs[:-3], refs[-3:]
    e = pl.program_id(1)
    te, tm = coef_ref.shape
    nk = 2 * rk2_ref.shape[1]
    rows = BF16_SUBLANES
    blk0 = pl.program_id(0) * (tm // SEQ_BLOCK)

    @pl.when(e == 0)
    def _():
        acc_ref[...] = jnp.zeros_like(acc_ref)
        per_lane_tile = LANE // SEQ_BLOCK
        for t in range(tm // LANE):
            hs = []
            for r in range(t * per_lane_tile, (t + 1) * per_lane_tile):
                rs = slice(r * SEQ_BLOCK, (r + 1) * SEQ_BLOCK)
                hs.append(_modulated_rows(x_ref[rs, :], gain_ref[...], shift_ref, scale_ref, sid_ref[blk0 + r]))
            hT_ref[:, t * LANE:(t + 1) * LANE] = jnp.concatenate(hs, axis=0).T.astype(jnp.bfloat16)

    def expert_acts(span):
        off, size = span
        return jnp.dot(u_ref[off:off + size, :], hT_ref[...], preferred_element_type=jnp.float32)

    def weigh(span, act):
        off, size = span
        for al in range(size // nk):
            a = off // nk + al
            for lg in range(tm // LANE):
                lanes = slice(lg * LANE, (lg + 1) * LANE)
                cnts = [jnp.broadcast_to(cnt_ref[h, a:a + 1, lanes], (rows, LANE)).astype(jnp.bfloat16)
                        for h in range(PEER_HEADS)]
                e1s = [jnp.broadcast_to(e1_ref[h, a:a + 1, lanes], (rows, LANE)).astype(jnp.bfloat16)
                       for h in range(PEER_HEADS)]
                for r in range(nk // rows):
                    words = slice(r * rows // 2, (r + 1) * rows // 2)
                    w = None
                    for h in range(PEER_HEADS):
                        e2 = pltpu.bitcast(e2_ref[h, words, lanes], jnp.bfloat16)
                        rk2 = pltpu.bitcast(rk2_ref[h, words, lanes], jnp.bfloat16)
                        t = jnp.where(rk2 < cnts[h], e2, jnp.zeros_like(e2)) * e1s[h]
                        w = t if w is None else w + t
                    x = act[al * nk + r * rows:al * nk + (r + 1) * rows, lanes]
                    g = 0.5 * x * (1.0 + lax.erf(x * (2.0 ** -0.5)))
                    coef_ref[off + al * nk + r * rows:off + al * nk + (r + 1) * rows, lanes] = (
                        w * g.astype(jnp.bfloat16))

    def accumulate(span):
        off, size = span
        acc_ref[...] += lax.dot_general(vT_ref[off:off + size, :], coef_ref[off:off + size, :],
                                        (((0,), (0,)), ((), ())), preferred_element_type=jnp.float32)

    spans, off = [], 0
    for frac in PEER_SUB_SPLIT:
        spans.append((off, te * frac // sum(PEER_SUB_SPLIT)))
        off += spans[-1][1]
    act = expert_acts(spans[0])
    for i, span in enumerate(spans):
        nxt = expert_acts(spans[i + 1]) if i + 1 < len(spans) else None
        weigh(span, act)
        accumulate(span)
        act = nxt

    @pl.when(e == pl.num_programs(1) - 1)
    def _():
        out = acc_ref[...].T
        for r in range(tm // SEQ_BLOCK):
            rs = slice(r * SEQ_BLOCK, (r + 1) * SEQ_BLOCK)
            y = x_ref[rs, :] + gate_ref[pl.ds(sid_ref[blk0 + r], 1), :] * out[rs, :]
            out_refs[0][rs, :] = y
            if final_norm:
                out_refs[1][rs, :] = y * lax.rsqrt(jnp.mean(y * y, axis=-1, keepdims=True) + EPS) * fnorm_ref[...]


def _peer_main(x, mod, gate, fnorm, sid, u, vT, rk2, e2, cnt, e1, *, tm, te, final_norm):
    M, D = x.shape
    E = u.shape[0]
    nk = cnt.shape[1]
    const = lambda a: pl.BlockSpec(a.shape, lambda i, e, s: (0,) * a.ndim)
    col_spec = pl.BlockSpec((PEER_HEADS, nk // 2, tm), lambda i, e, s: (0, 0, i))
    row_spec = pl.BlockSpec((PEER_HEADS, te // nk, tm), lambda i, e, s: (0, e, i))
    tok_spec = pl.BlockSpec((tm, D), lambda i, e, s: (i, 0))
    n_out = 2 if final_norm else 1
    return pl.pallas_call(
        functools.partial(_peer_main_kernel, final_norm=final_norm),
        grid_spec=pltpu.PrefetchScalarGridSpec(
            num_scalar_prefetch=1,
            grid=(M // tm, E // te),
            in_specs=[tok_spec, const(mod[0]), const(mod[1]), const(mod[2]), const(gate), const(fnorm),
                      pl.BlockSpec((te, D), lambda i, e, s: (e, 0)),
                      pl.BlockSpec((te, D), lambda i, e, s: (e, 0)),
                      col_spec, col_spec, row_spec, row_spec],
            out_specs=[tok_spec] * n_out,
            scratch_shapes=[pltpu.VMEM((D, tm), jnp.bfloat16),
                            pltpu.VMEM((te, tm), jnp.bfloat16),
                            pltpu.VMEM((D, tm), jnp.float32)]),
        out_shape=[jax.ShapeDtypeStruct((M, D), jnp.float32)] * n_out,
        compiler_params=pltpu.CompilerParams(
            dimension_semantics=("parallel", "arbitrary"),
            vmem_limit_bytes=V7X_VMEM_LIMIT_PEER_BYTES),
        name="peer_main",
    )(sid, x, *mod, gate, fnorm, u, vT, rk2, e2, cnt, e1)


def _peer(x, mod, gate, fnorm, sid, w_query, sub_keys, expert_u, expert_v, *, final_norm,
          tm_sel=256, tm=512, te=1024):
    M = x.shape[0]
    q = pmatmul(x, _weight_parts(w_query, 1), tn=2048, mod=mod, sid=sid)
    rk2, e2, cnt, e1 = _peer_select(q, sub_keys, tm=min(tm_sel, M))
    u = expert_u.astype(jnp.bfloat16)
    vT = expert_v.astype(jnp.bfloat16)
    return _peer_main(x, mod, gate, fnorm, sid, u, vT, rk2, e2, cnt, e1, tm=min(tm, M), te=te,
                      final_norm=final_norm)


def kernel(x_prompt, x_sample, c_prompt, c_sample, cache_k_l0, cache_v_l0, cache_kidx_l0, state_conv_l1, state_ssm_l1, norm1_l0, norm2_l0, ada_w_l0, ada_b_l0, attn_in_l0, attn_out_l0, peer_query_l0, peer_keys_l0, peer_u_l0, peer_v_l0, norm1_l1, norm2_l1, ada_w_l1, ada_b_l1, gdn_in_l1, gdn_conv_l1, gdn_a_log_l1, gdn_dt_bias_l1, gdn_norm_l1, gdn_out_l1, peer_query_l1, peer_keys_l1, peer_u_l1, peer_v_l1, final_norm):
    past_len = cache_k_l0.shape[1]
    norm1 = (norm1_l0, norm1_l1)
    norm2 = (norm2_l0, norm2_l1)
    ada_w = (ada_w_l0, ada_w_l1)
    ada_b = (ada_b_l0, ada_b_l1)
    peer_query = (peer_query_l0, peer_query_l1)
    peer_keys = (peer_keys_l0, peer_keys_l1)
    peer_u = (peer_u_l0, peer_u_l1)
    peer_v = (peer_v_l0, peer_v_l1)
    Bp, Tp, D = x_prompt.shape
    Bs, Ts, _ = x_sample.shape
    n_p, n_s = Bp * Tp, Bs * Ts
    x = jnp.concatenate([x_prompt.reshape(n_p, D), x_sample.reshape(n_s, D)], axis=0)
    sid = jnp.concatenate([jnp.repeat(jnp.arange(Bp, dtype=jnp.int32), Tp // SEQ_BLOCK),
                           Bp + jnp.repeat(jnp.arange(Bs, dtype=jnp.int32), Ts // SEQ_BLOCK)])
    c_all = jnp.concatenate([c_prompt, c_sample], axis=0)
    fnorm = final_norm.reshape(1, D)
    for i in range(2):
        mods = _adaln(c_all, ada_w[i], ada_b[i])
        mod1 = (norm1[i].reshape(1, D), mods[0], mods[1])
        if i == 0:
            x, ((nkp, nvp, nkip), (nks, nvs, nkis)) = _dsa_layer(
                x, mod1, mods[2], sid,
                [(0, Bp, Tp, 0, None, None, None), (n_p, Bs, Ts, past_len, cache_k_l0, cache_v_l0, cache_kidx_l0)],
                attn_in_l0, attn_out_l0)
        else:
            x, ((ncp, nsp), (ncs, nss)) = _gdn_layer(
                x, mod1, mods[2], sid, [(0, Bp, Tp, None, None), (n_p, Bs, Ts, state_conv_l1, state_ssm_l1)],
                gdn_in_l1, gdn_conv_l1, gdn_a_log_l1, gdn_dt_bias_l1, gdn_norm_l1, gdn_out_l1)
        mod2 = (norm2[i].reshape(1, D), mods[3], mods[4])
        outs = _peer(x, mod2, mods[5], fnorm, sid, peer_query[i], peer_keys[i], peer_u[i], peer_v[i],
                     final_norm=(i == 1))
        x = outs[0]
    y = outs[1]
    y_prompt = y[:n_p].reshape(Bp, Tp, D)
    y_sample = y[n_p:].reshape(Bs, Ts, D)
    return (y_prompt, y_sample, nkp, nvp, nkip, nks, nvs, nkis, ncp, nsp, ncs, nss)
```

```python
import functools

import jax
import jax.numpy as jnp
from jax import lax
from jax.experimental import pallas as pl
from jax.experimental.pallas import tpu as pltpu

CHUNK = 64
CHUNK_SHIFT = CHUNK.bit_length() - 1
EPS = 1e-6
ROPE_THETA = 500000.0
ROPE_FRACTION = 4
A_HEADS = 16
A_KV_HEADS = 4
IDX_HEADS = 8
IDX_DIM = 64
TOPK_MAX = 256
Q_BLOCK = 128
GDN_QK_HEADS = 16
GDN_V_HEADS = 32
GDN_DK = 128
GDN_DV = 128
CONV_W = 4
PEER_HEADS = 8
PEER_KEY_DIM = 256
PEER_HALF = PEER_KEY_DIM // 2
PEER_TOPK = 16

V7X_VMEM_LIMIT_BYTES = 48 * 1024 * 1024
V7X_VMEM_LIMIT_PEER_BYTES = 56 * 1024 * 1024
LANE = 128
SUBLANE = 8


def _round_up(n, m):
    return (n + m - 1) // m * m


SEQ_BLOCK = CHUNK


def _modulated_rows(x, gain, shift_ref, scale_ref, sid):
    y = x * lax.rsqrt(jnp.mean(x * x, axis=-1, keepdims=True) + EPS)
    return y * gain * (1.0 + scale_ref[pl.ds(sid, 1), :]) + shift_ref[pl.ds(sid, 1), :]


def _matmul_kernel(sid_ref, x_ref, *refs, passes, has_mod, has_res):
    refs = list(refs)
    mod_refs = [refs.pop(0) for _ in range(3)] if has_mod else None
    nparts = 1 + passes // 2
    w_refs = [refs.pop(0) for _ in range(nparts)]
    res_refs = [refs.pop(0) for _ in range(2)] if has_res else None
    o_ref, x_parts = refs[0], refs[1:]
    tm = x_ref.shape[0]
    blocks = range(tm // SEQ_BLOCK) if (has_mod or has_res) else ()
    blk0 = pl.program_id(0) * (tm // SEQ_BLOCK)

    @pl.when(pl.program_id(1) == 0)
    def _():
        def put(rows, x):
            hi = x.astype(jnp.bfloat16)
            x_parts[0][rows, :] = hi
            if passes == 3:
                x_parts[1][rows, :] = (x - hi.astype(jnp.float32)).astype(jnp.bfloat16)

        if has_mod:
            gain_ref, shift_ref, scale_ref = mod_refs
            for r in blocks:
                rows = slice(r * SEQ_BLOCK, (r + 1) * SEQ_BLOCK)
                put(rows, _modulated_rows(x_ref[rows, :], gain_ref[...], shift_ref, scale_ref, sid_ref[blk0 + r]))
        else:
            put(slice(None), x_ref[...].astype(jnp.float32))

    acc = jnp.dot(x_parts[0][...], w_refs[0][...], preferred_element_type=jnp.float32)
    if passes == 3:
        acc = acc + jnp.dot(x_parts[0][...], w_refs[1][...], preferred_element_type=jnp.float32)
        acc = acc + jnp.dot(x_parts[1][...], w_refs[0][...], preferred_element_type=jnp.float32)
    if has_res:
        res_ref, gate_ref = res_refs
        for r in blocks:
            rows = slice(r * SEQ_BLOCK, (r + 1) * SEQ_BLOCK)
            gate = gate_ref[pl.ds(sid_ref[blk0 + r], 1), :]
            o_ref[rows, :] = res_ref[rows, :] + gate * acc[rows, :]
    else:
        o_ref[...] = acc.astype(o_ref.dtype)


def _weight_parts(w, passes):
    hi = w.astype(jnp.bfloat16)
    if passes == 1:
        return (hi,)
    return (hi, (w - hi.astype(jnp.float32)).astype(jnp.bfloat16))


def pmatmul(x, w_parts, *, col0=0, ncols=None, tm=512, tn=1024, out_dtype=jnp.float32,
            mod=None, res=None, sid=None):
    passes = 1 if len(w_parts) == 1 else 3
    M, K = x.shape
    n_total = w_parts[0].shape[1]
    ncols = n_total - col0 if ncols is None else ncols
    tm = min(tm, _round_up(M, 2 * SUBLANE))
    tn = min(tn, ncols)
    assert ncols % tn == 0 and col0 % tn == 0 and (tn % LANE == 0 or tn == n_total)
    Mp = _round_up(M, tm)
    if mod is not None or res is not None:
        assert Mp == M and tm % SEQ_BLOCK == 0 and sid is not None
    else:
        sid = jnp.zeros((1,), jnp.int32)
    if Mp != M:
        x = jnp.pad(x, ((0, Mp - M), (0, 0)))
    c0 = col0 // tn
    in_specs = [pl.BlockSpec((tm, K), lambda i, j, s: (i, 0))]
    args = [x]
    if mod is not None:
        in_specs += [pl.BlockSpec(a.shape, lambda i, j, s: (0, 0)) for a in mod]
        args += list(mod)
    in_specs += [pl.BlockSpec((K, tn), lambda i, j, s: (0, c0 + j))] * len(w_parts)
    args += list(w_parts)
    if res is not None:
        in_specs += [pl.BlockSpec((tm, tn), lambda i, j, s: (i, j)),
                     pl.BlockSpec((res[1].shape[0], tn), lambda i, j, s: (0, j))]
        args += list(res)
        out_dtype = jnp.float32
    out = pl.pallas_call(
        functools.partial(_matmul_kernel, passes=passes, has_mod=mod is not None, has_res=res is not None),
        grid_spec=pltpu.PrefetchScalarGridSpec(
            num_scalar_prefetch=1,
            grid=(Mp // tm, ncols // tn),
            in_specs=in_specs,
            out_specs=pl.BlockSpec((tm, tn), lambda i, j, s: (i, j)),
            scratch_shapes=[pltpu.VMEM((tm, K), jnp.bfloat16)] * (1 + passes // 2)),
        out_shape=jax.ShapeDtypeStruct((Mp, ncols), out_dtype),
        compiler_params=pltpu.CompilerParams(
            dimension_semantics=("parallel", "arbitrary"),
            vmem_limit_bytes=V7X_VMEM_LIMIT_BYTES),
        name="matmul",
    )(sid, *args)
    return out[:M]


def _adaln_kernel(c_ref, w_ref, b_ref, o_ref):
    c = c_ref[...]
    o_ref[...] = _dot3(c * jax.nn.sigmoid(c), w_ref[...]) + b_ref[...]


def _adaln(c, w, b, *, tn=512):
    n, D = c.shape
    N = w.shape[1]
    rows = _round_up(n, SUBLANE)
    mod = pl.pallas_call(
        _adaln_kernel,
        grid=(N // tn,),
        in_specs=[pl.BlockSpec((rows, D), lambda j: (0, 0)),
                  pl.BlockSpec((D, tn), lambda j: (0, j)),
                  pl.BlockSpec((1, tn), lambda j: (0, j))],
        out_specs=pl.BlockSpec((rows, tn), lambda j: (0, j)),
        out_shape=jax.ShapeDtypeStruct((rows, N), jnp.float32),
        compiler_params=pltpu.CompilerParams(
            dimension_semantics=("parallel",), vmem_limit_bytes=V7X_VMEM_LIMIT_BYTES),
        name="adaln",
    )(jnp.pad(c, ((0, rows - n), (0, 0))), w, b.reshape(1, N))
    return jnp.split(mod, 6, axis=-1)


_NT_DIMS = (((1,), (1,)), ((), ()))
INT32_MIN = -2 ** 31
LOG2_E = 1.4426950408889634
_NEG_INF_KEY = -2139095041
IDX_PACK = 4 * IDX_DIM


def _ordered_key(x):
    bits = pltpu.bitcast(x, jnp.int32)
    return bits ^ ((bits >> 31) & 0x7FFFFFFF)


def _lane_tile_sum(x, width=LANE):
    out = x[:, :width]
    for c in range(1, x.shape[1] // width):
        out = out + x[:, c * width:(c + 1) * width]
    return out


def _dsa_select_bias(qi_ref, wi_ref, kidx_ref, key_ref, bias_ref, *, first, n_tiles, topk, tk):
    tq = wi_ref.shape[1]
    row = lax.broadcasted_iota(jnp.int32, (tq, 1), 0)
    lim = (((first + row) >> CHUNK_SHIFT) + 1) * CHUNK
    w = wi_ref[0]

    def score_tile(j, c):
        off = pl.multiple_of(j * tk, tk)
        kt = kidx_ref[0, pl.ds(off, tk), :]
        sc = lax.dot_general(qi_ref[0, 0], kt, _NT_DIMS,
                             preferred_element_type=jnp.float32)
        s = jnp.zeros((tq, tk), jnp.float32)
        for h in range(IDX_HEADS):
            s = s + w[:, h:h + 1] * jnp.maximum(sc[h * tq:(h + 1) * tq], 0.0)
        col = off + lax.broadcasted_iota(jnp.int32, (tq, tk), 1)
        s = jnp.where(col < lim, s + 0.0, NEG_INF)
        key_ref[:, pl.ds(off, tk)] = _ordered_key(s)
        return c

    lax.fori_loop(0, n_tiles, score_tile, 0)

    def bit_step(b, thr):
        cand = thr + lax.shift_left(jnp.int32(1), 31 - b)

        def count_tile(j, c):
            off = pl.multiple_of(j * tk, tk)
            ge = jnp.where(key_ref[:, pl.ds(off, tk)] >= cand, 1.0, 0.0)
            return c + _lane_tile_sum(ge)

        c = lax.fori_loop(0, n_tiles, count_tile, jnp.zeros((tq, LANE), jnp.float32))
        cnt = jnp.sum(c, axis=1, keepdims=True)
        return jnp.where(cnt >= topk, cand, thr)

    thr = lax.fori_loop(0, 32, bit_step, jnp.full((tq, 1), INT32_MIN, jnp.int32))
    thr = jnp.maximum(thr, _NEG_INF_KEY + 1)

    def bias_tile(j, c):
        off = pl.multiple_of(j * tk, tk)
        bias_ref[:, pl.ds(off, tk)] = jnp.where(key_ref[:, pl.ds(off, tk)] >= thr, 0.0, NEG_INF)
        return c

    lax.fori_loop(0, n_tiles, bias_tile, 0)


def _dsa_kernel(qi_ref, wi_ref, kidx_ref, q_ref, k_ref, v_ref, o_all_ref, o_ref,
                key_ref, bias_ref, qg_ref, m_ref, l_ref, acc_ref, *, pos0, topk, tk):
    del o_all_ref
    i = pl.program_id(1)
    tq = q_ref.shape[1]
    hd = acc_ref.shape[2]
    groups = acc_ref.shape[0]
    rep = q_ref.shape[2] // (groups * hd)
    first = pos0 + i * tq
    n_valid = (((first + tq - 1) >> CHUNK_SHIFT) + 1) * CHUNK
    n_tiles = (n_valid + tk - 1) // tk
    _dsa_select_bias(qi_ref, wi_ref, kidx_ref, key_ref, bias_ref, first=first, n_tiles=n_tiles, topk=topk, tk=tk)

    for g in range(groups):
        for r in range(rep):
            c0 = (g * rep + r) * hd
            qg_ref[g, r * tq:(r + 1) * tq, :] = q_ref[0, :, c0:c0 + hd]
    m_ref[...] = jnp.full(m_ref.shape, NEG_INF, jnp.float32)
    l_ref[...] = jnp.zeros(l_ref.shape, jnp.float32)
    acc_ref[...] = jnp.zeros(acc_ref.shape, jnp.float32)
    lane_reps = tk // LANE

    def att_tile(j, c):
        off = pl.multiple_of(j * tk, tk)
        b = bias_ref[:, pl.ds(off, tk)]
        bias = jnp.concatenate([b] * rep, axis=0)
        for g in range(groups):
            kt = k_ref[0, pl.ds(off, tk), g * hd:(g + 1) * hd]
            vt = v_ref[0, pl.ds(off, tk), g * hd:(g + 1) * hd]
            lg = lax.dot_general(qg_ref[g], kt, _NT_DIMS, preferred_element_type=jnp.float32) + bias
            m_old = m_ref[g]
            m_new = jnp.maximum(m_old, jnp.max(lg, axis=1, keepdims=True))
            m_safe = jnp.where(m_new == NEG_INF, 0.0, m_new)
            p = jnp.exp2(lg - jnp.tile(m_safe, (1, lane_reps)))
            alpha = jnp.exp2(m_old - m_safe)
            l_ref[g] = alpha * l_ref[g] + jnp.sum(p, axis=1, keepdims=True)
            acc_ref[g] = alpha * acc_ref[g] + jnp.dot(p.astype(jnp.bfloat16), vt,
                                                      preferred_element_type=jnp.float32)
            m_ref[g] = m_new
        return c

    lax.fori_loop(0, n_tiles, att_tile, 0)
    for g in range(groups):
        out = acc_ref[g] / l_ref[g]
        for r in range(rep):
            c0 = (g * rep + r) * hd
            o_ref[:, c0:c0 + hd] = out[r * tq:(r + 1) * tq].astype(o_ref.dtype)


def _dsa_core(q, qi3, wi, kb, vb, kidx3, pos0, topk, o_all, row0, *, tq, tk):
    B, T, qd = q.shape
    Sp = kb.shape[1]
    hd = qd // A_HEADS
    gw = qd // A_KV_HEADS
    nb = T // tq
    assert Sp % tk == 0 and hd == LANE and row0 % tq == 0
    blk0 = row0 // tq
    rows = (gw // hd) * tq
    kvw = A_KV_HEADS * hd
    return pl.pallas_call(
        functools.partial(_dsa_kernel, pos0=pos0, topk=topk, tk=tk),
        grid=(B, nb),
        in_specs=[pl.BlockSpec((1, 1, IDX_HEADS * tq, IDX_PACK), lambda b, i: (b, i, 0, 0)),
                  pl.BlockSpec((1, tq, IDX_HEADS), lambda b, i: (b, i, 0)),
                  pl.BlockSpec((1, Sp, IDX_PACK), lambda b, i: (b, 0, 0)),
                  pl.BlockSpec((1, tq, qd), lambda b, i: (b, i, 0)),
                  pl.BlockSpec((1, Sp, kvw), lambda b, i: (b, 0, 0)),
                  pl.BlockSpec((1, Sp, kvw), lambda b, i: (b, 0, 0)),
                  pl.BlockSpec(memory_space=pl.ANY)],
        out_specs=pl.BlockSpec((tq, qd), lambda b, i: (blk0 + b * nb + i, 0)),
        out_shape=jax.ShapeDtypeStruct(o_all.shape, o_all.dtype),
        input_output_aliases={6: 0},
        scratch_shapes=[pltpu.VMEM((tq, Sp), jnp.int32),
                        pltpu.VMEM((tq, Sp), jnp.float32),
                        pltpu.VMEM((A_KV_HEADS, rows, hd), jnp.bfloat16),
                        pltpu.VMEM((A_KV_HEADS, rows, LANE), jnp.float32),
                        pltpu.VMEM((A_KV_HEADS, rows, LANE), jnp.float32),
                        pltpu.VMEM((A_KV_HEADS, rows, hd), jnp.float32)],
        compiler_params=pltpu.CompilerParams(
            dimension_semantics=("parallel", "arbitrary"),
            vmem_limit_bytes=V7X_VMEM_LIMIT_BYTES),
        name="dsa_core",
    )(qi3, wi, kidx3, q, kb, vb, o_all)


def _rope_tables(pos, width, period):
    half = period // ROPE_FRACTION // 2
    inv_freq = ROPE_THETA ** (-jnp.arange(half, dtype=jnp.float32) / half)
    ang = pos.astype(jnp.float32)[:, None] * inv_freq[None, :]
    cos, sin = jnp.cos(ang), jnp.sin(ang)
    T = pos.shape[0]
    rest = period - 2 * half
    c = jnp.concatenate([cos, cos, jnp.ones((T, rest), jnp.float32)], axis=1)
    s_next = jnp.concatenate([-sin, jnp.zeros((T, period - half), jnp.float32)], axis=1)
    s_prev = jnp.concatenate([jnp.zeros((T, half), jnp.float32), sin, jnp.zeros((T, rest), jnp.float32)], axis=1)
    return jnp.stack([jnp.tile(t, (1, width // period)) for t in (c, s_next, s_prev)])


def _rope_lanes(x, tab_ref, half):
    return (x * tab_ref[0] + pltpu.roll(x, LANE - half, 1) * tab_ref[1] + pltpu.roll(x, half, 1) * tab_ref[2])


def _dsa_prep_kernel(q_ref, kv_ref, idx_ref, tq_ref, ti_ref,
                     qa_ref, kn_ref, vn_ref, kin_ref, kb_ref, vb_ref, qi3_ref, kidx3_ref, wi_ref, *, q_scale):
    tq = q_ref.shape[0]
    hd = LANE
    kvw = kv_ref.shape[1] // 2
    half_q = hd // ROPE_FRACTION // 2
    half_i = IDX_DIM // ROPE_FRACTION // 2
    left = lax.broadcasted_iota(jnp.int32, (tq, LANE), 1) < IDX_DIM
    zero = jnp.zeros((tq, LANE), jnp.float32)

    for h in range(q_ref.shape[1] // hd):
        cols = slice(h * hd, (h + 1) * hd)
        qa_ref[0, :, cols] = (_rope_lanes(q_ref[:, cols], tq_ref, half_q) * q_scale).astype(qa_ref.dtype)
    for h in range(kvw // hd):
        cols = slice(h * hd, (h + 1) * hd)
        k = _rope_lanes(kv_ref[:, cols], tq_ref, half_q)
        kn_ref[0, :, h, :] = k
        kb_ref[0, :, cols] = k.astype(kb_ref.dtype)
        vn_ref[0, :, h, :] = kv_ref[:, kvw + h * hd:kvw + (h + 1) * hd]
    vb_ref[0] = kv_ref[:, kvw:].astype(vb_ref.dtype)

    def hi_lo(x):
        hi = x.astype(jnp.bfloat16).astype(jnp.float32)
        return hi, x - hi

    for t in range(IDX_HEADS * IDX_DIM // LANE):
        hi, lo = hi_lo(_rope_lanes(idx_ref[:, t * LANE:(t + 1) * LANE], ti_ref, half_i))
        hi_sw, lo_sw = pltpu.roll(hi, IDX_DIM, 1), pltpu.roll(lo, IDX_DIM, 1)
        even = jnp.concatenate([jnp.where(left, hi, lo_sw), jnp.where(left, hi, zero)], axis=1)
        odd = jnp.concatenate([jnp.where(left, hi_sw, lo), jnp.where(left, hi_sw, zero)], axis=1)
        qi3_ref[0, 0, (2 * t) * tq:(2 * t + 1) * tq, :] = even.astype(qi3_ref.dtype)
        qi3_ref[0, 0, (2 * t + 1) * tq:(2 * t + 2) * tq, :] = odd.astype(qi3_ref.dtype)
    c0 = IDX_HEADS * IDX_DIM
    x = idx_ref[:, c0:c0 + LANE]
    r = jnp.where(left, _rope_lanes(x, ti_ref, half_i), x)
    kin_ref[0] = r[:, :IDX_DIM]
    hi, lo = hi_lo(r)
    kidx3 = jnp.concatenate([jnp.where(left, hi, pltpu.roll(hi, IDX_DIM, 1)), jnp.where(left, lo, zero)], axis=1)
    kidx3_ref[0] = kidx3.astype(kidx3_ref.dtype)
    wi_ref[0] = x[:, IDX_DIM:IDX_DIM + IDX_HEADS] * (IDX_HEADS ** -0.5 * IDX_DIM ** -0.5)


def _dsa_prep(q_all, kv_all, idx_all, row0, B, T, pos0, *, tq):
    qd, kv2 = A_HEADS * LANE, 2 * A_KV_HEADS * LANE
    kv_col = (kv_all.shape[1] - kv2) // kv2
    kvw = kv2 // 2
    nb = T // tq
    blk0 = row0 // tq
    assert row0 % tq == 0 and qd // A_HEADS == LANE
    pos = pos0 + jnp.arange(T, dtype=jnp.int32)
    tab_q = _rope_tables(pos, LANE, LANE)
    tab_i = _rope_tables(pos, LANE, IDX_DIM)
    row = lambda w: pl.BlockSpec((tq, w), lambda b, i: (blk0 + b * nb + i, 0))
    tab = pl.BlockSpec((3, tq, LANE), lambda b, i: (0, i, 0))
    out = lambda w: pl.BlockSpec((1, tq, w), lambda b, i: (b, i, 0))
    f32, bf16 = jnp.float32, jnp.bfloat16
    heads4 = pl.BlockSpec((1, tq, A_KV_HEADS, LANE), lambda b, i: (b, i, 0, 0))
    shapes = [((B, T, qd), bf16), ((B, T, A_KV_HEADS, LANE), f32), ((B, T, A_KV_HEADS, LANE), f32),
              ((B, T, IDX_DIM), f32),
              ((B, T, kvw), bf16), ((B, T, kvw), bf16), ((B, nb, IDX_HEADS * tq, IDX_PACK), bf16),
              ((B, T, IDX_PACK), bf16), ((B, T, IDX_HEADS), f32)]
    return pl.pallas_call(
        functools.partial(_dsa_prep_kernel, q_scale=LANE ** -0.5 * LOG2_E),
        grid=(B, nb),
        in_specs=[row(qd), pl.BlockSpec((tq, kv2), lambda b, i: (blk0 + b * nb + i, kv_col)),
                  row(idx_all.shape[1]), tab, tab],
        out_specs=[out(qd), heads4, heads4, out(IDX_DIM), out(kvw), out(kvw),
                   pl.BlockSpec((1, 1, IDX_HEADS * tq, IDX_PACK), lambda b, i: (b, i, 0, 0)),
                   out(IDX_PACK), out(IDX_HEADS)],
        out_shape=[jax.ShapeDtypeStruct(s, d) for s, d in shapes],
        compiler_params=pltpu.CompilerParams(
            dimension_semantics=("parallel", "parallel"), vmem_limit_bytes=V7X_VMEM_LIMIT_BYTES),
        name="dsa_prep",
    )(q_all, kv_all, idx_all, tab_q, tab_i)


def _dsa_layer(x, mod, gate, sid, streams, w_in, w_out):
    D = x.shape[1]
    hd = D // A_HEADS
    q_dim, kv_dim = A_HEADS * hd, A_KV_HEADS * hd
    n_idx = IDX_HEADS * IDX_DIM + IDX_DIM + IDX_HEADS
    w_main = _weight_parts(w_in[:, :q_dim + 2 * kv_dim], 1)
    w_idx = jnp.pad(w_in[:, q_dim + 2 * kv_dim:], ((0, 0), (0, _round_up(n_idx, LANE) - n_idx)))
    q_all = kv_all = pmatmul(x, w_main, tn=q_dim + 2 * kv_dim, mod=mod, sid=sid)
    idx_all = pmatmul(x, _weight_parts(w_idx, 3), tn=w_idx.shape[1], mod=mod, sid=sid)
    caches = []
    o_all = jnp.zeros((x.shape[0], q_dim), jnp.bfloat16)
    for row0, B, T, pos0, past_k, past_v, past_kidx in streams:
        o_all, k, v, ki = _dsa_stream(q_all, kv_all, idx_all, row0, B, T, pos0, past_k, past_v, past_kidx, o_all)
        caches.append((k, v, ki))
    x = pmatmul(o_all, _weight_parts(w_out, 1), tn=2048, res=(x, gate), sid=sid)
    return x, caches


DSA_KEY_TILE = 512
DSA_QUERY_TILE = 256


def _dsa_stream(q_all, kv_all, idx_all, row0, B, T, pos0, past_k, past_v, past_kidx, o_all):
    tq = min(DSA_QUERY_TILE, T)
    q, k, v, ki, kb, vb, qi3, kidx3, wi = _dsa_prep(q_all, kv_all, idx_all, row0, B, T, pos0, tq=tq)
    n_keys = T
    if past_k is not None:
        P = past_k.shape[1]
        n_keys = P + T
        kb = jnp.concatenate([past_k.astype(jnp.bfloat16).reshape(B, P, -1), kb], axis=1)
        vb = jnp.concatenate([past_v.astype(jnp.bfloat16).reshape(B, P, -1), vb], axis=1)
        ph, pl_ = _split_bf16(past_kidx)
        kidx3 = jnp.concatenate([jnp.concatenate([ph, ph, pl_, jnp.zeros_like(ph)], axis=-1), kidx3], axis=1)
    pad = ((0, 0), (0, _round_up(n_keys, DSA_KEY_TILE) - n_keys), (0, 0))
    kb, vb, kidx3 = (jnp.pad(a, pad) for a in (kb, vb, kidx3))
    topk = min(TOPK_MAX, n_keys // 4)
    o_all = _dsa_core(q, qi3, wi, kb, vb, kidx3, pos0, topk, o_all, row0, tq=tq, tk=DSA_KEY_TILE)
    return o_all, k, v, ki


GDN_HEAD_GROUP = 32


def _bf16_dot(a, b):
    return jnp.dot(a.astype(jnp.bfloat16), b.astype(jnp.bfloat16), preferred_element_type=jnp.float32)


def _dot3(a, b):
    ah, al = _split_bf16(a)
    bh, bl = _split_bf16(b)
    out = jnp.dot(ah, bh, preferred_element_type=jnp.float32)
    out = out + jnp.dot(ah, bl, preferred_element_type=jnp.float32)
    return out + jnp.dot(al, bh, preferred_element_type=jnp.float32)


def _conv_silu(x_ref, w_ref, xe_ref):
    C = x_ref.shape[0]
    taps = w_ref.shape[0]
    xe_ref[SUBLANE:, :] = x_ref[...]
    first = SUBLANE - (taps - 1)
    acc = xe_ref[first:first + C, :] * w_ref[0:1, :]
    for j in range(1, taps):
        acc = acc + xe_ref[first + j:first + j + C, :] * w_ref[j:j + 1, :]
    xe_ref[:SUBLANE, :] = xe_ref[C:, :]
    return acc * jax.nn.sigmoid(acc)


def _gdn_kernel(xq_ref, xk_ref, xv_ref, wq_ref, wk_ref, wv_ref, cq_ref, ck_ref, cv_ref,
                z_ref, g_ref, gt_ref, beta_ref, nw_ref, s0_ref, o_all_ref,
                o_ref, s_out_ref, s_ref, eq_ref, ek_ref, ev_ref):
    del o_all_ref
    n = pl.program_id(2)
    C = xq_ref.shape[0]
    hg = g_ref.shape[3]
    dk = s_ref.shape[1]
    dv = s_ref.shape[2]
    rep = hg // (xk_ref.shape[1] // dk)

    @pl.when(n == 0)
    def _():
        s_ref[...] = s0_ref[0]
        for e_ref, c_ref in ((eq_ref, cq_ref), (ek_ref, ck_ref), (ev_ref, cv_ref)):
            e_ref[:SUBLANE, :] = jnp.zeros((SUBLANE, e_ref.shape[1]), jnp.float32)
            e_ref[SUBLANE - c_ref.shape[1]:SUBLANE, :] = c_ref[0]

    qc = _conv_silu(xq_ref, wq_ref, eq_ref)
    kc = _conv_silu(xk_ref, wk_ref, ek_ref)
    vc = _conv_silu(xv_ref, wv_ref, ev_ref)
    q_heads, k_heads = [], []
    for i in range(hg // rep):
        qh = qc[:, i * dk:(i + 1) * dk]
        kh = kc[:, i * dk:(i + 1) * dk]
        q_heads.append(qh * (lax.rsqrt(jnp.sum(qh * qh, axis=1, keepdims=True) + EPS) * dk ** -0.5))
        k_heads.append(kh * lax.rsqrt(jnp.sum(kh * kh, axis=1, keepdims=True) + EPS))

    ri = lax.broadcasted_iota(jnp.int32, (C, C), 0)
    ci = lax.broadcasted_iota(jnp.int32, (C, C), 1)
    causal = ri >= ci
    strict = ri > ci
    eye = jnp.where(ri == ci, 1.0, 0.0)
    g = g_ref[0, 0]
    gc_all = _dot3(jnp.where(causal, 1.0, 0.0), g)
    gr_all = _dot3(gt_ref[0, 0, 0], jnp.where(ri <= ci, 1.0, 0.0))
    beta = beta_ref[0, 0]
    nw = nw_ref[...]

    heads = range(hg)
    qs = [q_heads[h // rep] for h in heads]
    ks = [k_heads[h // rep] for h in heads]
    gcs = [gc_all[:, h:h + 1] for h in heads]
    bcols = [beta[:, h:h + 1] for h in heads]
    decays = [jnp.where(causal, jnp.exp(jnp.where(causal, gcs[h] - gr_all[h:h + 1, :], 0.0)), 0.0)
              for h in heads]
    kbs = [ks[h] * bcols[h] for h in heads]
    kks = [lax.dot_general(kbs[h].astype(jnp.bfloat16), ks[h].astype(jnp.bfloat16), _NT_DIMS,
                           preferred_element_type=jnp.float32) for h in heads]
    bms = [jnp.where(strict, -(kks[h] * decays[h]), 0.0) for h in heads]
    egs = [jnp.exp(gcs[h]) for h in heads]
    rhss = [jnp.concatenate([vc[:, h * dv:(h + 1) * dv] * bcols[h], kbs[h] * egs[h]], axis=1)
            for h in heads]
    pairs = range(hg // 2)
    left = lax.broadcasted_iota(jnp.int32, (C, 2 * C), 1) < C
    zero16 = jnp.zeros((C, 2 * C), jnp.bfloat16)

    def blockdiag(part):
        return jnp.concatenate([jnp.where(left, part, zero16), jnp.where(left, zero16, part)], axis=0)

    def dot3_pairs(x_parts, y_parts):
        xh, xl = x_parts
        yh, yl = blockdiag(y_parts[0]), blockdiag(y_parts[1])
        out = jnp.dot(xh, yh, preferred_element_type=jnp.float32)
        out = out + jnp.dot(xh, yl, preferred_element_type=jnp.float32)
        return out + jnp.dot(xl, yh, preferred_element_type=jnp.float32)

    b2 = [jnp.concatenate([bms[2 * i], bms[2 * i + 1]], axis=1) for i in pairs]
    eye2 = jnp.concatenate([eye, eye], axis=1)
    p2 = [eye2 + b2[i] for i in pairs]
    b2_parts = [_split_bf16(b2[i]) for i in pairs]
    step = 2
    while step < C:
        b2 = [dot3_pairs(b2_parts[i], b2_parts[i]) for i in pairs]
        b2_parts = [_split_bf16(b2[i]) for i in pairs]
        p2 = [p2[i] + dot3_pairs(_split_bf16(p2[i]), b2_parts[i]) for i in pairs]
        step *= 2
    ps = [p2[h // 2][:, (h % 2) * C:(h % 2 + 1) * C] for h in heads]
    ws = [_dot3(ps[h], rhss[h]) for h in heads]
    qks = [lax.dot_general(qs[h].astype(jnp.bfloat16), ks[h].astype(jnp.bfloat16), _NT_DIMS,
                           preferred_element_type=jnp.float32) * decays[h] for h in heads]
    g_lasts = [gcs[h][C - 1:C, :] for h in heads]
    ss = [s_ref[h] for h in heads]
    us = [ws[h][:, :dv] - _bf16_dot(ws[h][:, dv:], ss[h]) for h in heads]
    os_ = [_bf16_dot(qs[h] * egs[h], ss[h]) + _bf16_dot(qks[h], us[h]) for h in heads]
    for h in heads:
        ke = ks[h] * jnp.exp(g_lasts[h] - gcs[h])
        s_ref[h] = ss[h] * jnp.exp(g_lasts[h]) + _bf16_dot(ke.T, us[h])
    for h in heads:
        o = os_[h]
        o = o * lax.rsqrt(jnp.mean(o * o, axis=1, keepdims=True) + EPS) * nw
        z = z_ref[:, h * dv:(h + 1) * dv]
        o_ref[:, h * dv:(h + 1) * dv] = (o * (z * jax.nn.sigmoid(z))).astype(o_ref.dtype)

    @pl.when(n == pl.num_programs(2) - 1)
    def _():
        s_out_ref[0] = s_ref[...]


def _gdn_core(qk, v, z, row0, conv_w, conv_state, g, beta, norm_w, s0, o_all):
    B, T, VH = g.shape
    dk, dv = s0.shape[2], s0.shape[3]
    vd = VH * dv
    qd = (conv_w.shape[1] - vd) // 2
    C = min(CHUNK, T)
    N = T // C
    assert row0 % C == 0
    blk0 = row0 // C
    hg = min(GDN_HEAD_GROUP, VH)
    ng = VH // hg
    qw = qd // ng
    vw = hg * dv
    assert qd % qw == 0 and (2 * qd) % vw == 0
    k0, v0 = qd // qw, (2 * qd) // vw
    z0 = (z.shape[1] - vd) // vw
    vx0 = (v.shape[1] - vd) // vw
    taps = conv_w.shape[0]

    def grouped(a):
        return a.reshape(B, T, ng, hg).transpose(0, 2, 1, 3)

    gg, bg = grouped(g), grouped(beta)
    gt = gg.reshape(B, ng, N, C, hg).transpose(0, 1, 2, 4, 3)
    small = pl.BlockSpec((1, 1, C, hg), lambda b, j, n: (b, j, n, 0))
    o, s = pl.pallas_call(
        _gdn_kernel,
        grid=(B, ng, N),
        in_specs=[pl.BlockSpec((C, qw), lambda b, j, n: (blk0 + b * N + n, j)),
                  pl.BlockSpec((C, qw), lambda b, j, n: (blk0 + b * N + n, k0 + j)),
                  pl.BlockSpec((C, vw), lambda b, j, n: (blk0 + b * N + n, vx0 + j)),
                  pl.BlockSpec((taps, qw), lambda b, j, n: (0, j)),
                  pl.BlockSpec((taps, qw), lambda b, j, n: (0, k0 + j)),
                  pl.BlockSpec((taps, vw), lambda b, j, n: (0, v0 + j)),
                  pl.BlockSpec((1, taps - 1, qw), lambda b, j, n: (b, 0, j)),
                  pl.BlockSpec((1, taps - 1, qw), lambda b, j, n: (b, 0, k0 + j)),
                  pl.BlockSpec((1, taps - 1, vw), lambda b, j, n: (b, 0, v0 + j)),
                  pl.BlockSpec((C, vw), lambda b, j, n: (blk0 + b * N + n, z0 + j)),
                  small,
                  pl.BlockSpec((1, 1, 1, hg, C), lambda b, j, n: (b, j, n, 0, 0)),
                  small,
                  pl.BlockSpec((1, dv), lambda b, j, n: (0, 0)),
                  pl.BlockSpec((1, hg, dk, dv), lambda b, j, n: (b, j, 0, 0)),
                  pl.BlockSpec(memory_space=pl.ANY)],
        out_specs=[pl.BlockSpec((C, hg * dv), lambda b, j, n: (blk0 + b * N + n, j)),
                   pl.BlockSpec((1, hg, dk, dv), lambda b, j, n: (b, j, 0, 0))],
        out_shape=[jax.ShapeDtypeStruct(o_all.shape, o_all.dtype),
                   jax.ShapeDtypeStruct(s0.shape, jnp.float32)],
        input_output_aliases={15: 0},
        scratch_shapes=[pltpu.VMEM((hg, dk, dv), jnp.float32),
                        pltpu.VMEM((SUBLANE + C, qw), jnp.float32),
                        pltpu.VMEM((SUBLANE + C, qw), jnp.float32),
                        pltpu.VMEM((SUBLANE + C, vw), jnp.float32)],
        compiler_params=pltpu.CompilerParams(
            dimension_semantics=("parallel", "parallel", "arbitrary"),
            vmem_limit_bytes=V7X_VMEM_LIMIT_BYTES),
        name="gdn_core",
    )(qk, qk, v, conv_w, conv_w, conv_w, conv_state, conv_state, conv_state,
      z, gg, gt, bg, norm_w.reshape(1, dv), s0, o_all)
    return o, s


def _gdn_layer(x, mod, gate, sid, streams, w_in, conv_w, a_log, dt_bias, norm_w, w_out):
    qk_dim, v_dim = GDN_QK_HEADS * GDN_DK, GDN_V_HEADS * GDN_DV
    conv_dim = 2 * qk_dim + v_dim
    n_gate = 2 * GDN_V_HEADS
    w_main = _weight_parts(w_in[:, :conv_dim + v_dim], 1)
    w_gate = jnp.pad(w_in[:, conv_dim + v_dim:], ((0, 0), (0, _round_up(n_gate, LANE) - n_gate)))
    assert 2 * qk_dim == v_dim
    qk_all, v_all, z_all = (pmatmul(x, w_main, col0=c * v_dim, ncols=v_dim, tm=256, tn=v_dim, mod=mod, sid=sid)
                            for c in range(3))
    gates_all = pmatmul(x, _weight_parts(w_gate, 1), tn=w_gate.shape[1], mod=mod, sid=sid)
    states = []
    o_all = jnp.zeros((x.shape[0], v_dim), jnp.bfloat16)
    for row0, B, T, conv_state, ssm_state in streams:
        rows = slice(row0, row0 + B * T)
        gates = gates_all[rows].reshape(B, T, -1)
        beta_raw, a_raw = gates[..., :GDN_V_HEADS], gates[..., GDN_V_HEADS:n_gate]
        if conv_state is None:
            conv_state = jnp.zeros((B, CONV_W - 1, conv_dim), x.dtype)
        if ssm_state is None:
            ssm_state = jnp.zeros((B, GDN_V_HEADS, GDN_DK, GDN_DV), jnp.float32)
        tail = jnp.stack([jnp.concatenate([a[row0 + (b + 1) * T - (CONV_W - 1):row0 + (b + 1) * T]
                                           for a in (qk_all, v_all)], axis=1) for b in range(B)])
        new_conv = jnp.concatenate([conv_state, tail], axis=1)[:, -(CONV_W - 1):]
        beta = jax.nn.sigmoid(beta_raw)
        g = -jnp.exp(a_log) * jax.nn.softplus(a_raw + dt_bias)
        o_all, S = _gdn_core(qk_all, v_all, z_all, row0, conv_w, conv_state, g, beta, norm_w, ssm_state, o_all)
        states.append((new_conv, S))
    x = pmatmul(o_all, _weight_parts(w_out, 1), tm=256, tn=w_out.shape[1], res=(x, gate), sid=sid)
    return x, states


NEG_INF = float("-inf")
_PEER_CAND_ROWS = tuple((i, PEER_TOPK // (i + 1)) for i in range(PEER_TOPK // 2))


def _split_bf16(x):
    hi = x.astype(jnp.bfloat16)
    lo = (x - hi.astype(jnp.float32)).astype(jnp.bfloat16)
    return hi, lo


def _dot3_nt(a, b):
    dn = (((1,), (1,)), ((), ()))
    ah, al = _split_bf16(a)
    bh, bl = _split_bf16(b)
    out = lax.dot_general(ah, bh, dn, preferred_element_type=jnp.float32)
    out = out + lax.dot_general(ah, bl, dn, preferred_element_type=jnp.float32)
    return out + lax.dot_general(al, bh, dn, preferred_element_type=jnp.float32)


def _top_rows_desc(s, n, with_rank=False):
    rows = []
    cur = s
    rank = jnp.full(s.shape, float(n), jnp.float32)
    for i in range(n):
        m = jnp.max(cur, axis=0, keepdims=True)
        rows.append(m)
        hit = cur == m
        if with_rank:
            rank = jnp.where(hit, float(i), rank)
        cur = jnp.where(hit, NEG_INF, cur)
    return (rows, rank) if with_rank else rows


def _stack_rows(rows, lanes):
    n = len(rows)
    rid = lax.broadcasted_iota(jnp.int32, (n, lanes), 0)
    out = jnp.zeros((n, lanes), jnp.float32)
    for i, r in enumerate(rows):
        out = jnp.where(rid == i, r, out)
    return out


def _peer_select_kernel(q_ref, keys_ref, rk2_ref, e2_ref, cnt_ref, e1_ref):
    tm = q_ref.shape[0]
    k = PEER_TOPK
    s1 = _dot3_nt(keys_ref[0], q_ref[:, :PEER_HALF])
    s2 = _dot3_nt(keys_ref[1], q_ref[:, PEER_HALF:])
    r1 = _top_rows_desc(s1, k)
    r2, rank2 = _top_rows_desc(s2, k, with_rank=True)
    v1 = _stack_rows(r1, tm)
    v2 = _stack_rows(r2, tm)
    v2h = v2[:k // 2]
    rid = lax.broadcasted_iota(jnp.int32, (k // 2, tm), 0)
    pieces = [r1[0] + v2]
    for i, n in _PEER_CAND_ROWS[1:]:
        pieces.append(jnp.where(rid < n, r1[i] + v2h, NEG_INF))
    pieces.append(v1[k // 2:] + r2[0])
    cand = jnp.concatenate(pieces, axis=0)
    tau = _top_rows_desc(cand, k)[-1]
    top = r1[0] + r2[0]
    z = jnp.sum(jnp.where(cand >= tau, jnp.exp(cand - top), 0.0), axis=0, keepdims=True)
    cnt = jnp.zeros(s1.shape, jnp.float32)
    for j in range(k):
        cnt = cnt + jnp.where(s1 + r2[j] >= tau, 1.0, 0.0)
    rk2_ref[0] = pltpu.bitcast(rank2.astype(jnp.bfloat16), rk2_ref.dtype)
    e2_ref[0] = pltpu.bitcast((jnp.exp(s2 - r2[0]) / z).astype(jnp.bfloat16), e2_ref.dtype)
    cnt_ref[0] = cnt
    e1_ref[0] = jnp.exp(s1 - r1[0])


def _peer_select(q, keys, *, tm):
    M = q.shape[0]
    nk = keys.shape[1]
    ospec = pl.BlockSpec((1, nk, tm), lambda i, h: (h, 0, i))
    pspec = pl.BlockSpec((1, nk // 2, tm), lambda i, h: (h, 0, i))
    return pl.pallas_call(
        _peer_select_kernel,
        grid=(M // tm, PEER_HEADS),
        in_specs=[pl.BlockSpec((tm, PEER_KEY_DIM), lambda i, h: (i, h)),
                  pl.BlockSpec(keys.shape, lambda i, h: (0, 0, 0))],
        out_specs=[pspec, pspec, ospec, ospec],
        out_shape=[jax.ShapeDtypeStruct((PEER_HEADS, nk // 2, M), jnp.int32)] * 2
        + [jax.ShapeDtypeStruct((PEER_HEADS, nk, M), jnp.float32)] * 2,
        compiler_params=pltpu.CompilerParams(
            dimension_semantics=("parallel", "arbitrary"),
            vmem_limit_bytes=V7X_VMEM_LIMIT_BYTES),
        name="peer_select",
    )(q, keys)


BF16_SUBLANES = 16
PEER_SUB_SPLIT = (1, 1)


def _peer_main_kernel(sid_ref, x_ref, gain_ref, shift_ref, scale_ref, gate_ref, fnorm_ref,
                      u_ref, vT_ref, rk2_ref, e2_ref, cnt_ref, e1_ref, *refs, final_norm):
    out_refs, (hT_ref, coef_ref, acc_ref) = refs[:-3], refs[-3:]
    e = pl.program_id(1)
    te, tm = coef_ref.shape
    nk = 2 * rk2_ref.shape[1]
    rows = BF16_SUBLANES
    blk0 = pl.program_id(0) * (tm // SEQ_BLOCK)

    @pl.when(e == 0)
    def _():
        acc_ref[...] = jnp.zeros_like(acc_ref)
        per_lane_tile = LANE // SEQ_BLOCK
        for t in range(tm // LANE):
            hs = []
            for r in range(t * per_lane_tile, (t + 1) * per_lane_tile):
                rs = slice(r * SEQ_BLOCK, (r + 1) * SEQ_BLOCK)
                hs.append(_modulated_rows(x_ref[rs, :], gain_ref[...], shift_ref, scale_ref, sid_ref[blk0 + r]))
            hT_ref[:, t * LANE:(t + 1) * LANE] = jnp.concatenate(hs, axis=0).T.astype(jnp.bfloat16)

    def expert_acts(span):
        off, size = span
        return jnp.dot(u_ref[off:off + size, :], hT_ref[...], preferred_element_type=jnp.float32)

    def weigh(span, act):
        off, size = span
        for al in range(size // nk):
            a = off // nk + al
            for lg in range(tm // LANE):
                lanes = slice(lg * LANE, (lg + 1) * LANE)
                cnts = [jnp.broadcast_to(cnt_ref[h, a:a + 1, lanes], (rows, LANE)).astype(jnp.bfloat16)
                        for h in range(PEER_HEADS)]
                e1s = [jnp.broadcast_to(e1_ref[h, a:a + 1, lanes], (rows, LANE)).astype(jnp.bfloat16)
                       for h in range(PEER_HEADS)]
                for r in range(nk // rows):
                    words = slice(r * rows // 2, (r + 1) * rows // 2)
                    w = None
                    for h in range(PEER_HEADS):
                        e2 = pltpu.bitcast(e2_ref[h, words, lanes], jnp.bfloat16)
                        rk2 = pltpu.bitcast(rk2_ref[h, words, lanes], jnp.bfloat16)
                        t = jnp.where(rk2 < cnts[h], e2, jnp.zeros_like(e2)) * e1s[h]
                        w = t if w is None else w + t
                    x = act[al * nk + r * rows:al * nk + (r + 1) * rows, lanes]
                    g = 0.5 * x * (1.0 + lax.erf(x * (2.0 ** -0.5)))
                    coef_ref[off + al * nk + r * rows:off + al * nk + (r + 1) * rows, lanes] = (
                        w * g.astype(jnp.bfloat16))

    def accumulate(span):
        off, size = span
        acc_ref[...] += lax.dot_general(vT_ref[off:off + size, :], coef_ref[off:off + size, :],
                                        (((0,), (0,)), ((), ())), preferred_element_type=jnp.float32)

    spans, off = [], 0
    for frac in PEER_SUB_SPLIT:
        spans.append((off, te * frac // sum(PEER_SUB_SPLIT)))
        off += spans[-1][1]
    act = expert_acts(spans[0])
    for i, span in enumerate(spans):
        nxt = expert_acts(spans[i + 1]) if i + 1 < len(spans) else None
        weigh(span, act)
        accumulate(span)
        act = nxt

    @pl.when(e == pl.num_programs(1) - 1)
    def _():
        out = acc_ref[...].T
        for r in range(tm // SEQ_BLOCK):
            rs = slice(r * SEQ_BLOCK, (r + 1) * SEQ_BLOCK)
            y = x_ref[rs, :] + gate_ref[pl.ds(sid_ref[blk0 + r], 1), :] * out[rs, :]
            out_refs[0][rs, :] = y
            if final_norm:
                out_refs[1][rs, :] = y * lax.rsqrt(jnp.mean(y * y, axis=-1, keepdims=True) + EPS) * fnorm_ref[...]


def _peer_main(x, mod, gate, fnorm, sid, u, vT, rk2, e2, cnt, e1, *, tm, te, final_norm):
    M, D = x.shape
    E = u.shape[0]
    nk = cnt.shape[1]
    const = lambda a: pl.BlockSpec(a.shape, lambda i, e, s: (0,) * a.ndim)
    col_spec = pl.BlockSpec((PEER_HEADS, nk // 2, tm), lambda i, e, s: (0, 0, i))
    row_spec = pl.BlockSpec((PEER_HEADS, te // nk, tm), lambda i, e, s: (0, e, i))
    tok_spec = pl.BlockSpec((tm, D), lambda i, e, s: (i, 0))
    n_out = 2 if final_norm else 1
    return pl.pallas_call(
        functools.partial(_peer_main_kernel, final_norm=final_norm),
        grid_spec=pltpu.PrefetchScalarGridSpec(
            num_scalar_prefetch=1,
            grid=(M // tm, E // te),
            in_specs=[tok_spec, const(mod[0]), const(mod[1]), const(mod[2]), const(gate), const(fnorm),
                      pl.BlockSpec((te, D), lambda i, e, s: (e, 0)),
                      pl.BlockSpec((te, D), lambda i, e, s: (e, 0)),
                      col_spec, col_spec, row_spec, row_spec],
            out_specs=[tok_spec] * n_out,
            scratch_shapes=[pltpu.VMEM((D, tm), jnp.bfloat16),
                            pltpu.VMEM((te, tm), jnp.bfloat16),
                            pltpu.VMEM((D, tm), jnp.float32)]),
        out_shape=[jax.ShapeDtypeStruct((M, D), jnp.float32)] * n_out,
        compiler_params=pltpu.CompilerParams(
            dimension_semantics=("parallel", "arbitrary"),
            vmem_limit_bytes=V7X_VMEM_LIMIT_PEER_BYTES),
        name="peer_main",
    )(sid, x, *mod, gate, fnorm, u, vT, rk2, e2, cnt, e1)


def _peer(x, mod, gate, fnorm, sid, w_query, sub_keys, expert_u, expert_v, *, final_norm,
          tm_sel=256, tm=512, te=1024):
    M = x.shape[0]
    q = pmatmul(x, _weight_parts(w_query, 1), tn=2048, mod=mod, sid=sid)
    rk2, e2, cnt, e1 = _peer_select(q, sub_keys, tm=min(tm_sel, M))
    u = expert_u.astype(jnp.bfloat16)
    vT = expert_v.astype(jnp.bfloat16)
    return _peer_main(x, mod, gate, fnorm, sid, u, vT, rk2, e2, cnt, e1, tm=min(tm, M), te=te,
                      final_norm=final_norm)


def kernel(x_prompt, x_sample, c_prompt, c_sample, cache_k_l0, cache_v_l0, cache_kidx_l0, state_conv_l1, state_ssm_l1, norm1_l0, norm2_l0, ada_w_l0, ada_b_l0, attn_in_l0, attn_out_l0, peer_query_l0, peer_keys_l0, peer_u_l0, peer_v_l0, norm1_l1, norm2_l1, ada_w_l1, ada_b_l1, gdn_in_l1, gdn_conv_l1, gdn_a_log_l1, gdn_dt_bias_l1, gdn_norm_l1, gdn_out_l1, peer_query_l1, peer_keys_l1, peer_u_l1, peer_v_l1, final_norm):
    past_len = cache_k_l0.shape[1]
    norm1 = (norm1_l0, norm1_l1)
    norm2 = (norm2_l0, norm2_l1)
    ada_w = (ada_w_l0, ada_w_l1)
    ada_b = (ada_b_l0, ada_b_l1)
    peer_query = (peer_query_l0, peer_query_l1)
    peer_keys = (peer_keys_l0, peer_keys_l1)
    peer_u = (peer_u_l0, peer_u_l1)
    peer_v = (peer_v_l0, peer_v_l1)
    Bp, Tp, D = x_prompt.shape
    Bs, Ts, _ = x_sample.shape
    n_p, n_s = Bp * Tp, Bs * Ts
    x = jnp.concatenate([x_prompt.reshape(n_p, D), x_sample.reshape(n_s, D)], axis=0)
    sid = jnp.concatenate([jnp.repeat(jnp.arange(Bp, dtype=jnp.int32), Tp // SEQ_BLOCK),
                           Bp + jnp.repeat(jnp.arange(Bs, dtype=jnp.int32), Ts // SEQ_BLOCK)])
    c_all = jnp.concatenate([c_prompt, c_sample], axis=0)
    fnorm = final_norm.reshape(1, D)
    for i in range(2):
        mods = _adaln(c_all, ada_w[i], ada_b[i])
        mod1 = (norm1[i].reshape(1, D), mods[0], mods[1])
        if i == 0:
            x, ((nkp, nvp, nkip), (nks, nvs, nkis)) = _dsa_layer(
                x, mod1, mods[2], sid,
                [(0, Bp, Tp, 0, None, None, None), (n_p, Bs, Ts, past_len, cache_k_l0, cache_v_l0, cache_kidx_l0)],
                attn_in_l0, attn_out_l0)
        else:
            x, ((ncp, nsp), (ncs, nss)) = _gdn_layer(
                x, mod1, mods[2], sid, [(0, Bp, Tp, None, None), (n_p, Bs, Ts, state_conv_l1, state_ssm_l1)],
                gdn_in_l1, gdn_conv_l1, gdn_a_log_l1, gdn_dt_bias_l1, gdn_norm_l1, gdn_out_l1)
        mod2 = (norm2[i].reshape(1, D), mods[3], mods[4])
        outs = _peer(x, mod2, mods[5], fnorm, sid, peer_query[i], peer_keys[i], peer_u[i], peer_v[i],
                     final_norm=(i == 1))
        x = outs[0]
    y = outs[1]
    y_prompt = y[:n_p].reshape(Bp, Tp, D)
    y_sample = y[n_p:].reshape(Bs, Ts, D)
    return (y_prompt, y_sample, nkp, nvp, nkip, nks, nvs, nkis, ncp, nsp, ncs, nss)
```
